```python
import jax, jax.numpy as jnp
from jax import lax
import numpy as np

D_MODEL = 4096
BATCH = 1
SEQ = 8192
DEPTH = 1

MEM_LEN = 256
MIX_WIDTH = D_MODEL
POOL_WIDTH = MIX_WIDTH // 2
POOL_WINDOWS = (2, 4, 8, 16)
POOL_GROUP = POOL_WIDTH // len(POOL_WINDOWS)
HGRN_WIDTH = MIX_WIDTH - POOL_WIDTH
HGRN_HEAD_K = 128
HGRN_HEADS = HGRN_WIDTH // HGRN_HEAD_K
HGRN_HEAD_V = HGRN_WIDTH // HGRN_HEADS
HGRN_KEY = HGRN_HEADS * HGRN_HEAD_K
CHUNK = 64
XATTN_HEADS = 4
XATTN_HEAD_DIM = D_MODEL // XATTN_HEADS
N_GROUPS = 4
EXPERTS_PER_GROUP = 8
N_EXPERTS = N_GROUPS * EXPERTS_PER_GROUP
TOP_K = 2
D_EXPERT = D_MODEL // 4
MOE_BLOCK = 128
EPS = 1e-6
IN_SPLITS = [POOL_WIDTH,
             POOL_WIDTH + HGRN_KEY,
             POOL_WIDTH + 2 * HGRN_KEY,
             POOL_WIDTH + 3 * HGRN_KEY,
             POOL_WIDTH + 3 * HGRN_KEY + HGRN_WIDTH]
IN_COLS = POOL_WIDTH + 3 * HGRN_KEY + 2 * HGRN_WIDTH

kernel_name = "hybrid_pool_hgrn2_memxattn_hmoe_encoder"


def rms_norm(x, gain):
    xf = x.astype(jnp.float32)
    y = xf * lax.rsqrt(jnp.mean(xf * xf, axis=-1, keepdims=True) + EPS)
    return (y * gain.astype(jnp.float32)).astype(x.dtype)


def multiscale_pool(u, w_pool, scale):
    S = u.shape[1]
    uf = u.astype(jnp.float32)
    cs = jnp.concatenate([jnp.zeros_like(uf[:, :1]), jnp.cumsum(uf, axis=1)], axis=1)
    t = jnp.arange(S)
    outs = []
    for gi, w in enumerate(POOL_WINDOWS):
        lo = jnp.clip(t - w // 2, 0, S - 1)
        hi = jnp.clip(t + w // 2 - 1, 0, S - 1)
        sl = slice(gi * POOL_GROUP, (gi + 1) * POOL_GROUP)
        csg = cs[..., sl]
        win_sum = csg[:, hi + 1] - csg[:, lo]
        count = (hi - lo + 1).astype(jnp.float32)[None, :, None]
        mixed = (win_sum / count - uf[..., sl]).astype(u.dtype)
        outs.append(jnp.einsum('bsc,cd->bsd', mixed, w_pool[gi]))
    return jnp.concatenate(outs, axis=-1) * scale


def hgrn2_scan(q, v, f_logit, lb):
    B, S, H, K = q.shape
    V = v.shape[-1]
    f = lb + (1.0 - lb) * jax.nn.sigmoid(f_logit)
    logf = jnp.log(f)
    k = 1.0 - f
    N = S // CHUNK
    qc = q.reshape(B, N, CHUNK, H, K)
    kc = k.reshape(B, N, CHUNK, H, K)
    vc = v.reshape(B, N, CHUNK, H, V)
    b = jnp.cumsum(logf.reshape(B, N, CHUNK, H, K), axis=2)
    b_last = b[:, :, -1]
    qd = qc * jnp.exp(b)
    kd = kc * jnp.exp(-b)
    mask = jnp.tril(jnp.ones((CHUNK, CHUNK), dtype=bool))
    A = jnp.where(mask, jnp.einsum('bnthk,bnshk->bnhts', qd, kd), 0.0)
    o_intra = jnp.einsum('bnhts,bnshv->bnthv', A, vc)
    k_end = kc * jnp.exp(b_last[:, :, None] - b)
    dS = jnp.einsum('bnshk,bnshv->bnhkv', k_end, vc)
    decay = jnp.exp(b_last)

    def step(state, inp):
        dec, ds = inp
        return dec[..., None] * state + ds, state

    s0 = jnp.zeros((B, H, K, V), jnp.float32)
    _, s_in = lax.scan(step, s0, (jnp.moveaxis(decay, 1, 0), jnp.moveaxis(dS, 1, 0)))
    s_in = jnp.moveaxis(s_in, 0, 1)
    o_inter = jnp.einsum('bnthk,bnhkv->bnthv', qd, s_in)
    return (o_intra + o_inter).reshape(B, S, H, V)


def hgrn2_mixer(q, f_fwd, f_bwd, i_in, g, lb_fwd, lb_bwd, norm_gain):
    B, S, _ = q.shape
    dt = q.dtype
    qh = jax.nn.silu(q.astype(jnp.float32).reshape(B, S, HGRN_HEADS, HGRN_HEAD_K))
    ih = i_in.astype(jnp.float32).reshape(B, S, HGRN_HEADS, HGRN_HEAD_V)
    ff = f_fwd.astype(jnp.float32).reshape(B, S, HGRN_HEADS, HGRN_HEAD_K)
    fb = f_bwd.astype(jnp.float32).reshape(B, S, HGRN_HEADS, HGRN_HEAD_K)
    o_f = hgrn2_scan(qh, ih, ff, lb_fwd.reshape(HGRN_HEADS, HGRN_HEAD_K))
    o_b = jnp.flip(hgrn2_scan(jnp.flip(qh, 1), jnp.flip(ih, 1), jnp.flip(fb, 1),
                              lb_bwd.reshape(HGRN_HEADS, HGRN_HEAD_K)), 1)
    o = o_f + o_b
    o = o * lax.rsqrt(jnp.mean(o * o, axis=-1, keepdims=True) + EPS)
    o = o * norm_gain.astype(jnp.float32).reshape(HGRN_HEADS, HGRN_HEAD_V)
    o = o.reshape(B, S, HGRN_WIDTH) * jax.nn.silu(g.astype(jnp.float32))
    return o.astype(dt)


def layer_lower_bound(lb_param, layer):
    return jnp.cumsum(jax.nn.softmax(lb_param.astype(jnp.float32), axis=0), axis=0)[layer]


def memory_cross_attention(h, mem_n, wq, wk, wv, wo):
    B, S, D = h.shape
    M = mem_n.shape[1]
    q = jnp.einsum('bsd,de->bse', h, wq).reshape(B, S, XATTN_HEADS, XATTN_HEAD_DIM)
    k = jnp.einsum('bmd,de->bme', mem_n, wk).reshape(B, M, XATTN_HEADS, XATTN_HEAD_DIM)
    v = jnp.einsum('bmd,de->bme', mem_n, wv).reshape(B, M, XATTN_HEADS, XATTN_HEAD_DIM)
    s = jnp.einsum('bshd,bmhd->bhsm', q, k).astype(jnp.float32) * (XATTN_HEAD_DIM ** -0.5)
    p = jax.nn.softmax(s, axis=-1).astype(v.dtype)
    o = jnp.einsum('bhsm,bmhd->bshd', p, v).reshape(B, S, D)
    return jnp.einsum('bsd,de->bse', o, wo)


def hierarchical_moe(h, layer, w_router_group, w_router_expert, w1, w3, w2):
    B, S, D = h.shape
    T = B * S
    xt = h.reshape(T, D)
    g_logits = jnp.einsum('td,dg->tg', xt, w_router_group[layer]).astype(jnp.float32)
    g_probs = jax.nn.softmax(g_logits, axis=-1)
    g_idx = jnp.argmax(g_logits, axis=-1).astype(jnp.int32)
    g_w = jnp.take_along_axis(g_probs, g_idx[:, None], axis=-1)
    e_logits = jnp.einsum('td,gde->tge', xt, w_router_expert[layer]).astype(jnp.float32)
    e_logits = jnp.take_along_axis(e_logits, g_idx[:, None, None], axis=1)[:, 0]
    top_v, top_i = lax.top_k(e_logits, TOP_K)
    e_w = jax.nn.softmax(top_v, axis=-1) * g_w
    eid = g_idx[:, None] * EXPERTS_PER_GROUP + top_i.astype(jnp.int32)

    A = T * TOP_K
    flat_e = eid.reshape(A)
    flat_t = jnp.repeat(jnp.arange(T, dtype=jnp.int32), TOP_K)
    flat_w = e_w.reshape(A)
    order = jnp.argsort(flat_e)
    se = flat_e[order]
    counts = jnp.bincount(flat_e, length=N_EXPERTS)
    padded = ((counts + MOE_BLOCK - 1) // MOE_BLOCK) * MOE_BLOCK
    start_sorted = jnp.cumsum(counts) - counts
    pad_end = jnp.cumsum(padded)
    start_pad = pad_end - padded
    dest = start_pad[se] + jnp.arange(A, dtype=jnp.int32) - start_sorted[se]
    P = A + N_EXPERTS * MOE_BLOCK
    NB = P // MOE_BLOCK
    buf_t = jnp.zeros((P,), jnp.int32).at[dest].set(flat_t[order])
    buf_w = jnp.zeros((P,), jnp.float32).at[dest].set(flat_w[order])
    block_e = jnp.minimum(jnp.searchsorted(pad_end, jnp.arange(NB) * MOE_BLOCK, side='right'),
                          N_EXPERTS - 1).astype(jnp.int32)

    def expert_block(args):
        tok, wt, e = args
        xb = xt[tok]
        a = xb @ w1[layer, e]
        c = xb @ w3[layer, e]
        y = (jax.nn.silu(a) * c) @ w2[layer, e]
        return y * wt[:, None].astype(y.dtype)

    y = lax.map(expert_block, (buf_t.reshape(NB, MOE_BLOCK), buf_w.reshape(NB, MOE_BLOCK), block_e))
    out = jnp.zeros((T, D), h.dtype).at[buf_t].add(y.reshape(P, D).astype(h.dtype))
    return out.reshape(B, S, D)


def setup_inputs(seed: int = 0) -> dict:
    key = jax.random.key(seed)
    ks = jax.random.split(key, 24)
    f32 = jnp.float32
    nrm = lambda k, shape, fan_in: jax.random.normal(k, shape, f32) * (fan_in ** -0.5)
    gain = lambda k, shape: 1.0 + 0.02 * jax.random.normal(k, shape, f32)
    return {
        "x": jax.random.normal(ks[0], (BATCH, SEQ, D_MODEL), f32),
        "mem": jax.random.normal(ks[1], (BATCH, MEM_LEN, D_MODEL), f32),
        "norm_mix": gain(ks[2], (DEPTH, D_MODEL)),
        "w_in": nrm(ks[3], (DEPTH, D_MODEL, IN_COLS), D_MODEL),
        "pool_w": nrm(ks[4], (DEPTH, len(POOL_WINDOWS), POOL_GROUP, POOL_GROUP), POOL_GROUP),
        "pool_scale": gain(ks[5], (DEPTH, POOL_WIDTH)),
        "lb_fwd": 0.1 * jax.random.normal(ks[6], (DEPTH + 1, HGRN_KEY), f32),
        "lb_bwd": 0.1 * jax.random.normal(ks[7], (DEPTH + 1, HGRN_KEY), f32),
        "hgrn_norm": gain(ks[8], (DEPTH, HGRN_WIDTH)),
        "w_out": nrm(ks[9], (DEPTH, MIX_WIDTH, D_MODEL), MIX_WIDTH),
        "norm_xattn": gain(ks[10], (DEPTH, D_MODEL)),
        "norm_mem": gain(ks[11], (DEPTH, D_MODEL)),
        "w_q": nrm(ks[12], (DEPTH, D_MODEL, D_MODEL), D_MODEL),
        "w_k": nrm(ks[13], (DEPTH, D_MODEL, D_MODEL), D_MODEL),
        "w_v": nrm(ks[14], (DEPTH, D_MODEL, D_MODEL), D_MODEL),
        "w_o": nrm(ks[15], (DEPTH, D_MODEL, D_MODEL), D_MODEL),
        "norm_moe": gain(ks[16], (DEPTH, D_MODEL)),
        "w_router_group": nrm(ks[17], (DEPTH, D_MODEL, N_GROUPS), D_MODEL),
        "w_router_expert": nrm(ks[18], (DEPTH, N_GROUPS, D_MODEL, EXPERTS_PER_GROUP), D_MODEL),
        "w1": nrm(ks[19], (DEPTH, N_EXPERTS, D_MODEL, D_EXPERT), D_MODEL),
        "w3": nrm(ks[20], (DEPTH, N_EXPERTS, D_MODEL, D_EXPERT), D_MODEL),
        "w2": nrm(ks[21], (DEPTH, N_EXPERTS, D_EXPERT, D_MODEL), D_EXPERT),
        "norm_final": gain(ks[22], (D_MODEL,)),
    }


def reference(x, mem, norm_mix, w_in, pool_w, pool_scale, lb_fwd, lb_bwd, hgrn_norm, w_out,
              norm_xattn, norm_mem, w_q, w_k, w_v, w_o, norm_moe, w_router_group,
              w_router_expert, w1, w3, w2, norm_final):
    for l in range(DEPTH):
        h = rms_norm(x, norm_mix[l])
        proj = jnp.einsum('bsd,dc->bsc', h, w_in[l])
        u_pool, q, f_f, f_b, i_in, g = jnp.split(proj, IN_SPLITS, axis=-1)
        y_pool = multiscale_pool(u_pool, pool_w[l], pool_scale[l])
        y_hgrn = hgrn2_mixer(q, f_f, f_b, i_in, g,
                             layer_lower_bound(lb_fwd, l), layer_lower_bound(lb_bwd, l),
                             hgrn_norm[l])
        mixed = jnp.concatenate([y_pool, y_hgrn.astype(y_pool.dtype)], axis=-1)
        x = x + jnp.einsum('bsc,cd->bsd', mixed, w_out[l])
        mem_n = rms_norm(mem, norm_mem[l])
        x = x + memory_cross_attention(rms_norm(x, norm_xattn[l]), mem_n,
                                       w_q[l], w_k[l], w_v[l], w_o[l])
        x = x + hierarchical_moe(rms_norm(x, norm_moe[l]), l, w_router_group,
                                 w_router_expert, w1, w3, w2)
    return rms_norm(x, norm_final)
```

```python
import functools

import jax
import jax.numpy as jnp
from jax import lax
from jax.experimental import pallas as pl
from jax.experimental.pallas import tpu as pltpu

F32 = jnp.float32
BF16 = jnp.bfloat16
U32 = jnp.uint32

EPS = 1e-6
LANES = 128
SUBLANES = 8
VMEM_BYTES_V7X = 64 * 1024 * 1024

POOL_WINDOWS = (2, 4, 8, 16)
HEAD = 128
CHUNK = 64
XATTN_HEADS = 4
N_GROUPS = 4
EXPERTS_PER_GROUP = 8
N_EXPERTS = N_GROUPS * EXPERTS_PER_GROUP
TOP_K = 2
ROUTER_COLS = LANES


def _params(sem, vmem_bytes):
    return pltpu.CompilerParams(dimension_semantics=sem, vmem_limit_bytes=int(vmem_bytes))


def _vmem_limit(block_bytes):
    return min(int(block_bytes * 1.25) + (6 << 20), VMEM_BYTES_V7X - (4 << 20))


def _rmsnorm_kernel(x_ref, g_ref, o_ref):
    x = x_ref[...]
    ms = jnp.mean(x * x, axis=-1, keepdims=True)
    o_ref[...] = (x * lax.rsqrt(ms + EPS) * g_ref[...]).astype(o_ref.dtype)


def _rmsnorm(x, gain, out_dtype, tm=256):
    t, d = x.shape
    blk = tm * d * (4 + jnp.dtype(out_dtype).itemsize) * 2
    return pl.pallas_call(
        _rmsnorm_kernel,
        grid=(t // tm,),
        in_specs=[pl.BlockSpec((tm, d), lambda i: (i, 0)),
                  pl.BlockSpec((1, d), lambda i: (0, 0))],
        out_specs=pl.BlockSpec((tm, d), lambda i: (i, 0)),
        out_shape=jax.ShapeDtypeStruct((t, d), out_dtype),
        compiler_params=_params(("arbitrary",), _vmem_limit(blk)),
        name="rmsnorm",
    )(x, gain.reshape(1, d))


def _mm_kernel(*refs, n_a, has_resid):
    a_refs = refs[:n_a]
    w_ref = refs[n_a]
    r_ref = refs[n_a + 1] if has_resid else None
    o_ref, wb_ref = refs[-2], refs[-1]

    @pl.when(pl.program_id(1) == 0)
    def _():
        wb_ref[...] = w_ref[...].astype(BF16)

    acc = None
    k0 = 0
    for a_ref in a_refs:
        kk = a_ref.shape[1]
        part = jnp.dot(a_ref[...], wb_ref[k0:k0 + kk, :], preferred_element_type=F32)
        acc = part if acc is None else acc + part
        k0 += kk
    if has_resid:
        acc = acc + r_ref[...]
    o_ref[...] = acc.astype(o_ref.dtype)


def _matmul(a_parts, w, *, resid=None, out_dtype=F32, tm=512, tn=512):
    m = a_parts[0].shape[0]
    k, n = w.shape
    assert sum(a.shape[1] for a in a_parts) == k
    tm = min(tm, m)
    in_specs = [pl.BlockSpec((tm, a.shape[1]), lambda j, i: (i, 0)) for a in a_parts]
    in_specs.append(pl.BlockSpec((k, tn), lambda j, i: (0, j)))
    args = list(a_parts) + [w]
    if resid is not None:
        in_specs.append(pl.BlockSpec((tm, tn), lambda j, i: (i, j)))
        args.append(resid)
    osz = jnp.dtype(out_dtype).itemsize
    blk = (2 * tm * k * 2 + 2 * k * tn * 4 + k * tn * 2 + 2 * tm * tn * osz
           + (2 * tm * tn * 4 if resid is not None else 0) + tm * tn * 4)
    return pl.pallas_call(
        functools.partial(_mm_kernel, n_a=len(a_parts), has_resid=resid is not None),
        grid=(n // tn, m // tm),
        in_specs=in_specs,
        out_specs=pl.BlockSpec((tm, tn), lambda j, i: (i, j)),
        out_shape=jax.ShapeDtypeStruct((m, n), out_dtype),
        scratch_shapes=[pltpu.VMEM((k, tn), BF16)],
        compiler_params=_params(("arbitrary", "arbitrary"), _vmem_limit(blk)),
        name="matmul",
    )(*args)


def _pool_kernel(u_ref, up_ref, un_ref, w_ref, s_ref, o_ref, *, tm, seq, group):
    i = pl.program_id(0)
    last = pl.num_programs(0) - 1
    n = tm + 2 * SUBLANES
    row = lax.broadcasted_iota(jnp.int32, (tm, 1), 0) + i * tm
    for gi, w in enumerate(POOL_WINDOWS):
        cs = slice(gi * group, (gi + 1) * group)
        u = u_ref[:, cs]
        prev = jnp.where(i > 0, up_ref[:, cs], 0.0)
        nxt = jnp.where(i < last, un_ref[:, cs], 0.0)
        f = jnp.concatenate([prev, u, nxt], axis=0)
        step = 1
        while step < w:
            f = f + pltpu.roll(f, n - step, 0)
            step *= 2
        half = w // 2
        first = SUBLANES - half
        if first:
            f = pltpu.roll(f, n - first, 0)
        win = f[:tm]
        lo = jnp.maximum(row - half, 0)
        hi = jnp.minimum(row + half - 1, seq - 1)
        cnt = (hi - lo + 1).astype(F32)
        mixed = win / cnt - u
        y = jnp.dot(mixed.astype(BF16), w_ref[gi].astype(BF16), preferred_element_type=F32)
        o_ref[:, cs] = (y * s_ref[:, cs]).astype(o_ref.dtype)


def _pool_mixer(proj, pool_w, pool_scale, *, tm=512):
    t = proj.shape[0]
    ng, group, _ = pool_w.shape
    width = ng * group
    rpb = tm // SUBLANES
    nhalo = t // SUBLANES
    blk = 2 * (tm * width * 4 + 2 * SUBLANES * width * 4 + ng * group * group * 4 + tm * width * 2) + 6 * tm * group * 4
    return pl.pallas_call(
        functools.partial(_pool_kernel, tm=tm, seq=t, group=group),
        grid=(t // tm,),
        in_specs=[pl.BlockSpec((tm, width), lambda i: (i, 0)),
                  pl.BlockSpec((SUBLANES, width), lambda i: (jnp.maximum(i * rpb - 1, 0), 0)),
                  pl.BlockSpec((SUBLANES, width), lambda i: (jnp.minimum((i + 1) * rpb, nhalo - 1), 0)),
                  pl.BlockSpec((ng, group, group), lambda i: (0, 0, 0)),
                  pl.BlockSpec((1, width), lambda i: (0, 0))],
        out_specs=pl.BlockSpec((tm, width), lambda i: (i, 0)),
        out_shape=jax.ShapeDtypeStruct((t, width), BF16),
        compiler_params=_params(("arbitrary",), _vmem_limit(blk)),
        name="pool_mixer",
    )(proj, proj, proj, pool_w, pool_scale.reshape(1, width))


def _chunk_cumsum(x, reverse):
    n = x.shape[0]
    pos = lax.broadcasted_iota(jnp.int32, x.shape, 0) % CHUNK
    s = 1
    while s < CHUNK:
        if reverse:
            x = x + jnp.where(pos < CHUNK - s, pltpu.roll(x, n - s, 0), 0.0)
        else:
            x = x + jnp.where(pos >= s, pltpu.roll(x, s, 0), 0.0)
        s *= 2
    return x


def _lower_bound(lb_raw, layer):
    e = jnp.exp(lb_raw - jnp.max(lb_raw, axis=0, keepdims=True))
    return jnp.sum(e[:layer + 1], axis=0, keepdims=True) / jnp.sum(e, axis=0, keepdims=True)


def _hgrn_direction(q_raw, f_raw, v, lb, st_ref, h, mask, *, reverse):
    tb = q_raw.shape[0]
    nc = tb // CHUNK
    q = q_raw * jax.nn.sigmoid(q_raw)
    f = lb + (1.0 - lb) * jax.nn.sigmoid(f_raw)
    logf = jnp.log(f)
    k = 1.0 - f
    b = _chunk_cumsum(logf, reverse)
    b3 = b.reshape(nc, CHUNK, HEAD)
    edge = 0 if reverse else CHUNK - 1
    b_last = b3[:, edge:edge + 1, :]
    qd = (q * jnp.exp(b)).astype(BF16)
    kd = (k * jnp.exp(-b)).astype(BF16)
    k_end = (k.reshape(nc, CHUNK, HEAD) * jnp.exp(b_last - b3)).astype(BF16)
    decay = jnp.exp(b_last)
    vb = v.astype(BF16)

    a = lax.dot_general(qd, kd, (((1,), (1,)), ((), ())), preferred_element_type=F32)
    a = jnp.where(mask, a, 0.0)
    o_intra = jnp.dot(a.astype(BF16), vb, preferred_element_type=F32)

    st = st_ref[h]
    o_inter = [None] * nc
    order = range(nc - 1, -1, -1) if reverse else range(nc)
    for c in order:
        rows = slice(c * CHUNK, (c + 1) * CHUNK)
        o_inter[c] = lax.dot_general(qd[rows], st.astype(BF16), (((1,), (1,)), ((), ())),
                                     preferred_element_type=F32)
        d_st = lax.dot_general(vb[rows], k_end[c], (((0,), (0,)), ((), ())), preferred_element_type=F32)
        st = st * decay[c] + d_st
    st_ref[h] = st
    return o_intra + jnp.concatenate(o_inter, axis=0)


def _intra_mask(tb, reverse):
    r = lax.broadcasted_iota(jnp.int32, (tb, tb), 0)
    c = lax.broadcasted_iota(jnp.int32, (tb, tb), 1)
    same = (r // CHUNK) == (c // CHUNK)
    return same & ((c >= r) if reverse else (c <= r))


def _hgrn_fwd_kernel(q_ref, f_ref, i_ref, lb_ref, o_ref, st_ref, *, layer, heads):
    @pl.when(pl.program_id(0) == 0)
    def _():
        st_ref[...] = jnp.zeros_like(st_ref)

    mask = _intra_mask(q_ref.shape[0], False)

    def head(h, carry):
        hs = pl.ds(pl.multiple_of(h * HEAD, HEAD), HEAD)
        lb = _lower_bound(lb_ref[:, hs], layer)
        o_ref[:, hs] = _hgrn_direction(q_ref[:, hs], f_ref[:, hs], i_ref[:, hs], lb, st_ref, h, mask,
                                       reverse=False)
        return carry

    lax.fori_loop(0, heads, head, 0)


def _hgrn_bwd_kernel(q_ref, f_ref, i_ref, g_ref, of_ref, lb_ref, gain_ref, y_ref, st_ref, *, layer, heads):
    @pl.when(pl.program_id(0) == 0)
    def _():
        st_ref[...] = jnp.zeros_like(st_ref)

    mask = _intra_mask(q_ref.shape[0], True)

    def head(h, carry):
        hs = pl.ds(pl.multiple_of(h * HEAD, HEAD), HEAD)
        lb = _lower_bound(lb_ref[:, hs], layer)
        o = of_ref[:, hs] + _hgrn_direction(q_ref[:, hs], f_ref[:, hs], i_ref[:, hs], lb, st_ref, h, mask,
                                            reverse=True)
        o = o * lax.rsqrt(jnp.mean(o * o, axis=-1, keepdims=True) + EPS) * gain_ref[:, hs]
        g = g_ref[:, hs]
        y_ref[:, hs] = (o * (g * jax.nn.sigmoid(g))).astype(y_ref.dtype)
        return carry

    lax.fori_loop(0, heads, head, 0)


def _hgrn_mixer(proj, lb_fwd, lb_bwd, hgrn_norm, *, layer, width, col0, tb=256):
    t = proj.shape[0]
    heads = width // HEAD
    nb = t // tb
    c = col0 // width
    layers = lb_fwd.shape[0]
    blk_f = 2 * (3 * tb * width * 4 + layers * width * 4 + tb * width * 4) + heads * HEAD * HEAD * 4
    o_f = pl.pallas_call(
        functools.partial(_hgrn_fwd_kernel, layer=layer, heads=heads),
        grid=(nb,),
        in_specs=[pl.BlockSpec((tb, width), lambda b: (b, c)),
                  pl.BlockSpec((tb, width), lambda b: (b, c + 1)),
                  pl.BlockSpec((tb, width), lambda b: (b, c + 3)),
                  pl.BlockSpec((layers, width), lambda b: (0, 0))],
        out_specs=pl.BlockSpec((tb, width), lambda b: (b, 0)),
        out_shape=jax.ShapeDtypeStruct((t, width), F32),
        scratch_shapes=[pltpu.VMEM((heads, HEAD, HEAD), F32)],
        compiler_params=_params(("arbitrary",), _vmem_limit(blk_f)),
        name="hgrn_fwd",
    )(proj, proj, proj, lb_fwd)
    blk_b = 2 * (5 * tb * width * 4 + layers * width * 4 + width * 4 + tb * width * 2) + heads * HEAD * HEAD * 4
    return pl.pallas_call(
        functools.partial(_hgrn_bwd_kernel, layer=layer, heads=heads),
        grid=(nb,),
        in_specs=[pl.BlockSpec((tb, width), lambda b: (nb - 1 - b, c)),
                  pl.BlockSpec((tb, width), lambda b: (nb - 1 - b, c + 2)),
                  pl.BlockSpec((tb, width), lambda b: (nb - 1 - b, c + 3)),
                  pl.BlockSpec((tb, width), lambda b: (nb - 1 - b, c + 4)),
                  pl.BlockSpec((tb, width), lambda b: (nb - 1 - b, 0)),
                  pl.BlockSpec((layers, width), lambda b: (0, 0)),
                  pl.BlockSpec((1, width), lambda b: (0, 0))],
        out_specs=pl.BlockSpec((tb, width), lambda b: (nb - 1 - b, 0)),
        out_shape=jax.ShapeDtypeStruct((t, width), BF16),
        scratch_shapes=[pltpu.VMEM((heads, HEAD, HEAD), F32)],
        compiler_params=_params(("arbitrary",), _vmem_limit(blk_b)),
        name="hgrn_bwd",
    )(proj, proj, proj, proj, o_f, lb_bwd, hgrn_norm.reshape(1, width))


def _xattn_kernel(q_ref, k_ref, v_ref, o_ref, *, scale):
    s = lax.dot_general(q_ref[...], k_ref[...], (((1,), (1,)), ((), ())), preferred_element_type=F32) * scale
    s = s - jnp.max(s, axis=-1, keepdims=True)
    p = jnp.exp(s)
    p = p / jnp.sum(p, axis=-1, keepdims=True)
    o_ref[...] = jnp.dot(p.astype(BF16), v_ref[...], preferred_element_type=F32).astype(o_ref.dtype)


def _xattn(q, k, v, *, tm=512):
    t, d = q.shape
    m = k.shape[0]
    dh = d // XATTN_HEADS
    blk = 2 * (2 * tm * dh * 2 + 2 * m * dh * 2) + 4 * tm * m * 4 + tm * dh * 4
    return pl.pallas_call(
        functools.partial(_xattn_kernel, scale=dh ** -0.5),
        grid=(XATTN_HEADS, t // tm),
        in_specs=[pl.BlockSpec((tm, dh), lambda h, i: (i, h)),
                  pl.BlockSpec((m, dh), lambda h, i: (0, h)),
                  pl.BlockSpec((m, dh), lambda h, i: (0, h))],
        out_specs=pl.BlockSpec((tm, dh), lambda h, i: (i, h)),
        out_shape=jax.ShapeDtypeStruct((t, d), BF16),
        compiler_params=_params(("arbitrary", "arbitrary"), _vmem_limit(blk)),
        name="xattn",
    )(q, k, v)


def _pack_halves(x):
    w = x.shape[1] // 2
    hi = lax.bitcast_convert_type(x[:, :w].astype(BF16).astype(F32), U32)
    lo = lax.bitcast_convert_type(x[:, w:].astype(BF16).astype(F32), U32)
    return hi | (lo >> 16)


def _unpack_halves(p):
    hi = lax.bitcast_convert_type(p & jnp.uint32(0xFFFF0000), F32)
    lo = lax.bitcast_convert_type(p << 16, F32)
    return hi, lo


def _router_kernel(x_ref, g_ref, wr_ref, hp_ref, lg_ref):
    x = x_ref[...]
    ms = jnp.mean(x * x, axis=-1, keepdims=True)
    h = x * lax.rsqrt(ms + EPS) * g_ref[...]
    hp_ref[...] = _pack_halves(h)
    lg_ref[...] = jnp.dot(h, wr_ref[...], preferred_element_type=F32, precision=lax.Precision.HIGHEST)


def _router(x, gain, w_router, *, tm=256):
    t, d = x.shape
    blk = 2 * (tm * d * 4 + d * 4 + d * ROUTER_COLS * 4 + tm * d * 2 + tm * ROUTER_COLS * 4) + 3 * tm * d * 4
    return pl.pallas_call(
        _router_kernel,
        grid=(t // tm,),
        in_specs=[pl.BlockSpec((tm, d), lambda i: (i, 0)),
                  pl.BlockSpec((1, d), lambda i: (0, 0)),
                  pl.BlockSpec((d, ROUTER_COLS), lambda i: (0, 0))],
        out_specs=[pl.BlockSpec((tm, d // 2), lambda i: (i, 0)),
                   pl.BlockSpec((tm, ROUTER_COLS), lambda i: (i, 0))],
        out_shape=[jax.ShapeDtypeStruct((t, d // 2), U32),
                   jax.ShapeDtypeStruct((t, ROUTER_COLS), F32)],
        compiler_params=_params(("arbitrary",), _vmem_limit(blk)),
        name="moe_router",
    )(x, gain.reshape(1, d), w_router)


GATHER_WINDOW = 32


def _gather_rows_kernel(idx_ref, src_ref, dst_ref, sem, *, n):
    def row_copy(src_row, dst_row):
        return pltpu.make_async_copy(src_ref.at[pl.ds(src_row, 1)], dst_ref.at[pl.ds(dst_row, 1)], sem)

    def issue(i, carry):
        row_copy(idx_ref[i], i).start()

        @pl.when(i >= GATHER_WINDOW)
        def _():
            row_copy(0, 0).wait()

        return carry

    lax.fori_loop(0, n, issue, 0)

    def drain(i, carry):
        row_copy(0, 0).wait()
        return carry

    lax.fori_loop(0, min(GATHER_WINDOW, n), drain, 0)


def _gather_rows(idx, src):
    n = idx.shape[0]
    return pl.pallas_call(
        functools.partial(_gather_rows_kernel, n=n),
        grid_spec=pltpu.PrefetchScalarGridSpec(
            num_scalar_prefetch=1,
            grid=(1,),
            in_specs=[pl.BlockSpec(memory_space=pl.ANY)],
            out_specs=pl.BlockSpec(memory_space=pl.ANY),
            scratch_shapes=[pltpu.SemaphoreType.DMA(())],
        ),
        out_shape=jax.ShapeDtypeStruct((n, src.shape[1]), src.dtype),
        compiler_params=pltpu.CompilerParams(dimension_semantics=("arbitrary",)),
        name="gather_rows",
    )(idx, src)


def _new_expert(be_ref, b):
    return (b == 0) | (be_ref[b] != be_ref[jnp.maximum(b - 1, 0)])


def _moe_up_kernel(be_ref, nu_ref, xs_ref, w1_ref, w3_ref, h_ref, w1b_ref, w3b_ref):
    b = pl.program_id(1)
    used = b < nu_ref[0]

    @pl.when(used & _new_expert(be_ref, b))
    def _():
        w1b_ref[...] = w1_ref[0].astype(BF16)
        w3b_ref[...] = w3_ref[0].astype(BF16)

    @pl.when(used)
    def _():
        hi, lo = _unpack_halves(xs_ref[...])
        hi = hi.astype(BF16)
        lo = lo.astype(BF16)
        d2 = hi.shape[1]
        a = (jnp.dot(hi, w1b_ref[:d2, :], preferred_element_type=F32)
             + jnp.dot(lo, w1b_ref[d2:, :], preferred_element_type=F32))
        c = (jnp.dot(hi, w3b_ref[:d2, :], preferred_element_type=F32)
             + jnp.dot(lo, w3b_ref[d2:, :], preferred_element_type=F32))
        h_ref[...] = (a * jax.nn.sigmoid(a) * c).astype(h_ref.dtype)

    @pl.when(jnp.logical_not(used))
    def _():
        h_ref[...] = jnp.zeros_like(h_ref)


def _moe_down_kernel(be_ref, nu_ref, h_ref, w2_ref, y_ref, w2b_ref):
    b = pl.program_id(1)
    used = b < nu_ref[0]

    @pl.when(used & _new_expert(be_ref, b))
    def _():
        w2b_ref[...] = w2_ref[0].astype(BF16)

    @pl.when(used)
    def _():
        y_ref[...] = _pack_halves(jnp.dot(h_ref[...], w2b_ref[...], preferred_element_type=F32))

    @pl.when(jnp.logical_not(used))
    def _():
        y_ref[...] = jnp.zeros_like(y_ref)


def _moe_experts(xs, block_e, n_used, w1, w3, w2, *, bm, fc=512, nc=2048):
    p, d2 = xs.shape
    d = 2 * d2
    de = w1.shape[2]
    nb = p // bm
    blk_up = 2 * (bm * d2 * 4 + 2 * d * fc * 4 + bm * fc * 2) + 2 * d * fc * 2 + 2 * bm * d * 2 + 3 * bm * fc * 4
    h = pl.pallas_call(
        _moe_up_kernel,
        grid_spec=pltpu.PrefetchScalarGridSpec(
            num_scalar_prefetch=2,
            grid=(de // fc, nb),
            in_specs=[pl.BlockSpec((bm, d2), lambda f, b, be, nu: (b, 0)),
                      pl.BlockSpec((1, d, fc), lambda f, b, be, nu: (be[b], 0, f)),
                      pl.BlockSpec((1, d, fc), lambda f, b, be, nu: (be[b], 0, f))],
            out_specs=pl.BlockSpec((bm, fc), lambda f, b, be, nu: (b, f)),
            scratch_shapes=[pltpu.VMEM((d, fc), BF16), pltpu.VMEM((d, fc), BF16)],
        ),
        out_shape=jax.ShapeDtypeStruct((p, de), BF16),
        compiler_params=_params(("arbitrary", "arbitrary"), _vmem_limit(blk_up)),
        name="moe_up",
    )(block_e, n_used, xs, w1, w3)
    blk_dn = 2 * (bm * de * 2 + de * nc * 4 + bm * nc * 2) + de * nc * 2 + 2 * bm * nc * 4
    return pl.pallas_call(
        _moe_down_kernel,
        grid_spec=pltpu.PrefetchScalarGridSpec(
            num_scalar_prefetch=2,
            grid=(d // nc, nb),
            in_specs=[pl.BlockSpec((bm, de), lambda c, b, be, nu: (b, 0)),
                      pl.BlockSpec((1, de, nc), lambda c, b, be, nu: (be[b], 0, c))],
            out_specs=pl.BlockSpec((bm, nc // 2), lambda c, b, be, nu: (b, c)),
            scratch_shapes=[pltpu.VMEM((de, nc), BF16)],
        ),
        out_shape=jax.ShapeDtypeStruct((p, d2), U32),
        compiler_params=_params(("arbitrary", "arbitrary"), _vmem_limit(blk_dn)),
        name="moe_down",
    )(block_e, n_used, h, w2)


def _combine_kernel(x_ref, y0_ref, y1_ref, ew_ref, g_ref, o_ref, *, nc):
    ew = ew_ref[...]
    w0 = ew[:, 0:1]
    w1 = ew[:, 1:2]
    half = nc // 2
    pieces = []
    ss = None
    for c in range(x_ref.shape[1] // nc):
        hi0, lo0 = _unpack_halves(y0_ref[:, c * half:(c + 1) * half])
        hi1, lo1 = _unpack_halves(y1_ref[:, c * half:(c + 1) * half])
        for j, (p0, p1) in enumerate(((hi0, hi1), (lo0, lo1))):
            cols = slice(c * nc + j * half, c * nc + (j + 1) * half)
            z = x_ref[:, cols] + w0 * p0 + w1 * p1
            pieces.append((cols, z))
            s = jnp.sum(z * z, axis=-1, keepdims=True)
            ss = s if ss is None else ss + s
    inv = lax.rsqrt(ss / x_ref.shape[1] + EPS)
    for cols, z in pieces:
        o_ref[:, cols] = z * inv * g_ref[:, cols]


def _combine(x, yu, e_w, gain, *, nc, tm=256):
    t, d = x.shape
    nt = t // tm
    blk = 2 * (2 * tm * d * 4 + 2 * tm * d * 2 + tm * LANES * 4 + d * 4) + 3 * tm * d * 4
    return pl.pallas_call(
        functools.partial(_combine_kernel, nc=nc),
        grid=(nt,),
        in_specs=[pl.BlockSpec((tm, d), lambda i: (i, 0)),
                  pl.BlockSpec((tm, d // 2), lambda i: (i, 0)),
                  pl.BlockSpec((tm, d // 2), lambda i: (i + nt, 0)),
                  pl.BlockSpec((tm, TOP_K), lambda i: (i, 0)),
                  pl.BlockSpec((1, d), lambda i: (0, 0))],
        out_specs=pl.BlockSpec((tm, d), lambda i: (i, 0)),
        out_shape=jax.ShapeDtypeStruct((t, d), F32),
        compiler_params=_params(("arbitrary",), _vmem_limit(blk)),
        name="moe_combine",
    )(x, yu, yu, e_w, gain.reshape(1, d))


def _route(logits, *, bm):
    t = logits.shape[0]
    g_logits = logits[:, :N_GROUPS]
    e_logits = logits[:, N_GROUPS:N_GROUPS + N_EXPERTS].reshape(t, N_GROUPS, EXPERTS_PER_GROUP)
    g_idx = jnp.argmax(g_logits, axis=-1).astype(jnp.int32)
    g_w = jnp.take_along_axis(jax.nn.softmax(g_logits, axis=-1), g_idx[:, None], axis=-1)
    e_sel = jnp.take_along_axis(e_logits, g_idx[:, None, None], axis=1)[:, 0]
    top_v, top_i = lax.top_k(e_sel, TOP_K)
    e_w = jax.nn.softmax(top_v, axis=-1) * g_w
    eid = g_idx[:, None] * EXPERTS_PER_GROUP + top_i.astype(jnp.int32)

    a = t * TOP_K
    flat_e = eid.reshape(a)
    onehot = (flat_e[:, None] == jnp.arange(N_EXPERTS, dtype=jnp.int32)[None, :]).astype(jnp.int32)
    csum = jnp.cumsum(onehot, axis=0)
    rank = jnp.take_along_axis(csum, flat_e[:, None], axis=1)[:, 0] - 1
    counts = csum[-1]
    padded = ((counts + bm - 1) // bm) * bm
    pad_end = jnp.cumsum(padded)
    start_pad = pad_end - padded
    dest = (start_pad[flat_e] + rank).astype(jnp.int32)
    p = a + N_EXPERTS * bm
    nb = p // bm
    n_used = (pad_end[-1] // bm).astype(jnp.int32)
    blk = jnp.arange(nb, dtype=jnp.int32)
    block_e = jnp.searchsorted(pad_end, jnp.minimum(blk, n_used - 1) * bm, side='right').astype(jnp.int32)
    block_e = jnp.minimum(block_e, N_EXPERTS - 1)
    src_tok = jnp.zeros((p,), jnp.int32).at[dest].set(jnp.arange(a, dtype=jnp.int32) // TOP_K)
    back = dest.reshape(t, TOP_K).T.reshape(a)
    return e_w, src_tok, back, block_e, n_used.reshape(1)


MOE_BLOCK_ROWS = 256
MOE_DOWN_COLS = 2048


def kernel(x, mem, norm_mix, w_in, pool_w, pool_scale, lb_fwd, lb_bwd, hgrn_norm, w_out, norm_xattn, norm_mem,
           w_q, w_k, w_v, w_o, norm_moe, w_router_group, w_router_expert, w1, w3, w2, norm_final):
    bsz, seq, d = x.shape
    assert bsz == 1 and w_in.shape[0] == 1, "one sequence, one layer (the final norm is fused into the MoE combine)"
    l = 0
    pool_width = pool_w.shape[1] * pool_w.shape[2]
    hgrn_width = hgrn_norm.shape[1]
    xt = x.reshape(seq, d)
    mem_t = mem.reshape(mem.shape[1], d)
    h = _rmsnorm(xt, norm_mix[l], BF16)
    proj = _matmul([h], w_in[l], out_dtype=F32)
    y_pool = _pool_mixer(proj, pool_w[l], pool_scale[l])
    y_hgrn = _hgrn_mixer(proj, lb_fwd, lb_bwd, hgrn_norm[l], layer=l, width=hgrn_width, col0=pool_width)
    xt = _matmul([y_pool, y_hgrn], w_out[l], resid=xt, out_dtype=F32)
    mem_n = _rmsnorm(mem_t, norm_mem[l], BF16)
    hx = _rmsnorm(xt, norm_xattn[l], BF16)
    q = _matmul([hx], w_q[l], out_dtype=BF16)
    k = _matmul([mem_n], w_k[l], out_dtype=BF16)
    v = _matmul([mem_n], w_v[l], out_dtype=BF16)
    o = _xattn(q, k, v)
    xt = _matmul([o], w_o[l], resid=xt, out_dtype=F32)
    w_router = jnp.concatenate(
        [w_router_group[l],
         jnp.transpose(w_router_expert[l], (1, 0, 2)).reshape(d, N_EXPERTS),
         jnp.zeros((d, ROUTER_COLS - N_GROUPS - N_EXPERTS), F32)], axis=1)
    hp, logits = _router(xt, norm_moe[l], w_router)
    e_w, src_tok, back, block_e, n_used = _route(logits, bm=MOE_BLOCK_ROWS)
    xs = _gather_rows(src_tok, hp)
    ys = _moe_experts(xs, block_e, n_used, w1[l], w3[l], w2[l], bm=MOE_BLOCK_ROWS, nc=MOE_DOWN_COLS)
    yu = _gather_rows(back, ys)
    out = _combine(xt, yu, e_w, norm_final, nc=MOE_DOWN_COLS)
    return out.reshape(bsz, seq, d)
```

```python
import functools

import jax
import jax.numpy as jnp
from jax import lax
from jax.experimental import pallas as pl
from jax.experimental.pallas import tpu as pltpu

F32 = jnp.float32
BF16 = jnp.bfloat16
U32 = jnp.uint32

EPS = 1e-6
LANES = 128
SUBLANES = 8
VMEM_BYTES_V7X = 64 * 1024 * 1024

POOL_WINDOWS = (2, 4, 8, 16)
HEAD = 128
CHUNK = 64
XATTN_HEADS = 4
N_GROUPS = 4
EXPERTS_PER_GROUP = 8
N_EXPERTS = N_GROUPS * EXPERTS_PER_GROUP
TOP_K = 2
ROUTER_COLS = LANES


def _params(sem, vmem_bytes):
    return pltpu.CompilerParams(dimension_semantics=sem, vmem_limit_bytes=int(vmem_bytes))


def _vmem_limit(block_bytes):
    return min(int(block_bytes * 1.25) + (6 << 20), VMEM_BYTES_V7X - (4 << 20))


def _rmsnorm_kernel(x_ref, g_ref, o_ref):
    x = x_ref[...]
    ms = jnp.mean(x * x, axis=-1, keepdims=True)
    o_ref[...] = (x * lax.rsqrt(ms + EPS) * g_ref[...]).astype(o_ref.dtype)


def _rmsnorm(x, gain, out_dtype, tm=256):
    t, d = x.shape
    blk = tm * d * (4 + jnp.dtype(out_dtype).itemsize) * 2
    return pl.pallas_call(
        _rmsnorm_kernel,
        grid=(t // tm,),
        in_specs=[pl.BlockSpec((tm, d), lambda i: (i, 0)),
                  pl.BlockSpec((1, d), lambda i: (0, 0))],
        out_specs=pl.BlockSpec((tm, d), lambda i: (i, 0)),
        out_shape=jax.ShapeDtypeStruct((t, d), out_dtype),
        compiler_params=_params(("arbitrary",), _vmem_limit(blk)),
        name="rmsnorm",
    )(x, gain.reshape(1, d))


def _mm_kernel(*refs, n_a, has_resid):
    a_refs = refs[:n_a]
    w_ref = refs[n_a]
    r_ref = refs[n_a + 1] if has_resid else None
    o_ref, wb_ref = refs[-2], refs[-1]

    @pl.when(pl.program_id(1) == 0)
    def _():
        wb_ref[...] = w_ref[...].astype(BF16)

    acc = None
    k0 = 0
    for a_ref in a_refs:
        kk = a_ref.shape[1]
        part = jnp.dot(a_ref[...], wb_ref[k0:k0 + kk, :], preferred_element_type=F32)
        acc = part if acc is None else acc + part
        k0 += kk
    if has_resid:
        acc = acc + r_ref[...]
    o_ref[...] = acc.astype(o_ref.dtype)


def _matmul(a_parts, w, *, resid=None, out_dtype=F32, tm=512, tn=512):
    m = a_parts[0].shape[0]
    k, n = w.shape
    assert sum(a.shape[1] for a in a_parts) == k
    tm = min(tm, m)
    in_specs = [pl.BlockSpec((tm, a.shape[1]), lambda j, i: (i, 0)) for a in a_parts]
    in_specs.append(pl.BlockSpec((k, tn), lambda j, i: (0, j)))
    args = list(a_parts) + [w]
    if resid is not None:
        in_specs.append(pl.BlockSpec((tm, tn), lambda j, i: (i, j)))
        args.append(resid)
    osz = jnp.dtype(out_dtype).itemsize
    blk = (2 * tm * k * 2 + 2 * k * tn * 4 + k * tn * 2 + 2 * tm * tn * osz
           + (2 * tm * tn * 4 if resid is not None else 0) + tm * tn * 4)
    return pl.pallas_call(
        functools.partial(_mm_kernel, n_a=len(a_parts), has_resid=resid is not None),
        grid=(n // tn, m // tm),
        in_specs=in_specs,
        out_specs=pl.BlockSpec((tm, tn), lambda j, i: (i, j)),
        out_shape=jax.ShapeDtypeStruct((m, n), out_dtype),
        scratch_shapes=[pltpu.VMEM((k, tn), BF16)],
        compiler_params=_params(("arbitrary", "arbitrary"), _vmem_limit(blk)),
        name="matmul",
    )(*args)


def _pool_kernel(u_ref, up_ref, un_ref, w_ref, s_ref, o_ref, *, tm, seq, group):
    i = pl.program_id(0)
    last = pl.num_programs(0) - 1
    n = tm + 2 * SUBLANES
    row = lax.broadcasted_iota(jnp.int32, (tm, 1), 0) + i * tm
    for gi, w in enumerate(POOL_WINDOWS):
        cs = slice(gi * group, (gi + 1) * group)
        u = u_ref[:, cs]
        prev = jnp.where(i > 0, up_ref[:, cs], 0.0)
        nxt = jnp.where(i < last, un_ref[:, cs], 0.0)
        f = jnp.concatenate([prev, u, nxt], axis=0)
        step = 1
        while step < w:
            f = f + pltpu.roll(f, n - step, 0)
            step *= 2
        half = w // 2
        first = SUBLANES - half
        if first:
            f = pltpu.roll(f, n - first, 0)
        win = f[:tm]
        lo = jnp.maximum(row - half, 0)
        hi = jnp.minimum(row + half - 1, seq - 1)
        cnt = (hi - lo + 1).astype(F32)
        mixed = win / cnt - u
        y = jnp.dot(mixed.astype(BF16), w_ref[gi].astype(BF16), preferred_element_type=F32)
        o_ref[:, cs] = (y * s_ref[:, cs]).astype(o_ref.dtype)


def _pool_mixer(proj, pool_w, pool_scale, *, tm=512):
    t = proj.shape[0]
    ng, group, _ = pool_w.shape
    width = ng * group
    rpb = tm // SUBLANES
    nhalo = t // SUBLANES
    blk = 2 * (tm * width * 4 + 2 * SUBLANES * width * 4 + ng * group * group * 4 + tm * width * 2) + 6 * tm * group * 4
    return pl.pallas_call(
        functools.partial(_pool_kernel, tm=tm, seq=t, group=group),
        grid=(t // tm,),
        in_specs=[pl.BlockSpec((tm, width), lambda i: (i, 0)),
                  pl.BlockSpec((SUBLANES, width), lambda i: (jnp.maximum(i * rpb - 1, 0), 0)),
                  pl.BlockSpec((SUBLANES, width), lambda i: (jnp.minimum((i + 1) * rpb, nhalo - 1), 0)),
                  pl.BlockSpec((ng, group, group), lambda i: (0, 0, 0)),
                  pl.BlockSpec((1, width), lambda i: (0, 0))],
        out_specs=pl.BlockSpec((tm, width), lambda i: (i, 0)),
        out_shape=jax.ShapeDtypeStruct((t, width), BF16),
        compiler_params=_params(("arbitrary",), _vmem_limit(blk)),
        name="pool_mixer",
    )(proj, proj, proj, pool_w, pool_scale.reshape(1, width))


def _chunk_cumsum(x, reverse):
    n = x.shape[0]
    pos = lax.broadcasted_iota(jnp.int32, x.shape, 0) % CHUNK
    s = 1
    while s < CHUNK:
        if reverse:
            x = x + jnp.where(pos < CHUNK - s, pltpu.roll(x, n - s, 0), 0.0)
        else:
            x = x + jnp.where(pos >= s, pltpu.roll(x, s, 0), 0.0)
        s *= 2
    return x


def _lower_bound(lb_raw, layer):
    e = jnp.exp(lb_raw - jnp.max(lb_raw, axis=0, keepdims=True))
    return jnp.sum(e[:layer + 1], axis=0, keepdims=True) / jnp.sum(e, axis=0, keepdims=True)


def _hgrn_direction(q_raw, f_raw, v, lb, st_ref, h, mask, *, reverse):
    tb = q_raw.shape[0]
    nc = tb // CHUNK
    q = q_raw * jax.nn.sigmoid(q_raw)
    f = lb + (1.0 - lb) * jax.nn.sigmoid(f_raw)
    logf = jnp.log(f)
    k = 1.0 - f
    b = _chunk_cumsum(logf, reverse)
    b3 = b.reshape(nc, CHUNK, HEAD)
    edge = 0 if reverse else CHUNK - 1
    b_last = b3[:, edge:edge + 1, :]
    qd = (q * jnp.exp(b)).astype(BF16)
    kd = (k * jnp.exp(-b)).astype(BF16)
    k_end = (k.reshape(nc, CHUNK, HEAD) * jnp.exp(b_last - b3)).astype(BF16)
    decay = jnp.exp(b_last)
    vb = v.astype(BF16)

    a = lax.dot_general(qd, kd, (((1,), (1,)), ((), ())), preferred_element_type=F32)
    a = jnp.where(mask, a, 0.0)
    o_intra = jnp.dot(a.astype(BF16), vb, preferred_element_type=F32)

    st = st_ref[h]
    o_inter = [None] * nc
    order = range(nc - 1, -1, -1) if reverse else range(nc)
    for c in order:
        rows = slice(c * CHUNK, (c + 1) * CHUNK)
        o_inter[c] = lax.dot_general(qd[rows], st.astype(BF16), (((1,), (1,)), ((), ())),
                                     preferred_element_type=F32)
        d_st = lax.dot_general(vb[rows], k_end[c], (((0,), (0,)), ((), ())), preferred_element_type=F32)
        st = st * decay[c] + d_st
    st_ref[h] = st
    return o_intra + jnp.concatenate(o_inter, axis=0)


def _intra_mask(tb, reverse):
    r = lax.broadcasted_iota(jnp.int32, (tb, tb), 0)
    c = lax.broadcasted_iota(jnp.int32, (tb, tb), 1)
    same = (r // CHUNK) == (c // CHUNK)
    return same & ((c >= r) if reverse else (c <= r))


def _hgrn_fwd_kernel(q_ref, f_ref, i_ref, lb_ref, o_ref, st_ref, *, layer, heads):
    @pl.when(pl.program_id(0) == 0)
    def _():
        st_ref[...] = jnp.zeros_like(st_ref)

    mask = _intra_mask(q_ref.shape[0], False)

    def head(h, carry):
        hs = pl.ds(pl.multiple_of(h * HEAD, HEAD), HEAD)
        lb = _lower_bound(lb_ref[:, hs], layer)
        o_ref[:, hs] = _hgrn_direction(q_ref[:, hs], f_ref[:, hs], i_ref[:, hs], lb, st_ref, h, mask,
                                       reverse=False)
        return carry

    lax.fori_loop(0, heads, head, 0)


def _hgrn_bwd_kernel(q_ref, f_ref, i_ref, g_ref, of_ref, lb_ref, gain_ref, y_ref, st_ref, *, layer, heads):
    @pl.when(pl.program_id(0) == 0)
    def _():
        st_ref[...] = jnp.zeros_like(st_ref)

    mask = _intra_mask(q_ref.shape[0], True)

    def head(h, carry):
        hs = pl.ds(pl.multiple_of(h * HEAD, HEAD), HEAD)
        lb = _lower_bound(lb_ref[:, hs], layer)
        o = of_ref[:, hs] + _hgrn_direction(q_ref[:, hs], f_ref[:, hs], i_ref[:, hs], lb, st_ref, h, mask,
                                            reverse=True)
        o = o * lax.rsqrt(jnp.mean(o * o, axis=-1, keepdims=True) + EPS) * gain_ref[:, hs]
        g = g_ref[:, hs]
        y_ref[:, hs] = (o * (g * jax.nn.sigmoid(g))).astype(y_ref.dtype)
        return carry

    lax.fori_loop(0, heads, head, 0)


def _hgrn_mixer(proj, lb_fwd, lb_bwd, hgrn_norm, *, layer, width, col0, tb=256):
    t = proj.shape[0]
    heads = width // HEAD
    nb = t // tb
    c = col0 // width
    layers = lb_fwd.shape[0]
    blk_f = 2 * (3 * tb * width * 4 + layers * width * 4 + tb * width * 4) + heads * HEAD * HEAD * 4
    o_f = pl.pallas_call(
        functools.partial(_hgrn_fwd_kernel, layer=layer, heads=heads),
        grid=(nb,),
        in_specs=[pl.BlockSpec((tb, width), lambda b: (b, c)),
                  pl.BlockSpec((tb, width), lambda b: (b, c + 1)),
                  pl.BlockSpec((tb, width), lambda b: (b, c + 3)),
                  pl.BlockSpec((layers, width), lambda b: (0, 0))],
        out_specs=pl.BlockSpec((tb, width), lambda b: (b, 0)),
        out_shape=jax.ShapeDtypeStruct((t, width), F32),
        scratch_shapes=[pltpu.VMEM((heads, HEAD, HEAD), F32)],
        compiler_params=_params(("arbitrary",), _vmem_limit(blk_f)),
        name="hgrn_fwd",
    )(proj, proj, proj, lb_fwd)
    blk_b = 2 * (5 * tb * width * 4 + layers * width * 4 + width * 4 + tb * width * 2) + heads * HEAD * HEAD * 4
    return pl.pallas_call(
        functools.partial(_hgrn_bwd_kernel, layer=layer, heads=heads),
        grid=(nb,),
        in_specs=[pl.BlockSpec((tb, width), lambda b: (nb - 1 - b, c)),
                  pl.BlockSpec((tb, width), lambda b: (nb - 1 - b, c + 2)),
                  pl.BlockSpec((tb, width), lambda b: (nb - 1 - b, c + 3)),
                  pl.BlockSpec((tb, width), lambda b: (nb - 1 - b, c + 4)),
                  pl.BlockSpec((tb, width), lambda b: (nb - 1 - b, 0)),
                  pl.BlockSpec((layers, width), lambda b: (0, 0)),
                  pl.BlockSpec((1, width), lambda b: (0, 0))],
        out_specs=pl.BlockSpec((tb, width), lambda b: (nb - 1 - b, 0)),
        out_shape=jax.ShapeDtypeStruct((t, width), BF16),
        scratch_shapes=[pltpu.VMEM((heads, HEAD, HEAD), F32)],
        compiler_params=_params(("arbitrary",), _vmem_limit(blk_b)),
        name="hgrn_bwd",
    )(proj, proj, proj, proj, o_f, lb_bwd, hgrn_norm.reshape(1, width))


def _xattn_kernel(q_ref, k_ref, v_ref, o_ref, *, scale):
    s = lax.dot_general(q_ref[...], k_ref[...], (((1,), (1,)), ((), ())), preferred_element_type=F32) * scale
    s = s - jnp.max(s, axis=-1, keepdims=True)
    p = jnp.exp(s)
    p = p / jnp.sum(p, axis=-1, keepdims=True)
    o_ref[...] = jnp.dot(p.astype(BF16), v_ref[...], preferred_element_type=F32).astype(o_ref.dtype)


def _xattn(q, k, v, *, tm=512):
    t, d = q.shape
    m = k.shape[0]
    dh = d // XATTN_HEADS
    blk = 2 * (2 * tm * dh * 2 + 2 * m * dh * 2) + 4 * tm * m * 4 + tm * dh * 4
    return pl.pallas_call(
        functools.partial(_xattn_kernel, scale=dh ** -0.5),
        grid=(XATTN_HEADS, t // tm),
        in_specs=[pl.BlockSpec((tm, dh), lambda h, i: (i, h)),
                  pl.BlockSpec((m, dh), lambda h, i: (0, h)),
                  pl.BlockSpec((m, dh), lambda h, i: (0, h))],
        out_specs=pl.BlockSpec((tm, dh), lambda h, i: (i, h)),
        out_shape=jax.ShapeDtypeStruct((t, d), BF16),
        compiler_params=_params(("arbitrary", "arbitrary"), _vmem_limit(blk)),
        name="xattn",
    )(q, k, v)


def _pack_halves(x):
    w = x.shape[1] // 2
    hi = lax.bitcast_convert_type(x[:, :w].astype(BF16).astype(F32), U32)
    lo = lax.bitcast_convert_type(x[:, w:].astype(BF16).astype(F32), U32)
    return hi | (lo >> 16)


def _unpack_halves(p):
    hi = lax.bitcast_convert_type(p & jnp.uint32(0xFFFF0000), F32)
    lo = lax.bitcast_convert_type(p << 16, F32)
    return hi, lo


def _router_kernel(x_ref, g_ref, wr_ref, hp_ref, lg_ref):
    x = x_ref[...]
    ms = jnp.mean(x * x, axis=-1, keepdims=True)
    h = x * lax.rsqrt(ms + EPS) * g_ref[...]
    hp_ref[:, 0, :] = _pack_halves(h)
    lg_ref[...] = jnp.dot(h, wr_ref[...], preferred_element_type=F32, precision=lax.Precision.HIGHEST)


def _router(x, gain, w_router, *, tm=256):
    t, d = x.shape
    blk = 2 * (tm * d * 4 + d * 4 + d * ROUTER_COLS * 4 + tm * d * 2 + tm * ROUTER_COLS * 4) + 3 * tm * d * 4
    return pl.pallas_call(
        _router_kernel,
        grid=(t // tm,),
        in_specs=[pl.BlockSpec((tm, d), lambda i: (i, 0)),
                  pl.BlockSpec((1, d), lambda i: (0, 0)),
                  pl.BlockSpec((d, ROUTER_COLS), lambda i: (0, 0))],
        out_specs=[pl.BlockSpec((tm, 1, d // 2), lambda i: (i, 0, 0)),
                   pl.BlockSpec((tm, ROUTER_COLS), lambda i: (i, 0))],
        out_shape=[jax.ShapeDtypeStruct((t, 1, d // 2), U32),
                   jax.ShapeDtypeStruct((t, ROUTER_COLS), F32)],
        compiler_params=_params(("arbitrary",), _vmem_limit(blk)),
        name="moe_router",
    )(x, gain.reshape(1, d), w_router)


GATHER_WINDOW = 512


def _gather_rows_kernel(idx_ref, src_ref, dst_ref, sem, *, n):
    def row_copy(src_row, dst_row):
        return pltpu.make_async_copy(src_ref.at[pl.ds(src_row, 1)], dst_ref.at[pl.ds(dst_row, 1)], sem)

    def issue(i, carry):
        row_copy(idx_ref[i], i).start()

        @pl.when(i >= GATHER_WINDOW)
        def _():
            row_copy(0, 0).wait()

        return carry

    lax.fori_loop(0, n, issue, 0)

    def drain(i, carry):
        row_copy(0, 0).wait()
        return carry

    lax.fori_loop(0, min(GATHER_WINDOW, n), drain, 0)


def _gather_rows(idx, src):
    n = idx.shape[0]
    return pl.pallas_call(
        functools.partial(_gather_rows_kernel, n=n),
        grid_spec=pltpu.PrefetchScalarGridSpec(
            num_scalar_prefetch=1,
            grid=(1,),
            in_specs=[pl.BlockSpec(memory_space=pl.ANY)],
            out_specs=pl.BlockSpec(memory_space=pl.ANY),
            scratch_shapes=[pltpu.SemaphoreType.DMA(())],
        ),
        out_shape=jax.ShapeDtypeStruct((n,) + src.shape[1:], src.dtype),
        compiler_params=pltpu.CompilerParams(dimension_semantics=("arbitrary",)),
        name="gather_rows",
    )(idx, src)


def _new_expert(be_ref, b):
    return (b == 0) | (be_ref[b] != be_ref[jnp.maximum(b - 1, 0)])


def _moe_up_kernel(be_ref, nu_ref, xs_ref, w1_ref, w3_ref, h_ref, w1b_ref, w3b_ref):
    b = pl.program_id(1)
    used = b < nu_ref[0]

    @pl.when(used & _new_expert(be_ref, b))
    def _():
        w1b_ref[...] = w1_ref[0].astype(BF16)
        w3b_ref[...] = w3_ref[0].astype(BF16)

    @pl.when(used)
    def _():
        hi, lo = _unpack_halves(xs_ref[:, 0, :])
        hi = hi.astype(BF16)
        lo = lo.astype(BF16)
        d2 = hi.shape[1]
        a = (jnp.dot(hi, w1b_ref[:d2, :], preferred_element_type=F32)
             + jnp.dot(lo, w1b_ref[d2:, :], preferred_element_type=F32))
        c = (jnp.dot(hi, w3b_ref[:d2, :], preferred_element_type=F32)
             + jnp.dot(lo, w3b_ref[d2:, :], preferred_element_type=F32))
        h_ref[...] = (a * jax.nn.sigmoid(a) * c).astype(h_ref.dtype)

    @pl.when(jnp.logical_not(used))
    def _():
        h_ref[...] = jnp.zeros_like(h_ref)


def _moe_down_kernel(be_ref, nu_ref, h_ref, w2_ref, y_ref, w2b_ref):
    b = pl.program_id(1)
    used = b < nu_ref[0]

    @pl.when(used & _new_expert(be_ref, b))
    def _():
        w2b_ref[...] = w2_ref[0].astype(BF16)

    @pl.when(used)
    def _():
        y_ref[:, 0, :] = _pack_halves(jnp.dot(h_ref[...], w2b_ref[...], preferred_element_type=F32))

    @pl.when(jnp.logical_not(used))
    def _():
        y_ref[...] = jnp.zeros_like(y_ref)


def _moe_experts(xs, block_e, n_used, w1, w3, w2, *, bm, fc=512, nc=2048):
    p, _, d2 = xs.shape
    d = 2 * d2
    de = w1.shape[2]
    nb = p // bm
    blk_up = 2 * (bm * d2 * 4 + 2 * d * fc * 4 + bm * fc * 2) + 2 * d * fc * 2 + 2 * bm * d * 2 + 3 * bm * fc * 4
    h = pl.pallas_call(
        _moe_up_kernel,
        grid_spec=pltpu.PrefetchScalarGridSpec(
            num_scalar_prefetch=2,
            grid=(de // fc, nb),
            in_specs=[pl.BlockSpec((bm, 1, d2), lambda f, b, be, nu: (b, 0, 0)),
                      pl.BlockSpec((1, d, fc), lambda f, b, be, nu: (be[b], 0, f)),
                      pl.BlockSpec((1, d, fc), lambda f, b, be, nu: (be[b], 0, f))],
            out_specs=pl.BlockSpec((bm, fc), lambda f, b, be, nu: (b, f)),
            scratch_shapes=[pltpu.VMEM((d, fc), BF16), pltpu.VMEM((d, fc), BF16)],
        ),
        out_shape=jax.ShapeDtypeStruct((p, de), BF16),
        compiler_params=_params(("arbitrary", "arbitrary"), _vmem_limit(blk_up)),
        name="moe_up",
    )(block_e, n_used, xs, w1, w3)
    blk_dn = 2 * (bm * de * 2 + de * nc * 4 + bm * nc * 2) + de * nc * 2 + 2 * bm * nc * 4
    return pl.pallas_call(
        _moe_down_kernel,
        grid_spec=pltpu.PrefetchScalarGridSpec(
            num_scalar_prefetch=2,
            grid=(d // nc, nb),
            in_specs=[pl.BlockSpec((bm, de), lambda c, b, be, nu: (b, 0)),
                      pl.BlockSpec((1, de, nc), lambda c, b, be, nu: (be[b], 0, c))],
            out_specs=pl.BlockSpec((bm, 1, nc // 2), lambda c, b, be, nu: (b, 0, c)),
            scratch_shapes=[pltpu.VMEM((de, nc), BF16)],
        ),
        out_shape=jax.ShapeDtypeStruct((p, 1, d2), U32),
        compiler_params=_params(("arbitrary", "arbitrary"), _vmem_limit(blk_dn)),
        name="moe_down",
    )(block_e, n_used, h, w2)


def _combine_kernel(x_ref, y0_ref, y1_ref, ew_ref, g_ref, o_ref, *, nc):
    ew = ew_ref[...]
    w0 = ew[:, 0:1]
    w1 = ew[:, 1:2]
    half = nc // 2
    pieces = []
    ss = None
    for c in range(x_ref.shape[1] // nc):
        hi0, lo0 = _unpack_halves(y0_ref[:, 0, c * half:(c + 1) * half])
        hi1, lo1 = _unpack_halves(y1_ref[:, 0, c * half:(c + 1) * half])
        for j, (p0, p1) in enumerate(((hi0, hi1), (lo0, lo1))):
            cols = slice(c * nc + j * half, c * nc + (j + 1) * half)
            z = x_ref[:, cols] + w0 * p0 + w1 * p1
            pieces.append((cols, z))
            s = jnp.sum(z * z, axis=-1, keepdims=True)
            ss = s if ss is None else ss + s
    inv = lax.rsqrt(ss / x_ref.shape[1] + EPS)
    for cols, z in pieces:
        o_ref[:, cols] = z * inv * g_ref[:, cols]


def _combine(x, yu, e_w, gain, *, nc, tm=256):
    t, d = x.shape
    nt = t // tm
    blk = 2 * (2 * tm * d * 4 + 2 * tm * d * 2 + tm * LANES * 4 + d * 4) + 3 * tm * d * 4
    return pl.pallas_call(
        functools.partial(_combine_kernel, nc=nc),
        grid=(nt,),
        in_specs=[pl.BlockSpec((tm, d), lambda i: (i, 0)),
                  pl.BlockSpec((tm, 1, d // 2), lambda i: (i, 0, 0)),
                  pl.BlockSpec((tm, 1, d // 2), lambda i: (i + nt, 0, 0)),
                  pl.BlockSpec((tm, TOP_K), lambda i: (i, 0)),
                  pl.BlockSpec((1, d), lambda i: (0, 0))],
        out_specs=pl.BlockSpec((tm, d), lambda i: (i, 0)),
        out_shape=jax.ShapeDtypeStruct((t, d), F32),
        compiler_params=_params(("arbitrary",), _vmem_limit(blk)),
        name="moe_combine",
    )(x, yu, yu, e_w, gain.reshape(1, d))


def _route(logits, *, bm):
    t = logits.shape[0]
    g_logits = logits[:, :N_GROUPS]
    e_logits = logits[:, N_GROUPS:N_GROUPS + N_EXPERTS].reshape(t, N_GROUPS, EXPERTS_PER_GROUP)
    g_idx = jnp.argmax(g_logits, axis=-1).astype(jnp.int32)
    g_w = jnp.take_along_axis(jax.nn.softmax(g_logits, axis=-1), g_idx[:, None], axis=-1)
    e_sel = jnp.take_along_axis(e_logits, g_idx[:, None, None], axis=1)[:, 0]
    top_v, top_i = lax.top_k(e_sel, TOP_K)
    e_w = jax.nn.softmax(top_v, axis=-1) * g_w
    eid = g_idx[:, None] * EXPERTS_PER_GROUP + top_i.astype(jnp.int32)

    a = t * TOP_K
    flat_e = eid.reshape(a)
    onehot = (flat_e[:, None] == jnp.arange(N_EXPERTS, dtype=jnp.int32)[None, :]).astype(jnp.int32)
    csum = jnp.cumsum(onehot, axis=0)
    rank = jnp.take_along_axis(csum, flat_e[:, None], axis=1)[:, 0] - 1
    counts = csum[-1]
    padded = ((counts + bm - 1) // bm) * bm
    pad_end = jnp.cumsum(padded)
    start_pad = pad_end - padded
    dest = (start_pad[flat_e] + rank).astype(jnp.int32)
    p = a + N_EXPERTS * bm
    nb = p // bm
    n_used = (pad_end[-1] // bm).astype(jnp.int32)
    blk = jnp.arange(nb, dtype=jnp.int32)
    block_e = jnp.searchsorted(pad_end, jnp.minimum(blk, n_used - 1) * bm, side='right').astype(jnp.int32)
    block_e = jnp.minimum(block_e, N_EXPERTS - 1)
    src_tok = jnp.zeros((p,), jnp.int32).at[dest].set(jnp.arange(a, dtype=jnp.int32) // TOP_K)
    back = dest.reshape(t, TOP_K).T.reshape(a)
    return e_w, src_tok, back, block_e, n_used.reshape(1)


MOE_BLOCK_ROWS = 256
MOE_DOWN_COLS = 2048


def kernel(x, mem, norm_mix, w_in, pool_w, pool_scale, lb_fwd, lb_bwd, hgrn_norm, w_out, norm_xattn, norm_mem,
           w_q, w_k, w_v, w_o, norm_moe, w_router_group, w_router_expert, w1, w3, w2, norm_final):
    bsz, seq, d = x.shape
    assert bsz == 1 and w_in.shape[0] == 1, "one sequence, one layer (the final norm is fused into the MoE combine)"
    l = 0
    pool_width = pool_w.shape[1] * pool_w.shape[2]
    hgrn_width = hgrn_norm.shape[1]
    xt = x.reshape(seq, d)
    mem_t = mem.reshape(mem.shape[1], d)
    h = _rmsnorm(xt, norm_mix[l], BF16)
    proj = _matmul([h], w_in[l], out_dtype=F32)
    y_pool = _pool_mixer(proj, pool_w[l], pool_scale[l])
    y_hgrn = _hgrn_mixer(proj, lb_fwd, lb_bwd, hgrn_norm[l], layer=l, width=hgrn_width, col0=pool_width)
    xt = _matmul([y_pool, y_hgrn], w_out[l], resid=xt, out_dtype=F32)
    mem_n = _rmsnorm(mem_t, norm_mem[l], BF16)
    hx = _rmsnorm(xt, norm_xattn[l], BF16)
    q = _matmul([hx], w_q[l], out_dtype=BF16)
    k = _matmul([mem_n], w_k[l], out_dtype=BF16)
    v = _matmul([mem_n], w_v[l], out_dtype=BF16)
    o = _xattn(q, k, v)
    xt = _matmul([o], w_o[l], resid=xt, out_dtype=F32)
    w_router = jnp.concatenate(
        [w_router_group[l],
         jnp.transpose(w_router_expert[l], (1, 0, 2)).reshape(d, N_EXPERTS),
         jnp.zeros((d, ROUTER_COLS - N_GROUPS - N_EXPERTS), F32)], axis=1)
    hp, logits = _router(xt, norm_moe[l], w_router)
    e_w, src_tok, back, block_e, n_used = _route(logits, bm=MOE_BLOCK_ROWS)
    xs = _gather_rows(src_tok, hp)
    ys = _moe_experts(xs, block_e, n_used, w1[l], w3[l], w2[l], bm=MOE_BLOCK_ROWS, nc=MOE_DOWN_COLS)
    yu = _gather_rows(back, ys)
    out = _combine(xt, yu, e_w, norm_final, nc=MOE_DOWN_COLS)
    return out.reshape(bsz, seq, d)
```

```python
import functools

import jax
import jax.numpy as jnp
from jax import lax
from jax.experimental import pallas as pl
from jax.experimental.pallas import tpu as pltpu

F32 = jnp.float32
BF16 = jnp.bfloat16
U32 = jnp.uint32

EPS = 1e-6
LANES = 128
SUBLANES = 8
VMEM_BYTES_V7X = 64 * 1024 * 1024

POOL_WINDOWS = (2, 4, 8, 16)
HEAD = 128
CHUNK = 64
XATTN_HEADS = 4
N_GROUPS = 4
EXPERTS_PER_GROUP = 8
N_EXPERTS = N_GROUPS * EXPERTS_PER_GROUP
TOP_K = 2
ROUTER_COLS = LANES


def _params(sem, vmem_bytes):
    return pltpu.CompilerParams(dimension_semantics=sem, vmem_limit_bytes=int(vmem_bytes))


def _vmem_limit(block_bytes):
    return min(int(block_bytes * 1.25) + (6 << 20), VMEM_BYTES_V7X - (4 << 20))


def _rmsnorm_kernel(x_ref, g_ref, o_ref):
    x = x_ref[...]
    ms = jnp.mean(x * x, axis=-1, keepdims=True)
    o_ref[...] = (x * lax.rsqrt(ms + EPS) * g_ref[...]).astype(o_ref.dtype)


def _rmsnorm(x, gain, out_dtype, tm=256):
    t, d = x.shape
    blk = tm * d * (4 + jnp.dtype(out_dtype).itemsize) * 2
    return pl.pallas_call(
        _rmsnorm_kernel,
        grid=(t // tm,),
        in_specs=[pl.BlockSpec((tm, d), lambda i: (i, 0)),
                  pl.BlockSpec((1, d), lambda i: (0, 0))],
        out_specs=pl.BlockSpec((tm, d), lambda i: (i, 0)),
        out_shape=jax.ShapeDtypeStruct((t, d), out_dtype),
        compiler_params=_params(("arbitrary",), _vmem_limit(blk)),
        name="rmsnorm",
    )(x, gain.reshape(1, d))


def _mm_kernel(*refs, n_a, has_resid):
    a_refs = refs[:n_a]
    w_ref = refs[n_a]
    r_ref = refs[n_a + 1] if has_resid else None
    o_ref, wb_ref = refs[-2], refs[-1]

    @pl.when(pl.program_id(1) == 0)
    def _():
        wb_ref[...] = w_ref[...].astype(BF16)

    acc = None
    k0 = 0
    for a_ref in a_refs:
        kk = a_ref.shape[1]
        part = jnp.dot(a_ref[...], wb_ref[k0:k0 + kk, :], preferred_element_type=F32)
        acc = part if acc is None else acc + part
        k0 += kk
    if has_resid:
        acc = acc + r_ref[...]
    o_ref[...] = acc.astype(o_ref.dtype)


def _matmul(a_parts, w, *, resid=None, out_dtype=F32, tm=512, tn=512):
    m = a_parts[0].shape[0]
    k, n = w.shape
    assert sum(a.shape[1] for a in a_parts) == k
    tm = min(tm, m)
    in_specs = [pl.BlockSpec((tm, a.shape[1]), lambda j, i: (i, 0)) for a in a_parts]
    in_specs.append(pl.BlockSpec((k, tn), lambda j, i: (0, j)))
    args = list(a_parts) + [w]
    if resid is not None:
        in_specs.append(pl.BlockSpec((tm, tn), lambda j, i: (i, j)))
        args.append(resid)
    osz = jnp.dtype(out_dtype).itemsize
    blk = (2 * tm * k * 2 + 2 * k * tn * 4 + k * tn * 2 + 2 * tm * tn * osz
           + (2 * tm * tn * 4 if resid is not None else 0) + tm * tn * 4)
    return pl.pallas_call(
        functools.partial(_mm_kernel, n_a=len(a_parts), has_resid=resid is not None),
        grid=(n // tn, m // tm),
        in_specs=in_specs,
        out_specs=pl.BlockSpec((tm, tn), lambda j, i: (i, j)),
        out_shape=jax.ShapeDtypeStruct((m, n), out_dtype),
        scratch_shapes=[pltpu.VMEM((k, tn), BF16)],
        compiler_params=_params(("arbitrary", "arbitrary"), _vmem_limit(blk)),
        name="matmul",
    )(*args)


def _pool_kernel(u_ref, up_ref, un_ref, w_ref, s_ref, o_ref, *, tm, seq, group):
    i = pl.program_id(0)
    last = pl.num_programs(0) - 1
    n = tm + 2 * SUBLANES
    row = lax.broadcasted_iota(jnp.int32, (tm, 1), 0) + i * tm
    for gi, w in enumerate(POOL_WINDOWS):
        cs = slice(gi * group, (gi + 1) * group)
        u = u_ref[:, cs]
        prev = jnp.where(i > 0, up_ref[:, cs], 0.0)
        nxt = jnp.where(i < last, un_ref[:, cs], 0.0)
        f = jnp.concatenate([prev, u, nxt], axis=0)
        step = 1
        while step < w:
            f = f + pltpu.roll(f, n - step, 0)
            step *= 2
        half = w // 2
        first = SUBLANES - half
        if first:
            f = pltpu.roll(f, n - first, 0)
        win = f[:tm]
        lo = jnp.maximum(row - half, 0)
        hi = jnp.minimum(row + half - 1, seq - 1)
        cnt = (hi - lo + 1).astype(F32)
        mixed = win / cnt - u
        y = jnp.dot(mixed.astype(BF16), w_ref[gi].astype(BF16), preferred_element_type=F32)
        o_ref[:, cs] = (y * s_ref[:, cs]).astype(o_ref.dtype)


def _pool_mixer(proj, pool_w, pool_scale, *, tm=512):
    t = proj.shape[0]
    ng, group, _ = pool_w.shape
    width = ng * group
    rpb = tm // SUBLANES
    nhalo = t // SUBLANES
    blk = 2 * (tm * width * 4 + 2 * SUBLANES * width * 4 + ng * group * group * 4 + tm * width * 2) + 6 * tm * group * 4
    return pl.pallas_call(
        functools.partial(_pool_kernel, tm=tm, seq=t, group=group),
        grid=(t // tm,),
        in_specs=[pl.BlockSpec((tm, width), lambda i: (i, 0)),
                  pl.BlockSpec((SUBLANES, width), lambda i: (jnp.maximum(i * rpb - 1, 0), 0)),
                  pl.BlockSpec((SUBLANES, width), lambda i: (jnp.minimum((i + 1) * rpb, nhalo - 1), 0)),
                  pl.BlockSpec((ng, group, group), lambda i: (0, 0, 0)),
                  pl.BlockSpec((1, width), lambda i: (0, 0))],
        out_specs=pl.BlockSpec((tm, width), lambda i: (i, 0)),
        out_shape=jax.ShapeDtypeStruct((t, width), BF16),
        compiler_params=_params(("arbitrary",), _vmem_limit(blk)),
        name="pool_mixer",
    )(proj, proj, proj, pool_w, pool_scale.reshape(1, width))


def _chunk_cumsum(x, reverse):
    n = x.shape[0]
    pos = lax.broadcasted_iota(jnp.int32, x.shape, 0) % CHUNK
    s = 1
    while s < CHUNK:
        if reverse:
            x = x + jnp.where(pos < CHUNK - s, pltpu.roll(x, n - s, 0), 0.0)
        else:
            x = x + jnp.where(pos >= s, pltpu.roll(x, s, 0), 0.0)
        s *= 2
    return x


def _lower_bound(lb_raw, layer):
    e = jnp.exp(lb_raw - jnp.max(lb_raw, axis=0, keepdims=True))
    return jnp.sum(e[:layer + 1], axis=0, keepdims=True) / jnp.sum(e, axis=0, keepdims=True)


def _hgrn_direction(q_raw, f_raw, v, lb, st_ref, h, mask, *, reverse):
    tb = q_raw.shape[0]
    nc = tb // CHUNK
    q = q_raw * jax.nn.sigmoid(q_raw)
    f = lb + (1.0 - lb) * jax.nn.sigmoid(f_raw)
    logf = jnp.log(f)
    k = 1.0 - f
    b = _chunk_cumsum(logf, reverse)
    b3 = b.reshape(nc, CHUNK, HEAD)
    edge = 0 if reverse else CHUNK - 1
    b_last = b3[:, edge:edge + 1, :]
    qd = (q * jnp.exp(b)).astype(BF16)
    kd = (k * jnp.exp(-b)).astype(BF16)
    k_end = (k.reshape(nc, CHUNK, HEAD) * jnp.exp(b_last - b3)).astype(BF16)
    decay = jnp.exp(b_last)
    vb = v.astype(BF16)

    a = lax.dot_general(qd, kd, (((1,), (1,)), ((), ())), preferred_element_type=F32)
    a = jnp.where(mask, a, 0.0)
    o_intra = jnp.dot(a.astype(BF16), vb, preferred_element_type=F32)

    st = st_ref[h]
    o_inter = [None] * nc
    order = range(nc - 1, -1, -1) if reverse else range(nc)
    for c in order:
        rows = slice(c * CHUNK, (c + 1) * CHUNK)
        o_inter[c] = lax.dot_general(qd[rows], st.astype(BF16), (((1,), (1,)), ((), ())),
                                     preferred_element_type=F32)
        d_st = lax.dot_general(vb[rows], k_end[c], (((0,), (0,)), ((), ())), preferred_element_type=F32)
        st = st * decay[c] + d_st
    st_ref[h] = st
    return o_intra + jnp.concatenate(o_inter, axis=0)


def _intra_mask(tb, reverse):
    r = lax.broadcasted_iota(jnp.int32, (tb, tb), 0)
    c = lax.broadcasted_iota(jnp.int32, (tb, tb), 1)
    same = (r // CHUNK) == (c // CHUNK)
    return same & ((c >= r) if reverse else (c <= r))


def _hgrn_fwd_kernel(q_ref, f_ref, i_ref, lb_ref, o_ref, st_ref, *, layer, heads):
    @pl.when(pl.program_id(0) == 0)
    def _():
        st_ref[...] = jnp.zeros_like(st_ref)

    mask = _intra_mask(q_ref.shape[0], False)

    def head(h, carry):
        hs = pl.ds(pl.multiple_of(h * HEAD, HEAD), HEAD)
        lb = _lower_bound(lb_ref[:, hs], layer)
        o_ref[:, hs] = _hgrn_direction(q_ref[:, hs], f_ref[:, hs], i_ref[:, hs], lb, st_ref, h, mask,
                                       reverse=False)
        return carry

    lax.fori_loop(0, heads, head, 0)


def _hgrn_bwd_kernel(q_ref, f_ref, i_ref, g_ref, of_ref, lb_ref, gain_ref, y_ref, st_ref, *, layer, heads):
    @pl.when(pl.program_id(0) == 0)
    def _():
        st_ref[...] = jnp.zeros_like(st_ref)

    mask = _intra_mask(q_ref.shape[0], True)

    def head(h, carry):
        hs = pl.ds(pl.multiple_of(h * HEAD, HEAD), HEAD)
        lb = _lower_bound(lb_ref[:, hs], layer)
        o = of_ref[:, hs] + _hgrn_direction(q_ref[:, hs], f_ref[:, hs], i_ref[:, hs], lb, st_ref, h, mask,
                                            reverse=True)
        o = o * lax.rsqrt(jnp.mean(o * o, axis=-1, keepdims=True) + EPS) * gain_ref[:, hs]
        g = g_ref[:, hs]
        y_ref[:, hs] = (o * (g * jax.nn.sigmoid(g))).astype(y_ref.dtype)
        return carry

    lax.fori_loop(0, heads, head, 0)


def _hgrn_mixer(proj, lb_fwd, lb_bwd, hgrn_norm, *, layer, width, col0, tb=256):
    t = proj.shape[0]
    heads = width // HEAD
    nb = t // tb
    c = col0 // width
    layers = lb_fwd.shape[0]
    blk_f = 2 * (3 * tb * width * 4 + layers * width * 4 + tb * width * 4) + heads * HEAD * HEAD * 4
    o_f = pl.pallas_call(
        functools.partial(_hgrn_fwd_kernel, layer=layer, heads=heads),
        grid=(nb,),
        in_specs=[pl.BlockSpec((tb, width), lambda b: (b, c)),
                  pl.BlockSpec((tb, width), lambda b: (b, c + 1)),
                  pl.BlockSpec((tb, width), lambda b: (b, c + 3)),
                  pl.BlockSpec((layers, width), lambda b: (0, 0))],
        out_specs=pl.BlockSpec((tb, width), lambda b: (b, 0)),
        out_shape=jax.ShapeDtypeStruct((t, width), F32),
        scratch_shapes=[pltpu.VMEM((heads, HEAD, HEAD), F32)],
        compiler_params=_params(("arbitrary",), _vmem_limit(blk_f)),
        name="hgrn_fwd",
    )(proj, proj, proj, lb_fwd)
    blk_b = 2 * (5 * tb * width * 4 + layers * width * 4 + width * 4 + tb * width * 2) + heads * HEAD * HEAD * 4
    return pl.pallas_call(
        functools.partial(_hgrn_bwd_kernel, layer=layer, heads=heads),
        grid=(nb,),
        in_specs=[pl.BlockSpec((tb, width), lambda b: (nb - 1 - b, c)),
                  pl.BlockSpec((tb, width), lambda b: (nb - 1 - b, c + 2)),
                  pl.BlockSpec((tb, width), lambda b: (nb - 1 - b, c + 3)),
                  pl.BlockSpec((tb, width), lambda b: (nb - 1 - b, c + 4)),
                  pl.BlockSpec((tb, width), lambda b: (nb - 1 - b, 0)),
                  pl.BlockSpec((layers, width), lambda b: (0, 0)),
                  pl.BlockSpec((1, width), lambda b: (0, 0))],
        out_specs=pl.BlockSpec((tb, width), lambda b: (nb - 1 - b, 0)),
        out_shape=jax.ShapeDtypeStruct((t, width), BF16),
        scratch_shapes=[pltpu.VMEM((heads, HEAD, HEAD), F32)],
        compiler_params=_params(("arbitrary",), _vmem_limit(blk_b)),
        name="hgrn_bwd",
    )(proj, proj, proj, proj, o_f, lb_bwd, hgrn_norm.reshape(1, width))


def _xattn_kernel(q_ref, k_ref, v_ref, o_ref, *, scale):
    s = lax.dot_general(q_ref[...], k_ref[...], (((1,), (1,)), ((), ())), preferred_element_type=F32) * scale
    s = s - jnp.max(s, axis=-1, keepdims=True)
    p = jnp.exp(s)
    p = p / jnp.sum(p, axis=-1, keepdims=True)
    o_ref[...] = jnp.dot(p.astype(BF16), v_ref[...], preferred_element_type=F32).astype(o_ref.dtype)


def _xattn(q, k, v, *, tm=512):
    t, d = q.shape
    m = k.shape[0]
    dh = d // XATTN_HEADS
    blk = 2 * (2 * tm * dh * 2 + 2 * m * dh * 2) + 4 * tm * m * 4 + tm * dh * 4
    return pl.pallas_call(
        functools.partial(_xattn_kernel, scale=dh ** -0.5),
        grid=(XATTN_HEADS, t // tm),
        in_specs=[pl.BlockSpec((tm, dh), lambda h, i: (i, h)),
                  pl.BlockSpec((m, dh), lambda h, i: (0, h)),
                  pl.BlockSpec((m, dh), lambda h, i: (0, h))],
        out_specs=pl.BlockSpec((tm, dh), lambda h, i: (i, h)),
        out_shape=jax.ShapeDtypeStruct((t, d), BF16),
        compiler_params=_params(("arbitrary", "arbitrary"), _vmem_limit(blk)),
        name="xattn",
    )(q, k, v)


def _pack_halves(x):
    w = x.shape[1] // 2
    hi = lax.bitcast_convert_type(x[:, :w].astype(BF16).astype(F32), U32)
    lo = lax.bitcast_convert_type(x[:, w:].astype(BF16).astype(F32), U32)
    return hi | (lo >> 16)


def _unpack_halves(p):
    hi = lax.bitcast_convert_type(p & jnp.uint32(0xFFFF0000), F32)
    lo = lax.bitcast_convert_type(p << 16, F32)
    return hi, lo


def _load_row_slabs(flat_ref, s, j0, nj):
    rows = flat_ref.shape[0] // s
    return jnp.concatenate([flat_ref[pl.ds(j0 + j, rows, stride=s), :] for j in range(nj)], axis=1)


def _store_row_slabs(ref, val):
    for j in range(ref.shape[1]):
        ref[:, j, :] = val[:, j * LANES:(j + 1) * LANES]


def _router_kernel(x_ref, g_ref, wr_ref, hp_ref, lg_ref):
    x = x_ref[...]
    ms = jnp.mean(x * x, axis=-1, keepdims=True)
    h = x * lax.rsqrt(ms + EPS) * g_ref[...]
    _store_row_slabs(hp_ref, _pack_halves(h))
    lg_ref[...] = jnp.dot(h, wr_ref[...], preferred_element_type=F32, precision=lax.Precision.HIGHEST)


def _router(x, gain, w_router, *, tm=256):
    t, d = x.shape
    blk = 2 * (tm * d * 4 + d * 4 + d * ROUTER_COLS * 4 + tm * d * 2 + tm * ROUTER_COLS * 4) + 3 * tm * d * 4
    return pl.pallas_call(
        _router_kernel,
        grid=(t // tm,),
        in_specs=[pl.BlockSpec((tm, d), lambda i: (i, 0)),
                  pl.BlockSpec((1, d), lambda i: (0, 0)),
                  pl.BlockSpec((d, ROUTER_COLS), lambda i: (0, 0))],
        out_specs=[pl.BlockSpec((tm, d // 2 // LANES, LANES), lambda i: (i, 0, 0)),
                   pl.BlockSpec((tm, ROUTER_COLS), lambda i: (i, 0))],
        out_shape=[jax.ShapeDtypeStruct((t, d // 2 // LANES, LANES), U32),
                   jax.ShapeDtypeStruct((t, ROUTER_COLS), F32)],
        compiler_params=_params(("arbitrary",), _vmem_limit(blk)),
        name="moe_router",
    )(x, gain.reshape(1, d), w_router)


def _gather_rows_kernel(idx_ref, src_ref, o_ref, sem, *, rows):
    base = pl.program_id(0) * rows

    def row_copy(src_row, r):
        return pltpu.make_async_copy(src_ref.at[src_row], o_ref.at[r], sem)

    def issue(r, carry):
        row_copy(idx_ref[base + r], r).start()
        return carry

    lax.fori_loop(0, rows, issue, 0, unroll=8)

    def drain(r, carry):
        row_copy(0, r).wait()
        return carry

    lax.fori_loop(0, rows, drain, 0, unroll=8)


def _gather_rows(idx, src, *, rows=256):
    n = idx.shape[0]
    slab = src.shape[1:]
    blk = 2 * rows * slab[0] * slab[1] * 4
    return pl.pallas_call(
        functools.partial(_gather_rows_kernel, rows=rows),
        grid_spec=pltpu.PrefetchScalarGridSpec(
            num_scalar_prefetch=1,
            grid=(n // rows,),
            in_specs=[pl.BlockSpec(memory_space=pl.ANY)],
            out_specs=pl.BlockSpec((rows,) + slab, lambda b, idx: (b, 0, 0)),
            scratch_shapes=[pltpu.SemaphoreType.DMA(())],
        ),
        out_shape=jax.ShapeDtypeStruct((n,) + slab, src.dtype),
        compiler_params=_params(("arbitrary",), _vmem_limit(blk)),
        name="gather_rows",
    )(idx, src)


def _new_expert(be_ref, b):
    return (b == 0) | (be_ref[b] != be_ref[jnp.maximum(b - 1, 0)])


def _moe_up_kernel(be_ref, nu_ref, xs_ref, w1_ref, w3_ref, h_ref, w1b_ref, w3b_ref):
    b = pl.program_id(1)
    used = b < nu_ref[0]

    @pl.when(used & _new_expert(be_ref, b))
    def _():
        w1b_ref[...] = w1_ref[0].astype(BF16)
        w3b_ref[...] = w3_ref[0].astype(BF16)

    @pl.when(used)
    def _():
        nslab = w1b_ref.shape[0] // 2 // LANES
        hi, lo = _unpack_halves(_load_row_slabs(xs_ref, nslab, 0, nslab))
        hi = hi.astype(BF16)
        lo = lo.astype(BF16)
        d2 = hi.shape[1]
        a = (jnp.dot(hi, w1b_ref[:d2, :], preferred_element_type=F32)
             + jnp.dot(lo, w1b_ref[d2:, :], preferred_element_type=F32))
        c = (jnp.dot(hi, w3b_ref[:d2, :], preferred_element_type=F32)
             + jnp.dot(lo, w3b_ref[d2:, :], preferred_element_type=F32))
        h_ref[...] = (a * jax.nn.sigmoid(a) * c).astype(h_ref.dtype)

    @pl.when(jnp.logical_not(used))
    def _():
        h_ref[...] = jnp.zeros_like(h_ref)


def _moe_down_kernel(be_ref, nu_ref, h_ref, w2_ref, y_ref, w2b_ref):
    b = pl.program_id(1)
    used = b < nu_ref[0]

    @pl.when(used & _new_expert(be_ref, b))
    def _():
        w2b_ref[...] = w2_ref[0].astype(BF16)

    @pl.when(used)
    def _():
        _store_row_slabs(y_ref, _pack_halves(jnp.dot(h_ref[...], w2b_ref[...], preferred_element_type=F32)))

    @pl.when(jnp.logical_not(used))
    def _():
        y_ref[...] = jnp.zeros_like(y_ref)


def _moe_experts(xs, block_e, n_used, w1, w3, w2, *, bm, fc=512, nc=2048):
    p, nslab, _ = xs.shape
    d2 = nslab * LANES
    d = 2 * d2
    de = w1.shape[2]
    nb = p // bm
    blk_up = 2 * (bm * d2 * 4 + 2 * d * fc * 4 + bm * fc * 2) + 2 * d * fc * 2 + 2 * bm * d * 2 + 3 * bm * fc * 4
    h = pl.pallas_call(
        _moe_up_kernel,
        grid_spec=pltpu.PrefetchScalarGridSpec(
            num_scalar_prefetch=2,
            grid=(de // fc, nb),
            in_specs=[pl.BlockSpec((bm * nslab, LANES), lambda f, b, be, nu: (b, 0)),
                      pl.BlockSpec((1, d, fc), lambda f, b, be, nu: (be[b], 0, f)),
                      pl.BlockSpec((1, d, fc), lambda f, b, be, nu: (be[b], 0, f))],
            out_specs=pl.BlockSpec((bm, fc), lambda f, b, be, nu: (b, f)),
            scratch_shapes=[pltpu.VMEM((d, fc), BF16), pltpu.VMEM((d, fc), BF16)],
        ),
        out_shape=jax.ShapeDtypeStruct((p, de), BF16),
        compiler_params=_params(("arbitrary", "arbitrary"), _vmem_limit(blk_up)),
        name="moe_up",
    )(block_e, n_used, xs.reshape(p * nslab, LANES), w1, w3)
    blk_dn = 2 * (bm * de * 2 + de * nc * 4 + bm * nc * 2) + de * nc * 2 + 2 * bm * nc * 4
    return pl.pallas_call(
        _moe_down_kernel,
        grid_spec=pltpu.PrefetchScalarGridSpec(
            num_scalar_prefetch=2,
            grid=(d // nc, nb),
            in_specs=[pl.BlockSpec((bm, de), lambda c, b, be, nu: (b, 0)),
                      pl.BlockSpec((1, de, nc), lambda c, b, be, nu: (be[b], 0, c))],
            out_specs=pl.BlockSpec((bm, nc // 2 // LANES, LANES), lambda c, b, be, nu: (b, c, 0)),
            scratch_shapes=[pltpu.VMEM((de, nc), BF16)],
        ),
        out_shape=jax.ShapeDtypeStruct((p, nslab, LANES), U32),
        compiler_params=_params(("arbitrary", "arbitrary"), _vmem_limit(blk_dn)),
        name="moe_down",
    )(block_e, n_used, h, w2)


def _combine_kernel(x_ref, y0_ref, y1_ref, ew_ref, g_ref, o_ref, *, nc):
    ew = ew_ref[...]
    w0 = ew[:, 0:1]
    w1 = ew[:, 1:2]
    half = nc // 2
    nslab = x_ref.shape[1] // 2 // LANES
    pieces = []
    ss = None
    for c in range(x_ref.shape[1] // nc):
        hi0, lo0 = _unpack_halves(_load_row_slabs(y0_ref, nslab, c * half // LANES, half // LANES))
        hi1, lo1 = _unpack_halves(_load_row_slabs(y1_ref, nslab, c * half // LANES, half // LANES))
        for j, (p0, p1) in enumerate(((hi0, hi1), (lo0, lo1))):
            cols = slice(c * nc + j * half, c * nc + (j + 1) * half)
            z = x_ref[:, cols] + w0 * p0 + w1 * p1
            pieces.append((cols, z))
            s = jnp.sum(z * z, axis=-1, keepdims=True)
            ss = s if ss is None else ss + s
    inv = lax.rsqrt(ss / x_ref.shape[1] + EPS)
    for cols, z in pieces:
        o_ref[:, cols] = z * inv * g_ref[:, cols]


def _combine(x, yu, e_w, gain, *, nc, tm=256):
    t, d = x.shape
    nt = t // tm
    nslab = yu.shape[1]
    yu2 = yu.reshape(yu.shape[0] * nslab, LANES)
    blk = 2 * (2 * tm * d * 4 + 2 * tm * d * 2 + tm * LANES * 4 + d * 4) + 3 * tm * d * 4
    return pl.pallas_call(
        functools.partial(_combine_kernel, nc=nc),
        grid=(nt,),
        in_specs=[pl.BlockSpec((tm, d), lambda i: (i, 0)),
                  pl.BlockSpec((tm * nslab, LANES), lambda i: (i, 0)),
                  pl.BlockSpec((tm * nslab, LANES), lambda i: (i + nt, 0)),
                  pl.BlockSpec((tm, TOP_K), lambda i: (i, 0)),
                  pl.BlockSpec((1, d), lambda i: (0, 0))],
        out_specs=pl.BlockSpec((tm, d), lambda i: (i, 0)),
        out_shape=jax.ShapeDtypeStruct((t, d), F32),
        compiler_params=_params(("arbitrary",), _vmem_limit(blk)),
        name="moe_combine",
    )(x, yu2, yu2, e_w, gain.reshape(1, d))


def _route(logits, *, bm):
    t = logits.shape[0]
    g_logits = logits[:, :N_GROUPS]
    e_logits = logits[:, N_GROUPS:N_GROUPS + N_EXPERTS].reshape(t, N_GROUPS, EXPERTS_PER_GROUP)
    g_idx = jnp.argmax(g_logits, axis=-1).astype(jnp.int32)
    g_w = jnp.take_along_axis(jax.nn.softmax(g_logits, axis=-1), g_idx[:, None], axis=-1)
    e_sel = jnp.take_along_axis(e_logits, g_idx[:, None, None], axis=1)[:, 0]
    top_v, top_i = lax.top_k(e_sel, TOP_K)
    e_w = jax.nn.softmax(top_v, axis=-1) * g_w
    eid = g_idx[:, None] * EXPERTS_PER_GROUP + top_i.astype(jnp.int32)

    a = t * TOP_K
    flat_e = eid.reshape(a)
    onehot = (flat_e[:, None] == jnp.arange(N_EXPERTS, dtype=jnp.int32)[None, :]).astype(jnp.int32)
    csum = jnp.cumsum(onehot, axis=0)
    rank = jnp.take_along_axis(csum, flat_e[:, None], axis=1)[:, 0] - 1
    counts = csum[-1]
    padded = ((counts + bm - 1) // bm) * bm
    pad_end = jnp.cumsum(padded)
    start_pad = pad_end - padded
    dest = (start_pad[flat_e] + rank).astype(jnp.int32)
    p = a + N_EXPERTS * bm
    nb = p // bm
    n_used = (pad_end[-1] // bm).astype(jnp.int32)
    blk = jnp.arange(nb, dtype=jnp.int32)
    block_e = jnp.searchsorted(pad_end, jnp.minimum(blk, n_used - 1) * bm, side='right').astype(jnp.int32)
    block_e = jnp.minimum(block_e, N_EXPERTS - 1)
    src_tok = jnp.zeros((p,), jnp.int32).at[dest].set(jnp.arange(a, dtype=jnp.int32) // TOP_K)
    back = dest.reshape(t, TOP_K).T.reshape(a)
    return e_w, src_tok, back, block_e, n_used.reshape(1)


MOE_BLOCK_ROWS = 256
MOE_DOWN_COLS = 2048


def kernel(x, mem, norm_mix, w_in, pool_w, pool_scale, lb_fwd, lb_bwd, hgrn_norm, w_out, norm_xattn, norm_mem,
           w_q, w_k, w_v, w_o, norm_moe, w_router_group, w_router_expert, w1, w3, w2, norm_final):
    bsz, seq, d = x.shape
    assert bsz == 1 and w_in.shape[0] == 1, "one sequence, one layer (the final norm is fused into the MoE combine)"
    l = 0
    pool_width = pool_w.shape[1] * pool_w.shape[2]
    hgrn_width = hgrn_norm.shape[1]
    xt = x.reshape(seq, d)
    mem_t = mem.reshape(mem.shape[1], d)
    h = _rmsnorm(xt, norm_mix[l], BF16)
    proj = _matmul([h], w_in[l], out_dtype=F32)
    y_pool = _pool_mixer(proj, pool_w[l], pool_scale[l])
    y_hgrn = _hgrn_mixer(proj, lb_fwd, lb_bwd, hgrn_norm[l], layer=l, width=hgrn_width, col0=pool_width)
    xt = _matmul([y_pool, y_hgrn], w_out[l], resid=xt, out_dtype=F32)
    mem_n = _rmsnorm(mem_t, norm_mem[l], BF16)
    hx = _rmsnorm(xt, norm_xattn[l], BF16)
    q = _matmul([hx], w_q[l], out_dtype=BF16)
    k = _matmul([mem_n], w_k[l], out_dtype=BF16)
    v = _matmul([mem_n], w_v[l], out_dtype=BF16)
    o = _xattn(q, k, v)
    xt = _matmul([o], w_o[l], resid=xt, out_dtype=F32)
    w_router = jnp.concatenate(
        [w_router_group[l],
         jnp.transpose(w_router_expert[l], (1, 0, 2)).reshape(d, N_EXPERTS),
         jnp.zeros((d, ROUTER_COLS - N_GROUPS - N_EXPERTS), F32)], axis=1)
    hp, logits = _router(xt, norm_moe[l], w_router)
    e_w, src_tok, back, block_e, n_used = _route(logits, bm=MOE_BLOCK_ROWS)
    xs = _gather_rows(src_tok, hp)
    ys = _moe_experts(xs, block_e, n_used, w1[l], w3[l], w2[l], bm=MOE_BLOCK_ROWS, nc=MOE_DOWN_COLS)
    yu = _gather_rows(back, ys)
    out = _combine(xt, yu, e_w, norm_final, nc=MOE_DOWN_COLS)
    return out.reshape(bsz, seq, d)
```

```python
import functools

import jax
import jax.numpy as jnp
from jax import lax
from jax.experimental import pallas as pl
from jax.experimental.pallas import tpu as pltpu

F32 = jnp.float32
BF16 = jnp.bfloat16
U32 = jnp.uint32

EPS = 1e-6
LANES = 128
SUBLANES = 8
VMEM_BYTES_V7X = 64 * 1024 * 1024

POOL_WINDOWS = (2, 4, 8, 16)
HEAD = 128
CHUNK = 64
XATTN_HEADS = 4
N_GROUPS = 4
EXPERTS_PER_GROUP = 8
N_EXPERTS = N_GROUPS * EXPERTS_PER_GROUP
TOP_K = 2
ROUTER_COLS = LANES


def _params(sem, vmem_bytes):
    return pltpu.CompilerParams(dimension_semantics=sem, vmem_limit_bytes=int(vmem_bytes))


def _vmem_limit(block_bytes):
    return min(int(block_bytes * 1.25) + (6 << 20), VMEM_BYTES_V7X - (4 << 20))


def _rmsnorm_kernel(x_ref, g_ref, o_ref):
    x = x_ref[...]
    ms = jnp.mean(x * x, axis=-1, keepdims=True)
    o_ref[...] = (x * lax.rsqrt(ms + EPS) * g_ref[...]).astype(o_ref.dtype)


def _rmsnorm(x, gain, out_dtype, tm=256):
    t, d = x.shape
    blk = tm * d * (4 + jnp.dtype(out_dtype).itemsize) * 2
    return pl.pallas_call(
        _rmsnorm_kernel,
        grid=(t // tm,),
        in_specs=[pl.BlockSpec((tm, d), lambda i: (i, 0)),
                  pl.BlockSpec((1, d), lambda i: (0, 0))],
        out_specs=pl.BlockSpec((tm, d), lambda i: (i, 0)),
        out_shape=jax.ShapeDtypeStruct((t, d), out_dtype),
        compiler_params=_params(("arbitrary",), _vmem_limit(blk)),
        name="rmsnorm",
    )(x, gain.reshape(1, d))


def _mm_kernel(*refs, n_a, has_resid):
    a_refs = refs[:n_a]
    w_ref = refs[n_a]
    r_ref = refs[n_a + 1] if has_resid else None
    o_ref, wb_ref = refs[-2], refs[-1]

    @pl.when(pl.program_id(1) == 0)
    def _():
        wb_ref[...] = w_ref[...].astype(BF16)

    acc = None
    k0 = 0
    for a_ref in a_refs:
        kk = a_ref.shape[1]
        part = jnp.dot(a_ref[...], wb_ref[k0:k0 + kk, :], preferred_element_type=F32)
        acc = part if acc is None else acc + part
        k0 += kk
    if has_resid:
        acc = acc + r_ref[...]
    o_ref[...] = acc.astype(o_ref.dtype)


def _matmul(a_parts, w, *, resid=None, out_dtype=F32, tm=1024, tn=512):
    m = a_parts[0].shape[0]
    k, n = w.shape
    assert sum(a.shape[1] for a in a_parts) == k
    tm = min(tm, m)
    in_specs = [pl.BlockSpec((tm, a.shape[1]), lambda j, i: (i, 0)) for a in a_parts]
    in_specs.append(pl.BlockSpec((k, tn), lambda j, i: (0, j)))
    args = list(a_parts) + [w]
    if resid is not None:
        in_specs.append(pl.BlockSpec((tm, tn), lambda j, i: (i, j)))
        args.append(resid)
    osz = jnp.dtype(out_dtype).itemsize
    blk = (2 * tm * k * 2 + 2 * k * tn * 4 + k * tn * 2 + 2 * tm * tn * osz
           + (2 * tm * tn * 4 if resid is not None else 0) + tm * tn * 4)
    return pl.pallas_call(
        functools.partial(_mm_kernel, n_a=len(a_parts), has_resid=resid is not None),
        grid=(n // tn, m // tm),
        in_specs=in_specs,
        out_specs=pl.BlockSpec((tm, tn), lambda j, i: (i, j)),
        out_shape=jax.ShapeDtypeStruct((m, n), out_dtype),
        scratch_shapes=[pltpu.VMEM((k, tn), BF16)],
        compiler_params=_params(("arbitrary", "arbitrary"), _vmem_limit(blk)),
        name="matmul",
    )(*args)


def _pool_kernel(u_ref, up_ref, un_ref, w_ref, s_ref, o_ref, *, tm, seq, group):
    i = pl.program_id(0)
    last = pl.num_programs(0) - 1
    n = tm + 2 * SUBLANES
    row = lax.broadcasted_iota(jnp.int32, (tm, 1), 0) + i * tm
    for gi, w in enumerate(POOL_WINDOWS):
        cs = slice(gi * group, (gi + 1) * group)
        u = u_ref[:, cs]
        prev = jnp.where(i > 0, up_ref[:, cs], 0.0)
        nxt = jnp.where(i < last, un_ref[:, cs], 0.0)
        f = jnp.concatenate([prev, u, nxt], axis=0)
        step = 1
        while step < w:
            f = f + pltpu.roll(f, n - step, 0)
            step *= 2
        half = w // 2
        first = SUBLANES - half
        if first:
            f = pltpu.roll(f, n - first, 0)
        win = f[:tm]
        lo = jnp.maximum(row - half, 0)
        hi = jnp.minimum(row + half - 1, seq - 1)
        cnt = (hi - lo + 1).astype(F32)
        mixed = win / cnt - u
        y = jnp.dot(mixed.astype(BF16), w_ref[gi].astype(BF16), preferred_element_type=F32)
        o_ref[:, cs] = (y * s_ref[:, cs]).astype(o_ref.dtype)


def _pool_mixer(proj, pool_w, pool_scale, *, tm=512):
    t = proj.shape[0]
    ng, group, _ = pool_w.shape
    width = ng * group
    rpb = tm // SUBLANES
    nhalo = t // SUBLANES
    blk = 2 * (tm * width * 4 + 2 * SUBLANES * width * 4 + ng * group * group * 4 + tm * width * 2) + 6 * tm * group * 4
    return pl.pallas_call(
        functools.partial(_pool_kernel, tm=tm, seq=t, group=group),
        grid=(t // tm,),
        in_specs=[pl.BlockSpec((tm, width), lambda i: (i, 0)),
                  pl.BlockSpec((SUBLANES, width), lambda i: (jnp.maximum(i * rpb - 1, 0), 0)),
                  pl.BlockSpec((SUBLANES, width), lambda i: (jnp.minimum((i + 1) * rpb, nhalo - 1), 0)),
                  pl.BlockSpec((ng, group, group), lambda i: (0, 0, 0)),
                  pl.BlockSpec((1, width), lambda i: (0, 0))],
        out_specs=pl.BlockSpec((tm, width), lambda i: (i, 0)),
        out_shape=jax.ShapeDtypeStruct((t, width), BF16),
        compiler_params=_params(("arbitrary",), _vmem_limit(blk)),
        name="pool_mixer",
    )(proj, proj, proj, pool_w, pool_scale.reshape(1, width))


def _chunk_cumsum(x, reverse):
    n = x.shape[0]
    pos = lax.broadcasted_iota(jnp.int32, x.shape, 0) % CHUNK
    s = 1
    while s < CHUNK:
        if reverse:
            x = x + jnp.where(pos < CHUNK - s, pltpu.roll(x, n - s, 0), 0.0)
        else:
            x = x + jnp.where(pos >= s, pltpu.roll(x, s, 0), 0.0)
        s *= 2
    return x


def _lower_bound(lb_raw, layer):
    e = jnp.exp(lb_raw - jnp.max(lb_raw, axis=0, keepdims=True))
    return jnp.sum(e[:layer + 1], axis=0, keepdims=True) / jnp.sum(e, axis=0, keepdims=True)


def _hgrn_direction(q_raw, f_raw, v, lb, st_ref, h, mask, *, reverse):
    tb = q_raw.shape[0]
    nc = tb // CHUNK
    q = q_raw * jax.nn.sigmoid(q_raw)
    f = lb + (1.0 - lb) * jax.nn.sigmoid(f_raw)
    logf = jnp.log(f)
    k = 1.0 - f
    b = _chunk_cumsum(logf, reverse)
    b3 = b.reshape(nc, CHUNK, HEAD)
    edge = 0 if reverse else CHUNK - 1
    b_last = b3[:, edge:edge + 1, :]
    qd = (q * jnp.exp(b)).astype(BF16)
    kd = (k * jnp.exp(-b)).astype(BF16)
    k_end = (k.reshape(nc, CHUNK, HEAD) * jnp.exp(b_last - b3)).astype(BF16)
    decay = jnp.exp(b_last)
    vb = v.astype(BF16)

    a = lax.dot_general(qd, kd, (((1,), (1,)), ((), ())), preferred_element_type=F32)
    a = jnp.where(mask, a, 0.0)
    o_intra = jnp.dot(a.astype(BF16), vb, preferred_element_type=F32)

    st = st_ref[h]
    o_inter = [None] * nc
    order = range(nc - 1, -1, -1) if reverse else range(nc)
    for c in order:
        rows = slice(c * CHUNK, (c + 1) * CHUNK)
        o_inter[c] = lax.dot_general(qd[rows], st.astype(BF16), (((1,), (1,)), ((), ())),
                                     preferred_element_type=F32)
        d_st = lax.dot_general(vb[rows], k_end[c], (((0,), (0,)), ((), ())), preferred_element_type=F32)
        st = st * decay[c] + d_st
    st_ref[h] = st
    return o_intra + jnp.concatenate(o_inter, axis=0)


def _intra_mask(tb, reverse):
    r = lax.broadcasted_iota(jnp.int32, (tb, tb), 0)
    c = lax.broadcasted_iota(jnp.int32, (tb, tb), 1)
    same = (r // CHUNK) == (c // CHUNK)
    return same & ((c >= r) if reverse else (c <= r))


def _hgrn_fwd_kernel(q_ref, f_ref, i_ref, lb_ref, o_ref, st_ref, *, layer, heads):
    @pl.when(pl.program_id(0) == 0)
    def _():
        st_ref[...] = jnp.zeros_like(st_ref)

    mask = _intra_mask(q_ref.shape[0], False)

    def head(h, carry):
        hs = pl.ds(pl.multiple_of(h * HEAD, HEAD), HEAD)
        lb = _lower_bound(lb_ref[:, hs], layer)
        o_ref[:, hs] = _hgrn_direction(q_ref[:, hs], f_ref[:, hs], i_ref[:, hs], lb, st_ref, h, mask,
                                       reverse=False)
        return carry

    lax.fori_loop(0, heads, head, 0)


def _hgrn_bwd_kernel(q_ref, f_ref, i_ref, g_ref, of_ref, lb_ref, gain_ref, y_ref, st_ref, *, layer, heads):
    @pl.when(pl.program_id(0) == 0)
    def _():
        st_ref[...] = jnp.zeros_like(st_ref)

    mask = _intra_mask(q_ref.shape[0], True)

    def head(h, carry):
        hs = pl.ds(pl.multiple_of(h * HEAD, HEAD), HEAD)
        lb = _lower_bound(lb_ref[:, hs], layer)
        o = of_ref[:, hs] + _hgrn_direction(q_ref[:, hs], f_ref[:, hs], i_ref[:, hs], lb, st_ref, h, mask,
                                            reverse=True)
        o = o * lax.rsqrt(jnp.mean(o * o, axis=-1, keepdims=True) + EPS) * gain_ref[:, hs]
        g = g_ref[:, hs]
        y_ref[:, hs] = (o * (g * jax.nn.sigmoid(g))).astype(y_ref.dtype)
        return carry

    lax.fori_loop(0, heads, head, 0)


def _hgrn_mixer(proj, lb_fwd, lb_bwd, hgrn_norm, *, layer, width, col0, tb=256):
    t = proj.shape[0]
    heads = width // HEAD
    nb = t // tb
    c = col0 // width
    layers = lb_fwd.shape[0]
    blk_f = 2 * (3 * tb * width * 4 + layers * width * 4 + tb * width * 4) + heads * HEAD * HEAD * 4
    o_f = pl.pallas_call(
        functools.partial(_hgrn_fwd_kernel, layer=layer, heads=heads),
        grid=(nb,),
        in_specs=[pl.BlockSpec((tb, width), lambda b: (b, c)),
                  pl.BlockSpec((tb, width), lambda b: (b, c + 1)),
                  pl.BlockSpec((tb, width), lambda b: (b, c + 3)),
                  pl.BlockSpec((layers, width), lambda b: (0, 0))],
        out_specs=pl.BlockSpec((tb, width), lambda b: (b, 0)),
        out_shape=jax.ShapeDtypeStruct((t, width), F32),
        scratch_shapes=[pltpu.VMEM((heads, HEAD, HEAD), F32)],
        compiler_params=_params(("arbitrary",), _vmem_limit(blk_f)),
        name="hgrn_fwd",
    )(proj, proj, proj, lb_fwd)
    blk_b = 2 * (5 * tb * width * 4 + layers * width * 4 + width * 4 + tb * width * 2) + heads * HEAD * HEAD * 4
    return pl.pallas_call(
        functools.partial(_hgrn_bwd_kernel, layer=layer, heads=heads),
        grid=(nb,),
        in_specs=[pl.BlockSpec((tb, width), lambda b: (nb - 1 - b, c)),
                  pl.BlockSpec((tb, width), lambda b: (nb - 1 - b, c + 2)),
                  pl.BlockSpec((tb, width), lambda b: (nb - 1 - b, c + 3)),
                  pl.BlockSpec((tb, width), lambda b: (nb - 1 - b, c + 4)),
                  pl.BlockSpec((tb, width), lambda b: (nb - 1 - b, 0)),
                  pl.BlockSpec((layers, width), lambda b: (0, 0)),
                  pl.BlockSpec((1, width), lambda b: (0, 0))],
        out_specs=pl.BlockSpec((tb, width), lambda b: (nb - 1 - b, 0)),
        out_shape=jax.ShapeDtypeStruct((t, width), BF16),
        scratch_shapes=[pltpu.VMEM((heads, HEAD, HEAD), F32)],
        compiler_params=_params(("arbitrary",), _vmem_limit(blk_b)),
        name="hgrn_bwd",
    )(proj, proj, proj, proj, o_f, lb_bwd, hgrn_norm.reshape(1, width))


def _xattn_kernel(q_ref, k_ref, v_ref, o_ref, *, scale):
    s = lax.dot_general(q_ref[...], k_ref[...], (((1,), (1,)), ((), ())), preferred_element_type=F32) * scale
    s = s - jnp.max(s, axis=-1, keepdims=True)
    p = jnp.exp(s)
    p = p / jnp.sum(p, axis=-1, keepdims=True)
    o_ref[...] = jnp.dot(p.astype(BF16), v_ref[...], preferred_element_type=F32).astype(o_ref.dtype)


def _xattn(q, k, v, *, tm=512):
    t, d = q.shape
    m = k.shape[0]
    dh = d // XATTN_HEADS
    blk = 2 * (2 * tm * dh * 2 + 2 * m * dh * 2) + 4 * tm * m * 4 + tm * dh * 4
    return pl.pallas_call(
        functools.partial(_xattn_kernel, scale=dh ** -0.5),
        grid=(XATTN_HEADS, t // tm),
        in_specs=[pl.BlockSpec((tm, dh), lambda h, i: (i, h)),
                  pl.BlockSpec((m, dh), lambda h, i: (0, h)),
                  pl.BlockSpec((m, dh), lambda h, i: (0, h))],
        out_specs=pl.BlockSpec((tm, dh), lambda h, i: (i, h)),
        out_shape=jax.ShapeDtypeStruct((t, d), BF16),
        compiler_params=_params(("arbitrary", "arbitrary"), _vmem_limit(blk)),
        name="xattn",
    )(q, k, v)


def _pack_halves(x):
    w = x.shape[1] // 2
    hi = lax.bitcast_convert_type(x[:, :w].astype(BF16).astype(F32), U32)
    lo = lax.bitcast_convert_type(x[:, w:].astype(BF16).astype(F32), U32)
    return hi | (lo >> 16)


def _unpack_halves(p):
    hi = lax.bitcast_convert_type(p & jnp.uint32(0xFFFF0000), F32)
    lo = lax.bitcast_convert_type(p << 16, F32)
    return hi, lo


def _load_row_slabs(flat_ref, s, j0, nj):
    rows = flat_ref.shape[0] // s
    return jnp.concatenate([flat_ref[pl.ds(j0 + j, rows, stride=s), :] for j in range(nj)], axis=1)


def _store_row_slabs(ref, val):
    for j in range(ref.shape[1]):
        ref[:, j, :] = val[:, j * LANES:(j + 1) * LANES]


def _router_kernel(x_ref, g_ref, wr_ref, hp_ref, lg_ref):
    x = x_ref[...]
    ms = jnp.mean(x * x, axis=-1, keepdims=True)
    h = x * lax.rsqrt(ms + EPS) * g_ref[...]
    _store_row_slabs(hp_ref, _pack_halves(h))
    lg_ref[...] = jnp.dot(h, wr_ref[...], preferred_element_type=F32, precision=lax.Precision.HIGHEST)


def _router(x, gain, w_router, *, tm=256):
    t, d = x.shape
    blk = 2 * (tm * d * 4 + d * 4 + d * ROUTER_COLS * 4 + tm * d * 2 + tm * ROUTER_COLS * 4) + 3 * tm * d * 4
    return pl.pallas_call(
        _router_kernel,
        grid=(t // tm,),
        in_specs=[pl.BlockSpec((tm, d), lambda i: (i, 0)),
                  pl.BlockSpec((1, d), lambda i: (0, 0)),
                  pl.BlockSpec((d, ROUTER_COLS), lambda i: (0, 0))],
        out_specs=[pl.BlockSpec((tm, d // 2 // LANES, LANES), lambda i: (i, 0, 0)),
                   pl.BlockSpec((tm, ROUTER_COLS), lambda i: (i, 0))],
        out_shape=[jax.ShapeDtypeStruct((t, d // 2 // LANES, LANES), U32),
                   jax.ShapeDtypeStruct((t, ROUTER_COLS), F32)],
        compiler_params=_params(("arbitrary",), _vmem_limit(blk)),
        name="moe_router",
    )(x, gain.reshape(1, d), w_router)


def _gather_rows_kernel(idx_ref, nrows_ref, src_ref, o_ref, sem, *, rows):
    base = pl.program_id(0) * rows

    def row_copy(src_row, r):
        return pltpu.make_async_copy(src_ref.at[src_row], o_ref.at[r], sem)

    @pl.when(base < nrows_ref[0])
    def _():
        def issue(r, carry):
            row_copy(idx_ref[base + r], r).start()
            return carry

        lax.fori_loop(0, rows, issue, 0, unroll=8)

        def drain(r, carry):
            row_copy(0, r).wait()
            return carry

        lax.fori_loop(0, rows, drain, 0, unroll=8)

    @pl.when(base >= nrows_ref[0])
    def _():
        o_ref[...] = jnp.zeros_like(o_ref)


def _gather_rows(idx, nrows, src, *, rows=256):
    n = idx.shape[0]
    slab = src.shape[1:]
    blk = 2 * rows * slab[0] * slab[1] * 4
    return pl.pallas_call(
        functools.partial(_gather_rows_kernel, rows=rows),
        grid_spec=pltpu.PrefetchScalarGridSpec(
            num_scalar_prefetch=2,
            grid=(n // rows,),
            in_specs=[pl.BlockSpec(memory_space=pl.ANY)],
            out_specs=pl.BlockSpec((rows,) + slab, lambda b, idx, nrows: (b, 0, 0)),
            scratch_shapes=[pltpu.SemaphoreType.DMA(())],
        ),
        out_shape=jax.ShapeDtypeStruct((n,) + slab, src.dtype),
        compiler_params=_params(("arbitrary",), _vmem_limit(blk)),
        name="gather_rows",
    )(idx, nrows, src)


def _expert_block_loop(nblk, in_copy, out_copy, prologue, compute):
    in_copy(0, 0).start()
    prologue()

    def body(k, carry):
        slot = k % 2

        @pl.when(k + 1 < nblk)
        def _():
            in_copy(k + 1, 1 - slot).start()

        in_copy(k, slot).wait()

        @pl.when(k >= 2)
        def _():
            out_copy(k - 2, slot).wait()

        compute(slot)
        out_copy(k, slot).start()
        return carry

    lax.fori_loop(0, nblk, body, 0)

    @pl.when(nblk >= 2)
    def _():
        out_copy(nblk - 2, nblk % 2).wait()

    out_copy(nblk - 1, (nblk - 1) % 2).wait()


def _moe_up_kernel(cnt_ref, start_ref, xs_ref, w1_ref, w3_ref, h_ref, w1b_ref, w3b_ref, xbuf, obuf, xsem, osem,
                   *, bm):
    f = pl.program_id(0)
    e = pl.program_id(1)
    nblk = cnt_ref[e]
    blk0 = start_ref[e]
    d, fc = w1b_ref.shape
    nslab = d // 2 // LANES
    xrows = bm * nslab

    def x_copy(k, slot):
        row0 = pl.multiple_of((blk0 + k) * xrows, xrows)
        return pltpu.make_async_copy(xs_ref.at[pl.ds(row0, xrows)], xbuf.at[slot], xsem.at[slot])

    def h_copy(k, slot):
        row0 = pl.multiple_of((blk0 + k) * bm, bm)
        col0 = pl.multiple_of(f * fc, fc)
        return pltpu.make_async_copy(obuf.at[slot], h_ref.at[pl.ds(row0, bm), pl.ds(col0, fc)], osem.at[slot])

    def cast_weights():
        w1b_ref[...] = w1_ref[0].astype(BF16)
        w3b_ref[...] = w3_ref[0].astype(BF16)

    def compute(slot):
        hi, lo = _unpack_halves(_load_row_slabs(xbuf.at[slot], nslab, 0, nslab))
        hi = hi.astype(BF16)
        lo = lo.astype(BF16)
        d2 = d // 2
        a = (jnp.dot(hi, w1b_ref[:d2, :], preferred_element_type=F32)
             + jnp.dot(lo, w1b_ref[d2:, :], preferred_element_type=F32))
        c = (jnp.dot(hi, w3b_ref[:d2, :], preferred_element_type=F32)
             + jnp.dot(lo, w3b_ref[d2:, :], preferred_element_type=F32))
        obuf[slot] = (a * jax.nn.sigmoid(a) * c).astype(obuf.dtype)

    @pl.when(nblk > 0)
    def _():
        _expert_block_loop(nblk, x_copy, h_copy, cast_weights, compute)


def _moe_down_kernel(cnt_ref, start_ref, h_ref, w2_ref, y_ref, w2b_ref, hbuf, obuf, hsem, osem, *, bm):
    c = pl.program_id(0)
    e = pl.program_id(1)
    nblk = cnt_ref[e]
    blk0 = start_ref[e]
    nj = obuf.shape[2]

    def h_copy(k, slot):
        row0 = pl.multiple_of((blk0 + k) * bm, bm)
        return pltpu.make_async_copy(h_ref.at[pl.ds(row0, bm)], hbuf.at[slot], hsem.at[slot])

    def y_copy(k, slot):
        row0 = pl.multiple_of((blk0 + k) * bm, bm)
        j0 = pl.multiple_of(c * nj, nj)
        return pltpu.make_async_copy(obuf.at[slot], y_ref.at[pl.ds(row0, bm), pl.ds(j0, nj)], osem.at[slot])

    def cast_weights():
        w2b_ref[...] = w2_ref[0].astype(BF16)

    def compute(slot):
        y = jnp.dot(hbuf[slot], w2b_ref[...], preferred_element_type=F32)
        _store_row_slabs(obuf.at[slot], _pack_halves(y))

    @pl.when(nblk > 0)
    def _():
        _expert_block_loop(nblk, h_copy, y_copy, cast_weights, compute)


def _moe_experts(xs, blk_cnt, blk_start, w1, w3, w2, *, bm, fc=512, nc=2048):
    p, nslab, _ = xs.shape
    d = 2 * nslab * LANES
    ne, _, de = w1.shape
    any_spec = pl.BlockSpec(memory_space=pl.ANY)
    blk_up = (4 * d * fc * 4 + 2 * d * fc * 2 + 2 * bm * nslab * LANES * 4 + 2 * bm * fc * 2
              + 2 * bm * d * 2 + 3 * bm * fc * 4)
    h = pl.pallas_call(
        functools.partial(_moe_up_kernel, bm=bm),
        grid_spec=pltpu.PrefetchScalarGridSpec(
            num_scalar_prefetch=2,
            grid=(de // fc, ne),
            in_specs=[any_spec,
                      pl.BlockSpec((1, d, fc), lambda f, e, cnt, start: (e, 0, f)),
                      pl.BlockSpec((1, d, fc), lambda f, e, cnt, start: (e, 0, f))],
            out_specs=any_spec,
            scratch_shapes=[pltpu.VMEM((d, fc), BF16), pltpu.VMEM((d, fc), BF16),
                            pltpu.VMEM((2, bm * nslab, LANES), U32), pltpu.VMEM((2, bm, fc), BF16),
                            pltpu.SemaphoreType.DMA((2,)), pltpu.SemaphoreType.DMA((2,))],
        ),
        out_shape=jax.ShapeDtypeStruct((p, de), BF16),
        compiler_params=_params(("arbitrary", "arbitrary"), _vmem_limit(blk_up)),
        name="moe_up",
    )(blk_cnt, blk_start, xs.reshape(p * nslab, LANES), w1, w3)
    nj = nc // 2 // LANES
    blk_dn = 2 * de * nc * 4 + de * nc * 2 + 2 * bm * de * 2 + 2 * bm * nj * LANES * 4 + 2 * bm * nc * 4
    return pl.pallas_call(
        functools.partial(_moe_down_kernel, bm=bm),
        grid_spec=pltpu.PrefetchScalarGridSpec(
            num_scalar_prefetch=2,
            grid=(d // nc, ne),
            in_specs=[any_spec,
                      pl.BlockSpec((1, de, nc), lambda c, e, cnt, start: (e, 0, c))],
            out_specs=any_spec,
            scratch_shapes=[pltpu.VMEM((de, nc), BF16),
                            pltpu.VMEM((2, bm, de), BF16), pltpu.VMEM((2, bm, nj, LANES), U32),
                            pltpu.SemaphoreType.DMA((2,)), pltpu.SemaphoreType.DMA((2,))],
        ),
        out_shape=jax.ShapeDtypeStruct((p, nslab, LANES), U32),
        compiler_params=_params(("arbitrary", "arbitrary"), _vmem_limit(blk_dn)),
        name="moe_down",
    )(blk_cnt, blk_start, h, w2)


def _combine_kernel(x_ref, y0_ref, y1_ref, ew_ref, g_ref, o_ref, *, nc):
    ew = ew_ref[...]
    w0 = ew[:, 0:1]
    w1 = ew[:, 1:2]
    half = nc // 2
    nslab = x_ref.shape[1] // 2 // LANES
    pieces = []
    ss = None
    for c in range(x_ref.shape[1] // nc):
        hi0, lo0 = _unpack_halves(_load_row_slabs(y0_ref, nslab, c * half // LANES, half // LANES))
        hi1, lo1 = _unpack_halves(_load_row_slabs(y1_ref, nslab, c * half // LANES, half // LANES))
        for j, (p0, p1) in enumerate(((hi0, hi1), (lo0, lo1))):
            cols = slice(c * nc + j * half, c * nc + (j + 1) * half)
            z = x_ref[:, cols] + w0 * p0 + w1 * p1
            pieces.append((cols, z))
            s = jnp.sum(z * z, axis=-1, keepdims=True)
            ss = s if ss is None else ss + s
    inv = lax.rsqrt(ss / x_ref.shape[1] + EPS)
    for cols, z in pieces:
        o_ref[:, cols] = z * inv * g_ref[:, cols]


def _combine(x, yu, e_w, gain, *, nc, tm=256):
    t, d = x.shape
    nt = t // tm
    nslab = yu.shape[1]
    yu2 = yu.reshape(yu.shape[0] * nslab, LANES)
    blk = 2 * (2 * tm * d * 4 + 2 * tm * d * 2 + tm * LANES * 4 + d * 4) + 3 * tm * d * 4
    return pl.pallas_call(
        functools.partial(_combine_kernel, nc=nc),
        grid=(nt,),
        in_specs=[pl.BlockSpec((tm, d), lambda i: (i, 0)),
                  pl.BlockSpec((tm * nslab, LANES), lambda i: (i, 0)),
                  pl.BlockSpec((tm * nslab, LANES), lambda i: (i + nt, 0)),
                  pl.BlockSpec((tm, TOP_K), lambda i: (i, 0)),
                  pl.BlockSpec((1, d), lambda i: (0, 0))],
        out_specs=pl.BlockSpec((tm, d), lambda i: (i, 0)),
        out_shape=jax.ShapeDtypeStruct((t, d), F32),
        compiler_params=_params(("arbitrary",), _vmem_limit(blk)),
        name="moe_combine",
    )(x, yu2, yu2, e_w, gain.reshape(1, d))


def _route(logits, *, bm):
    t = logits.shape[0]
    g_logits = logits[:, :N_GROUPS]
    e_logits = logits[:, N_GROUPS:N_GROUPS + N_EXPERTS].reshape(t, N_GROUPS, EXPERTS_PER_GROUP)
    g_idx = jnp.argmax(g_logits, axis=-1).astype(jnp.int32)
    g_w = jnp.take_along_axis(jax.nn.softmax(g_logits, axis=-1), g_idx[:, None], axis=-1)
    e_sel = jnp.take_along_axis(e_logits, g_idx[:, None, None], axis=1)[:, 0]
    top_v, top_i = lax.top_k(e_sel, TOP_K)
    e_w = jax.nn.softmax(top_v, axis=-1) * g_w
    eid = g_idx[:, None] * EXPERTS_PER_GROUP + top_i.astype(jnp.int32)

    a = t * TOP_K
    flat_e = eid.reshape(a)
    onehot = (flat_e[:, None] == jnp.arange(N_EXPERTS, dtype=jnp.int32)[None, :]).astype(jnp.int32)
    csum = jnp.cumsum(onehot, axis=0)
    rank = jnp.take_along_axis(csum, flat_e[:, None], axis=1)[:, 0] - 1
    counts = csum[-1]
    padded = ((counts + bm - 1) // bm) * bm
    pad_end = jnp.cumsum(padded)
    start_pad = pad_end - padded
    dest = (start_pad[flat_e] + rank).astype(jnp.int32)
    p = a + N_EXPERTS * bm
    blk_cnt = (padded // bm).astype(jnp.int32)
    blk_start = (start_pad // bm).astype(jnp.int32)
    rows_used = pad_end[-1:].astype(jnp.int32)
    src_tok = (jnp.arange(p, dtype=jnp.int32) % t).at[dest].set(jnp.arange(a, dtype=jnp.int32) // TOP_K)
    back = dest.reshape(t, TOP_K).T.reshape(a)
    return e_w, src_tok, back, blk_cnt, blk_start, rows_used


MOE_BLOCK_ROWS = 256
MOE_DOWN_COLS = 2048


def kernel(x, mem, norm_mix, w_in, pool_w, pool_scale, lb_fwd, lb_bwd, hgrn_norm, w_out, norm_xattn, norm_mem,
           w_q, w_k, w_v, w_o, norm_moe, w_router_group, w_router_expert, w1, w3, w2, norm_final):
    bsz, seq, d = x.shape
    assert bsz == 1 and w_in.shape[0] == 1, "one sequence, one layer (the final norm is fused into the MoE combine)"
    l = 0
    pool_width = pool_w.shape[1] * pool_w.shape[2]
    hgrn_width = hgrn_norm.shape[1]
    xt = x.reshape(seq, d)
    mem_t = mem.reshape(mem.shape[1], d)
    h = _rmsnorm(xt, norm_mix[l], BF16)
    proj = _matmul([h], w_in[l], out_dtype=F32)
    y_pool = _pool_mixer(proj, pool_w[l], pool_scale[l])
    y_hgrn = _hgrn_mixer(proj, lb_fwd, lb_bwd, hgrn_norm[l], layer=l, width=hgrn_width, col0=pool_width)
    xt = _matmul([y_pool, y_hgrn], w_out[l], resid=xt, out_dtype=F32)
    mem_n = _rmsnorm(mem_t, norm_mem[l], BF16)
    hx = _rmsnorm(xt, norm_xattn[l], BF16)
    q = _matmul([hx], w_q[l], out_dtype=BF16)
    k = _matmul([mem_n], w_k[l], out_dtype=BF16)
    v = _matmul([mem_n], w_v[l], out_dtype=BF16)
    o = _xattn(q, k, v)
    xt = _matmul([o], w_o[l], resid=xt, out_dtype=F32)
    w_router = jnp.concatenate(
        [w_router_group[l],
         jnp.transpose(w_router_expert[l], (1, 0, 2)).reshape(d, N_EXPERTS),
         jnp.zeros((d, ROUTER_COLS - N_GROUPS - N_EXPERTS), F32)], axis=1)
    hp, logits = _router(xt, norm_moe[l], w_router)
    e_w, src_tok, back, blk_cnt, blk_start, rows_used = _route(logits, bm=MOE_BLOCK_ROWS)
    xs = _gather_rows(src_tok, rows_used, hp)
    ys = _moe_experts(xs, blk_cnt, blk_start, w1[l], w3[l], w2[l], bm=MOE_BLOCK_ROWS, nc=MOE_DOWN_COLS)
    yu = _gather_rows(back, jnp.full((1,), back.shape[0], jnp.int32), ys)
    out = _combine(xt, yu, e_w, norm_final, nc=MOE_DOWN_COLS)
    return out.reshape(bsz, seq, d)
```

```python
import functools

import jax
import jax.numpy as jnp
from jax import lax
from jax.experimental import pallas as pl
from jax.experimental.pallas import tpu as pltpu

F32 = jnp.float32
BF16 = jnp.bfloat16
U32 = jnp.uint32

EPS = 1e-6
LANES = 128
SUBLANES = 8
VMEM_BYTES_V7X = 64 * 1024 * 1024

POOL_WINDOWS = (2, 4, 8, 16)
HEAD = 128
CHUNK = 64
XATTN_HEADS = 4
N_GROUPS = 4
EXPERTS_PER_GROUP = 8
N_EXPERTS = N_GROUPS * EXPERTS_PER_GROUP
TOP_K = 2
ROUTER_COLS = LANES


def _params(sem, vmem_bytes):
    return pltpu.CompilerParams(dimension_semantics=sem, vmem_limit_bytes=int(vmem_bytes))


def _vmem_limit(block_bytes):
    return min(int(block_bytes * 1.25) + (6 << 20), VMEM_BYTES_V7X - (4 << 20))


def _rmsnorm_kernel(x_ref, g_ref, o_ref):
    x = x_ref[...]
    ms = jnp.mean(x * x, axis=-1, keepdims=True)
    o_ref[...] = (x * lax.rsqrt(ms + EPS) * g_ref[...]).astype(o_ref.dtype)


def _rmsnorm(x, gain, out_dtype, tm=256):
    t, d = x.shape
    blk = tm * d * (4 + jnp.dtype(out_dtype).itemsize) * 2
    return pl.pallas_call(
        _rmsnorm_kernel,
        grid=(t // tm,),
        in_specs=[pl.BlockSpec((tm, d), lambda i: (i, 0)),
                  pl.BlockSpec((1, d), lambda i: (0, 0))],
        out_specs=pl.BlockSpec((tm, d), lambda i: (i, 0)),
        out_shape=jax.ShapeDtypeStruct((t, d), out_dtype),
        compiler_params=_params(("arbitrary",), _vmem_limit(blk)),
        name="rmsnorm",
    )(x, gain.reshape(1, d))


def _mm_kernel(*refs, n_a, has_resid):
    a_refs = refs[:n_a]
    w_ref = refs[n_a]
    r_ref = refs[n_a + 1] if has_resid else None
    o_ref, wb_ref = refs[-2], refs[-1]

    @pl.when(pl.program_id(1) == 0)
    def _():
        wb_ref[...] = w_ref[...].astype(BF16)

    acc = None
    k0 = 0
    for a_ref in a_refs:
        kk = a_ref.shape[1]
        part = jnp.dot(a_ref[...], wb_ref[k0:k0 + kk, :], preferred_element_type=F32)
        acc = part if acc is None else acc + part
        k0 += kk
    if has_resid:
        acc = acc + r_ref[...]
    o_ref[...] = acc.astype(o_ref.dtype)


def _matmul(a_parts, w, *, resid=None, out_dtype=F32, tm=1024, tn=512):
    m = a_parts[0].shape[0]
    k, n = w.shape
    assert sum(a.shape[1] for a in a_parts) == k
    tm = min(tm, m)
    in_specs = [pl.BlockSpec((tm, a.shape[1]), lambda j, i: (i, 0)) for a in a_parts]
    in_specs.append(pl.BlockSpec((k, tn), lambda j, i: (0, j)))
    args = list(a_parts) + [w]
    if resid is not None:
        in_specs.append(pl.BlockSpec((tm, tn), lambda j, i: (i, j)))
        args.append(resid)
    osz = jnp.dtype(out_dtype).itemsize
    blk = (2 * tm * k * 2 + 2 * k * tn * 4 + k * tn * 2 + 2 * tm * tn * osz
           + (2 * tm * tn * 4 if resid is not None else 0) + tm * tn * 4)
    return pl.pallas_call(
        functools.partial(_mm_kernel, n_a=len(a_parts), has_resid=resid is not None),
        grid=(n // tn, m // tm),
        in_specs=in_specs,
        out_specs=pl.BlockSpec((tm, tn), lambda j, i: (i, j)),
        out_shape=jax.ShapeDtypeStruct((m, n), out_dtype),
        scratch_shapes=[pltpu.VMEM((k, tn), BF16)],
        compiler_params=_params(("arbitrary", "arbitrary"), _vmem_limit(blk)),
        name="matmul",
    )(*args)


def _pool_kernel(u_ref, up_ref, un_ref, w_ref, s_ref, o_ref, *, tm, seq, group):
    i = pl.program_id(0)
    last = pl.num_programs(0) - 1
    n = tm + 2 * SUBLANES
    row = lax.broadcasted_iota(jnp.int32, (tm, 1), 0) + i * tm
    for gi, w in enumerate(POOL_WINDOWS):
        cs = slice(gi * group, (gi + 1) * group)
        u = u_ref[:, cs]
        prev = jnp.where(i > 0, up_ref[:, cs], 0.0)
        nxt = jnp.where(i < last, un_ref[:, cs], 0.0)
        f = jnp.concatenate([prev, u, nxt], axis=0)
        step = 1
        while step < w:
            f = f + pltpu.roll(f, n - step, 0)
            step *= 2
        half = w // 2
        first = SUBLANES - half
        if first:
            f = pltpu.roll(f, n - first, 0)
        win = f[:tm]
        lo = jnp.maximum(row - half, 0)
        hi = jnp.minimum(row + half - 1, seq - 1)
        cnt = (hi - lo + 1).astype(F32)
        mixed = win / cnt - u
        y = jnp.dot(mixed.astype(BF16), w_ref[gi].astype(BF16), preferred_element_type=F32)
        o_ref[:, cs] = (y * s_ref[:, cs]).astype(o_ref.dtype)


def _pool_mixer(proj, pool_w, pool_scale, *, tm=512):
    t = proj.shape[0]
    ng, group, _ = pool_w.shape
    width = ng * group
    rpb = tm // SUBLANES
    nhalo = t // SUBLANES
    blk = 2 * (tm * width * 4 + 2 * SUBLANES * width * 4 + ng * group * group * 4 + tm * width * 2) + 6 * tm * group * 4
    return pl.pallas_call(
        functools.partial(_pool_kernel, tm=tm, seq=t, group=group),
        grid=(t // tm,),
        in_specs=[pl.BlockSpec((tm, width), lambda i: (i, 0)),
                  pl.BlockSpec((SUBLANES, width), lambda i: (jnp.maximum(i * rpb - 1, 0), 0)),
                  pl.BlockSpec((SUBLANES, width), lambda i: (jnp.minimum((i + 1) * rpb, nhalo - 1), 0)),
                  pl.BlockSpec((ng, group, group), lambda i: (0, 0, 0)),
                  pl.BlockSpec((1, width), lambda i: (0, 0))],
        out_specs=pl.BlockSpec((tm, width), lambda i: (i, 0)),
        out_shape=jax.ShapeDtypeStruct((t, width), BF16),
        compiler_params=_params(("arbitrary",), _vmem_limit(blk)),
        name="pool_mixer",
    )(proj, proj, proj, pool_w, pool_scale.reshape(1, width))


def _chunk_cumsum(x, reverse):
    n = x.shape[0]
    pos = lax.broadcasted_iota(jnp.int32, x.shape, 0) % CHUNK
    s = 1
    while s < CHUNK:
        if reverse:
            x = x + jnp.where(pos < CHUNK - s, pltpu.roll(x, n - s, 0), 0.0)
        else:
            x = x + jnp.where(pos >= s, pltpu.roll(x, s, 0), 0.0)
        s *= 2
    return x


def _lower_bound(lb_raw, layer):
    e = jnp.exp(lb_raw - jnp.max(lb_raw, axis=0, keepdims=True))
    return jnp.sum(e[:layer + 1], axis=0, keepdims=True) / jnp.sum(e, axis=0, keepdims=True)


def _hgrn_direction(q_raw, f_raw, v, lb, st_ref, h, mask, *, reverse):
    tb = q_raw.shape[0]
    nc = tb // CHUNK
    q = q_raw * jax.nn.sigmoid(q_raw)
    f = lb + (1.0 - lb) * jax.nn.sigmoid(f_raw)
    logf = jnp.log(f)
    k = 1.0 - f
    b = _chunk_cumsum(logf, reverse)
    b3 = b.reshape(nc, CHUNK, HEAD)
    edge = 0 if reverse else CHUNK - 1
    b_last = b3[:, edge:edge + 1, :]
    qd = (q * jnp.exp(b)).astype(BF16)
    kd = (k * jnp.exp(-b)).astype(BF16)
    k_end = (k.reshape(nc, CHUNK, HEAD) * jnp.exp(b_last - b3)).astype(BF16)
    decay = jnp.exp(b_last)
    vb = v.astype(BF16)

    a = lax.dot_general(qd, kd, (((1,), (1,)), ((), ())), preferred_element_type=F32)
    a = jnp.where(mask, a, 0.0)
    o_intra = jnp.dot(a.astype(BF16), vb, preferred_element_type=F32)

    st = st_ref[h]
    o_inter = [None] * nc
    order = range(nc - 1, -1, -1) if reverse else range(nc)
    for c in order:
        rows = slice(c * CHUNK, (c + 1) * CHUNK)
        o_inter[c] = lax.dot_general(qd[rows], st.astype(BF16), (((1,), (1,)), ((), ())),
                                     preferred_element_type=F32)
        d_st = lax.dot_general(vb[rows], k_end[c], (((0,), (0,)), ((), ())), preferred_element_type=F32)
        st = st * decay[c] + d_st
    st_ref[h] = st
    return o_intra + jnp.concatenate(o_inter, axis=0)


def _intra_mask(tb, reverse):
    r = lax.broadcasted_iota(jnp.int32, (tb, tb), 0)
    c = lax.broadcasted_iota(jnp.int32, (tb, tb), 1)
    same = (r // CHUNK) == (c // CHUNK)
    return same & ((c >= r) if reverse else (c <= r))


def _hgrn_fwd_kernel(q_ref, f_ref, i_ref, lb_ref, o_ref, st_ref, *, layer, heads):
    @pl.when(pl.program_id(0) == 0)
    def _():
        st_ref[...] = jnp.zeros_like(st_ref)

    mask = _intra_mask(q_ref.shape[0], False)

    def head(h, carry):
        hs = pl.ds(pl.multiple_of(h * HEAD, HEAD), HEAD)
        lb = _lower_bound(lb_ref[:, hs], layer)
        o_ref[:, hs] = _hgrn_direction(q_ref[:, hs], f_ref[:, hs], i_ref[:, hs], lb, st_ref, h, mask,
                                       reverse=False)
        return carry

    lax.fori_loop(0, heads, head, 0, unroll=2)


def _hgrn_bwd_kernel(q_ref, f_ref, i_ref, g_ref, of_ref, lb_ref, gain_ref, y_ref, st_ref, *, layer, heads):
    @pl.when(pl.program_id(0) == 0)
    def _():
        st_ref[...] = jnp.zeros_like(st_ref)

    mask = _intra_mask(q_ref.shape[0], True)

    def head(h, carry):
        hs = pl.ds(pl.multiple_of(h * HEAD, HEAD), HEAD)
        lb = _lower_bound(lb_ref[:, hs], layer)
        o = of_ref[:, hs] + _hgrn_direction(q_ref[:, hs], f_ref[:, hs], i_ref[:, hs], lb, st_ref, h, mask,
                                            reverse=True)
        o = o * lax.rsqrt(jnp.mean(o * o, axis=-1, keepdims=True) + EPS) * gain_ref[:, hs]
        g = g_ref[:, hs]
        y_ref[:, hs] = (o * (g * jax.nn.sigmoid(g))).astype(y_ref.dtype)
        return carry

    lax.fori_loop(0, heads, head, 0, unroll=2)


def _hgrn_mixer(proj, lb_fwd, lb_bwd, hgrn_norm, *, layer, width, col0, tb=256):
    t = proj.shape[0]
    heads = width // HEAD
    nb = t // tb
    c = col0 // width
    layers = lb_fwd.shape[0]
    blk_f = 2 * (3 * tb * width * 4 + layers * width * 4 + tb * width * 4) + heads * HEAD * HEAD * 4
    o_f = pl.pallas_call(
        functools.partial(_hgrn_fwd_kernel, layer=layer, heads=heads),
        grid=(nb,),
        in_specs=[pl.BlockSpec((tb, width), lambda b: (b, c)),
                  pl.BlockSpec((tb, width), lambda b: (b, c + 1)),
                  pl.BlockSpec((tb, width), lambda b: (b, c + 3)),
                  pl.BlockSpec((layers, width), lambda b: (0, 0))],
        out_specs=pl.BlockSpec((tb, width), lambda b: (b, 0)),
        out_shape=jax.ShapeDtypeStruct((t, width), F32),
        scratch_shapes=[pltpu.VMEM((heads, HEAD, HEAD), F32)],
        compiler_params=_params(("arbitrary",), _vmem_limit(blk_f)),
        name="hgrn_fwd",
    )(proj, proj, proj, lb_fwd)
    blk_b = 2 * (5 * tb * width * 4 + layers * width * 4 + width * 4 + tb * width * 2) + heads * HEAD * HEAD * 4
    return pl.pallas_call(
        functools.partial(_hgrn_bwd_kernel, layer=layer, heads=heads),
        grid=(nb,),
        in_specs=[pl.BlockSpec((tb, width), lambda b: (nb - 1 - b, c)),
                  pl.BlockSpec((tb, width), lambda b: (nb - 1 - b, c + 2)),
                  pl.BlockSpec((tb, width), lambda b: (nb - 1 - b, c + 3)),
                  pl.BlockSpec((tb, width), lambda b: (nb - 1 - b, c + 4)),
                  pl.BlockSpec((tb, width), lambda b: (nb - 1 - b, 0)),
                  pl.BlockSpec((layers, width), lambda b: (0, 0)),
                  pl.BlockSpec((1, width), lambda b: (0, 0))],
        out_specs=pl.BlockSpec((tb, width), lambda b: (nb - 1 - b, 0)),
        out_shape=jax.ShapeDtypeStruct((t, width), BF16),
        scratch_shapes=[pltpu.VMEM((heads, HEAD, HEAD), F32)],
        compiler_params=_params(("arbitrary",), _vmem_limit(blk_b)),
        name="hgrn_bwd",
    )(proj, proj, proj, proj, o_f, lb_bwd, hgrn_norm.reshape(1, width))


def _xattn_kernel(q_ref, k_ref, v_ref, o_ref, *, scale):
    s = lax.dot_general(q_ref[...], k_ref[...], (((1,), (1,)), ((), ())), preferred_element_type=F32) * scale
    s = s - jnp.max(s, axis=-1, keepdims=True)
    p = jnp.exp(s)
    p = p / jnp.sum(p, axis=-1, keepdims=True)
    o_ref[...] = jnp.dot(p.astype(BF16), v_ref[...], preferred_element_type=F32).astype(o_ref.dtype)


def _xattn(q, k, v, *, tm=512):
    t, d = q.shape
    m = k.shape[0]
    dh = d // XATTN_HEADS
    blk = 2 * (2 * tm * dh * 2 + 2 * m * dh * 2) + 4 * tm * m * 4 + tm * dh * 4
    return pl.pallas_call(
        functools.partial(_xattn_kernel, scale=dh ** -0.5),
        grid=(XATTN_HEADS, t // tm),
        in_specs=[pl.BlockSpec((tm, dh), lambda h, i: (i, h)),
                  pl.BlockSpec((m, dh), lambda h, i: (0, h)),
                  pl.BlockSpec((m, dh), lambda h, i: (0, h))],
        out_specs=pl.BlockSpec((tm, dh), lambda h, i: (i, h)),
        out_shape=jax.ShapeDtypeStruct((t, d), BF16),
        compiler_params=_params(("arbitrary", "arbitrary"), _vmem_limit(blk)),
        name="xattn",
    )(q, k, v)


def _pack_halves(x):
    w = x.shape[1] // 2
    hi = lax.bitcast_convert_type(x[:, :w].astype(BF16).astype(F32), U32)
    lo = lax.bitcast_convert_type(x[:, w:].astype(BF16).astype(F32), U32)
    return hi | (lo >> 16)


def _unpack_halves(p):
    hi = lax.bitcast_convert_type(p & jnp.uint32(0xFFFF0000), F32)
    lo = lax.bitcast_convert_type(p << 16, F32)
    return hi, lo


def _load_row_slabs(flat_ref, s, j0, nj):
    rows = flat_ref.shape[0] // s
    return jnp.concatenate([flat_ref[pl.ds(j0 + j, rows, stride=s), :] for j in range(nj)], axis=1)


def _store_row_slabs(ref, val):
    for j in range(ref.shape[1]):
        ref[:, j, :] = val[:, j * LANES:(j + 1) * LANES]


def _router_kernel(x_ref, g_ref, wr_ref, hp_ref, lg_ref):
    x = x_ref[...]
    ms = jnp.mean(x * x, axis=-1, keepdims=True)
    h = x * lax.rsqrt(ms + EPS) * g_ref[...]
    _store_row_slabs(hp_ref, _pack_halves(h))
    lg_ref[...] = jnp.dot(h, wr_ref[...], preferred_element_type=F32, precision=lax.Precision.HIGHEST)


def _router(x, gain, w_router, *, tm=256):
    t, d = x.shape
    blk = 2 * (tm * d * 4 + d * 4 + d * ROUTER_COLS * 4 + tm * d * 2 + tm * ROUTER_COLS * 4) + 3 * tm * d * 4
    return pl.pallas_call(
        _router_kernel,
        grid=(t // tm,),
        in_specs=[pl.BlockSpec((tm, d), lambda i: (i, 0)),
                  pl.BlockSpec((1, d), lambda i: (0, 0)),
                  pl.BlockSpec((d, ROUTER_COLS), lambda i: (0, 0))],
        out_specs=[pl.BlockSpec((tm, d // 2 // LANES, LANES), lambda i: (i, 0, 0)),
                   pl.BlockSpec((tm, ROUTER_COLS), lambda i: (i, 0))],
        out_shape=[jax.ShapeDtypeStruct((t, d // 2 // LANES, LANES), U32),
                   jax.ShapeDtypeStruct((t, ROUTER_COLS), F32)],
        compiler_params=_params(("arbitrary",), _vmem_limit(blk)),
        name="moe_router",
    )(x, gain.reshape(1, d), w_router)


def _gather_rows_kernel(idx_ref, nrows_ref, src_ref, o_ref, sem, *, rows):
    base = pl.program_id(0) * rows

    def row_copy(src_row, r):
        return pltpu.make_async_copy(src_ref.at[src_row], o_ref.at[r], sem)

    @pl.when(base < nrows_ref[0])
    def _():
        def issue(r, carry):
            row_copy(idx_ref[base + r], r).start()
            return carry

        lax.fori_loop(0, rows, issue, 0, unroll=8)

        def drain(r, carry):
            row_copy(0, r).wait()
            return carry

        lax.fori_loop(0, rows, drain, 0, unroll=8)

    @pl.when(base >= nrows_ref[0])
    def _():
        o_ref[...] = jnp.zeros_like(o_ref)


def _gather_rows(idx, nrows, src, *, rows=256):
    n = idx.shape[0]
    slab = src.shape[1:]
    blk = 2 * rows * slab[0] * slab[1] * 4
    return pl.pallas_call(
        functools.partial(_gather_rows_kernel, rows=rows),
        grid_spec=pltpu.PrefetchScalarGridSpec(
            num_scalar_prefetch=2,
            grid=(n // rows,),
            in_specs=[pl.BlockSpec(memory_space=pl.ANY)],
            out_specs=pl.BlockSpec((rows,) + slab, lambda b, idx, nrows: (b, 0, 0)),
            scratch_shapes=[pltpu.SemaphoreType.DMA(())],
        ),
        out_shape=jax.ShapeDtypeStruct((n,) + slab, src.dtype),
        compiler_params=_params(("arbitrary",), _vmem_limit(blk)),
        name="gather_rows",
    )(idx, nrows, src)


BLOCK_COPY_PRIORITY = 1


def _expert_block_loop(nblk, in_copy, out_copy, prologue, compute):
    in_copy(0, 0).start(priority=BLOCK_COPY_PRIORITY)
    prologue()

    def body(k, carry):
        slot = k % 2

        @pl.when(k + 1 < nblk)
        def _():
            in_copy(k + 1, 1 - slot).start(priority=BLOCK_COPY_PRIORITY)

        in_copy(k, slot).wait()

        @pl.when(k >= 2)
        def _():
            out_copy(k - 2, slot).wait()

        compute(slot)
        out_copy(k, slot).start(priority=BLOCK_COPY_PRIORITY)
        return carry

    lax.fori_loop(0, nblk, body, 0)

    @pl.when(nblk >= 2)
    def _():
        out_copy(nblk - 2, nblk % 2).wait()

    out_copy(nblk - 1, (nblk - 1) % 2).wait()


def _moe_up_kernel(cnt_ref, start_ref, xs_ref, w1_ref, w3_ref, h_ref, w1b_ref, w3b_ref, xbuf, obuf, xsem, osem,
                   *, bm):
    f = pl.program_id(0)
    e = pl.program_id(1)
    nblk = cnt_ref[e]
    blk0 = start_ref[e]
    d, fc = w1b_ref.shape
    nslab = d // 2 // LANES
    xrows = bm * nslab

    def x_copy(k, slot):
        row0 = pl.multiple_of((blk0 + k) * xrows, xrows)
        return pltpu.make_async_copy(xs_ref.at[pl.ds(row0, xrows)], xbuf.at[slot], xsem.at[slot])

    def h_copy(k, slot):
        row0 = pl.multiple_of((blk0 + k) * bm, bm)
        col0 = pl.multiple_of(f * fc, fc)
        return pltpu.make_async_copy(obuf.at[slot], h_ref.at[pl.ds(row0, bm), pl.ds(col0, fc)], osem.at[slot])

    def cast_weights():
        w1b_ref[...] = w1_ref[0].astype(BF16)
        w3b_ref[...] = w3_ref[0].astype(BF16)

    def compute(slot):
        hi, lo = _unpack_halves(_load_row_slabs(xbuf.at[slot], nslab, 0, nslab))
        hi = hi.astype(BF16)
        lo = lo.astype(BF16)
        d2 = d // 2
        a = (jnp.dot(hi, w1b_ref[:d2, :], preferred_element_type=F32)
             + jnp.dot(lo, w1b_ref[d2:, :], preferred_element_type=F32))
        c = (jnp.dot(hi, w3b_ref[:d2, :], preferred_element_type=F32)
             + jnp.dot(lo, w3b_ref[d2:, :], preferred_element_type=F32))
        obuf[slot] = (a * jax.nn.sigmoid(a) * c).astype(obuf.dtype)

    @pl.when(nblk > 0)
    def _():
        _expert_block_loop(nblk, x_copy, h_copy, cast_weights, compute)


def _moe_down_kernel(cnt_ref, start_ref, h_ref, w2_ref, y_ref, w2b_ref, hbuf, obuf, hsem, osem, *, bm):
    c = pl.program_id(0)
    e = pl.program_id(1)
    nblk = cnt_ref[e]
    blk0 = start_ref[e]
    nj = obuf.shape[2]

    def h_copy(k, slot):
        row0 = pl.multiple_of((blk0 + k) * bm, bm)
        return pltpu.make_async_copy(h_ref.at[pl.ds(row0, bm)], hbuf.at[slot], hsem.at[slot])

    def y_copy(k, slot):
        row0 = pl.multiple_of((blk0 + k) * bm, bm)
        j0 = pl.multiple_of(c * nj, nj)
        return pltpu.make_async_copy(obuf.at[slot], y_ref.at[pl.ds(row0, bm), pl.ds(j0, nj)], osem.at[slot])

    def cast_weights():
        w2b_ref[...] = w2_ref[0].astype(BF16)

    def compute(slot):
        y = jnp.dot(hbuf[slot], w2b_ref[...], preferred_element_type=F32)
        _store_row_slabs(obuf.at[slot], _pack_halves(y))

    @pl.when(nblk > 0)
    def _():
        _expert_block_loop(nblk, h_copy, y_copy, cast_weights, compute)


def _moe_experts(xs, blk_cnt, blk_start, w1, w3, w2, *, bm, fc=512, nc=2048):
    p, nslab, _ = xs.shape
    d = 2 * nslab * LANES
    ne, _, de = w1.shape
    any_spec = pl.BlockSpec(memory_space=pl.ANY)
    blk_up = (4 * d * fc * 4 + 2 * d * fc * 2 + 2 * bm * nslab * LANES * 4 + 2 * bm * fc * 2
              + 2 * bm * d * 2 + 3 * bm * fc * 4)
    h = pl.pallas_call(
        functools.partial(_moe_up_kernel, bm=bm),
        grid_spec=pltpu.PrefetchScalarGridSpec(
            num_scalar_prefetch=2,
            grid=(de // fc, ne),
            in_specs=[any_spec,
                      pl.BlockSpec((1, d, fc), lambda f, e, cnt, start: (e, 0, f)),
                      pl.BlockSpec((1, d, fc), lambda f, e, cnt, start: (e, 0, f))],
            out_specs=any_spec,
            scratch_shapes=[pltpu.VMEM((d, fc), BF16), pltpu.VMEM((d, fc), BF16),
                            pltpu.VMEM((2, bm * nslab, LANES), U32), pltpu.VMEM((2, bm, fc), BF16),
                            pltpu.SemaphoreType.DMA((2,)), pltpu.SemaphoreType.DMA((2,))],
        ),
        out_shape=jax.ShapeDtypeStruct((p, de), BF16),
        compiler_params=_params(("arbitrary", "arbitrary"), _vmem_limit(blk_up)),
        name="moe_up",
    )(blk_cnt, blk_start, xs.reshape(p * nslab, LANES), w1, w3)
    nj = nc // 2 // LANES
    blk_dn = 2 * de * nc * 4 + de * nc * 2 + 2 * bm * de * 2 + 2 * bm * nj * LANES * 4 + 2 * bm * nc * 4
    return pl.pallas_call(
        functools.partial(_moe_down_kernel, bm=bm),
        grid_spec=pltpu.PrefetchScalarGridSpec(
            num_scalar_prefetch=2,
            grid=(d // nc, ne),
            in_specs=[any_spec,
                      pl.BlockSpec((1, de, nc), lambda c, e, cnt, start: (e, 0, c))],
            out_specs=any_spec,
            scratch_shapes=[pltpu.VMEM((de, nc), BF16),
                            pltpu.VMEM((2, bm, de), BF16), pltpu.VMEM((2, bm, nj, LANES), U32),
                            pltpu.SemaphoreType.DMA((2,)), pltpu.SemaphoreType.DMA((2,))],
        ),
        out_shape=jax.ShapeDtypeStruct((p, nslab, LANES), U32),
        compiler_params=_params(("arbitrary", "arbitrary"), _vmem_limit(blk_dn)),
        name="moe_down",
    )(blk_cnt, blk_start, h, w2)


def _combine_kernel(x_ref, y0_ref, y1_ref, ew_ref, g_ref, o_ref, *, nc):
    ew = ew_ref[...]
    w0 = ew[:, 0:1]
    w1 = ew[:, 1:2]
    half = nc // 2
    nslab = x_ref.shape[1] // 2 // LANES
    pieces = []
    ss = None
    for c in range(x_ref.shape[1] // nc):
        hi0, lo0 = _unpack_halves(_load_row_slabs(y0_ref, nslab, c * half // LANES, half // LANES))
        hi1, lo1 = _unpack_halves(_load_row_slabs(y1_ref, nslab, c * half // LANES, half // LANES))
        for j, (p0, p1) in enumerate(((hi0, hi1), (lo0, lo1))):
            cols = slice(c * nc + j * half, c * nc + (j + 1) * half)
            z = x_ref[:, cols] + w0 * p0 + w1 * p1
            pieces.append((cols, z))
            s = jnp.sum(z * z, axis=-1, keepdims=True)
            ss = s if ss is None else ss + s
    inv = lax.rsqrt(ss / x_ref.shape[1] + EPS)
    for cols, z in pieces:
        o_ref[:, cols] = z * inv * g_ref[:, cols]


def _combine(x, yu, e_w, gain, *, nc, tm=256):
    t, d = x.shape
    nt = t // tm
    nslab = yu.shape[1]
    yu2 = yu.reshape(yu.shape[0] * nslab, LANES)
    blk = 2 * (2 * tm * d * 4 + 2 * tm * d * 2 + tm * LANES * 4 + d * 4) + 3 * tm * d * 4
    return pl.pallas_call(
        functools.partial(_combine_kernel, nc=nc),
        grid=(nt,),
        in_specs=[pl.BlockSpec((tm, d), lambda i: (i, 0)),
                  pl.BlockSpec((tm * nslab, LANES), lambda i: (i, 0)),
                  pl.BlockSpec((tm * nslab, LANES), lambda i: (i + nt, 0)),
                  pl.BlockSpec((tm, TOP_K), lambda i: (i, 0)),
                  pl.BlockSpec((1, d), lambda i: (0, 0))],
        out_specs=pl.BlockSpec((tm, d), lambda i: (i, 0)),
        out_shape=jax.ShapeDtypeStruct((t, d), F32),
        compiler_params=_params(("arbitrary",), _vmem_limit(blk)),
        name="moe_combine",
    )(x, yu2, yu2, e_w, gain.reshape(1, d))


def _route(logits, *, bm):
    t = logits.shape[0]
    g_logits = logits[:, :N_GROUPS]
    e_logits = logits[:, N_GROUPS:N_GROUPS + N_EXPERTS].reshape(t, N_GROUPS, EXPERTS_PER_GROUP)
    g_idx = jnp.argmax(g_logits, axis=-1).astype(jnp.int32)
    g_w = jnp.take_along_axis(jax.nn.softmax(g_logits, axis=-1), g_idx[:, None], axis=-1)
    e_sel = jnp.take_along_axis(e_logits, g_idx[:, None, None], axis=1)[:, 0]
    top_v, top_i = lax.top_k(e_sel, TOP_K)
    e_w = jax.nn.softmax(top_v, axis=-1) * g_w
    eid = g_idx[:, None] * EXPERTS_PER_GROUP + top_i.astype(jnp.int32)

    a = t * TOP_K
    flat_e = eid.reshape(a)
    onehot = (flat_e[:, None] == jnp.arange(N_EXPERTS, dtype=jnp.int32)[None, :]).astype(jnp.int32)
    csum = jnp.cumsum(onehot, axis=0)
    rank = jnp.take_along_axis(csum, flat_e[:, None], axis=1)[:, 0] - 1
    counts = csum[-1]
    padded = ((counts + bm - 1) // bm) * bm
    pad_end = jnp.cumsum(padded)
    start_pad = pad_end - padded
    dest = (start_pad[flat_e] + rank).astype(jnp.int32)
    p = a + N_EXPERTS * bm
    blk_cnt = (padded // bm).astype(jnp.int32)
    blk_start = (start_pad // bm).astype(jnp.int32)
    rows_used = pad_end[-1:].astype(jnp.int32)
    src_tok = (jnp.arange(p, dtype=jnp.int32) % t).at[dest].set(jnp.arange(a, dtype=jnp.int32) // TOP_K)
    back = dest.reshape(t, TOP_K).T.reshape(a)
    return e_w, src_tok, back, blk_cnt, blk_start, rows_used


MOE_BLOCK_ROWS = 256
MOE_DOWN_COLS = 2048


def kernel(x, mem, norm_mix, w_in, pool_w, pool_scale, lb_fwd, lb_bwd, hgrn_norm, w_out, norm_xattn, norm_mem,
           w_q, w_k, w_v, w_o, norm_moe, w_router_group, w_router_expert, w1, w3, w2, norm_final):
    bsz, seq, d = x.shape
    assert bsz == 1 and w_in.shape[0] == 1, "one sequence, one layer (the final norm is fused into the MoE combine)"
    l = 0
    pool_width = pool_w.shape[1] * pool_w.shape[2]
    hgrn_width = hgrn_norm.shape[1]
    xt = x.reshape(seq, d)
    mem_t = mem.reshape(mem.shape[1], d)
    h = _rmsnorm(xt, norm_mix[l], BF16)
    proj = _matmul([h], w_in[l], out_dtype=F32)
    y_pool = _pool_mixer(proj, pool_w[l], pool_scale[l])
    y_hgrn = _hgrn_mixer(proj, lb_fwd, lb_bwd, hgrn_norm[l], layer=l, width=hgrn_width, col0=pool_width)
    xt = _matmul([y_pool, y_hgrn], w_out[l], resid=xt, out_dtype=F32)
    mem_n = _rmsnorm(mem_t, norm_mem[l], BF16)
    hx = _rmsnorm(xt, norm_xattn[l], BF16)
    q = _matmul([hx], w_q[l], out_dtype=BF16)
    k = _matmul([mem_n], w_k[l], out_dtype=BF16)
    v = _matmul([mem_n], w_v[l], out_dtype=BF16)
    o = _xattn(q, k, v)
    xt = _matmul([o], w_o[l], resid=xt, out_dtype=F32)
    w_router = jnp.concatenate(
        [w_router_group[l],
         jnp.transpose(w_router_expert[l], (1, 0, 2)).reshape(d, N_EXPERTS),
         jnp.zeros((d, ROUTER_COLS - N_GROUPS - N_EXPERTS), F32)], axis=1)
    hp, logits = _router(xt, norm_moe[l], w_router)
    e_w, src_tok, back, blk_cnt, blk_start, rows_used = _route(logits, bm=MOE_BLOCK_ROWS)
    xs = _gather_rows(src_tok, rows_used, hp)
    ys = _moe_experts(xs, blk_cnt, blk_start, w1[l], w3[l], w2[l], bm=MOE_BLOCK_ROWS, nc=MOE_DOWN_COLS)
    yu = _gather_rows(back, jnp.full((1,), back.shape[0], jnp.int32), ys)
    out = _combine(xt, yu, e_w, norm_final, nc=MOE_DOWN_COLS)
    return out.reshape(bsz, seq, d)
```

```python
import functools

import jax
import jax.numpy as jnp
from jax import lax
from jax.experimental import pallas as pl
from jax.experimental.pallas import tpu as pltpu

F32 = jnp.float32
BF16 = jnp.bfloat16
U32 = jnp.uint32

EPS = 1e-6
LANES = 128
SUBLANES = 8
VMEM_BYTES_V7X = 64 * 1024 * 1024

POOL_WINDOWS = (2, 4, 8, 16)
HEAD = 128
CHUNK = 64
XATTN_HEADS = 4
N_GROUPS = 4
EXPERTS_PER_GROUP = 8
N_EXPERTS = N_GROUPS * EXPERTS_PER_GROUP
TOP_K = 2
ROUTER_COLS = LANES


def _params(sem, vmem_bytes):
    return pltpu.CompilerParams(dimension_semantics=sem, vmem_limit_bytes=int(vmem_bytes))


def _vmem_limit(block_bytes):
    return min(int(block_bytes * 1.25) + (6 << 20), VMEM_BYTES_V7X - (4 << 20))


def _rmsnorm_kernel(x_ref, g_ref, o_ref):
    x = x_ref[...]
    ms = jnp.mean(x * x, axis=-1, keepdims=True)
    o_ref[...] = (x * lax.rsqrt(ms + EPS) * g_ref[...]).astype(o_ref.dtype)


def _rmsnorm(x, gain, out_dtype, tm=256):
    t, d = x.shape
    blk = tm * d * (4 + jnp.dtype(out_dtype).itemsize) * 2
    return pl.pallas_call(
        _rmsnorm_kernel,
        grid=(t // tm,),
        in_specs=[pl.BlockSpec((tm, d), lambda i: (i, 0)),
                  pl.BlockSpec((1, d), lambda i: (0, 0))],
        out_specs=pl.BlockSpec((tm, d), lambda i: (i, 0)),
        out_shape=jax.ShapeDtypeStruct((t, d), out_dtype),
        compiler_params=_params(("arbitrary",), _vmem_limit(blk)),
        name="rmsnorm",
    )(x, gain.reshape(1, d))


def _mm_kernel(*refs, n_a, has_resid):
    a_refs = refs[:n_a]
    w_ref = refs[n_a]
    r_ref = refs[n_a + 1] if has_resid else None
    o_ref, wb_ref = refs[-2], refs[-1]

    @pl.when(pl.program_id(1) == 0)
    def _():
        wb_ref[...] = w_ref[...].astype(BF16)

    acc = None
    k0 = 0
    for a_ref in a_refs:
        kk = a_ref.shape[1]
        part = jnp.dot(a_ref[...], wb_ref[k0:k0 + kk, :], preferred_element_type=F32)
        acc = part if acc is None else acc + part
        k0 += kk
    if has_resid:
        acc = acc + r_ref[...]
    o_ref[...] = acc.astype(o_ref.dtype)


def _matmul(a_parts, w, *, resid=None, out_dtype=F32, tm=1024, tn=512):
    m = a_parts[0].shape[0]
    k, n = w.shape
    assert sum(a.shape[1] for a in a_parts) == k
    tm = min(tm, m)
    in_specs = [pl.BlockSpec((tm, a.shape[1]), lambda j, i: (i, 0)) for a in a_parts]
    in_specs.append(pl.BlockSpec((k, tn), lambda j, i: (0, j)))
    args = list(a_parts) + [w]
    if resid is not None:
        in_specs.append(pl.BlockSpec((tm, tn), lambda j, i: (i, j)))
        args.append(resid)
    osz = jnp.dtype(out_dtype).itemsize
    blk = (2 * tm * k * 2 + 2 * k * tn * 4 + k * tn * 2 + 2 * tm * tn * osz
           + (2 * tm * tn * 4 if resid is not None else 0) + tm * tn * 4)
    return pl.pallas_call(
        functools.partial(_mm_kernel, n_a=len(a_parts), has_resid=resid is not None),
        grid=(n // tn, m // tm),
        in_specs=in_specs,
        out_specs=pl.BlockSpec((tm, tn), lambda j, i: (i, j)),
        out_shape=jax.ShapeDtypeStruct((m, n), out_dtype),
        scratch_shapes=[pltpu.VMEM((k, tn), BF16)],
        compiler_params=_params(("arbitrary", "arbitrary"), _vmem_limit(blk)),
        name="matmul",
    )(*args)


def _pool_kernel(u_ref, up_ref, un_ref, w_ref, s_ref, o_ref, *, tm, seq, group):
    i = pl.program_id(0)
    last = pl.num_programs(0) - 1
    n = tm + 2 * SUBLANES
    row = lax.broadcasted_iota(jnp.int32, (tm, 1), 0) + i * tm
    for gi, w in enumerate(POOL_WINDOWS):
        cs = slice(gi * group, (gi + 1) * group)
        u = u_ref[:, cs]
        prev = jnp.where(i > 0, up_ref[:, cs], 0.0)
        nxt = jnp.where(i < last, un_ref[:, cs], 0.0)
        f = jnp.concatenate([prev, u, nxt], axis=0)
        step = 1
        while step < w:
            f = f + pltpu.roll(f, n - step, 0)
            step *= 2
        half = w // 2
        first = SUBLANES - half
        if first:
            f = pltpu.roll(f, n - first, 0)
        win = f[:tm]
        lo = jnp.maximum(row - half, 0)
        hi = jnp.minimum(row + half - 1, seq - 1)
        cnt = (hi - lo + 1).astype(F32)
        mixed = win / cnt - u
        y = jnp.dot(mixed.astype(BF16), w_ref[gi].astype(BF16), preferred_element_type=F32)
        o_ref[:, cs] = (y * s_ref[:, cs]).astype(o_ref.dtype)


def _pool_mixer(proj, pool_w, pool_scale, *, tm=512):
    t = proj.shape[0]
    ng, group, _ = pool_w.shape
    width = ng * group
    rpb = tm // SUBLANES
    nhalo = t // SUBLANES
    blk = 2 * (tm * width * 4 + 2 * SUBLANES * width * 4 + ng * group * group * 4 + tm * width * 2) + 6 * tm * group * 4
    return pl.pallas_call(
        functools.partial(_pool_kernel, tm=tm, seq=t, group=group),
        grid=(t // tm,),
        in_specs=[pl.BlockSpec((tm, width), lambda i: (i, 0)),
                  pl.BlockSpec((SUBLANES, width), lambda i: (jnp.maximum(i * rpb - 1, 0), 0)),
                  pl.BlockSpec((SUBLANES, width), lambda i: (jnp.minimum((i + 1) * rpb, nhalo - 1), 0)),
                  pl.BlockSpec((ng, group, group), lambda i: (0, 0, 0)),
                  pl.BlockSpec((1, width), lambda i: (0, 0))],
        out_specs=pl.BlockSpec((tm, width), lambda i: (i, 0)),
        out_shape=jax.ShapeDtypeStruct((t, width), BF16),
        compiler_params=_params(("arbitrary",), _vmem_limit(blk)),
        name="pool_mixer",
    )(proj, proj, proj, pool_w, pool_scale.reshape(1, width))


def _chunk_cumsum(x, reverse):
    n = x.shape[0]
    pos = lax.broadcasted_iota(jnp.int32, x.shape, 0) % CHUNK
    s = 1
    while s < CHUNK:
        if reverse:
            x = x + jnp.where(pos < CHUNK - s, pltpu.roll(x, n - s, 0), 0.0)
        else:
            x = x + jnp.where(pos >= s, pltpu.roll(x, s, 0), 0.0)
        s *= 2
    return x


def _lower_bound(lb_raw, layer):
    e = jnp.exp(lb_raw - jnp.max(lb_raw, axis=0, keepdims=True))
    return jnp.sum(e[:layer + 1], axis=0, keepdims=True) / jnp.sum(e, axis=0, keepdims=True)


def _hgrn_direction(q_raw, f_raw, v, lb, st_ref, h, mask, *, reverse):
    tb = q_raw.shape[0]
    nc = tb // CHUNK
    q = q_raw * jax.nn.sigmoid(q_raw)
    f = lb + (1.0 - lb) * jax.nn.sigmoid(f_raw)
    logf = jnp.log(f)
    k = 1.0 - f
    b = _chunk_cumsum(logf, reverse)
    b3 = b.reshape(nc, CHUNK, HEAD)
    edge = 0 if reverse else CHUNK - 1
    b_last = b3[:, edge:edge + 1, :]
    qd = (q * jnp.exp(b)).astype(BF16)
    kd32 = k * jnp.exp(-b)
    kd = kd32.astype(BF16)
    decay = jnp.exp(b_last)
    k_end = (kd32.reshape(nc, CHUNK, HEAD) * decay).astype(BF16)
    vb = v.astype(BF16)

    a = lax.dot_general(qd, kd, (((1,), (1,)), ((), ())), preferred_element_type=F32)
    a = jnp.where(mask, a, 0.0)
    o_intra = jnp.dot(a.astype(BF16), vb, preferred_element_type=F32)

    st = st_ref[h]
    o_inter = [None] * nc
    order = range(nc - 1, -1, -1) if reverse else range(nc)
    for c in order:
        rows = slice(c * CHUNK, (c + 1) * CHUNK)
        o_inter[c] = lax.dot_general(qd[rows], st.astype(BF16), (((1,), (1,)), ((), ())),
                                     preferred_element_type=F32)
        d_st = lax.dot_general(vb[rows], k_end[c], (((0,), (0,)), ((), ())), preferred_element_type=F32)
        st = st * decay[c] + d_st
    st_ref[h] = st
    return o_intra + jnp.concatenate(o_inter, axis=0)


def _intra_mask(tb, reverse):
    r = lax.broadcasted_iota(jnp.int32, (tb, tb), 0)
    c = lax.broadcasted_iota(jnp.int32, (tb, tb), 1)
    same = (r // CHUNK) == (c // CHUNK)
    return same & ((c >= r) if reverse else (c <= r))


def _hgrn_fwd_kernel(q_ref, f_ref, i_ref, lb_ref, o_ref, st_ref, *, layer, heads):
    @pl.when(pl.program_id(0) == 0)
    def _():
        st_ref[...] = jnp.zeros_like(st_ref)

    mask = _intra_mask(q_ref.shape[0], False)

    def head(h, carry):
        hs = pl.ds(pl.multiple_of(h * HEAD, HEAD), HEAD)
        lb = _lower_bound(lb_ref[:, hs], layer)
        o_ref[:, hs] = _hgrn_direction(q_ref[:, hs], f_ref[:, hs], i_ref[:, hs], lb, st_ref, h, mask,
                                       reverse=False)
        return carry

    lax.fori_loop(0, heads, head, 0, unroll=2)


def _hgrn_bwd_kernel(q_ref, f_ref, i_ref, g_ref, of_ref, lb_ref, gain_ref, y_ref, st_ref, *, layer, heads):
    @pl.when(pl.program_id(0) == 0)
    def _():
        st_ref[...] = jnp.zeros_like(st_ref)

    mask = _intra_mask(q_ref.shape[0], True)

    def head(h, carry):
        hs = pl.ds(pl.multiple_of(h * HEAD, HEAD), HEAD)
        lb = _lower_bound(lb_ref[:, hs], layer)
        o = of_ref[:, hs] + _hgrn_direction(q_ref[:, hs], f_ref[:, hs], i_ref[:, hs], lb, st_ref, h, mask,
                                            reverse=True)
        o = o * lax.rsqrt(jnp.mean(o * o, axis=-1, keepdims=True) + EPS) * gain_ref[:, hs]
        g = g_ref[:, hs]
        y_ref[:, hs] = (o * (g * jax.nn.sigmoid(g))).astype(y_ref.dtype)
        return carry

    lax.fori_loop(0, heads, head, 0, unroll=2)


def _hgrn_mixer(proj, lb_fwd, lb_bwd, hgrn_norm, *, layer, width, col0, tb=256):
    t = proj.shape[0]
    heads = width // HEAD
    nb = t // tb
    c = col0 // width
    layers = lb_fwd.shape[0]
    blk_f = 2 * (3 * tb * width * 4 + layers * width * 4 + tb * width * 4) + heads * HEAD * HEAD * 4
    o_f = pl.pallas_call(
        functools.partial(_hgrn_fwd_kernel, layer=layer, heads=heads),
        grid=(nb,),
        in_specs=[pl.BlockSpec((tb, width), lambda b: (b, c)),
                  pl.BlockSpec((tb, width), lambda b: (b, c + 1)),
                  pl.BlockSpec((tb, width), lambda b: (b, c + 3)),
                  pl.BlockSpec((layers, width), lambda b: (0, 0))],
        out_specs=pl.BlockSpec((tb, width), lambda b: (b, 0)),
        out_shape=jax.ShapeDtypeStruct((t, width), F32),
        scratch_shapes=[pltpu.VMEM((heads, HEAD, HEAD), F32)],
        compiler_params=_params(("arbitrary",), _vmem_limit(blk_f)),
        name="hgrn_fwd",
    )(proj, proj, proj, lb_fwd)
    blk_b = 2 * (5 * tb * width * 4 + layers * width * 4 + width * 4 + tb * width * 2) + heads * HEAD * HEAD * 4
    return pl.pallas_call(
        functools.partial(_hgrn_bwd_kernel, layer=layer, heads=heads),
        grid=(nb,),
        in_specs=[pl.BlockSpec((tb, width), lambda b: (nb - 1 - b, c)),
                  pl.BlockSpec((tb, width), lambda b: (nb - 1 - b, c + 2)),
                  pl.BlockSpec((tb, width), lambda b: (nb - 1 - b, c + 3)),
                  pl.BlockSpec((tb, width), lambda b: (nb - 1 - b, c + 4)),
                  pl.BlockSpec((tb, width), lambda b: (nb - 1 - b, 0)),
                  pl.BlockSpec((layers, width), lambda b: (0, 0)),
                  pl.BlockSpec((1, width), lambda b: (0, 0))],
        out_specs=pl.BlockSpec((tb, width), lambda b: (nb - 1 - b, 0)),
        out_shape=jax.ShapeDtypeStruct((t, width), BF16),
        scratch_shapes=[pltpu.VMEM((heads, HEAD, HEAD), F32)],
        compiler_params=_params(("arbitrary",), _vmem_limit(blk_b)),
        name="hgrn_bwd",
    )(proj, proj, proj, proj, o_f, lb_bwd, hgrn_norm.reshape(1, width))


def _xattn_kernel(q_ref, k_ref, v_ref, o_ref, *, scale):
    s = lax.dot_general(q_ref[...], k_ref[...], (((1,), (1,)), ((), ())), preferred_element_type=F32) * scale
    s = s - jnp.max(s, axis=-1, keepdims=True)
    p = jnp.exp(s)
    p = p / jnp.sum(p, axis=-1, keepdims=True)
    o_ref[...] = jnp.dot(p.astype(BF16), v_ref[...], preferred_element_type=F32).astype(o_ref.dtype)


def _xattn(q, k, v, *, tm=512):
    t, d = q.shape
    m = k.shape[0]
    dh = d // XATTN_HEADS
    blk = 2 * (2 * tm * dh * 2 + 2 * m * dh * 2) + 4 * tm * m * 4 + tm * dh * 4
    return pl.pallas_call(
        functools.partial(_xattn_kernel, scale=dh ** -0.5),
        grid=(XATTN_HEADS, t // tm),
        in_specs=[pl.BlockSpec((tm, dh), lambda h, i: (i, h)),
                  pl.BlockSpec((m, dh), lambda h, i: (0, h)),
                  pl.BlockSpec((m, dh), lambda h, i: (0, h))],
        out_specs=pl.BlockSpec((tm, dh), lambda h, i: (i, h)),
        out_shape=jax.ShapeDtypeStruct((t, d), BF16),
        compiler_params=_params(("arbitrary", "arbitrary"), _vmem_limit(blk)),
        name="xattn",
    )(q, k, v)


def _pack_halves(x):
    w = x.shape[1] // 2
    hi = lax.bitcast_convert_type(x[:, :w].astype(BF16).astype(F32), U32)
    lo = lax.bitcast_convert_type(x[:, w:].astype(BF16).astype(F32), U32)
    return hi | (lo >> 16)


def _unpack_halves(p):
    hi = lax.bitcast_convert_type(p & jnp.uint32(0xFFFF0000), F32)
    lo = lax.bitcast_convert_type(p << 16, F32)
    return hi, lo


def _load_row_slabs(flat_ref, s, j0, nj):
    rows = flat_ref.shape[0] // s
    return jnp.concatenate([flat_ref[pl.ds(j0 + j, rows, stride=s), :] for j in range(nj)], axis=1)


def _store_row_slabs(ref, val):
    for j in range(ref.shape[1]):
        ref[:, j, :] = val[:, j * LANES:(j + 1) * LANES]


def _router_kernel(x_ref, g_ref, wr_ref, hp_ref, lg_ref):
    x = x_ref[...]
    ms = jnp.mean(x * x, axis=-1, keepdims=True)
    h = x * lax.rsqrt(ms + EPS) * g_ref[...]
    _store_row_slabs(hp_ref, _pack_halves(h))
    w = wr_ref[...]
    h_hi = h.astype(BF16)
    h_lo = (h - h_hi.astype(F32)).astype(BF16)
    w_hi = w.astype(BF16)
    w_lo = (w - w_hi.astype(F32)).astype(BF16)
    lg_ref[...] = (jnp.dot(h_hi, w_hi, preferred_element_type=F32)
                   + (jnp.dot(h_hi, w_lo, preferred_element_type=F32)
                      + jnp.dot(h_lo, w_hi, preferred_element_type=F32)))


def _router(x, gain, w_router, *, tm=256):
    t, d = x.shape
    blk = 2 * (tm * d * 4 + d * 4 + d * ROUTER_COLS * 4 + tm * d * 2 + tm * ROUTER_COLS * 4) + 3 * tm * d * 4
    return pl.pallas_call(
        _router_kernel,
        grid=(t // tm,),
        in_specs=[pl.BlockSpec((tm, d), lambda i: (i, 0)),
                  pl.BlockSpec((1, d), lambda i: (0, 0)),
                  pl.BlockSpec((d, ROUTER_COLS), lambda i: (0, 0))],
        out_specs=[pl.BlockSpec((tm, d // 2 // LANES, LANES), lambda i: (i, 0, 0)),
                   pl.BlockSpec((tm, ROUTER_COLS), lambda i: (i, 0))],
        out_shape=[jax.ShapeDtypeStruct((t, d // 2 // LANES, LANES), U32),
                   jax.ShapeDtypeStruct((t, ROUTER_COLS), F32)],
        compiler_params=_params(("arbitrary",), _vmem_limit(blk)),
        name="moe_router",
    )(x, gain.reshape(1, d), w_router)


def _gather_rows_kernel(idx_ref, nrows_ref, src_ref, o_ref, sem, *, rows):
    base = pl.program_id(0) * rows

    def row_copy(src_row, r):
        return pltpu.make_async_copy(src_ref.at[src_row], o_ref.at[r], sem)

    @pl.when(base < nrows_ref[0])
    def _():
        def issue(r, carry):
            row_copy(idx_ref[base + r], r).start()
            return carry

        lax.fori_loop(0, rows, issue, 0, unroll=8)

        def drain(r, carry):
            row_copy(0, r).wait()
            return carry

        lax.fori_loop(0, rows, drain, 0, unroll=8)

    @pl.when(base >= nrows_ref[0])
    def _():
        o_ref[...] = jnp.zeros_like(o_ref)


def _gather_rows(idx, nrows, src, *, rows=256):
    n = idx.shape[0]
    slab = src.shape[1:]
    blk = 2 * rows * slab[0] * slab[1] * 4
    return pl.pallas_call(
        functools.partial(_gather_rows_kernel, rows=rows),
        grid_spec=pltpu.PrefetchScalarGridSpec(
            num_scalar_prefetch=2,
            grid=(n // rows,),
            in_specs=[pl.BlockSpec(memory_space=pltpu.HBM)],
            out_specs=pl.BlockSpec((rows,) + slab, lambda b, idx, nrows: (b, 0, 0)),
            scratch_shapes=[pltpu.SemaphoreType.DMA(())],
        ),
        out_shape=jax.ShapeDtypeStruct((n,) + slab, src.dtype),
        compiler_params=_params(("arbitrary",), _vmem_limit(blk)),
        name="gather_rows",
    )(idx, nrows, src)


BLOCK_COPY_PRIORITY = 1


def _expert_block_loop(nblk, in_copy, out_copy, prologue, compute):
    in_copy(0, 0).start(priority=BLOCK_COPY_PRIORITY)
    prologue()

    def body(k, carry):
        slot = k % 2

        @pl.when(k + 1 < nblk)
        def _():
            in_copy(k + 1, 1 - slot).start(priority=BLOCK_COPY_PRIORITY)

        in_copy(k, slot).wait()

        @pl.when(k >= 2)
        def _():
            out_copy(k - 2, slot).wait()

        compute(slot)
        out_copy(k, slot).start(priority=BLOCK_COPY_PRIORITY)
        return carry

    lax.fori_loop(0, nblk, body, 0)

    @pl.when(nblk >= 2)
    def _():
        out_copy(nblk - 2, nblk % 2).wait()

    out_copy(nblk - 1, (nblk - 1) % 2).wait()


def _moe_up_kernel(cnt_ref, start_ref, xs_ref, w1_ref, w3_ref, h_ref, w1b_ref, w3b_ref, xbuf, obuf, xsem, osem,
                   *, bm):
    f = pl.program_id(0)
    e = pl.program_id(1)
    nblk = cnt_ref[e]
    blk0 = start_ref[e]
    d, fc = w1b_ref.shape
    nslab = d // 2 // LANES
    xrows = bm * nslab

    def x_copy(k, slot):
        row0 = pl.multiple_of((blk0 + k) * xrows, xrows)
        return pltpu.make_async_copy(xs_ref.at[pl.ds(row0, xrows)], xbuf.at[slot], xsem.at[slot])

    def h_copy(k, slot):
        row0 = pl.multiple_of((blk0 + k) * bm, bm)
        col0 = pl.multiple_of(f * fc, fc)
        return pltpu.make_async_copy(obuf.at[slot], h_ref.at[pl.ds(row0, bm), pl.ds(col0, fc)], osem.at[slot])

    def cast_weights():
        w1b_ref[...] = w1_ref[0].astype(BF16)
        w3b_ref[...] = w3_ref[0].astype(BF16)

    def compute(slot):
        hi, lo = _unpack_halves(_load_row_slabs(xbuf.at[slot], nslab, 0, nslab))
        hi = hi.astype(BF16)
        lo = lo.astype(BF16)
        d2 = d // 2
        a = (jnp.dot(hi, w1b_ref[:d2, :], preferred_element_type=F32)
             + jnp.dot(lo, w1b_ref[d2:, :], preferred_element_type=F32))
        c = (jnp.dot(hi, w3b_ref[:d2, :], preferred_element_type=F32)
             + jnp.dot(lo, w3b_ref[d2:, :], preferred_element_type=F32))
        obuf[slot] = (a * jax.nn.sigmoid(a) * c).astype(obuf.dtype)

    @pl.when(nblk > 0)
    def _():
        _expert_block_loop(nblk, x_copy, h_copy, cast_weights, compute)


def _moe_down_kernel(cnt_ref, start_ref, h_ref, w2_ref, y_ref, w2b_ref, hbuf, obuf, hsem, osem, *, bm):
    c = pl.program_id(0)
    e = pl.program_id(1)
    nblk = cnt_ref[e]
    blk0 = start_ref[e]
    nj = obuf.shape[2]

    def h_copy(k, slot):
        row0 = pl.multiple_of((blk0 + k) * bm, bm)
        return pltpu.make_async_copy(h_ref.at[pl.ds(row0, bm)], hbuf.at[slot], hsem.at[slot])

    def y_copy(k, slot):
        row0 = pl.multiple_of((blk0 + k) * bm, bm)
        j0 = pl.multiple_of(c * nj, nj)
        return pltpu.make_async_copy(obuf.at[slot], y_ref.at[pl.ds(row0, bm), pl.ds(j0, nj)], osem.at[slot])

    def cast_weights():
        w2b_ref[...] = w2_ref[0].astype(BF16)

    def compute(slot):
        y = jnp.dot(hbuf[slot].astype(BF16), w2b_ref[...], preferred_element_type=F32)
        _store_row_slabs(obuf.at[slot], _pack_halves(y))

    @pl.when(nblk > 0)
    def _():
        _expert_block_loop(nblk, h_copy, y_copy, cast_weights, compute)


def _moe_experts(xs, blk_cnt, blk_start, w1, w3, w2, *, bm, fc=512, nc=2048):
    p, nslab, _ = xs.shape
    d = 2 * nslab * LANES
    ne, _, de = w1.shape
    any_spec = pl.BlockSpec(memory_space=pltpu.HBM)
    blk_up = (4 * d * fc * 4 + 2 * d * fc * 2 + 2 * bm * nslab * LANES * 4 + 2 * bm * fc * 4
              + 2 * bm * d * 2 + 3 * bm * fc * 4)
    h = pl.pallas_call(
        functools.partial(_moe_up_kernel, bm=bm),
        grid_spec=pltpu.PrefetchScalarGridSpec(
            num_scalar_prefetch=2,
            grid=(de // fc, ne),
            in_specs=[any_spec,
                      pl.BlockSpec((1, d, fc), lambda f, e, cnt, start: (e, 0, f)),
                      pl.BlockSpec((1, d, fc), lambda f, e, cnt, start: (e, 0, f))],
            out_specs=any_spec,
            scratch_shapes=[pltpu.VMEM((d, fc), BF16), pltpu.VMEM((d, fc), BF16),
                            pltpu.VMEM((2, bm * nslab, LANES), U32), pltpu.VMEM((2, bm, fc), F32),
                            pltpu.SemaphoreType.DMA((2,)), pltpu.SemaphoreType.DMA((2,))],
        ),
        out_shape=jax.ShapeDtypeStruct((p, de), F32),
        compiler_params=_params(("arbitrary", "arbitrary"), _vmem_limit(blk_up)),
        name="moe_up",
    )(blk_cnt, blk_start, xs.reshape(p * nslab, LANES), w1, w3)
    nj = nc // 2 // LANES
    blk_dn = 2 * de * nc * 4 + de * nc * 2 + 3 * bm * de * 4 + 2 * bm * nj * LANES * 4 + 2 * bm * nc * 4
    return pl.pallas_call(
        functools.partial(_moe_down_kernel, bm=bm),
        grid_spec=pltpu.PrefetchScalarGridSpec(
            num_scalar_prefetch=2,
            grid=(d // nc, ne),
            in_specs=[any_spec,
                      pl.BlockSpec((1, de, nc), lambda c, e, cnt, start: (e, 0, c))],
            out_specs=any_spec,
            scratch_shapes=[pltpu.VMEM((de, nc), BF16),
                            pltpu.VMEM((2, bm, de), F32), pltpu.VMEM((2, bm, nj, LANES), U32),
                            pltpu.SemaphoreType.DMA((2,)), pltpu.SemaphoreType.DMA((2,))],
        ),
        out_shape=jax.ShapeDtypeStruct((p, nslab, LANES), U32),
        compiler_params=_params(("arbitrary", "arbitrary"), _vmem_limit(blk_dn)),
        name="moe_down",
    )(blk_cnt, blk_start, h, w2)


def _combine_kernel(x_ref, y0_ref, y1_ref, ew_ref, g_ref, o_ref, *, nc):
    ew = ew_ref[...]
    w0 = ew[:, 0:1]
    w1 = ew[:, 1:2]
    half = nc // 2
    nslab = x_ref.shape[1] // 2 // LANES
    pieces = []
    ss = None
    for c in range(x_ref.shape[1] // nc):
        hi0, lo0 = _unpack_halves(_load_row_slabs(y0_ref, nslab, c * half // LANES, half // LANES))
        hi1, lo1 = _unpack_halves(_load_row_slabs(y1_ref, nslab, c * half // LANES, half // LANES))
        for j, (p0, p1) in enumerate(((hi0, hi1), (lo0, lo1))):
            cols = slice(c * nc + j * half, c * nc + (j + 1) * half)
            z = x_ref[:, cols] + w0 * p0 + w1 * p1
            pieces.append((cols, z))
            s = jnp.sum(z * z, axis=-1, keepdims=True)
            ss = s if ss is None else ss + s
    inv = lax.rsqrt(ss / x_ref.shape[1] + EPS)
    for cols, z in pieces:
        o_ref[:, cols] = z * inv * g_ref[:, cols]


def _combine(x, yu, e_w, gain, *, nc, tm=256):
    t, d = x.shape
    nt = t // tm
    nslab = yu.shape[1]
    yu2 = yu.reshape(yu.shape[0] * nslab, LANES)
    blk = 2 * (2 * tm * d * 4 + 2 * tm * d * 2 + tm * LANES * 4 + d * 4) + 3 * tm * d * 4
    return pl.pallas_call(
        functools.partial(_combine_kernel, nc=nc),
        grid=(nt,),
        in_specs=[pl.BlockSpec((tm, d), lambda i: (i, 0)),
                  pl.BlockSpec((tm * nslab, LANES), lambda i: (i, 0)),
                  pl.BlockSpec((tm * nslab, LANES), lambda i: (i + nt, 0)),
                  pl.BlockSpec((tm, TOP_K), lambda i: (i, 0)),
                  pl.BlockSpec((1, d), lambda i: (0, 0))],
        out_specs=pl.BlockSpec((tm, d), lambda i: (i, 0)),
        out_shape=jax.ShapeDtypeStruct((t, d), F32),
        compiler_params=_params(("arbitrary",), _vmem_limit(blk)),
        name="moe_combine",
    )(x, yu2, yu2, e_w, gain.reshape(1, d))


def _route(logits, *, bm):
    t = logits.shape[0]
    g_logits = logits[:, :N_GROUPS]
    e_logits = logits[:, N_GROUPS:N_GROUPS + N_EXPERTS].reshape(t, N_GROUPS, EXPERTS_PER_GROUP)
    g_idx = jnp.argmax(g_logits, axis=-1).astype(jnp.int32)
    g_w = jnp.take_along_axis(jax.nn.softmax(g_logits, axis=-1), g_idx[:, None], axis=-1)
    e_sel = jnp.take_along_axis(e_logits, g_idx[:, None, None], axis=1)[:, 0]
    top_v, top_i = lax.top_k(e_sel, TOP_K)
    e_w = jax.nn.softmax(top_v, axis=-1) * g_w
    eid = g_idx[:, None] * EXPERTS_PER_GROUP + top_i.astype(jnp.int32)

    a = t * TOP_K
    flat_e = eid.reshape(a)
    onehot = (flat_e[:, None] == jnp.arange(N_EXPERTS, dtype=jnp.int32)[None, :]).astype(jnp.int32)
    csum = jnp.cumsum(onehot, axis=0)
    rank = jnp.take_along_axis(csum, flat_e[:, None], axis=1)[:, 0] - 1
    counts = csum[-1]
    padded = ((counts + bm - 1) // bm) * bm
    pad_end = jnp.cumsum(padded)
    start_pad = pad_end - padded
    dest = (start_pad[flat_e] + rank).astype(jnp.int32)
    p = a + N_EXPERTS * bm
    blk_cnt = (padded // bm).astype(jnp.int32)
    blk_start = (start_pad // bm).astype(jnp.int32)
    rows_used = pad_end[-1:].astype(jnp.int32)
    src_tok = (jnp.arange(p, dtype=jnp.int32) % t).at[dest].set(jnp.arange(a, dtype=jnp.int32) // TOP_K)
    back = dest.reshape(t, TOP_K).T.reshape(a)
    return e_w, src_tok, back, blk_cnt, blk_start, rows_used


MOE_BLOCK_ROWS = 256
MOE_DOWN_COLS = 2048


def kernel(x, mem, norm_mix, w_in, pool_w, pool_scale, lb_fwd, lb_bwd, hgrn_norm, w_out, norm_xattn, norm_mem,
           w_q, w_k, w_v, w_o, norm_moe, w_router_group, w_router_expert, w1, w3, w2, norm_final):
    bsz, seq, d = x.shape
    assert bsz == 1 and w_in.shape[0] == 1, "one sequence, one layer (the final norm is fused into the MoE combine)"
    l = 0
    pool_width = pool_w.shape[1] * pool_w.shape[2]
    hgrn_width = hgrn_norm.shape[1]
    xt = x.reshape(seq, d)
    mem_t = mem.reshape(mem.shape[1], d)
    h = _rmsnorm(xt, norm_mix[l], BF16)
    proj = _matmul([h], w_in[l], out_dtype=F32)
    y_pool = _pool_mixer(proj, pool_w[l], pool_scale[l])
    y_hgrn = _hgrn_mixer(proj, lb_fwd, lb_bwd, hgrn_norm[l], layer=l, width=hgrn_width, col0=pool_width)
    xt = _matmul([y_pool, y_hgrn], w_out[l], resid=xt, out_dtype=F32)
    mem_n = _rmsnorm(mem_t, norm_mem[l], BF16)
    hx = _rmsnorm(xt, norm_xattn[l], BF16)
    q = _matmul([hx], w_q[l], out_dtype=BF16)
    k = _matmul([mem_n], w_k[l], out_dtype=BF16)
    v = _matmul([mem_n], w_v[l], out_dtype=BF16)
    o = _xattn(q, k, v)
    xt = _matmul([o], w_o[l], resid=xt, out_dtype=F32)
    w_router = jnp.concatenate(
        [w_router_group[l],
         jnp.transpose(w_router_expert[l], (1, 0, 2)).reshape(d, N_EXPERTS),
         jnp.zeros((d, ROUTER_COLS - N_GROUPS - N_EXPERTS), F32)], axis=1)
    hp, logits = _router(xt, norm_moe[l], w_router)
    e_w, src_tok, back, blk_cnt, blk_start, rows_used = _route(logits, bm=MOE_BLOCK_ROWS)
    xs = _gather_rows(src_tok, rows_used, hp)
    ys = _moe_experts(xs, blk_cnt, blk_start, w1[l], w3[l], w2[l], bm=MOE_BLOCK_ROWS, nc=MOE_DOWN_COLS)
    yu = _gather_rows(back, jnp.full((1,), back.shape[0], jnp.int32), ys)
    out = _combine(xt, yu, e_w, norm_final, nc=MOE_DOWN_COLS)
    return out.reshape(bsz, seq, d)
```

```python
import functools

import jax
import jax.numpy as jnp
from jax import lax
from jax.experimental import pallas as pl
from jax.experimental.pallas import tpu as pltpu

F32 = jnp.float32
BF16 = jnp.bfloat16
U32 = jnp.uint32

EPS = 1e-6
LANES = 128
SUBLANES = 8
VMEM_BYTES_V7X = 64 * 1024 * 1024

POOL_WINDOWS = (2, 4, 8, 16)
HEAD = 128
CHUNK = 64
XATTN_HEADS = 4
N_GROUPS = 4
EXPERTS_PER_GROUP = 8
N_EXPERTS = N_GROUPS * EXPERTS_PER_GROUP
TOP_K = 2
ROUTER_COLS = LANES


def _params(sem, vmem_bytes):
    return pltpu.CompilerParams(dimension_semantics=sem, vmem_limit_bytes=int(vmem_bytes))


def _vmem_limit(block_bytes):
    return min(int(block_bytes * 1.25) + (6 << 20), VMEM_BYTES_V7X - (4 << 20))


def _rmsnorm_kernel(x_ref, g_ref, o_ref):
    x = x_ref[...]
    ms = jnp.mean(x * x, axis=-1, keepdims=True)
    o_ref[...] = (x * lax.rsqrt(ms + EPS) * g_ref[...]).astype(o_ref.dtype)


def _rmsnorm(x, gain, out_dtype, tm=256):
    t, d = x.shape
    blk = tm * d * (4 + jnp.dtype(out_dtype).itemsize) * 2
    return pl.pallas_call(
        _rmsnorm_kernel,
        grid=(t // tm,),
        in_specs=[pl.BlockSpec((tm, d), lambda i: (i, 0)),
                  pl.BlockSpec((1, d), lambda i: (0, 0))],
        out_specs=pl.BlockSpec((tm, d), lambda i: (i, 0)),
        out_shape=jax.ShapeDtypeStruct((t, d), out_dtype),
        compiler_params=_params(("arbitrary",), _vmem_limit(blk)),
        name="rmsnorm",
    )(x, gain.reshape(1, d))


def _mm_kernel(*refs, n_a, has_resid):
    a_refs = refs[:n_a]
    w_ref = refs[n_a]
    r_ref = refs[n_a + 1] if has_resid else None
    o_ref, wb_ref = refs[-2], refs[-1]

    @pl.when(pl.program_id(1) == 0)
    def _():
        wb_ref[...] = w_ref[...].astype(BF16)

    acc = None
    k0 = 0
    for a_ref in a_refs:
        kk = a_ref.shape[1]
        part = jnp.dot(a_ref[...], wb_ref[k0:k0 + kk, :], preferred_element_type=F32)
        acc = part if acc is None else acc + part
        k0 += kk
    if has_resid:
        acc = acc + r_ref[...]
    o_ref[...] = acc.astype(o_ref.dtype)


def _matmul(a_parts, w, *, resid=None, out_dtype=F32, tm=1024, tn=512):
    m = a_parts[0].shape[0]
    k, n = w.shape
    assert sum(a.shape[1] for a in a_parts) == k
    tm = min(tm, m)
    in_specs = [pl.BlockSpec((tm, a.shape[1]), lambda j, i: (i, 0)) for a in a_parts]
    in_specs.append(pl.BlockSpec((k, tn), lambda j, i: (0, j)))
    args = list(a_parts) + [w]
    if resid is not None:
        in_specs.append(pl.BlockSpec((tm, tn), lambda j, i: (i, j)))
        args.append(resid)
    osz = jnp.dtype(out_dtype).itemsize
    blk = (2 * tm * k * 2 + 2 * k * tn * 4 + k * tn * 2 + 2 * tm * tn * osz
           + (2 * tm * tn * 4 if resid is not None else 0) + tm * tn * 4)
    return pl.pallas_call(
        functools.partial(_mm_kernel, n_a=len(a_parts), has_resid=resid is not None),
        grid=(n // tn, m // tm),
        in_specs=in_specs,
        out_specs=pl.BlockSpec((tm, tn), lambda j, i: (i, j)),
        out_shape=jax.ShapeDtypeStruct((m, n), out_dtype),
        scratch_shapes=[pltpu.VMEM((k, tn), BF16)],
        compiler_params=_params(("arbitrary", "arbitrary"), _vmem_limit(blk)),
        name="matmul",
    )(*args)


def _pool_kernel(u_ref, up_ref, un_ref, w_ref, s_ref, o_ref, *, tm, seq, group):
    i = pl.program_id(0)
    last = pl.num_programs(0) - 1
    n = tm + 2 * SUBLANES
    row = lax.broadcasted_iota(jnp.int32, (tm, 1), 0) + i * tm
    for gi, w in enumerate(POOL_WINDOWS):
        cs = slice(gi * group, (gi + 1) * group)
        u = u_ref[:, cs]
        prev = jnp.where(i > 0, up_ref[:, cs], 0.0)
        nxt = jnp.where(i < last, un_ref[:, cs], 0.0)
        f = jnp.concatenate([prev, u, nxt], axis=0)
        step = 1
        while step < w:
            f = f + pltpu.roll(f, n - step, 0)
            step *= 2
        half = w // 2
        first = SUBLANES - half
        if first:
            f = pltpu.roll(f, n - first, 0)
        win = f[:tm]
        lo = jnp.maximum(row - half, 0)
        hi = jnp.minimum(row + half - 1, seq - 1)
        cnt = (hi - lo + 1).astype(F32)
        mixed = win / cnt - u
        y = jnp.dot(mixed.astype(BF16), w_ref[gi].astype(BF16), preferred_element_type=F32)
        o_ref[:, cs] = (y * s_ref[:, cs]).astype(o_ref.dtype)


def _pool_mixer(proj, pool_w, pool_scale, *, tm=512):
    t = proj.shape[0]
    ng, group, _ = pool_w.shape
    width = ng * group
    rpb = tm // SUBLANES
    nhalo = t // SUBLANES
    blk = 2 * (tm * width * 4 + 2 * SUBLANES * width * 4 + ng * group * group * 4 + tm * width * 2) + 6 * tm * group * 4
    return pl.pallas_call(
        functools.partial(_pool_kernel, tm=tm, seq=t, group=group),
        grid=(t // tm,),
        in_specs=[pl.BlockSpec((tm, width), lambda i: (i, 0)),
                  pl.BlockSpec((SUBLANES, width), lambda i: (jnp.maximum(i * rpb - 1, 0), 0)),
                  pl.BlockSpec((SUBLANES, width), lambda i: (jnp.minimum((i + 1) * rpb, nhalo - 1), 0)),
                  pl.BlockSpec((ng, group, group), lambda i: (0, 0, 0)),
                  pl.BlockSpec((1, width), lambda i: (0, 0))],
        out_specs=pl.BlockSpec((tm, width), lambda i: (i, 0)),
        out_shape=jax.ShapeDtypeStruct((t, width), BF16),
        compiler_params=_params(("arbitrary",), _vmem_limit(blk)),
        name="pool_mixer",
    )(proj, proj, proj, pool_w, pool_scale.reshape(1, width))


def _chunk_cumsum(x, reverse):
    n = x.shape[0]
    pos = lax.broadcasted_iota(jnp.int32, x.shape, 0) % CHUNK
    s = 1
    while s < CHUNK:
        if reverse:
            x = x + jnp.where(pos < CHUNK - s, pltpu.roll(x, n - s, 0), 0.0)
        else:
            x = x + jnp.where(pos >= s, pltpu.roll(x, s, 0), 0.0)
        s *= 2
    return x


def _lower_bound(lb_raw, layer):
    e = jnp.exp(lb_raw - jnp.max(lb_raw, axis=0, keepdims=True))
    return jnp.sum(e[:layer + 1], axis=0, keepdims=True) / jnp.sum(e, axis=0, keepdims=True)


def _hgrn_direction(q_raw, f_raw, v, lb, st_ref, h, mask, *, reverse):
    tb = q_raw.shape[0]
    nc = tb // CHUNK
    q = q_raw * jax.nn.sigmoid(q_raw)
    f = lb + (1.0 - lb) * jax.nn.sigmoid(f_raw)
    logf = jnp.log(f)
    k = 1.0 - f
    b = _chunk_cumsum(logf, reverse)
    b3 = b.reshape(nc, CHUNK, HEAD)
    edge = 0 if reverse else CHUNK - 1
    b_last = b3[:, edge:edge + 1, :]
    qd = (q * jnp.exp(b)).astype(BF16)
    kd = (k * jnp.exp(-b)).astype(BF16)
    k_end = (k.reshape(nc, CHUNK, HEAD) * jnp.exp(b_last - b3)).astype(BF16)
    decay = jnp.exp(b_last)
    vb = v.astype(BF16)

    a = lax.dot_general(qd, kd, (((1,), (1,)), ((), ())), preferred_element_type=F32)
    a = jnp.where(mask, a, 0.0)
    o_intra = jnp.dot(a.astype(BF16), vb, preferred_element_type=F32)

    st = st_ref[h]
    o_inter = [None] * nc
    order = range(nc - 1, -1, -1) if reverse else range(nc)
    for c in order:
        rows = slice(c * CHUNK, (c + 1) * CHUNK)
        o_inter[c] = lax.dot_general(qd[rows], st.astype(BF16), (((1,), (1,)), ((), ())),
                                     preferred_element_type=F32)
        d_st = lax.dot_general(vb[rows], k_end[c], (((0,), (0,)), ((), ())), preferred_element_type=F32)
        st = st * decay[c] + d_st
    st_ref[h] = st
    return o_intra + jnp.concatenate(o_inter, axis=0)


def _intra_mask(tb, reverse):
    r = lax.broadcasted_iota(jnp.int32, (tb, tb), 0)
    c = lax.broadcasted_iota(jnp.int32, (tb, tb), 1)
    same = (r // CHUNK) == (c // CHUNK)
    return same & ((c >= r) if reverse else (c <= r))


def _hgrn_fwd_kernel(q_ref, f_ref, i_ref, lb_ref, o_ref, st_ref, *, layer, heads):
    @pl.when(pl.program_id(0) == 0)
    def _():
        st_ref[...] = jnp.zeros_like(st_ref)

    mask = _intra_mask(q_ref.shape[0], False)

    def head(h, carry):
        hs = pl.ds(pl.multiple_of(h * HEAD, HEAD), HEAD)
        lb = _lower_bound(lb_ref[:, hs], layer)
        o_ref[:, hs] = _hgrn_direction(q_ref[:, hs], f_ref[:, hs], i_ref[:, hs], lb, st_ref, h, mask,
                                       reverse=False)
        return carry

    lax.fori_loop(0, heads, head, 0, unroll=2)


def _hgrn_bwd_kernel(q_ref, f_ref, i_ref, g_ref, of_ref, lb_ref, gain_ref, y_ref, st_ref, *, layer, heads):
    @pl.when(pl.program_id(0) == 0)
    def _():
        st_ref[...] = jnp.zeros_like(st_ref)

    mask = _intra_mask(q_ref.shape[0], True)

    def head(h, carry):
        hs = pl.ds(pl.multiple_of(h * HEAD, HEAD), HEAD)
        lb = _lower_bound(lb_ref[:, hs], layer)
        o = of_ref[:, hs] + _hgrn_direction(q_ref[:, hs], f_ref[:, hs], i_ref[:, hs], lb, st_ref, h, mask,
                                            reverse=True)
        o = o * lax.rsqrt(jnp.mean(o * o, axis=-1, keepdims=True) + EPS) * gain_ref[:, hs]
        g = g_ref[:, hs]
        y_ref[:, hs] = (o * (g * jax.nn.sigmoid(g))).astype(y_ref.dtype)
        return carry

    lax.fori_loop(0, heads, head, 0, unroll=2)


def _hgrn_mixer(proj, lb_fwd, lb_bwd, hgrn_norm, *, layer, width, col0, tb=256):
    t = proj.shape[0]
    heads = width // HEAD
    nb = t // tb
    c = col0 // width
    layers = lb_fwd.shape[0]
    blk_f = 2 * (3 * tb * width * 4 + layers * width * 4 + tb * width * 4) + heads * HEAD * HEAD * 4
    o_f = pl.pallas_call(
        functools.partial(_hgrn_fwd_kernel, layer=layer, heads=heads),
        grid=(nb,),
        in_specs=[pl.BlockSpec((tb, width), lambda b: (b, c)),
                  pl.BlockSpec((tb, width), lambda b: (b, c + 1)),
                  pl.BlockSpec((tb, width), lambda b: (b, c + 3)),
                  pl.BlockSpec((layers, width), lambda b: (0, 0))],
        out_specs=pl.BlockSpec((tb, width), lambda b: (b, 0)),
        out_shape=jax.ShapeDtypeStruct((t, width), F32),
        scratch_shapes=[pltpu.VMEM((heads, HEAD, HEAD), F32)],
        compiler_params=_params(("arbitrary",), _vmem_limit(blk_f)),
        name="hgrn_fwd",
    )(proj, proj, proj, lb_fwd)
    blk_b = 2 * (5 * tb * width * 4 + layers * width * 4 + width * 4 + tb * width * 2) + heads * HEAD * HEAD * 4
    return pl.pallas_call(
        functools.partial(_hgrn_bwd_kernel, layer=layer, heads=heads),
        grid=(nb,),
        in_specs=[pl.BlockSpec((tb, width), lambda b: (nb - 1 - b, c)),
                  pl.BlockSpec((tb, width), lambda b: (nb - 1 - b, c + 2)),
                  pl.BlockSpec((tb, width), lambda b: (nb - 1 - b, c + 3)),
                  pl.BlockSpec((tb, width), lambda b: (nb - 1 - b, c + 4)),
                  pl.BlockSpec((tb, width), lambda b: (nb - 1 - b, 0)),
                  pl.BlockSpec((layers, width), lambda b: (0, 0)),
                  pl.BlockSpec((1, width), lambda b: (0, 0))],
        out_specs=pl.BlockSpec((tb, width), lambda b: (nb - 1 - b, 0)),
        out_shape=jax.ShapeDtypeStruct((t, width), BF16),
        scratch_shapes=[pltpu.VMEM((heads, HEAD, HEAD), F32)],
        compiler_params=_params(("arbitrary",), _vmem_limit(blk_b)),
        name="hgrn_bwd",
    )(proj, proj, proj, proj, o_f, lb_bwd, hgrn_norm.reshape(1, width))


def _xattn_kernel(q_ref, k_ref, v_ref, o_ref, *, scale):
    s = lax.dot_general(q_ref[...], k_ref[...], (((1,), (1,)), ((), ())), preferred_element_type=F32) * scale
    s = s - jnp.max(s, axis=-1, keepdims=True)
    p = jnp.exp(s)
    p = p / jnp.sum(p, axis=-1, keepdims=True)
    o_ref[...] = jnp.dot(p.astype(BF16), v_ref[...], preferred_element_type=F32).astype(o_ref.dtype)


def _xattn(q, k, v, *, tm=512):
    t, d = q.shape
    m = k.shape[0]
    dh = d // XATTN_HEADS
    blk = 2 * (2 * tm * dh * 2 + 2 * m * dh * 2) + 4 * tm * m * 4 + tm * dh * 4
    return pl.pallas_call(
        functools.partial(_xattn_kernel, scale=dh ** -0.5),
        grid=(XATTN_HEADS, t // tm),
        in_specs=[pl.BlockSpec((tm, dh), lambda h, i: (i, h)),
                  pl.BlockSpec((m, dh), lambda h, i: (0, h)),
                  pl.BlockSpec((m, dh), lambda h, i: (0, h))],
        out_specs=pl.BlockSpec((tm, dh), lambda h, i: (i, h)),
        out_shape=jax.ShapeDtypeStruct((t, d), BF16),
        compiler_params=_params(("arbitrary", "arbitrary"), _vmem_limit(blk)),
        name="xattn",
    )(q, k, v)


def _pack_halves(x):
    w = x.shape[1] // 2
    hi = lax.bitcast_convert_type(x[:, :w].astype(BF16).astype(F32), U32)
    lo = lax.bitcast_convert_type(x[:, w:].astype(BF16).astype(F32), U32)
    return hi | (lo >> 16)


def _unpack_halves(p):
    hi = lax.bitcast_convert_type(p & jnp.uint32(0xFFFF0000), F32)
    lo = lax.bitcast_convert_type(p << 16, F32)
    return hi, lo


def _load_row_slabs(flat_ref, s, j0, nj):
    rows = flat_ref.shape[0] // s
    return jnp.concatenate([flat_ref[pl.ds(j0 + j, rows, stride=s), :] for j in range(nj)], axis=1)


def _store_row_slabs(ref, val):
    for j in range(ref.shape[1]):
        ref[:, j, :] = val[:, j * LANES:(j + 1) * LANES]


def _router_kernel(x_ref, g_ref, wr_ref, hp_ref, lg_ref):
    x = x_ref[...]
    ms = jnp.mean(x * x, axis=-1, keepdims=True)
    h = x * lax.rsqrt(ms + EPS) * g_ref[...]
    _store_row_slabs(hp_ref, _pack_halves(h))
    w = wr_ref[...]
    h_hi = h.astype(BF16)
    h_lo = (h - h_hi.astype(F32)).astype(BF16)
    w_hi = w.astype(BF16)
    w_lo = (w - w_hi.astype(F32)).astype(BF16)
    lg_ref[...] = (jnp.dot(h_hi, w_hi, preferred_element_type=F32)
                   + (jnp.dot(h_hi, w_lo, preferred_element_type=F32)
                      + jnp.dot(h_lo, w_hi, preferred_element_type=F32)))


def _router(x, gain, w_router, *, tm=256):
    t, d = x.shape
    blk = 2 * (tm * d * 4 + d * 4 + d * ROUTER_COLS * 4 + tm * d * 2 + tm * ROUTER_COLS * 4) + 3 * tm * d * 4
    return pl.pallas_call(
        _router_kernel,
        grid=(t // tm,),
        in_specs=[pl.BlockSpec((tm, d), lambda i: (i, 0)),
                  pl.BlockSpec((1, d), lambda i: (0, 0)),
                  pl.BlockSpec((d, ROUTER_COLS), lambda i: (0, 0))],
        out_specs=[pl.BlockSpec((tm, d // 2 // LANES, LANES), lambda i: (i, 0, 0)),
                   pl.BlockSpec((tm, ROUTER_COLS), lambda i: (i, 0))],
        out_shape=[jax.ShapeDtypeStruct((t, d // 2 // LANES, LANES), U32),
                   jax.ShapeDtypeStruct((t, ROUTER_COLS), F32)],
        compiler_params=_params(("arbitrary",), _vmem_limit(blk)),
        name="moe_router",
    )(x, gain.reshape(1, d), w_router)


def _gather_rows_kernel(idx_ref, nrows_ref, src_ref, o_ref, sem, *, rows):
    base = pl.program_id(0) * rows

    def row_copy(src_row, r):
        return pltpu.make_async_copy(src_ref.at[src_row], o_ref.at[r], sem)

    @pl.when(base < nrows_ref[0])
    def _():
        def issue(r, carry):
            row_copy(idx_ref[base + r], r).start()
            return carry

        lax.fori_loop(0, rows, issue, 0, unroll=8)

        def drain(r, carry):
            row_copy(0, r).wait()
            return carry

        lax.fori_loop(0, rows, drain, 0, unroll=8)

    @pl.when(base >= nrows_ref[0])
    def _():
        o_ref[...] = jnp.zeros_like(o_ref)


def _gather_rows(idx, nrows, src, *, rows=256):
    n = idx.shape[0]
    slab = src.shape[1:]
    blk = 2 * rows * slab[0] * slab[1] * 4
    return pl.pallas_call(
        functools.partial(_gather_rows_kernel, rows=rows),
        grid_spec=pltpu.PrefetchScalarGridSpec(
            num_scalar_prefetch=2,
            grid=(n // rows,),
            in_specs=[pl.BlockSpec(memory_space=pltpu.HBM)],
            out_specs=pl.BlockSpec((rows,) + slab, lambda b, idx, nrows: (b, 0, 0)),
            scratch_shapes=[pltpu.SemaphoreType.DMA(())],
        ),
        out_shape=jax.ShapeDtypeStruct((n,) + slab, src.dtype),
        compiler_params=_params(("arbitrary",), _vmem_limit(blk)),
        name="gather_rows",
    )(idx, nrows, src)


BLOCK_COPY_PRIORITY = 0


def _start_first_blocks(cnt_ref, start_ref, e, in_copy_at):
    n = cnt_ref[e]
    b = start_ref[e]

    @pl.when(n > 0)
    def _():
        in_copy_at(b, 0).start(priority=BLOCK_COPY_PRIORITY)

    @pl.when(n > 1)
    def _():
        in_copy_at(b + 1, 1).start(priority=BLOCK_COPY_PRIORITY)


def _expert_grid_step(cnt_ref, start_ref, in_copy_at, out_copy_at, prologue, compute):
    e = pl.program_id(1)
    ne = pl.num_programs(1)
    step = pl.program_id(0) * ne + e
    nblk = cnt_ref[e]
    blk0 = start_ref[e]

    @pl.when(step == 0)
    def _():
        _start_first_blocks(cnt_ref, start_ref, e, in_copy_at)

    @pl.when(nblk > 0)
    def _():
        prologue()

        def body(k, carry):
            slot = k % 2
            in_copy_at(blk0 + k, slot).wait()

            @pl.when(k >= 2)
            def _():
                out_copy_at(blk0 + k - 2, slot).wait()

            compute(slot)
            out_copy_at(blk0 + k, slot).start(priority=BLOCK_COPY_PRIORITY)

            @pl.when(k + 2 < nblk)
            def _():
                in_copy_at(blk0 + k + 2, slot).start(priority=BLOCK_COPY_PRIORITY)

            return carry

        lax.fori_loop(0, nblk, body, 0)

        @pl.when(nblk >= 2)
        def _():
            out_copy_at(blk0 + nblk - 2, nblk % 2).wait()

        out_copy_at(blk0 + nblk - 1, (nblk - 1) % 2).wait()

    @pl.when(step + 1 < pl.num_programs(0) * ne)
    def _():
        _start_first_blocks(cnt_ref, start_ref, (e + 1) % ne, in_copy_at)


def _moe_up_kernel(cnt_ref, start_ref, xs_ref, w1_ref, w3_ref, h_ref, w1b_ref, w3b_ref, xbuf, obuf, xsem, osem,
                   *, bm):
    f = pl.program_id(0)
    d, fc = w1b_ref.shape
    nslab = d // 2 // LANES
    xrows = bm * nslab

    def x_copy(blk, slot):
        row0 = pl.multiple_of(blk * xrows, xrows)
        return pltpu.make_async_copy(xs_ref.at[pl.ds(row0, xrows)], xbuf.at[slot], xsem.at[slot])

    def h_copy(blk, slot):
        row0 = pl.multiple_of(blk * bm, bm)
        col0 = pl.multiple_of(f * fc, fc)
        return pltpu.make_async_copy(obuf.at[slot], h_ref.at[pl.ds(row0, bm), pl.ds(col0, fc)], osem.at[slot])

    def cast_weights():
        w1b_ref[...] = w1_ref[0].astype(BF16)
        w3b_ref[...] = w3_ref[0].astype(BF16)

    def compute(slot):
        hi, lo = _unpack_halves(_load_row_slabs(xbuf.at[slot], nslab, 0, nslab))
        hi = hi.astype(BF16)
        lo = lo.astype(BF16)
        d2 = d // 2
        a = (jnp.dot(hi, w1b_ref[:d2, :], preferred_element_type=F32)
             + jnp.dot(lo, w1b_ref[d2:, :], preferred_element_type=F32))
        c = (jnp.dot(hi, w3b_ref[:d2, :], preferred_element_type=F32)
             + jnp.dot(lo, w3b_ref[d2:, :], preferred_element_type=F32))
        obuf[slot] = (a * jax.nn.sigmoid(a) * c).astype(obuf.dtype)

    _expert_grid_step(cnt_ref, start_ref, x_copy, h_copy, cast_weights, compute)


def _moe_down_kernel(cnt_ref, start_ref, h_ref, w2_ref, y_ref, w2b_ref, hbuf, obuf, hsem, osem, *, bm):
    c = pl.program_id(0)
    nj = obuf.shape[2]

    def h_copy(blk, slot):
        row0 = pl.multiple_of(blk * bm, bm)
        return pltpu.make_async_copy(h_ref.at[pl.ds(row0, bm)], hbuf.at[slot], hsem.at[slot])

    def y_copy(blk, slot):
        row0 = pl.multiple_of(blk * bm, bm)
        j0 = pl.multiple_of(c * nj, nj)
        return pltpu.make_async_copy(obuf.at[slot], y_ref.at[pl.ds(row0, bm), pl.ds(j0, nj)], osem.at[slot])

    def cast_weights():
        w2b_ref[...] = w2_ref[0].astype(BF16)

    def compute(slot):
        y = jnp.dot(hbuf[slot].astype(BF16), w2b_ref[...], preferred_element_type=F32)
        _store_row_slabs(obuf.at[slot], _pack_halves(y))

    _expert_grid_step(cnt_ref, start_ref, h_copy, y_copy, cast_weights, compute)


def _moe_experts(xs, blk_cnt, blk_start, w1, w3, w2, *, bm, fc=512, nc=2048):
    p, nslab, _ = xs.shape
    d = 2 * nslab * LANES
    ne, _, de = w1.shape
    any_spec = pl.BlockSpec(memory_space=pltpu.HBM)
    blk_up = (4 * d * fc * 4 + 2 * d * fc * 2 + 2 * bm * nslab * LANES * 4 + 2 * bm * fc * 4
              + 2 * bm * d * 2 + 3 * bm * fc * 4)
    h = pl.pallas_call(
        functools.partial(_moe_up_kernel, bm=bm),
        grid_spec=pltpu.PrefetchScalarGridSpec(
            num_scalar_prefetch=2,
            grid=(de // fc, ne),
            in_specs=[any_spec,
                      pl.BlockSpec((1, d, fc), lambda f, e, cnt, start: (e, 0, f)),
                      pl.BlockSpec((1, d, fc), lambda f, e, cnt, start: (e, 0, f))],
            out_specs=any_spec,
            scratch_shapes=[pltpu.VMEM((d, fc), BF16), pltpu.VMEM((d, fc), BF16),
                            pltpu.VMEM((2, bm * nslab, LANES), U32), pltpu.VMEM((2, bm, fc), F32),
                            pltpu.SemaphoreType.DMA((2,)), pltpu.SemaphoreType.DMA((2,))],
        ),
        out_shape=jax.ShapeDtypeStruct((p, de), F32),
        compiler_params=_params(("arbitrary", "arbitrary"), _vmem_limit(blk_up)),
        name="moe_up",
    )(blk_cnt, blk_start, xs.reshape(p * nslab, LANES), w1, w3)
    nj = nc // 2 // LANES
    blk_dn = 2 * de * nc * 4 + de * nc * 2 + 3 * bm * de * 4 + 2 * bm * nj * LANES * 4 + 2 * bm * nc * 4
    return pl.pallas_call(
        functools.partial(_moe_down_kernel, bm=bm),
        grid_spec=pltpu.PrefetchScalarGridSpec(
            num_scalar_prefetch=2,
            grid=(d // nc, ne),
            in_specs=[any_spec,
                      pl.BlockSpec((1, de, nc), lambda c, e, cnt, start: (e, 0, c))],
            out_specs=any_spec,
            scratch_shapes=[pltpu.VMEM((de, nc), BF16),
                            pltpu.VMEM((2, bm, de), F32), pltpu.VMEM((2, bm, nj, LANES), U32),
                            pltpu.SemaphoreType.DMA((2,)), pltpu.SemaphoreType.DMA((2,))],
        ),
        out_shape=jax.ShapeDtypeStruct((p, nslab, LANES), U32),
        compiler_params=_params(("arbitrary", "arbitrary"), _vmem_limit(blk_dn)),
        name="moe_down",
    )(blk_cnt, blk_start, h, w2)


def _combine_kernel(x_ref, y0_ref, y1_ref, ew_ref, g_ref, o_ref, *, nc):
    ew = ew_ref[...]
    w0 = ew[:, 0:1]
    w1 = ew[:, 1:2]
    half = nc // 2
    nslab = x_ref.shape[1] // 2 // LANES
    pieces = []
    ss = None
    for c in range(x_ref.shape[1] // nc):
        hi0, lo0 = _unpack_halves(_load_row_slabs(y0_ref, nslab, c * half // LANES, half // LANES))
        hi1, lo1 = _unpack_halves(_load_row_slabs(y1_ref, nslab, c * half // LANES, half // LANES))
        for j, (p0, p1) in enumerate(((hi0, hi1), (lo0, lo1))):
            cols = slice(c * nc + j * half, c * nc + (j + 1) * half)
            z = x_ref[:, cols] + w0 * p0 + w1 * p1
            pieces.append((cols, z))
            s = jnp.sum(z * z, axis=-1, keepdims=True)
            ss = s if ss is None else ss + s
    inv = lax.rsqrt(ss / x_ref.shape[1] + EPS)
    for cols, z in pieces:
        o_ref[:, cols] = z * inv * g_ref[:, cols]


def _combine(x, yu, e_w, gain, *, nc, tm=256):
    t, d = x.shape
    nt = t // tm
    nslab = yu.shape[1]
    yu2 = yu.reshape(yu.shape[0] * nslab, LANES)
    blk = 2 * (2 * tm * d * 4 + 2 * tm * d * 2 + tm * LANES * 4 + d * 4) + 3 * tm * d * 4
    return pl.pallas_call(
        functools.partial(_combine_kernel, nc=nc),
        grid=(nt,),
        in_specs=[pl.BlockSpec((tm, d), lambda i: (i, 0)),
                  pl.BlockSpec((tm * nslab, LANES), lambda i: (i, 0)),
                  pl.BlockSpec((tm * nslab, LANES), lambda i: (i + nt, 0)),
                  pl.BlockSpec((tm, TOP_K), lambda i: (i, 0)),
                  pl.BlockSpec((1, d), lambda i: (0, 0))],
        out_specs=pl.BlockSpec((tm, d), lambda i: (i, 0)),
        out_shape=jax.ShapeDtypeStruct((t, d), F32),
        compiler_params=_params(("arbitrary",), _vmem_limit(blk)),
        name="moe_combine",
    )(x, yu2, yu2, e_w, gain.reshape(1, d))


def _route(logits, *, bm):
    t = logits.shape[0]
    g_logits = logits[:, :N_GROUPS]
    e_logits = logits[:, N_GROUPS:N_GROUPS + N_EXPERTS].reshape(t, N_GROUPS, EXPERTS_PER_GROUP)
    g_idx = jnp.argmax(g_logits, axis=-1).astype(jnp.int32)
    g_w = jnp.take_along_axis(jax.nn.softmax(g_logits, axis=-1), g_idx[:, None], axis=-1)
    e_sel = jnp.take_along_axis(e_logits, g_idx[:, None, None], axis=1)[:, 0]
    top_v, top_i = lax.top_k(e_sel, TOP_K)
    e_w = jax.nn.softmax(top_v, axis=-1) * g_w
    eid = g_idx[:, None] * EXPERTS_PER_GROUP + top_i.astype(jnp.int32)

    a = t * TOP_K
    flat_e = eid.reshape(a)
    onehot = (flat_e[:, None] == jnp.arange(N_EXPERTS, dtype=jnp.int32)[None, :]).astype(jnp.int32)
    csum = jnp.cumsum(onehot, axis=0)
    rank = jnp.take_along_axis(csum, flat_e[:, None], axis=1)[:, 0] - 1
    counts = csum[-1]
    padded = ((counts + bm - 1) // bm) * bm
    pad_end = jnp.cumsum(padded)
    start_pad = pad_end - padded
    dest = (start_pad[flat_e] + rank).astype(jnp.int32)
    p = a + N_EXPERTS * bm
    blk_cnt = (padded // bm).astype(jnp.int32)
    blk_start = (start_pad // bm).astype(jnp.int32)
    rows_used = pad_end[-1:].astype(jnp.int32)
    src_tok = (jnp.arange(p, dtype=jnp.int32) % t).at[dest].set(jnp.arange(a, dtype=jnp.int32) // TOP_K)
    back = dest.reshape(t, TOP_K).T.reshape(a)
    return e_w, src_tok, back, blk_cnt, blk_start, rows_used


MOE_BLOCK_ROWS = 256
MOE_DOWN_COLS = 4096


def kernel(x, mem, norm_mix, w_in, pool_w, pool_scale, lb_fwd, lb_bwd, hgrn_norm, w_out, norm_xattn, norm_mem,
           w_q, w_k, w_v, w_o, norm_moe, w_router_group, w_router_expert, w1, w3, w2, norm_final):
    bsz, seq, d = x.shape
    assert bsz == 1 and w_in.shape[0] == 1, "one sequence, one layer (the final norm is fused into the MoE combine)"
    l = 0
    pool_width = pool_w.shape[1] * pool_w.shape[2]
    hgrn_width = hgrn_norm.shape[1]
    xt = x.reshape(seq, d)
    mem_t = mem.reshape(mem.shape[1], d)
    h = _rmsnorm(xt, norm_mix[l], BF16)
    proj = _matmul([h], w_in[l], out_dtype=F32)
    y_pool = _pool_mixer(proj, pool_w[l], pool_scale[l])
    y_hgrn = _hgrn_mixer(proj, lb_fwd, lb_bwd, hgrn_norm[l], layer=l, width=hgrn_width, col0=pool_width)
    xt = _matmul([y_pool, y_hgrn], w_out[l], resid=xt, out_dtype=F32)
    mem_n = _rmsnorm(mem_t, norm_mem[l], BF16)
    hx = _rmsnorm(xt, norm_xattn[l], BF16)
    q = _matmul([hx], w_q[l], out_dtype=BF16)
    k = _matmul([mem_n], w_k[l], out_dtype=BF16)
    v = _matmul([mem_n], w_v[l], out_dtype=BF16)
    o = _xattn(q, k, v)
    xt = _matmul([o], w_o[l], resid=xt, out_dtype=F32)
    w_router = jnp.concatenate(
        [w_router_group[l],
         jnp.transpose(w_router_expert[l], (1, 0, 2)).reshape(d, N_EXPERTS),
         jnp.zeros((d, ROUTER_COLS - N_GROUPS - N_EXPERTS), F32)], axis=1)
    hp, logits = _router(xt, norm_moe[l], w_router)
    e_w, src_tok, back, blk_cnt, blk_start, rows_used = _route(logits, bm=MOE_BLOCK_ROWS)
    xs = _gather_rows(src_tok, rows_used, hp)
    ys = _moe_experts(xs, blk_cnt, blk_start, w1[l], w3[l], w2[l], bm=MOE_BLOCK_ROWS, nc=MOE_DOWN_COLS)
    yu = _gather_rows(back, jnp.full((1,), back.shape[0], jnp.int32), ys)
    out = _combine(xt, yu, e_w, norm_final, nc=MOE_DOWN_COLS)
    return out.reshape(bsz, seq, d)
```

```python
import functools

import jax
import jax.numpy as jnp
from jax import lax
from jax.experimental import pallas as pl
from jax.experimental.pallas import tpu as pltpu

F32 = jnp.float32
BF16 = jnp.bfloat16
U32 = jnp.uint32

EPS = 1e-6
LANES = 128
SUBLANES = 8
VMEM_BYTES_V7X = 64 * 1024 * 1024

POOL_WINDOWS = (2, 4, 8, 16)
HEAD = 128
CHUNK = 64
XATTN_HEADS = 4
N_GROUPS = 4
EXPERTS_PER_GROUP = 8
N_EXPERTS = N_GROUPS * EXPERTS_PER_GROUP
TOP_K = 2
ROUTER_COLS = LANES


def _params(sem, vmem_bytes):
    return pltpu.CompilerParams(dimension_semantics=sem, vmem_limit_bytes=int(vmem_bytes))


def _vmem_limit(block_bytes):
    return min(int(block_bytes * 1.25) + (6 << 20), VMEM_BYTES_V7X - (4 << 20))


def _rmsnorm_kernel(x_ref, g_ref, o_ref):
    x = x_ref[...]
    ms = jnp.mean(x * x, axis=-1, keepdims=True)
    o_ref[...] = (x * lax.rsqrt(ms + EPS) * g_ref[...]).astype(o_ref.dtype)


def _rmsnorm(x, gain, out_dtype, tm=256):
    t, d = x.shape
    blk = tm * d * (4 + jnp.dtype(out_dtype).itemsize) * 2
    return pl.pallas_call(
        _rmsnorm_kernel,
        grid=(t // tm,),
        in_specs=[pl.BlockSpec((tm, d), lambda i: (i, 0)),
                  pl.BlockSpec((1, d), lambda i: (0, 0))],
        out_specs=pl.BlockSpec((tm, d), lambda i: (i, 0)),
        out_shape=jax.ShapeDtypeStruct((t, d), out_dtype),
        compiler_params=_params(("arbitrary",), _vmem_limit(blk)),
        name="rmsnorm",
    )(x, gain.reshape(1, d))


def _mm_kernel(*refs, n_a, has_resid):
    a_refs = refs[:n_a]
    w_ref = refs[n_a]
    r_ref = refs[n_a + 1] if has_resid else None
    o_ref, wb_ref = refs[-2], refs[-1]

    @pl.when(pl.program_id(1) == 0)
    def _():
        wb_ref[...] = w_ref[...].astype(BF16)

    acc = None
    k0 = 0
    for a_ref in a_refs:
        kk = a_ref.shape[1]
        part = jnp.dot(a_ref[...], wb_ref[k0:k0 + kk, :], preferred_element_type=F32)
        acc = part if acc is None else acc + part
        k0 += kk
    if has_resid:
        acc = acc + r_ref[...]
    o_ref[...] = acc.astype(o_ref.dtype)


def _matmul(a_parts, w, *, resid=None, out_dtype=F32, tm=1024, tn=512):
    m = a_parts[0].shape[0]
    k, n = w.shape
    assert sum(a.shape[1] for a in a_parts) == k
    tm = min(tm, m)
    in_specs = [pl.BlockSpec((tm, a.shape[1]), lambda j, i: (i, 0)) for a in a_parts]
    in_specs.append(pl.BlockSpec((k, tn), lambda j, i: (0, j)))
    args = list(a_parts) + [w]
    if resid is not None:
        in_specs.append(pl.BlockSpec((tm, tn), lambda j, i: (i, j)))
        args.append(resid)
    osz = jnp.dtype(out_dtype).itemsize
    blk = (2 * tm * k * 2 + 2 * k * tn * 4 + k * tn * 2 + 2 * tm * tn * osz
           + (2 * tm * tn * 4 if resid is not None else 0) + tm * tn * 4)
    return pl.pallas_call(
        functools.partial(_mm_kernel, n_a=len(a_parts), has_resid=resid is not None),
        grid=(n // tn, m // tm),
        in_specs=in_specs,
        out_specs=pl.BlockSpec((tm, tn), lambda j, i: (i, j)),
        out_shape=jax.ShapeDtypeStruct((m, n), out_dtype),
        scratch_shapes=[pltpu.VMEM((k, tn), BF16)],
        compiler_params=_params(("arbitrary", "arbitrary"), _vmem_limit(blk)),
        name="matmul",
    )(*args)


def _pool_kernel(u_ref, up_ref, un_ref, w_ref, s_ref, o_ref, *, tm, seq, group):
    i = pl.program_id(0)
    last = pl.num_programs(0) - 1
    n = tm + 2 * SUBLANES
    row = lax.broadcasted_iota(jnp.int32, (tm, 1), 0) + i * tm
    for gi, w in enumerate(POOL_WINDOWS):
        cs = slice(gi * group, (gi + 1) * group)
        u = u_ref[:, cs]
        prev = jnp.where(i > 0, up_ref[:, cs], 0.0)
        nxt = jnp.where(i < last, un_ref[:, cs], 0.0)
        f = jnp.concatenate([prev, u, nxt], axis=0)
        step = 1
        while step < w:
            f = f + pltpu.roll(f, n - step, 0)
            step *= 2
        half = w // 2
        first = SUBLANES - half
        if first:
            f = pltpu.roll(f, n - first, 0)
        win = f[:tm]
        lo = jnp.maximum(row - half, 0)
        hi = jnp.minimum(row + half - 1, seq - 1)
        cnt = (hi - lo + 1).astype(F32)
        mixed = win / cnt - u
        y = jnp.dot(mixed.astype(BF16), w_ref[gi].astype(BF16), preferred_element_type=F32)
        o_ref[:, cs] = (y * s_ref[:, cs]).astype(o_ref.dtype)


def _pool_mixer(proj, pool_w, pool_scale, *, tm=512):
    t = proj.shape[0]
    ng, group, _ = pool_w.shape
    width = ng * group
    rpb = tm // SUBLANES
    nhalo = t // SUBLANES
    blk = 2 * (tm * width * 4 + 2 * SUBLANES * width * 4 + ng * group * group * 4 + tm * width * 2) + 6 * tm * group * 4
    return pl.pallas_call(
        functools.partial(_pool_kernel, tm=tm, seq=t, group=group),
        grid=(t // tm,),
        in_specs=[pl.BlockSpec((tm, width), lambda i: (i, 0)),
                  pl.BlockSpec((SUBLANES, width), lambda i: (jnp.maximum(i * rpb - 1, 0), 0)),
                  pl.BlockSpec((SUBLANES, width), lambda i: (jnp.minimum((i + 1) * rpb, nhalo - 1), 0)),
                  pl.BlockSpec((ng, group, group), lambda i: (0, 0, 0)),
                  pl.BlockSpec((1, width), lambda i: (0, 0))],
        out_specs=pl.BlockSpec((tm, width), lambda i: (i, 0)),
        out_shape=jax.ShapeDtypeStruct((t, width), BF16),
        compiler_params=_params(("arbitrary",), _vmem_limit(blk)),
        name="pool_mixer",
    )(proj, proj, proj, pool_w, pool_scale.reshape(1, width))


def _chunk_cumsum(x, reverse):
    n = x.shape[0]
    pos = lax.broadcasted_iota(jnp.int32, x.shape, 0) % CHUNK
    s = 1
    while s < CHUNK:
        if reverse:
            x = x + jnp.where(pos < CHUNK - s, pltpu.roll(x, n - s, 0), 0.0)
        else:
            x = x + jnp.where(pos >= s, pltpu.roll(x, s, 0), 0.0)
        s *= 2
    return x


def _lower_bound(lb_raw, layer):
    e = jnp.exp(lb_raw - jnp.max(lb_raw, axis=0, keepdims=True))
    return jnp.sum(e[:layer + 1], axis=0, keepdims=True) / jnp.sum(e, axis=0, keepdims=True)


def _hgrn_direction(q_raw, f_raw, v, lb, st_ref, h, mask, *, reverse):
    tb = q_raw.shape[0]
    nc = tb // CHUNK
    q = q_raw * jax.nn.sigmoid(q_raw)
    f = lb + (1.0 - lb) * jax.nn.sigmoid(f_raw)
    logf = jnp.log(f)
    k = 1.0 - f
    b = _chunk_cumsum(logf, reverse)
    b3 = b.reshape(nc, CHUNK, HEAD)
    edge = 0 if reverse else CHUNK - 1
    b_last = b3[:, edge:edge + 1, :]
    qd = (q * jnp.exp(b)).astype(BF16)
    kd = (k * jnp.exp(-b)).astype(BF16)
    k_end = (k.reshape(nc, CHUNK, HEAD) * jnp.exp(b_last - b3)).astype(BF16)
    decay = jnp.exp(b_last)
    vb = v.astype(BF16)

    a = lax.dot_general(qd, kd, (((1,), (1,)), ((), ())), preferred_element_type=F32)
    a = jnp.where(mask, a, 0.0)
    o_intra = jnp.dot(a.astype(BF16), vb, preferred_element_type=F32)

    st = st_ref[h]
    o_inter = [None] * nc
    order = range(nc - 1, -1, -1) if reverse else range(nc)
    for c in order:
        rows = slice(c * CHUNK, (c + 1) * CHUNK)
        o_inter[c] = lax.dot_general(qd[rows], st.astype(BF16), (((1,), (1,)), ((), ())),
                                     preferred_element_type=F32)
        d_st = lax.dot_general(vb[rows], k_end[c], (((0,), (0,)), ((), ())), preferred_element_type=F32)
        st = st * decay[c] + d_st
    st_ref[h] = st
    return o_intra + jnp.concatenate(o_inter, axis=0)


def _intra_mask(tb, reverse):
    r = lax.broadcasted_iota(jnp.int32, (tb, tb), 0)
    c = lax.broadcasted_iota(jnp.int32, (tb, tb), 1)
    same = (r // CHUNK) == (c // CHUNK)
    return same & ((c >= r) if reverse else (c <= r))


def _hgrn_fwd_kernel(q_ref, f_ref, i_ref, lb_ref, o_ref, st_ref, *, layer, heads):
    @pl.when(pl.program_id(0) == 0)
    def _():
        st_ref[...] = jnp.zeros_like(st_ref)

    mask = _intra_mask(q_ref.shape[0], False)

    def head(h, carry):
        hs = pl.ds(pl.multiple_of(h * HEAD, HEAD), HEAD)
        lb = _lower_bound(lb_ref[:, hs], layer)
        o_ref[:, hs] = _hgrn_direction(q_ref[:, hs], f_ref[:, hs], i_ref[:, hs], lb, st_ref, h, mask,
                                       reverse=False)
        return carry

    lax.fori_loop(0, heads, head, 0, unroll=4)


def _hgrn_bwd_kernel(q_ref, f_ref, i_ref, g_ref, of_ref, lb_ref, gain_ref, y_ref, st_ref, *, layer, heads):
    @pl.when(pl.program_id(0) == 0)
    def _():
        st_ref[...] = jnp.zeros_like(st_ref)

    mask = _intra_mask(q_ref.shape[0], True)

    def head(h, carry):
        hs = pl.ds(pl.multiple_of(h * HEAD, HEAD), HEAD)
        lb = _lower_bound(lb_ref[:, hs], layer)
        o = of_ref[:, hs] + _hgrn_direction(q_ref[:, hs], f_ref[:, hs], i_ref[:, hs], lb, st_ref, h, mask,
                                            reverse=True)
        o = o * lax.rsqrt(jnp.mean(o * o, axis=-1, keepdims=True) + EPS) * gain_ref[:, hs]
        g = g_ref[:, hs]
        y_ref[:, hs] = (o * (g * jax.nn.sigmoid(g))).astype(y_ref.dtype)
        return carry

    lax.fori_loop(0, heads, head, 0, unroll=4)


def _hgrn_mixer(proj, lb_fwd, lb_bwd, hgrn_norm, *, layer, width, col0, tb=256):
    t = proj.shape[0]
    heads = width // HEAD
    nb = t // tb
    c = col0 // width
    layers = lb_fwd.shape[0]
    blk_f = 2 * (3 * tb * width * 4 + layers * width * 4 + tb * width * 4) + heads * HEAD * HEAD * 4
    o_f = pl.pallas_call(
        functools.partial(_hgrn_fwd_kernel, layer=layer, heads=heads),
        grid=(nb,),
        in_specs=[pl.BlockSpec((tb, width), lambda b: (b, c)),
                  pl.BlockSpec((tb, width), lambda b: (b, c + 1)),
                  pl.BlockSpec((tb, width), lambda b: (b, c + 3)),
                  pl.BlockSpec((layers, width), lambda b: (0, 0))],
        out_specs=pl.BlockSpec((tb, width), lambda b: (b, 0)),
        out_shape=jax.ShapeDtypeStruct((t, width), F32),
        scratch_shapes=[pltpu.VMEM((heads, HEAD, HEAD), F32)],
        compiler_params=_params(("arbitrary",), _vmem_limit(blk_f)),
        name="hgrn_fwd",
    )(proj, proj, proj, lb_fwd)
    blk_b = 2 * (5 * tb * width * 4 + layers * width * 4 + width * 4 + tb * width * 2) + heads * HEAD * HEAD * 4
    return pl.pallas_call(
        functools.partial(_hgrn_bwd_kernel, layer=layer, heads=heads),
        grid=(nb,),
        in_specs=[pl.BlockSpec((tb, width), lambda b: (nb - 1 - b, c)),
                  pl.BlockSpec((tb, width), lambda b: (nb - 1 - b, c + 2)),
                  pl.BlockSpec((tb, width), lambda b: (nb - 1 - b, c + 3)),
                  pl.BlockSpec((tb, width), lambda b: (nb - 1 - b, c + 4)),
                  pl.BlockSpec((tb, width), lambda b: (nb - 1 - b, 0)),
                  pl.BlockSpec((layers, width), lambda b: (0, 0)),
                  pl.BlockSpec((1, width), lambda b: (0, 0))],
        out_specs=pl.BlockSpec((tb, width), lambda b: (nb - 1 - b, 0)),
        out_shape=jax.ShapeDtypeStruct((t, width), BF16),
        scratch_shapes=[pltpu.VMEM((heads, HEAD, HEAD), F32)],
        compiler_params=_params(("arbitrary",), _vmem_limit(blk_b)),
        name="hgrn_bwd",
    )(proj, proj, proj, proj, o_f, lb_bwd, hgrn_norm.reshape(1, width))


def _xattn_kernel(q_ref, k_ref, v_ref, o_ref, *, scale):
    s = lax.dot_general(q_ref[...], k_ref[...], (((1,), (1,)), ((), ())), preferred_element_type=F32) * scale
    s = s - jnp.max(s, axis=-1, keepdims=True)
    p = jnp.exp(s)
    p = p / jnp.sum(p, axis=-1, keepdims=True)
    o_ref[...] = jnp.dot(p.astype(BF16), v_ref[...], preferred_element_type=F32).astype(o_ref.dtype)


def _xattn(q, k, v, *, tm=512):
    t, d = q.shape
    m = k.shape[0]
    dh = d // XATTN_HEADS
    blk = 2 * (2 * tm * dh * 2 + 2 * m * dh * 2) + 4 * tm * m * 4 + tm * dh * 4
    return pl.pallas_call(
        functools.partial(_xattn_kernel, scale=dh ** -0.5),
        grid=(XATTN_HEADS, t // tm),
        in_specs=[pl.BlockSpec((tm, dh), lambda h, i: (i, h)),
                  pl.BlockSpec((m, dh), lambda h, i: (0, h)),
                  pl.BlockSpec((m, dh), lambda h, i: (0, h))],
        out_specs=pl.BlockSpec((tm, dh), lambda h, i: (i, h)),
        out_shape=jax.ShapeDtypeStruct((t, d), BF16),
        compiler_params=_params(("arbitrary", "arbitrary"), _vmem_limit(blk)),
        name="xattn",
    )(q, k, v)


def _pack_halves(x):
    w = x.shape[1] // 2
    hi = lax.bitcast_convert_type(x[:, :w].astype(BF16).astype(F32), U32)
    lo = lax.bitcast_convert_type(x[:, w:].astype(BF16).astype(F32), U32)
    return hi | (lo >> 16)


def _unpack_halves(p):
    hi = lax.bitcast_convert_type(p & jnp.uint32(0xFFFF0000), F32)
    lo = lax.bitcast_convert_type(p << 16, F32)
    return hi, lo


def _load_row_slabs(flat_ref, s, j0, nj):
    rows = flat_ref.shape[0] // s
    return jnp.concatenate([flat_ref[pl.ds(j0 + j, rows, stride=s), :] for j in range(nj)], axis=1)


def _store_row_slabs(ref, val):
    for j in range(ref.shape[1]):
        ref[:, j, :] = val[:, j * LANES:(j + 1) * LANES]


def _router_kernel(x_ref, g_ref, wr_ref, hp_ref, lg_ref):
    x = x_ref[...]
    ms = jnp.mean(x * x, axis=-1, keepdims=True)
    h = x * lax.rsqrt(ms + EPS) * g_ref[...]
    _store_row_slabs(hp_ref, _pack_halves(h))
    w = wr_ref[...]
    h_hi = h.astype(BF16)
    h_lo = (h - h_hi.astype(F32)).astype(BF16)
    w_hi = w.astype(BF16)
    w_lo = (w - w_hi.astype(F32)).astype(BF16)
    lg_ref[...] = (jnp.dot(h_hi, w_hi, preferred_element_type=F32)
                   + (jnp.dot(h_hi, w_lo, preferred_element_type=F32)
                      + jnp.dot(h_lo, w_hi, preferred_element_type=F32)))


def _router(x, gain, w_router, *, tm=256):
    t, d = x.shape
    blk = 2 * (tm * d * 4 + d * 4 + d * ROUTER_COLS * 4 + tm * d * 2 + tm * ROUTER_COLS * 4) + 3 * tm * d * 4
    return pl.pallas_call(
        _router_kernel,
        grid=(t // tm,),
        in_specs=[pl.BlockSpec((tm, d), lambda i: (i, 0)),
                  pl.BlockSpec((1, d), lambda i: (0, 0)),
                  pl.BlockSpec((d, ROUTER_COLS), lambda i: (0, 0))],
        out_specs=[pl.BlockSpec((tm, d // 2 // LANES, LANES), lambda i: (i, 0, 0)),
                   pl.BlockSpec((tm, ROUTER_COLS), lambda i: (i, 0))],
        out_shape=[jax.ShapeDtypeStruct((t, d // 2 // LANES, LANES), U32),
                   jax.ShapeDtypeStruct((t, ROUTER_COLS), F32)],
        compiler_params=_params(("arbitrary",), _vmem_limit(blk)),
        name="moe_router",
    )(x, gain.reshape(1, d), w_router)


def _start_row_copies(idx_ref, base, n, src_ref, dst_at, sem):
    def issue(r, carry):
        pltpu.make_async_copy(src_ref.at[idx_ref[base + r]], dst_at(r), sem).start()
        return carry

    lax.fori_loop(0, n, issue, 0, unroll=8)


def _wait_row_copies(n, src_ref, dst_at, sem):
    def drain(r, carry):
        pltpu.make_async_copy(src_ref.at[0], dst_at(r), sem).wait()
        return carry

    lax.fori_loop(0, n, drain, 0, unroll=8)


def _gather_rows_kernel(idx_ref, nrows_ref, src_ref, o_ref, buf, sem, *, rows):
    b = pl.program_id(0)
    slot = b % 2
    used = b * rows < nrows_ref[0]
    next_used = ((b + 1) * rows < nrows_ref[0]) & (b + 1 < pl.num_programs(0))

    @pl.when((b == 0) & used)
    def _():
        _start_row_copies(idx_ref, 0, rows, src_ref, lambda r: buf.at[0, r], sem.at[0])

    @pl.when(next_used)
    def _():
        _start_row_copies(idx_ref, (b + 1) * rows, rows, src_ref, lambda r: buf.at[1 - slot, r], sem.at[1 - slot])

    @pl.when(used)
    def _():
        _wait_row_copies(rows, src_ref, lambda r: buf.at[slot, r], sem.at[slot])
        o_ref[...] = buf[slot]

    @pl.when(jnp.logical_not(used))
    def _():
        o_ref[...] = jnp.zeros_like(o_ref)


def _gather_rows(idx, nrows, src, *, rows=256):
    n = idx.shape[0]
    slab = src.shape[1:]
    blk = 4 * rows * slab[0] * slab[1] * 4
    return pl.pallas_call(
        functools.partial(_gather_rows_kernel, rows=rows),
        grid_spec=pltpu.PrefetchScalarGridSpec(
            num_scalar_prefetch=2,
            grid=(n // rows,),
            in_specs=[pl.BlockSpec(memory_space=pltpu.HBM)],
            out_specs=pl.BlockSpec((rows,) + slab, lambda b, idx, nrows: (b, 0, 0)),
            scratch_shapes=[pltpu.VMEM((2, rows) + slab, src.dtype), pltpu.SemaphoreType.DMA((2,))],
        ),
        out_shape=jax.ShapeDtypeStruct((n,) + slab, src.dtype),
        compiler_params=_params(("arbitrary",), _vmem_limit(blk)),
        name="gather_rows",
    )(idx, nrows, src)


BLOCK_COPY_PRIORITY = 0


def _start_first_blocks(cnt_ref, start_ref, e, in_copy_at):
    n = cnt_ref[e]
    b = start_ref[e]

    @pl.when(n > 0)
    def _():
        in_copy_at(b, 0).start(priority=BLOCK_COPY_PRIORITY)

    @pl.when(n > 1)
    def _():
        in_copy_at(b + 1, 1).start(priority=BLOCK_COPY_PRIORITY)


def _expert_grid_step(cnt_ref, start_ref, in_copy_at, out_copy_at, prologue, compute):
    e = pl.program_id(1)
    ne = pl.num_programs(1)
    step = pl.program_id(0) * ne + e
    nblk = cnt_ref[e]
    blk0 = start_ref[e]

    @pl.when(step == 0)
    def _():
        _start_first_blocks(cnt_ref, start_ref, e, in_copy_at)

    @pl.when(nblk > 0)
    def _():
        prologue()

        def body(k, carry):
            slot = k % 2
            in_copy_at(blk0 + k, slot).wait()

            @pl.when(k >= 2)
            def _():
                out_copy_at(blk0 + k - 2, slot).wait()

            compute(slot)
            out_copy_at(blk0 + k, slot).start(priority=BLOCK_COPY_PRIORITY)

            @pl.when(k + 2 < nblk)
            def _():
                in_copy_at(blk0 + k + 2, slot).start(priority=BLOCK_COPY_PRIORITY)

            return carry

        lax.fori_loop(0, nblk, body, 0)

        @pl.when(nblk >= 2)
        def _():
            out_copy_at(blk0 + nblk - 2, nblk % 2).wait()

        out_copy_at(blk0 + nblk - 1, (nblk - 1) % 2).wait()

    @pl.when(step + 1 < pl.num_programs(0) * ne)
    def _():
        _start_first_blocks(cnt_ref, start_ref, (e + 1) % ne, in_copy_at)


def _moe_up_kernel(cnt_ref, start_ref, xs_ref, w1_ref, w3_ref, h_ref, w1b_ref, w3b_ref, xbuf, obuf, xsem, osem,
                   *, bm):
    f = pl.program_id(0)
    d, fc = w1b_ref.shape
    nslab = d // 2 // LANES
    xrows = bm * nslab

    def x_copy(blk, slot):
        row0 = pl.multiple_of(blk * xrows, xrows)
        return pltpu.make_async_copy(xs_ref.at[pl.ds(row0, xrows)], xbuf.at[slot], xsem.at[slot])

    def h_copy(blk, slot):
        row0 = pl.multiple_of(blk * bm, bm)
        col0 = pl.multiple_of(f * fc, fc)
        return pltpu.make_async_copy(obuf.at[slot], h_ref.at[pl.ds(row0, bm), pl.ds(col0, fc)], osem.at[slot])

    def cast_weights():
        w1b_ref[...] = w1_ref[0].astype(BF16)
        w3b_ref[...] = w3_ref[0].astype(BF16)

    def compute(slot):
        hi, lo = _unpack_halves(_load_row_slabs(xbuf.at[slot], nslab, 0, nslab))
        hi = hi.astype(BF16)
        lo = lo.astype(BF16)
        d2 = d // 2
        a = (jnp.dot(hi, w1b_ref[:d2, :], preferred_element_type=F32)
             + jnp.dot(lo, w1b_ref[d2:, :], preferred_element_type=F32))
        c = (jnp.dot(hi, w3b_ref[:d2, :], preferred_element_type=F32)
             + jnp.dot(lo, w3b_ref[d2:, :], preferred_element_type=F32))
        obuf[slot] = (a * jax.nn.sigmoid(a) * c).astype(obuf.dtype)

    _expert_grid_step(cnt_ref, start_ref, x_copy, h_copy, cast_weights, compute)


def _moe_down_kernel(cnt_ref, start_ref, h_ref, w2_ref, y_ref, w2b_ref, hbuf, obuf, hsem, osem, *, bm):
    c = pl.program_id(0)
    nj = obuf.shape[2]

    def h_copy(blk, slot):
        row0 = pl.multiple_of(blk * bm, bm)
        return pltpu.make_async_copy(h_ref.at[pl.ds(row0, bm)], hbuf.at[slot], hsem.at[slot])

    def y_copy(blk, slot):
        row0 = pl.multiple_of(blk * bm, bm)
        j0 = pl.multiple_of(c * nj, nj)
        return pltpu.make_async_copy(obuf.at[slot], y_ref.at[pl.ds(row0, bm), pl.ds(j0, nj)], osem.at[slot])

    def cast_weights():
        w2b_ref[...] = w2_ref[0].astype(BF16)

    def compute(slot):
        y = jnp.dot(hbuf[slot].astype(BF16), w2b_ref[...], preferred_element_type=F32)
        _store_row_slabs(obuf.at[slot], _pack_halves(y))

    _expert_grid_step(cnt_ref, start_ref, h_copy, y_copy, cast_weights, compute)


def _moe_experts(xs, blk_cnt, blk_start, w1, w3, w2, *, bm, fc=512, nc=2048):
    p, nslab, _ = xs.shape
    d = 2 * nslab * LANES
    ne, _, de = w1.shape
    any_spec = pl.BlockSpec(memory_space=pltpu.HBM)
    blk_up = (4 * d * fc * 4 + 2 * d * fc * 2 + 2 * bm * nslab * LANES * 4 + 2 * bm * fc * 4
              + 2 * bm * d * 2 + 3 * bm * fc * 4)
    h = pl.pallas_call(
        functools.partial(_moe_up_kernel, bm=bm),
        grid_spec=pltpu.PrefetchScalarGridSpec(
            num_scalar_prefetch=2,
            grid=(de // fc, ne),
            in_specs=[any_spec,
                      pl.BlockSpec((1, d, fc), lambda f, e, cnt, start: (e, 0, f)),
                      pl.BlockSpec((1, d, fc), lambda f, e, cnt, start: (e, 0, f))],
            out_specs=any_spec,
            scratch_shapes=[pltpu.VMEM((d, fc), BF16), pltpu.VMEM((d, fc), BF16),
                            pltpu.VMEM((2, bm * nslab, LANES), U32), pltpu.VMEM((2, bm, fc), F32),
                            pltpu.SemaphoreType.DMA((2,)), pltpu.SemaphoreType.DMA((2,))],
        ),
        out_shape=jax.ShapeDtypeStruct((p, de), F32),
        compiler_params=_params(("arbitrary", "arbitrary"), _vmem_limit(blk_up)),
        name="moe_up",
    )(blk_cnt, blk_start, xs.reshape(p * nslab, LANES), w1, w3)
    nj = nc // 2 // LANES
    blk_dn = 2 * de * nc * 4 + de * nc * 2 + 3 * bm * de * 4 + 2 * bm * nj * LANES * 4 + 2 * bm * nc * 4
    return pl.pallas_call(
        functools.partial(_moe_down_kernel, bm=bm),
        grid_spec=pltpu.PrefetchScalarGridSpec(
            num_scalar_prefetch=2,
            grid=(d // nc, ne),
            in_specs=[any_spec,
                      pl.BlockSpec((1, de, nc), lambda c, e, cnt, start: (e, 0, c))],
            out_specs=any_spec,
            scratch_shapes=[pltpu.VMEM((de, nc), BF16),
                            pltpu.VMEM((2, bm, de), F32), pltpu.VMEM((2, bm, nj, LANES), U32),
                            pltpu.SemaphoreType.DMA((2,)), pltpu.SemaphoreType.DMA((2,))],
        ),
        out_shape=jax.ShapeDtypeStruct((p, nslab, LANES), U32),
        compiler_params=_params(("arbitrary", "arbitrary"), _vmem_limit(blk_dn)),
        name="moe_down",
    )(blk_cnt, blk_start, h, w2)


def _combine_kernel(back_ref, x_ref, ys_ref, ew_ref, g_ref, o_ref, ybuf, sem, *, nc):
    i = pl.program_id(0)
    nt = pl.num_programs(0)
    tm, d = x_ref.shape
    t = nt * tm
    nslab = d // 2 // LANES
    slot = i % 2

    def row_at(s, k):
        def dst_at(r):
            return ybuf.at[s, pl.ds(pl.multiple_of((k * tm + r) * nslab, nslab), nslab)]
        return dst_at

    def start_tile(tile, s):
        for k in range(TOP_K):
            _start_row_copies(back_ref, k * t + tile * tm, tm, ys_ref, row_at(s, k), sem.at[s])

    @pl.when(i == 0)
    def _():
        start_tile(0, 0)

    @pl.when(i + 1 < nt)
    def _():
        start_tile(i + 1, 1 - slot)

    for k in range(TOP_K):
        _wait_row_copies(tm, ys_ref, row_at(slot, k), sem.at[slot])

    ew = ew_ref[...]
    w0 = ew[:, 0:1]
    w1 = ew[:, 1:2]
    half = nc // 2
    y0_ref = ybuf.at[slot, pl.ds(0, tm * nslab)]
    y1_ref = ybuf.at[slot, pl.ds(tm * nslab, tm * nslab)]
    pieces = []
    ss = None
    for c in range(x_ref.shape[1] // nc):
        hi0, lo0 = _unpack_halves(_load_row_slabs(y0_ref, nslab, c * half // LANES, half // LANES))
        hi1, lo1 = _unpack_halves(_load_row_slabs(y1_ref, nslab, c * half // LANES, half // LANES))
        for j, (p0, p1) in enumerate(((hi0, hi1), (lo0, lo1))):
            cols = slice(c * nc + j * half, c * nc + (j + 1) * half)
            z = x_ref[:, cols] + w0 * p0 + w1 * p1
            pieces.append((cols, z))
            s = jnp.sum(z * z, axis=-1, keepdims=True)
            ss = s if ss is None else ss + s
    inv = lax.rsqrt(ss / x_ref.shape[1] + EPS)
    for cols, z in pieces:
        o_ref[:, cols] = z * inv * g_ref[:, cols]


def _combine(x, ys, back, e_w, gain, *, nc, tm=256):
    t, d = x.shape
    nt = t // tm
    nslab = ys.shape[1]
    blk = 2 * (2 * tm * d * 4 + tm * LANES * 4 + d * 4) + 2 * TOP_K * tm * d * 2 + 3 * tm * d * 4
    return pl.pallas_call(
        functools.partial(_combine_kernel, nc=nc),
        grid_spec=pltpu.PrefetchScalarGridSpec(
            num_scalar_prefetch=1,
            grid=(nt,),
            in_specs=[pl.BlockSpec((tm, d), lambda i, back: (i, 0)),
                      pl.BlockSpec(memory_space=pltpu.HBM),
                      pl.BlockSpec((tm, TOP_K), lambda i, back: (i, 0)),
                      pl.BlockSpec((1, d), lambda i, back: (0, 0))],
            out_specs=pl.BlockSpec((tm, d), lambda i, back: (i, 0)),
            scratch_shapes=[pltpu.VMEM((2, TOP_K * tm * nslab, LANES), ys.dtype), pltpu.SemaphoreType.DMA((2,))],
        ),
        out_shape=jax.ShapeDtypeStruct((t, d), F32),
        compiler_params=_params(("arbitrary",), _vmem_limit(blk)),
        name="moe_combine",
    )(back, x, ys, e_w, gain.reshape(1, d))


def _route(logits, *, bm):
    t = logits.shape[0]
    g_logits = logits[:, :N_GROUPS]
    e_logits = logits[:, N_GROUPS:N_GROUPS + N_EXPERTS].reshape(t, N_GROUPS, EXPERTS_PER_GROUP)
    g_idx = jnp.argmax(g_logits, axis=-1).astype(jnp.int32)
    g_w = jnp.take_along_axis(jax.nn.softmax(g_logits, axis=-1), g_idx[:, None], axis=-1)
    e_sel = jnp.take_along_axis(e_logits, g_idx[:, None, None], axis=1)[:, 0]
    top_v, top_i = lax.top_k(e_sel, TOP_K)
    e_w = jax.nn.softmax(top_v, axis=-1) * g_w
    eid = g_idx[:, None] * EXPERTS_PER_GROUP + top_i.astype(jnp.int32)

    a = t * TOP_K
    flat_e = eid.reshape(a)
    onehot = (flat_e[:, None] == jnp.arange(N_EXPERTS, dtype=jnp.int32)[None, :]).astype(jnp.int32)
    csum = jnp.cumsum(onehot, axis=0)
    rank = jnp.take_along_axis(csum, flat_e[:, None], axis=1)[:, 0] - 1
    counts = csum[-1]
    padded = ((counts + bm - 1) // bm) * bm
    pad_end = jnp.cumsum(padded)
    start_pad = pad_end - padded
    dest = (start_pad[flat_e] + rank).astype(jnp.int32)
    p = a + N_EXPERTS * bm
    blk_cnt = (padded // bm).astype(jnp.int32)
    blk_start = (start_pad // bm).astype(jnp.int32)
    rows_used = pad_end[-1:].astype(jnp.int32)
    src_tok = (jnp.arange(p, dtype=jnp.int32) % t).at[dest].set(jnp.arange(a, dtype=jnp.int32) // TOP_K)
    back = dest.reshape(t, TOP_K).T.reshape(a)
    return e_w, src_tok, back, blk_cnt, blk_start, rows_used


MOE_BLOCK_ROWS = 256
MOE_DOWN_COLS = 4096


def kernel(x, mem, norm_mix, w_in, pool_w, pool_scale, lb_fwd, lb_bwd, hgrn_norm, w_out, norm_xattn, norm_mem,
           w_q, w_k, w_v, w_o, norm_moe, w_router_group, w_router_expert, w1, w3, w2, norm_final):
    bsz, seq, d = x.shape
    assert bsz == 1 and w_in.shape[0] == 1, "one sequence, one layer (the final norm is fused into the MoE combine)"
    l = 0
    pool_width = pool_w.shape[1] * pool_w.shape[2]
    hgrn_width = hgrn_norm.shape[1]
    xt = x.reshape(seq, d)
    mem_t = mem.reshape(mem.shape[1], d)
    h = _rmsnorm(xt, norm_mix[l], BF16)
    proj = _matmul([h], w_in[l], out_dtype=F32)
    y_pool = _pool_mixer(proj, pool_w[l], pool_scale[l])
    y_hgrn = _hgrn_mixer(proj, lb_fwd, lb_bwd, hgrn_norm[l], layer=l, width=hgrn_width, col0=pool_width)
    xt = _matmul([y_pool, y_hgrn], w_out[l], resid=xt, out_dtype=F32)
    mem_n = _rmsnorm(mem_t, norm_mem[l], BF16)
    hx = _rmsnorm(xt, norm_xattn[l], BF16)
    q = _matmul([hx], w_q[l], out_dtype=BF16)
    k = _matmul([mem_n], w_k[l], out_dtype=BF16)
    v = _matmul([mem_n], w_v[l], out_dtype=BF16)
    o = _xattn(q, k, v)
    xt = _matmul([o], w_o[l], resid=xt, out_dtype=F32)
    w_router = jnp.concatenate(
        [w_router_group[l],
         jnp.transpose(w_router_expert[l], (1, 0, 2)).reshape(d, N_EXPERTS),
         jnp.zeros((d, ROUTER_COLS - N_GROUPS - N_EXPERTS), F32)], axis=1)
    hp, logits = _router(xt, norm_moe[l], w_router)
    e_w, src_tok, back, blk_cnt, blk_start, rows_used = _route(logits, bm=MOE_BLOCK_ROWS)
    xs = _gather_rows(src_tok, rows_used, hp)
    ys = _moe_experts(xs, blk_cnt, blk_start, w1[l], w3[l], w2[l], bm=MOE_BLOCK_ROWS, nc=MOE_DOWN_COLS)
    out = _combine(xt, ys, back, e_w, norm_final, nc=MOE_DOWN_COLS)
    return out.reshape(bsz, seq, d)
```

```python
import functools

import jax
import jax.numpy as jnp
from jax import lax
from jax.experimental import pallas as pl
from jax.experimental.pallas import tpu as pltpu

F32 = jnp.float32
BF16 = jnp.bfloat16
U32 = jnp.uint32

EPS = 1e-6
LANES = 128
SUBLANES = 8
VMEM_BYTES_V7X = 64 * 1024 * 1024

POOL_WINDOWS = (2, 4, 8, 16)
HEAD = 128
CHUNK = 64
XATTN_HEADS = 4
N_GROUPS = 4
EXPERTS_PER_GROUP = 8
N_EXPERTS = N_GROUPS * EXPERTS_PER_GROUP
TOP_K = 2
ROUTER_COLS = LANES


def _params(sem, vmem_bytes):
    return pltpu.CompilerParams(dimension_semantics=sem, vmem_limit_bytes=int(vmem_bytes))


def _vmem_limit(block_bytes):
    return min(int(block_bytes * 1.25) + (6 << 20), VMEM_BYTES_V7X - (4 << 20))


def _rmsnorm_kernel(x_ref, g_ref, o_ref):
    x = x_ref[...]
    ms = jnp.mean(x * x, axis=-1, keepdims=True)
    o_ref[...] = (x * lax.rsqrt(ms + EPS) * g_ref[...]).astype(o_ref.dtype)


def _rmsnorm(x, gain, out_dtype, tm=512):
    t, d = x.shape
    tm = min(tm, t)
    blk = tm * d * (4 + jnp.dtype(out_dtype).itemsize) * 2
    return pl.pallas_call(
        _rmsnorm_kernel,
        grid=(t // tm,),
        in_specs=[pl.BlockSpec((tm, d), lambda i: (i, 0)),
                  pl.BlockSpec((1, d), lambda i: (0, 0))],
        out_specs=pl.BlockSpec((tm, d), lambda i: (i, 0)),
        out_shape=jax.ShapeDtypeStruct((t, d), out_dtype),
        compiler_params=_params(("arbitrary",), _vmem_limit(blk)),
        name="rmsnorm",
    )(x, gain.reshape(1, d))


def _mm_kernel(*refs, n_a, has_resid):
    a_refs = refs[:n_a]
    w_ref = refs[n_a]
    r_ref = refs[n_a + 1] if has_resid else None
    o_ref, wb_ref = refs[-2], refs[-1]

    @pl.when(pl.program_id(1) == 0)
    def _():
        wb_ref[...] = w_ref[...].astype(BF16)

    acc = None
    k0 = 0
    for a_ref in a_refs:
        kk = a_ref.shape[1]
        part = jnp.dot(a_ref[...], wb_ref[k0:k0 + kk, :], preferred_element_type=F32)
        acc = part if acc is None else acc + part
        k0 += kk
    if has_resid:
        acc = acc + r_ref[...]
    o_ref[...] = acc.astype(o_ref.dtype)


def _matmul(a_parts, w, *, resid=None, out_dtype=F32, tm=1024, tn=512):
    m = a_parts[0].shape[0]
    k, n = w.shape
    assert sum(a.shape[1] for a in a_parts) == k
    tm = min(tm, m)
    in_specs = [pl.BlockSpec((tm, a.shape[1]), lambda j, i: (i, 0)) for a in a_parts]
    in_specs.append(pl.BlockSpec((k, tn), lambda j, i: (0, j)))
    args = list(a_parts) + [w]
    if resid is not None:
        in_specs.append(pl.BlockSpec((tm, tn), lambda j, i: (i, j)))
        args.append(resid)
    osz = jnp.dtype(out_dtype).itemsize
    blk = (2 * tm * k * 2 + 2 * k * tn * 4 + k * tn * 2 + 2 * tm * tn * osz
           + (2 * tm * tn * 4 if resid is not None else 0) + tm * tn * 4)
    return pl.pallas_call(
        functools.partial(_mm_kernel, n_a=len(a_parts), has_resid=resid is not None),
        grid=(n // tn, m // tm),
        in_specs=in_specs,
        out_specs=pl.BlockSpec((tm, tn), lambda j, i: (i, j)),
        out_shape=jax.ShapeDtypeStruct((m, n), out_dtype),
        scratch_shapes=[pltpu.VMEM((k, tn), BF16)],
        compiler_params=_params(("arbitrary", "arbitrary"), _vmem_limit(blk)),
        name="matmul",
    )(*args)


def _pool_kernel(u_ref, up_ref, un_ref, w_ref, s_ref, o_ref, *, tm, seq, group):
    i = pl.program_id(0)
    last = pl.num_programs(0) - 1
    n = tm + 2 * SUBLANES
    row = lax.broadcasted_iota(jnp.int32, (tm, 1), 0) + i * tm
    for gi, w in enumerate(POOL_WINDOWS):
        cs = slice(gi * group, (gi + 1) * group)
        u = u_ref[:, cs]
        prev = jnp.where(i > 0, up_ref[:, cs], 0.0)
        nxt = jnp.where(i < last, un_ref[:, cs], 0.0)
        f = jnp.concatenate([prev, u, nxt], axis=0)
        step = 1
        while step < w:
            f = f + pltpu.roll(f, n - step, 0)
            step *= 2
        half = w // 2
        first = SUBLANES - half
        if first:
            f = pltpu.roll(f, n - first, 0)
        win = f[:tm]
        lo = jnp.maximum(row - half, 0)
        hi = jnp.minimum(row + half - 1, seq - 1)
        cnt = (hi - lo + 1).astype(F32)
        mixed = win / cnt - u
        y = jnp.dot(mixed.astype(BF16), w_ref[gi].astype(BF16), preferred_element_type=F32)
        o_ref[:, cs] = (y * s_ref[:, cs]).astype(o_ref.dtype)


def _pool_mixer(proj, pool_w, pool_scale, *, tm=512):
    t = proj.shape[0]
    ng, group, _ = pool_w.shape
    width = ng * group
    rpb = tm // SUBLANES
    nhalo = t // SUBLANES
    blk = 2 * (tm * width * 4 + 2 * SUBLANES * width * 4 + ng * group * group * 4 + tm * width * 2) + 6 * tm * group * 4
    return pl.pallas_call(
        functools.partial(_pool_kernel, tm=tm, seq=t, group=group),
        grid=(t // tm,),
        in_specs=[pl.BlockSpec((tm, width), lambda i: (i, 0)),
                  pl.BlockSpec((SUBLANES, width), lambda i: (jnp.maximum(i * rpb - 1, 0), 0)),
                  pl.BlockSpec((SUBLANES, width), lambda i: (jnp.minimum((i + 1) * rpb, nhalo - 1), 0)),
                  pl.BlockSpec((ng, group, group), lambda i: (0, 0, 0)),
                  pl.BlockSpec((1, width), lambda i: (0, 0))],
        out_specs=pl.BlockSpec((tm, width), lambda i: (i, 0)),
        out_shape=jax.ShapeDtypeStruct((t, width), BF16),
        compiler_params=_params(("arbitrary",), _vmem_limit(blk)),
        name="pool_mixer",
    )(proj, proj, proj, pool_w, pool_scale.reshape(1, width))


def _chunk_cumsum(x, reverse):
    n = x.shape[0]
    pos = lax.broadcasted_iota(jnp.int32, x.shape, 0) % CHUNK
    s = 1
    while s < CHUNK:
        if reverse:
            x = x + jnp.where(pos < CHUNK - s, pltpu.roll(x, n - s, 0), 0.0)
        else:
            x = x + jnp.where(pos >= s, pltpu.roll(x, s, 0), 0.0)
        s *= 2
    return x


def _lower_bound(lb_raw, layer):
    e = jnp.exp(lb_raw - jnp.max(lb_raw, axis=0, keepdims=True))
    return jnp.sum(e[:layer + 1], axis=0, keepdims=True) / jnp.sum(e, axis=0, keepdims=True)


def _hgrn_direction(q_raw, f_raw, v, lb, st_ref, h, mask, *, reverse):
    tb = q_raw.shape[0]
    nc = tb // CHUNK
    q = q_raw * jax.nn.sigmoid(q_raw)
    f = lb + (1.0 - lb) * jax.nn.sigmoid(f_raw)
    logf = jnp.log(f)
    k = 1.0 - f
    b = _chunk_cumsum(logf, reverse)
    b3 = b.reshape(nc, CHUNK, HEAD)
    edge = 0 if reverse else CHUNK - 1
    b_last = b3[:, edge:edge + 1, :]
    qd = (q * jnp.exp(b)).astype(BF16)
    kd = (k * jnp.exp(-b)).astype(BF16)
    k_end = (k.reshape(nc, CHUNK, HEAD) * jnp.exp(b_last - b3)).astype(BF16)
    decay = jnp.exp(b_last)
    vb = v.astype(BF16)

    a = lax.dot_general(qd, kd, (((1,), (1,)), ((), ())), preferred_element_type=F32)
    a = jnp.where(mask, a, 0.0)
    o_intra = jnp.dot(a.astype(BF16), vb, preferred_element_type=F32)

    st = st_ref[h]
    o_inter = [None] * nc
    order = range(nc - 1, -1, -1) if reverse else range(nc)
    for c in order:
        rows = slice(c * CHUNK, (c + 1) * CHUNK)
        o_inter[c] = lax.dot_general(qd[rows], st.astype(BF16), (((1,), (1,)), ((), ())),
                                     preferred_element_type=F32)
        d_st = lax.dot_general(vb[rows], k_end[c], (((0,), (0,)), ((), ())), preferred_element_type=F32)
        st = st * decay[c] + d_st
    st_ref[h] = st
    return o_intra + jnp.concatenate(o_inter, axis=0)


def _intra_mask(tb, reverse):
    r = lax.broadcasted_iota(jnp.int32, (tb, tb), 0)
    c = lax.broadcasted_iota(jnp.int32, (tb, tb), 1)
    same = (r // CHUNK) == (c // CHUNK)
    return same & ((c >= r) if reverse else (c <= r))


def _hgrn_fwd_kernel(q_ref, f_ref, i_ref, lb_ref, o_ref, st_ref, *, layer, heads):
    @pl.when(pl.program_id(0) == 0)
    def _():
        st_ref[...] = jnp.zeros_like(st_ref)

    mask = _intra_mask(q_ref.shape[0], False)

    def head(h, carry):
        hs = pl.ds(pl.multiple_of(h * HEAD, HEAD), HEAD)
        lb = _lower_bound(lb_ref[:, hs], layer)
        o_ref[:, hs] = _hgrn_direction(q_ref[:, hs], f_ref[:, hs], i_ref[:, hs], lb, st_ref, h, mask,
                                       reverse=False)
        return carry

    lax.fori_loop(0, heads, head, 0, unroll=4)


def _hgrn_bwd_kernel(q_ref, f_ref, i_ref, g_ref, of_ref, lb_ref, gain_ref, y_ref, st_ref, *, layer, heads):
    @pl.when(pl.program_id(0) == 0)
    def _():
        st_ref[...] = jnp.zeros_like(st_ref)

    mask = _intra_mask(q_ref.shape[0], True)

    def head(h, carry):
        hs = pl.ds(pl.multiple_of(h * HEAD, HEAD), HEAD)
        lb = _lower_bound(lb_ref[:, hs], layer)
        o = of_ref[:, hs] + _hgrn_direction(q_ref[:, hs], f_ref[:, hs], i_ref[:, hs], lb, st_ref, h, mask,
                                            reverse=True)
        o = o * lax.rsqrt(jnp.mean(o * o, axis=-1, keepdims=True) + EPS) * gain_ref[:, hs]
        g = g_ref[:, hs]
        y_ref[:, hs] = (o * (g * jax.nn.sigmoid(g))).astype(y_ref.dtype)
        return carry

    lax.fori_loop(0, heads, head, 0, unroll=4)


def _hgrn_mixer(proj, lb_fwd, lb_bwd, hgrn_norm, *, layer, width, col0, tb=256):
    t = proj.shape[0]
    heads = width // HEAD
    nb = t // tb
    c = col0 // width
    layers = lb_fwd.shape[0]
    blk_f = 2 * (3 * tb * width * 4 + layers * width * 4 + tb * width * 4) + heads * HEAD * HEAD * 4
    o_f = pl.pallas_call(
        functools.partial(_hgrn_fwd_kernel, layer=layer, heads=heads),
        grid=(nb,),
        in_specs=[pl.BlockSpec((tb, width), lambda b: (b, c)),
                  pl.BlockSpec((tb, width), lambda b: (b, c + 1)),
                  pl.BlockSpec((tb, width), lambda b: (b, c + 3)),
                  pl.BlockSpec((layers, width), lambda b: (0, 0))],
        out_specs=pl.BlockSpec((tb, width), lambda b: (b, 0)),
        out_shape=jax.ShapeDtypeStruct((t, width), F32),
        scratch_shapes=[pltpu.VMEM((heads, HEAD, HEAD), F32)],
        compiler_params=_params(("arbitrary",), _vmem_limit(blk_f)),
        name="hgrn_fwd",
    )(proj, proj, proj, lb_fwd)
    blk_b = 2 * (5 * tb * width * 4 + layers * width * 4 + width * 4 + tb * width * 2) + heads * HEAD * HEAD * 4
    return pl.pallas_call(
        functools.partial(_hgrn_bwd_kernel, layer=layer, heads=heads),
        grid=(nb,),
        in_specs=[pl.BlockSpec((tb, width), lambda b: (nb - 1 - b, c)),
                  pl.BlockSpec((tb, width), lambda b: (nb - 1 - b, c + 2)),
                  pl.BlockSpec((tb, width), lambda b: (nb - 1 - b, c + 3)),
                  pl.BlockSpec((tb, width), lambda b: (nb - 1 - b, c + 4)),
                  pl.BlockSpec((tb, width), lambda b: (nb - 1 - b, 0)),
                  pl.BlockSpec((layers, width), lambda b: (0, 0)),
                  pl.BlockSpec((1, width), lambda b: (0, 0))],
        out_specs=pl.BlockSpec((tb, width), lambda b: (nb - 1 - b, 0)),
        out_shape=jax.ShapeDtypeStruct((t, width), BF16),
        scratch_shapes=[pltpu.VMEM((heads, HEAD, HEAD), F32)],
        compiler_params=_params(("arbitrary",), _vmem_limit(blk_b)),
        name="hgrn_bwd",
    )(proj, proj, proj, proj, o_f, lb_bwd, hgrn_norm.reshape(1, width))


def _xattn_kernel(q_ref, k_ref, v_ref, o_ref, *, scale):
    s = lax.dot_general(q_ref[...], k_ref[...], (((1,), (1,)), ((), ())), preferred_element_type=F32) * scale
    s = s - jnp.max(s, axis=-1, keepdims=True)
    p = jnp.exp(s)
    p = p / jnp.sum(p, axis=-1, keepdims=True)
    o_ref[...] = jnp.dot(p.astype(BF16), v_ref[...], preferred_element_type=F32).astype(o_ref.dtype)


def _xattn(q, k, v, *, tm=1024):
    t, d = q.shape
    m = k.shape[0]
    dh = d // XATTN_HEADS
    blk = 2 * (2 * tm * dh * 2 + 2 * m * dh * 2) + 4 * tm * m * 4 + tm * dh * 4
    return pl.pallas_call(
        functools.partial(_xattn_kernel, scale=dh ** -0.5),
        grid=(XATTN_HEADS, t // tm),
        in_specs=[pl.BlockSpec((tm, dh), lambda h, i: (i, h)),
                  pl.BlockSpec((m, dh), lambda h, i: (0, h)),
                  pl.BlockSpec((m, dh), lambda h, i: (0, h))],
        out_specs=pl.BlockSpec((tm, dh), lambda h, i: (i, h)),
        out_shape=jax.ShapeDtypeStruct((t, d), BF16),
        compiler_params=_params(("arbitrary", "arbitrary"), _vmem_limit(blk)),
        name="xattn",
    )(q, k, v)


def _pack_halves(x):
    w = x.shape[1] // 2
    hi = lax.bitcast_convert_type(x[:, :w].astype(BF16).astype(F32), U32)
    lo = lax.bitcast_convert_type(x[:, w:].astype(BF16).astype(F32), U32)
    return hi | (lo >> 16)


def _unpack_halves(p):
    hi = lax.bitcast_convert_type(p & jnp.uint32(0xFFFF0000), F32)
    lo = lax.bitcast_convert_type(p << 16, F32)
    return hi, lo


def _load_row_slabs(flat_ref, s, j0, nj):
    rows = flat_ref.shape[0] // s
    return jnp.concatenate([flat_ref[pl.ds(j0 + j, rows, stride=s), :] for j in range(nj)], axis=1)


def _store_row_slabs(ref, val):
    for j in range(ref.shape[1]):
        ref[:, j, :] = val[:, j * LANES:(j + 1) * LANES]


def _router_kernel(x_ref, g_ref, wr_ref, hp_ref, lg_ref):
    x = x_ref[...]
    ms = jnp.mean(x * x, axis=-1, keepdims=True)
    h = x * lax.rsqrt(ms + EPS) * g_ref[...]
    _store_row_slabs(hp_ref, _pack_halves(h))
    w = wr_ref[...]
    h_hi = h.astype(BF16)
    h_lo = (h - h_hi.astype(F32)).astype(BF16)
    w_hi = w.astype(BF16)
    w_lo = (w - w_hi.astype(F32)).astype(BF16)
    lg_ref[...] = (jnp.dot(h_hi, w_hi, preferred_element_type=F32)
                   + (jnp.dot(h_hi, w_lo, preferred_element_type=F32)
                      + jnp.dot(h_lo, w_hi, preferred_element_type=F32)))


def _router(x, gain, w_router, *, tm=256):
    t, d = x.shape
    blk = 2 * (tm * d * 4 + d * 4 + d * ROUTER_COLS * 4 + tm * d * 2 + tm * ROUTER_COLS * 4) + 3 * tm * d * 4
    return pl.pallas_call(
        _router_kernel,
        grid=(t // tm,),
        in_specs=[pl.BlockSpec((tm, d), lambda i: (i, 0)),
                  pl.BlockSpec((1, d), lambda i: (0, 0)),
                  pl.BlockSpec((d, ROUTER_COLS), lambda i: (0, 0))],
        out_specs=[pl.BlockSpec((tm, d // 2 // LANES, LANES), lambda i: (i, 0, 0)),
                   pl.BlockSpec((tm, ROUTER_COLS), lambda i: (i, 0))],
        out_shape=[jax.ShapeDtypeStruct((t, d // 2 // LANES, LANES), U32),
                   jax.ShapeDtypeStruct((t, ROUTER_COLS), F32)],
        compiler_params=_params(("arbitrary",), _vmem_limit(blk)),
        name="moe_router",
    )(x, gain.reshape(1, d), w_router)


def _start_row_copies(idx_ref, base, n, src_ref, dst_at, sem):
    def issue(r, carry):
        pltpu.make_async_copy(src_ref.at[idx_ref[base + r]], dst_at(r), sem).start()
        return carry

    lax.fori_loop(0, n, issue, 0, unroll=8)


def _wait_row_copies(n, src_ref, dst_at, sem):
    def drain(r, carry):
        pltpu.make_async_copy(src_ref.at[0], dst_at(r), sem).wait()
        return carry

    lax.fori_loop(0, n, drain, 0, unroll=8)


def _dispatch_rows_kernel(dest_ref, pad0_ref, src_ref, xs_ref, buf, zeros, sem, zsem, *, bm):
    i = pl.program_id(0)
    nt = pl.num_programs(0)
    tm = src_ref.shape[0]
    slot = i % 2

    def fill_copy(e):
        row0 = pl.multiple_of(jnp.maximum(pad0_ref[e], 0), bm)
        return pltpu.make_async_copy(zeros, xs_ref.at[pl.ds(row0, bm)], zsem)

    @pl.when(i == 0)
    def _():
        zeros[...] = jnp.zeros_like(zeros)

        def fill(e, carry):
            @pl.when(pad0_ref[e] >= 0)
            def _():
                fill_copy(e).start()

            return carry

        lax.fori_loop(0, pad0_ref.shape[0], fill, 0)

        def fill_wait(e, carry):
            @pl.when(pad0_ref[e] >= 0)
            def _():
                fill_copy(e).wait()

            return carry

        lax.fori_loop(0, pad0_ref.shape[0], fill_wait, 0)

    def wait_tile(s):
        def drain(r, carry):
            for _ in range(TOP_K):
                pltpu.make_async_copy(buf.at[s, 0], xs_ref.at[0], sem.at[s]).wait()
            return carry

        lax.fori_loop(0, tm, drain, 0, unroll=4)

    @pl.when(i >= 2)
    def _():
        wait_tile(slot)

    buf[slot] = src_ref[...]

    def issue(r, carry):
        for k in range(TOP_K):
            row = dest_ref[(i * tm + r) * TOP_K + k]
            pltpu.make_async_copy(buf.at[slot, r], xs_ref.at[row], sem.at[slot]).start()
        return carry

    lax.fori_loop(0, tm, issue, 0, unroll=4)

    @pl.when(i == nt - 1)
    def _():
        wait_tile(slot)

        @pl.when(nt >= 2)
        def _():
            wait_tile(1 - slot)


def _dispatch_rows(dest, pad0, src, *, p, bm, tm=256):
    t = src.shape[0]
    slab = src.shape[1:]
    blk = 5 * tm * slab[0] * slab[1] * 4 + bm * slab[0] * slab[1] * 4
    return pl.pallas_call(
        functools.partial(_dispatch_rows_kernel, bm=bm),
        grid_spec=pltpu.PrefetchScalarGridSpec(
            num_scalar_prefetch=2,
            grid=(t // tm,),
            in_specs=[pl.BlockSpec((tm,) + slab, lambda i, dest, pad0: (i, 0, 0))],
            out_specs=pl.BlockSpec(memory_space=pltpu.HBM),
            scratch_shapes=[pltpu.VMEM((2, tm) + slab, src.dtype), pltpu.VMEM((bm,) + slab, src.dtype),
                            pltpu.SemaphoreType.DMA((2,)), pltpu.SemaphoreType.DMA(())],
        ),
        out_shape=jax.ShapeDtypeStruct((p,) + slab, src.dtype),
        compiler_params=_params(("arbitrary",), _vmem_limit(blk)),
        name="dispatch_rows",
    )(dest, pad0, src)


BLOCK_COPY_PRIORITY = 0


def _start_first_blocks(cnt_ref, start_ref, e, in_copy_at):
    n = cnt_ref[e]
    b = start_ref[e]

    @pl.when(n > 0)
    def _():
        in_copy_at(b, 0).start(priority=BLOCK_COPY_PRIORITY)

    @pl.when(n > 1)
    def _():
        in_copy_at(b + 1, 1).start(priority=BLOCK_COPY_PRIORITY)


def _expert_grid_step(cnt_ref, start_ref, done_ref, in_copy_at, out_copy_at, prologue, compute):
    e = pl.program_id(1)
    ne = pl.num_programs(1)
    step = pl.program_id(0) * ne + e
    last_step = pl.num_programs(0) * ne - 1
    nblk = cnt_ref[e]
    blk0 = start_ref[e]

    @pl.when(step == 0)
    def _():
        done_ref[0] = 0
        _start_first_blocks(cnt_ref, start_ref, e, in_copy_at)

    done = done_ref[0]

    @pl.when(nblk > 0)
    def _():
        prologue()

        def body(k, carry):
            in_slot = k % 2
            out_slot = (done + k) % 2
            in_copy_at(blk0 + k, in_slot).wait()

            @pl.when(done + k >= 2)
            def _():
                out_copy_at(0, out_slot).wait()

            compute(in_slot, out_slot)
            out_copy_at(blk0 + k, out_slot).start(priority=BLOCK_COPY_PRIORITY)

            @pl.when(k + 2 < nblk)
            def _():
                in_copy_at(blk0 + k + 2, in_slot).start(priority=BLOCK_COPY_PRIORITY)

            return carry

        lax.fori_loop(0, nblk, body, 0)
        done_ref[0] = done + nblk

    @pl.when(step < last_step)
    def _():
        _start_first_blocks(cnt_ref, start_ref, (e + 1) % ne, in_copy_at)

    @pl.when(step == last_step)
    def _():
        total = done + nblk

        @pl.when(total >= 2)
        def _():
            out_copy_at(0, total % 2).wait()

        @pl.when(total >= 1)
        def _():
            out_copy_at(0, (total - 1) % 2).wait()


def _moe_up_kernel(cnt_ref, start_ref, xs_ref, w1_ref, w3_ref, h_ref, w1b_ref, w3b_ref, xbuf, obuf, xsem, osem,
                   done_ref, *, bm):
    f = pl.program_id(0)
    d, fc = w1b_ref.shape
    nslab = d // 2 // LANES
    xrows = bm * nslab

    def x_copy(blk, slot):
        row0 = pl.multiple_of(blk * xrows, xrows)
        return pltpu.make_async_copy(xs_ref.at[pl.ds(row0, xrows)], xbuf.at[slot], xsem.at[slot])

    def h_copy(blk, slot):
        row0 = pl.multiple_of(blk * bm, bm)
        col0 = pl.multiple_of(f * fc, fc)
        return pltpu.make_async_copy(obuf.at[slot], h_ref.at[pl.ds(row0, bm), pl.ds(col0, fc)], osem.at[slot])

    def cast_weights():
        w1b_ref[...] = w1_ref[0].astype(BF16)
        w3b_ref[...] = w3_ref[0].astype(BF16)

    def compute(in_slot, out_slot):
        hi, lo = _unpack_halves(_load_row_slabs(xbuf.at[in_slot], nslab, 0, nslab))
        hi = hi.astype(BF16)
        lo = lo.astype(BF16)
        d2 = d // 2
        a = (jnp.dot(hi, w1b_ref[:d2, :], preferred_element_type=F32)
             + jnp.dot(lo, w1b_ref[d2:, :], preferred_element_type=F32))
        c = (jnp.dot(hi, w3b_ref[:d2, :], preferred_element_type=F32)
             + jnp.dot(lo, w3b_ref[d2:, :], preferred_element_type=F32))
        obuf[out_slot] = (a * jax.nn.sigmoid(a) * c).astype(obuf.dtype)

    _expert_grid_step(cnt_ref, start_ref, done_ref, x_copy, h_copy, cast_weights, compute)


def _moe_down_kernel(cnt_ref, start_ref, h_ref, w2_ref, y_ref, w2b_ref, hbuf, obuf, hsem, osem, done_ref, *, bm):
    c = pl.program_id(0)
    nj = obuf.shape[2]

    def h_copy(blk, slot):
        row0 = pl.multiple_of(blk * bm, bm)
        return pltpu.make_async_copy(h_ref.at[pl.ds(row0, bm)], hbuf.at[slot], hsem.at[slot])

    def y_copy(blk, slot):
        row0 = pl.multiple_of(blk * bm, bm)
        j0 = pl.multiple_of(c * nj, nj)
        return pltpu.make_async_copy(obuf.at[slot], y_ref.at[pl.ds(row0, bm), pl.ds(j0, nj)], osem.at[slot])

    def cast_weights():
        w2b_ref[...] = w2_ref[0].astype(BF16)

    def compute(in_slot, out_slot):
        y = jnp.dot(hbuf[in_slot].astype(BF16), w2b_ref[...], preferred_element_type=F32)
        _store_row_slabs(obuf.at[out_slot], _pack_halves(y))

    _expert_grid_step(cnt_ref, start_ref, done_ref, h_copy, y_copy, cast_weights, compute)


def _moe_experts(xs, blk_cnt, blk_start, w1, w3, w2, *, bm, fc=512, nc=2048):
    p, nslab, _ = xs.shape
    d = 2 * nslab * LANES
    ne, _, de = w1.shape
    any_spec = pl.BlockSpec(memory_space=pltpu.HBM)
    blk_up = (4 * d * fc * 4 + 2 * d * fc * 2 + 2 * bm * nslab * LANES * 4 + 2 * bm * fc * 4
              + 2 * bm * d * 2 + 3 * bm * fc * 4)
    h = pl.pallas_call(
        functools.partial(_moe_up_kernel, bm=bm),
        grid_spec=pltpu.PrefetchScalarGridSpec(
            num_scalar_prefetch=2,
            grid=(de // fc, ne),
            in_specs=[any_spec,
                      pl.BlockSpec((1, d, fc), lambda f, e, cnt, start: (e, 0, f)),
                      pl.BlockSpec((1, d, fc), lambda f, e, cnt, start: (e, 0, f))],
            out_specs=any_spec,
            scratch_shapes=[pltpu.VMEM((d, fc), BF16), pltpu.VMEM((d, fc), BF16),
                            pltpu.VMEM((2, bm * nslab, LANES), U32), pltpu.VMEM((2, bm, fc), F32),
                            pltpu.SemaphoreType.DMA((2,)), pltpu.SemaphoreType.DMA((2,)),
                            pltpu.SMEM((1,), jnp.int32)],
        ),
        out_shape=jax.ShapeDtypeStruct((p, de), F32),
        compiler_params=_params(("arbitrary", "arbitrary"), _vmem_limit(blk_up)),
        name="moe_up",
    )(blk_cnt, blk_start, xs.reshape(p * nslab, LANES), w1, w3)
    nj = nc // 2 // LANES
    blk_dn = 2 * de * nc * 4 + de * nc * 2 + 3 * bm * de * 4 + 2 * bm * nj * LANES * 4 + 2 * bm * nc * 4
    return pl.pallas_call(
        functools.partial(_moe_down_kernel, bm=bm),
        grid_spec=pltpu.PrefetchScalarGridSpec(
            num_scalar_prefetch=2,
            grid=(d // nc, ne),
            in_specs=[any_spec,
                      pl.BlockSpec((1, de, nc), lambda c, e, cnt, start: (e, 0, c))],
            out_specs=any_spec,
            scratch_shapes=[pltpu.VMEM((de, nc), BF16),
                            pltpu.VMEM((2, bm, de), F32), pltpu.VMEM((2, bm, nj, LANES), U32),
                            pltpu.SemaphoreType.DMA((2,)), pltpu.SemaphoreType.DMA((2,)),
                            pltpu.SMEM((1,), jnp.int32)],
        ),
        out_shape=jax.ShapeDtypeStruct((p, nslab, LANES), U32),
        compiler_params=_params(("arbitrary", "arbitrary"), _vmem_limit(blk_dn)),
        name="moe_down",
    )(blk_cnt, blk_start, h, w2)


def _combine_kernel(back_ref, x_ref, ys_ref, ew_ref, g_ref, o_ref, ybuf, sem, *, nc):
    i = pl.program_id(0)
    nt = pl.num_programs(0)
    tm, d = x_ref.shape
    t = nt * tm
    nslab = d // 2 // LANES
    slot = i % 2

    def row_at(s, k):
        def dst_at(r):
            return ybuf.at[s, pl.ds(pl.multiple_of((k * tm + r) * nslab, nslab), nslab)]
        return dst_at

    def start_tile(tile, s):
        for k in range(TOP_K):
            _start_row_copies(back_ref, k * t + tile * tm, tm, ys_ref, row_at(s, k), sem.at[s])

    @pl.when(i == 0)
    def _():
        start_tile(0, 0)

    @pl.when(i + 1 < nt)
    def _():
        start_tile(i + 1, 1 - slot)

    for k in range(TOP_K):
        _wait_row_copies(tm, ys_ref, row_at(slot, k), sem.at[slot])

    ew = ew_ref[...]
    w0 = ew[:, 0:1]
    w1 = ew[:, 1:2]
    half = nc // 2
    y0_ref = ybuf.at[slot, pl.ds(0, tm * nslab)]
    y1_ref = ybuf.at[slot, pl.ds(tm * nslab, tm * nslab)]
    pieces = []
    ss = None
    for c in range(x_ref.shape[1] // nc):
        hi0, lo0 = _unpack_halves(_load_row_slabs(y0_ref, nslab, c * half // LANES, half // LANES))
        hi1, lo1 = _unpack_halves(_load_row_slabs(y1_ref, nslab, c * half // LANES, half // LANES))
        for j, (p0, p1) in enumerate(((hi0, hi1), (lo0, lo1))):
            cols = slice(c * nc + j * half, c * nc + (j + 1) * half)
            z = x_ref[:, cols] + w0 * p0 + w1 * p1
            pieces.append((cols, z))
            s = jnp.sum(z * z, axis=-1, keepdims=True)
            ss = s if ss is None else ss + s
    inv = lax.rsqrt(ss / x_ref.shape[1] + EPS)
    for cols, z in pieces:
        o_ref[:, cols] = z * inv * g_ref[:, cols]


def _combine(x, ys, back, e_w, gain, *, nc, tm=256):
    t, d = x.shape
    nt = t // tm
    nslab = ys.shape[1]
    blk = 2 * (2 * tm * d * 4 + tm * LANES * 4 + d * 4) + 2 * TOP_K * tm * d * 2 + 3 * tm * d * 4
    return pl.pallas_call(
        functools.partial(_combine_kernel, nc=nc),
        grid_spec=pltpu.PrefetchScalarGridSpec(
            num_scalar_prefetch=1,
            grid=(nt,),
            in_specs=[pl.BlockSpec((tm, d), lambda i, back: (i, 0)),
                      pl.BlockSpec(memory_space=pltpu.HBM),
                      pl.BlockSpec((tm, TOP_K), lambda i, back: (i, 0)),
                      pl.BlockSpec((1, d), lambda i, back: (0, 0))],
            out_specs=pl.BlockSpec((tm, d), lambda i, back: (i, 0)),
            scratch_shapes=[pltpu.VMEM((2, TOP_K * tm * nslab, LANES), ys.dtype), pltpu.SemaphoreType.DMA((2,))],
        ),
        out_shape=jax.ShapeDtypeStruct((t, d), F32),
        compiler_params=_params(("arbitrary",), _vmem_limit(blk)),
        name="moe_combine",
    )(back, x, ys, e_w, gain.reshape(1, d))


def _route(logits, *, bm):
    t = logits.shape[0]
    g_logits = logits[:, :N_GROUPS]
    e_logits = logits[:, N_GROUPS:N_GROUPS + N_EXPERTS].reshape(t, N_GROUPS, EXPERTS_PER_GROUP)
    g_idx = jnp.argmax(g_logits, axis=-1).astype(jnp.int32)
    g_w = jnp.take_along_axis(jax.nn.softmax(g_logits, axis=-1), g_idx[:, None], axis=-1)
    e_sel = jnp.take_along_axis(e_logits, g_idx[:, None, None], axis=1)[:, 0]
    top_v, top_i = lax.top_k(e_sel, TOP_K)
    e_w = jax.nn.softmax(top_v, axis=-1) * g_w
    eid = g_idx[:, None] * EXPERTS_PER_GROUP + top_i.astype(jnp.int32)

    a = t * TOP_K
    flat_e = eid.reshape(a)
    onehot = (flat_e[:, None] == jnp.arange(N_EXPERTS, dtype=jnp.int32)[None, :]).astype(jnp.int32)
    csum = jnp.cumsum(onehot, axis=0)
    rank = jnp.take_along_axis(csum, flat_e[:, None], axis=1)[:, 0] - 1
    counts = csum[-1]
    padded = ((counts + bm - 1) // bm) * bm
    pad_end = jnp.cumsum(padded)
    start_pad = pad_end - padded
    dest = (start_pad[flat_e] + rank).astype(jnp.int32)
    blk_cnt = (padded // bm).astype(jnp.int32)
    blk_start = (start_pad // bm).astype(jnp.int32)
    pad0 = jnp.where(padded > 0, pad_end - bm, -1).astype(jnp.int32)
    back = dest.reshape(t, TOP_K).T.reshape(a)
    return e_w, dest, back, pad0, blk_cnt, blk_start


MOE_BLOCK_ROWS = 256
MOE_DOWN_COLS = 4096


def kernel(x, mem, norm_mix, w_in, pool_w, pool_scale, lb_fwd, lb_bwd, hgrn_norm, w_out, norm_xattn, norm_mem,
           w_q, w_k, w_v, w_o, norm_moe, w_router_group, w_router_expert, w1, w3, w2, norm_final):
    bsz, seq, d = x.shape
    assert bsz == 1 and w_in.shape[0] == 1, "one sequence, one layer (the final norm is fused into the MoE combine)"
    l = 0
    pool_width = pool_w.shape[1] * pool_w.shape[2]
    hgrn_width = hgrn_norm.shape[1]
    xt = x.reshape(seq, d)
    mem_t = mem.reshape(mem.shape[1], d)
    h = _rmsnorm(xt, norm_mix[l], BF16)
    proj = _matmul([h], w_in[l], out_dtype=F32)
    y_pool = _pool_mixer(proj, pool_w[l], pool_scale[l])
    y_hgrn = _hgrn_mixer(proj, lb_fwd, lb_bwd, hgrn_norm[l], layer=l, width=hgrn_width, col0=pool_width)
    xt = _matmul([y_pool, y_hgrn], w_out[l], resid=xt, out_dtype=F32)
    mem_n = _rmsnorm(mem_t, norm_mem[l], BF16)
    hx = _rmsnorm(xt, norm_xattn[l], BF16)
    q = _matmul([hx], w_q[l], out_dtype=BF16)
    k = _matmul([mem_n], w_k[l], out_dtype=BF16)
    v = _matmul([mem_n], w_v[l], out_dtype=BF16)
    o = _xattn(q, k, v)
    xt = _matmul([o], w_o[l], resid=xt, out_dtype=F32)
    w_router = jnp.concatenate(
        [w_router_group[l],
         jnp.transpose(w_router_expert[l], (1, 0, 2)).reshape(d, N_EXPERTS),
         jnp.zeros((d, ROUTER_COLS - N_GROUPS - N_EXPERTS), F32)], axis=1)
    hp, logits = _router(xt, norm_moe[l], w_router)
    e_w, dest, back, pad0, blk_cnt, blk_start = _route(logits, bm=MOE_BLOCK_ROWS)
    xs = _dispatch_rows(dest, pad0, hp, p=TOP_K * seq + N_EXPERTS * MOE_BLOCK_ROWS, bm=MOE_BLOCK_ROWS)
    ys = _moe_experts(xs, blk_cnt, blk_start, w1[l], w3[l], w2[l], bm=MOE_BLOCK_ROWS, nc=MOE_DOWN_COLS)
    out = _combine(xt, ys, back, e_w, norm_final, nc=MOE_DOWN_COLS)
    return out.reshape(bsz, seq, d)
```

```python
import functools

import jax
import jax.numpy as jnp
from jax import lax
from jax.experimental import pallas as pl
from jax.experimental.pallas import tpu as pltpu

F32 = jnp.float32
BF16 = jnp.bfloat16
U32 = jnp.uint32

EPS = 1e-6
LANES = 128
SUBLANES = 8
VMEM_BYTES_V7X = 64 * 1024 * 1024

POOL_WINDOWS = (2, 4, 8, 16)
HEAD = 128
CHUNK = 64
XATTN_HEADS = 4
N_GROUPS = 4
EXPERTS_PER_GROUP = 8
N_EXPERTS = N_GROUPS * EXPERTS_PER_GROUP
TOP_K = 2
ROUTER_COLS = LANES


def _params(sem, vmem_bytes):
    return pltpu.CompilerParams(dimension_semantics=sem, vmem_limit_bytes=int(vmem_bytes))


def _vmem_limit(block_bytes):
    return min(int(block_bytes * 1.25) + (6 << 20), VMEM_BYTES_V7X - (4 << 20))


def _rmsnorm_kernel(x_ref, g_ref, o_ref):
    x = x_ref[...]
    ms = jnp.mean(x * x, axis=-1, keepdims=True)
    o_ref[...] = (x * lax.rsqrt(ms + EPS) * g_ref[...]).astype(o_ref.dtype)


def _rmsnorm(x, gain, out_dtype, tm=512):
    t, d = x.shape
    tm = min(tm, t)
    blk = tm * d * (4 + jnp.dtype(out_dtype).itemsize) * 2
    return pl.pallas_call(
        _rmsnorm_kernel,
        grid=(t // tm,),
        in_specs=[pl.BlockSpec((tm, d), lambda i: (i, 0)),
                  pl.BlockSpec((1, d), lambda i: (0, 0))],
        out_specs=pl.BlockSpec((tm, d), lambda i: (i, 0)),
        out_shape=jax.ShapeDtypeStruct((t, d), out_dtype),
        compiler_params=_params(("arbitrary",), _vmem_limit(blk)),
        name="rmsnorm",
    )(x, gain.reshape(1, d))


def _mm_kernel(*refs, n_a, has_resid):
    a_refs = refs[:n_a]
    w_ref = refs[n_a]
    r_ref = refs[n_a + 1] if has_resid else None
    o_ref, wb_ref = refs[-2], refs[-1]

    @pl.when(pl.program_id(1) == 0)
    def _():
        wb_ref[...] = w_ref[...].astype(BF16)

    acc = None
    k0 = 0
    for a_ref in a_refs:
        kk = a_ref.shape[1]
        part = jnp.dot(a_ref[...], wb_ref[k0:k0 + kk, :], preferred_element_type=F32)
        acc = part if acc is None else acc + part
        k0 += kk
    if has_resid:
        acc = acc + r_ref[...]
    o_ref[...] = acc.astype(o_ref.dtype)


def _matmul(a_parts, w, *, resid=None, out_dtype=F32, tm=1024, tn=512):
    m = a_parts[0].shape[0]
    k, n = w.shape
    assert sum(a.shape[1] for a in a_parts) == k
    tm = min(tm, m)
    in_specs = [pl.BlockSpec((tm, a.shape[1]), lambda j, i: (i, 0)) for a in a_parts]
    in_specs.append(pl.BlockSpec((k, tn), lambda j, i: (0, j)))
    args = list(a_parts) + [w]
    if resid is not None:
        in_specs.append(pl.BlockSpec((tm, tn), lambda j, i: (i, j)))
        args.append(resid)
    osz = jnp.dtype(out_dtype).itemsize
    blk = (2 * tm * k * 2 + 2 * k * tn * 4 + k * tn * 2 + 2 * tm * tn * osz
           + (2 * tm * tn * 4 if resid is not None else 0) + tm * tn * 4)
    return pl.pallas_call(
        functools.partial(_mm_kernel, n_a=len(a_parts), has_resid=resid is not None),
        grid=(n // tn, m // tm),
        in_specs=in_specs,
        out_specs=pl.BlockSpec((tm, tn), lambda j, i: (i, j)),
        out_shape=jax.ShapeDtypeStruct((m, n), out_dtype),
        scratch_shapes=[pltpu.VMEM((k, tn), BF16)],
        compiler_params=_params(("arbitrary", "arbitrary"), _vmem_limit(blk)),
        name="matmul",
    )(*args)


def _pool_kernel(u_ref, up_ref, un_ref, w_ref, s_ref, o_ref, *, tm, seq, group):
    i = pl.program_id(0)
    last = pl.num_programs(0) - 1
    n = tm + 2 * SUBLANES
    row = lax.broadcasted_iota(jnp.int32, (tm, 1), 0) + i * tm
    for gi, w in enumerate(POOL_WINDOWS):
        cs = slice(gi * group, (gi + 1) * group)
        u = u_ref[:, cs]
        prev = jnp.where(i > 0, up_ref[:, cs], 0.0)
        nxt = jnp.where(i < last, un_ref[:, cs], 0.0)
        f = jnp.concatenate([prev, u, nxt], axis=0)
        step = 1
        while step < w:
            f = f + pltpu.roll(f, n - step, 0)
            step *= 2
        half = w // 2
        first = SUBLANES - half
        if first:
            f = pltpu.roll(f, n - first, 0)
        win = f[:tm]
        lo = jnp.maximum(row - half, 0)
        hi = jnp.minimum(row + half - 1, seq - 1)
        cnt = (hi - lo + 1).astype(F32)
        mixed = win / cnt - u
        y = jnp.dot(mixed.astype(BF16), w_ref[gi].astype(BF16), preferred_element_type=F32)
        o_ref[:, cs] = (y * s_ref[:, cs]).astype(o_ref.dtype)


def _pool_mixer(proj, pool_w, pool_scale, *, tm=512):
    t = proj.shape[0]
    ng, group, _ = pool_w.shape
    width = ng * group
    rpb = tm // SUBLANES
    nhalo = t // SUBLANES
    blk = 2 * (tm * width * 4 + 2 * SUBLANES * width * 4 + ng * group * group * 4 + tm * width * 2) + 6 * tm * group * 4
    return pl.pallas_call(
        functools.partial(_pool_kernel, tm=tm, seq=t, group=group),
        grid=(t // tm,),
        in_specs=[pl.BlockSpec((tm, width), lambda i: (i, 0)),
                  pl.BlockSpec((SUBLANES, width), lambda i: (jnp.maximum(i * rpb - 1, 0), 0)),
                  pl.BlockSpec((SUBLANES, width), lambda i: (jnp.minimum((i + 1) * rpb, nhalo - 1), 0)),
                  pl.BlockSpec((ng, group, group), lambda i: (0, 0, 0)),
                  pl.BlockSpec((1, width), lambda i: (0, 0))],
        out_specs=pl.BlockSpec((tm, width), lambda i: (i, 0)),
        out_shape=jax.ShapeDtypeStruct((t, width), BF16),
        compiler_params=_params(("arbitrary",), _vmem_limit(blk)),
        name="pool_mixer",
    )(proj, proj, proj, pool_w, pool_scale.reshape(1, width))


def _chunk_cumsum(x, reverse):
    n = x.shape[0]
    pos = lax.broadcasted_iota(jnp.int32, x.shape, 0) % CHUNK
    s = 1
    while s < CHUNK:
        if reverse:
            x = x + jnp.where(pos < CHUNK - s, pltpu.roll(x, n - s, 0), 0.0)
        else:
            x = x + jnp.where(pos >= s, pltpu.roll(x, s, 0), 0.0)
        s *= 2
    return x


def _lower_bound(lb_raw, layer):
    e = jnp.exp(lb_raw - jnp.max(lb_raw, axis=0, keepdims=True))
    return jnp.sum(e[:layer + 1], axis=0, keepdims=True) / jnp.sum(e, axis=0, keepdims=True)


def _hgrn_direction(q_raw, f_raw, v, lb, st_ref, h, mask, *, reverse):
    tb = q_raw.shape[0]
    nc = tb // CHUNK
    q = q_raw * jax.nn.sigmoid(q_raw)
    f = lb + (1.0 - lb) * jax.nn.sigmoid(f_raw)
    logf = jnp.log(f)
    k = 1.0 - f
    b = _chunk_cumsum(logf, reverse)
    b3 = b.reshape(nc, CHUNK, HEAD)
    edge = 0 if reverse else CHUNK - 1
    b_last = b3[:, edge:edge + 1, :]
    qd = (q * jnp.exp(b)).astype(BF16)
    kd = (k * jnp.exp(-b)).astype(BF16)
    k_end = (k.reshape(nc, CHUNK, HEAD) * jnp.exp(b_last - b3)).astype(BF16)
    decay = jnp.exp(b_last)
    vb = v.astype(BF16)

    a = lax.dot_general(qd, kd, (((1,), (1,)), ((), ())), preferred_element_type=F32)
    a = jnp.where(mask, a, 0.0)
    o_intra = jnp.dot(a.astype(BF16), vb, preferred_element_type=F32)

    st = st_ref[h]
    o_inter = [None] * nc
    order = range(nc - 1, -1, -1) if reverse else range(nc)
    for c in order:
        rows = slice(c * CHUNK, (c + 1) * CHUNK)
        o_inter[c] = lax.dot_general(qd[rows], st.astype(BF16), (((1,), (1,)), ((), ())),
                                     preferred_element_type=F32)
        d_st = lax.dot_general(vb[rows], k_end[c], (((0,), (0,)), ((), ())), preferred_element_type=F32)
        st = st * decay[c] + d_st
    st_ref[h] = st
    return o_intra + jnp.concatenate(o_inter, axis=0)


def _intra_mask(tb, reverse):
    r = lax.broadcasted_iota(jnp.int32, (tb, tb), 0)
    c = lax.broadcasted_iota(jnp.int32, (tb, tb), 1)
    same = (r // CHUNK) == (c // CHUNK)
    return same & ((c >= r) if reverse else (c <= r))


def _hgrn_fwd_kernel(q_ref, f_ref, i_ref, lb_ref, o_ref, st_ref, *, layer, heads):
    @pl.when(pl.program_id(0) == 0)
    def _():
        st_ref[...] = jnp.zeros_like(st_ref)

    mask = _intra_mask(q_ref.shape[0], False)

    def head(h, carry):
        hs = pl.ds(pl.multiple_of(h * HEAD, HEAD), HEAD)
        lb = _lower_bound(lb_ref[:, hs], layer)
        o_ref[:, hs] = _hgrn_direction(q_ref[:, hs], f_ref[:, hs], i_ref[:, hs], lb, st_ref, h, mask,
                                       reverse=False)
        return carry

    lax.fori_loop(0, heads, head, 0, unroll=4)


def _hgrn_bwd_kernel(q_ref, f_ref, i_ref, g_ref, of_ref, lb_ref, gain_ref, y_ref, st_ref, *, layer, heads):
    @pl.when(pl.program_id(0) == 0)
    def _():
        st_ref[...] = jnp.zeros_like(st_ref)

    mask = _intra_mask(q_ref.shape[0], True)

    def head(h, carry):
        hs = pl.ds(pl.multiple_of(h * HEAD, HEAD), HEAD)
        lb = _lower_bound(lb_ref[:, hs], layer)
        o = of_ref[:, hs] + _hgrn_direction(q_ref[:, hs], f_ref[:, hs], i_ref[:, hs], lb, st_ref, h, mask,
                                            reverse=True)
        o = o * lax.rsqrt(jnp.mean(o * o, axis=-1, keepdims=True) + EPS) * gain_ref[:, hs]
        g = g_ref[:, hs]
        y_ref[:, hs] = (o * (g * jax.nn.sigmoid(g))).astype(y_ref.dtype)
        return carry

    lax.fori_loop(0, heads, head, 0, unroll=4)


def _hgrn_mixer(proj, lb_fwd, lb_bwd, hgrn_norm, *, layer, width, col0, tb=256):
    t = proj.shape[0]
    heads = width // HEAD
    nb = t // tb
    c = col0 // width
    layers = lb_fwd.shape[0]
    blk_f = 2 * (3 * tb * width * 4 + layers * width * 4 + tb * width * 4) + heads * HEAD * HEAD * 4
    o_f = pl.pallas_call(
        functools.partial(_hgrn_fwd_kernel, layer=layer, heads=heads),
        grid=(nb,),
        in_specs=[pl.BlockSpec((tb, width), lambda b: (b, c)),
                  pl.BlockSpec((tb, width), lambda b: (b, c + 1)),
                  pl.BlockSpec((tb, width), lambda b: (b, c + 3)),
                  pl.BlockSpec((layers, width), lambda b: (0, 0))],
        out_specs=pl.BlockSpec((tb, width), lambda b: (b, 0)),
        out_shape=jax.ShapeDtypeStruct((t, width), F32),
        scratch_shapes=[pltpu.VMEM((heads, HEAD, HEAD), F32)],
        compiler_params=_params(("arbitrary",), _vmem_limit(blk_f)),
        name="hgrn_fwd",
    )(proj, proj, proj, lb_fwd)
    blk_b = 2 * (5 * tb * width * 4 + layers * width * 4 + width * 4 + tb * width * 2) + heads * HEAD * HEAD * 4
    return pl.pallas_call(
        functools.partial(_hgrn_bwd_kernel, layer=layer, heads=heads),
        grid=(nb,),
        in_specs=[pl.BlockSpec((tb, width), lambda b: (nb - 1 - b, c)),
                  pl.BlockSpec((tb, width), lambda b: (nb - 1 - b, c + 2)),
                  pl.BlockSpec((tb, width), lambda b: (nb - 1 - b, c + 3)),
                  pl.BlockSpec((tb, width), lambda b: (nb - 1 - b, c + 4)),
                  pl.BlockSpec((tb, width), lambda b: (nb - 1 - b, 0)),
                  pl.BlockSpec((layers, width), lambda b: (0, 0)),
                  pl.BlockSpec((1, width), lambda b: (0, 0))],
        out_specs=pl.BlockSpec((tb, width), lambda b: (nb - 1 - b, 0)),
        out_shape=jax.ShapeDtypeStruct((t, width), BF16),
        scratch_shapes=[pltpu.VMEM((heads, HEAD, HEAD), F32)],
        compiler_params=_params(("arbitrary",), _vmem_limit(blk_b)),
        name="hgrn_bwd",
    )(proj, proj, proj, proj, o_f, lb_bwd, hgrn_norm.reshape(1, width))


def _xattn_kernel(q_ref, k_ref, v_ref, o_ref, *, scale):
    s = lax.dot_general(q_ref[...], k_ref[...], (((1,), (1,)), ((), ())), preferred_element_type=F32) * scale
    s = s - jnp.max(s, axis=-1, keepdims=True)
    p = jnp.exp(s)
    p = p / jnp.sum(p, axis=-1, keepdims=True)
    o_ref[...] = jnp.dot(p.astype(BF16), v_ref[...], preferred_element_type=F32).astype(o_ref.dtype)


def _xattn(q, k, v, *, tm=1024):
    t, d = q.shape
    m = k.shape[0]
    dh = d // XATTN_HEADS
    blk = 2 * (2 * tm * dh * 2 + 2 * m * dh * 2) + 4 * tm * m * 4 + tm * dh * 4
    return pl.pallas_call(
        functools.partial(_xattn_kernel, scale=dh ** -0.5),
        grid=(XATTN_HEADS, t // tm),
        in_specs=[pl.BlockSpec((tm, dh), lambda h, i: (i, h)),
                  pl.BlockSpec((m, dh), lambda h, i: (0, h)),
                  pl.BlockSpec((m, dh), lambda h, i: (0, h))],
        out_specs=pl.BlockSpec((tm, dh), lambda h, i: (i, h)),
        out_shape=jax.ShapeDtypeStruct((t, d), BF16),
        compiler_params=_params(("arbitrary", "arbitrary"), _vmem_limit(blk)),
        name="xattn",
    )(q, k, v)


def _pack_halves(x):
    w = x.shape[1] // 2
    hi = lax.bitcast_convert_type(x[:, :w].astype(BF16).astype(F32), U32)
    lo = lax.bitcast_convert_type(x[:, w:].astype(BF16).astype(F32), U32)
    return hi | (lo >> 16)


def _unpack_halves(p):
    hi = lax.bitcast_convert_type(p & jnp.uint32(0xFFFF0000), F32)
    lo = lax.bitcast_convert_type(p << 16, F32)
    return hi, lo


def _load_row_slabs(flat_ref, s, j0, nj):
    rows = flat_ref.shape[0] // s
    return jnp.concatenate([flat_ref[pl.ds(j0 + j, rows, stride=s), :] for j in range(nj)], axis=1)


def _store_row_slabs(ref, val):
    for j in range(ref.shape[1]):
        ref[:, j, :] = val[:, j * LANES:(j + 1) * LANES]


def _router_kernel(x_ref, g_ref, wr_ref, hp_ref, lg_ref, whi_ref, wlo_ref):
    @pl.when(pl.program_id(0) == 0)
    def _():
        w = wr_ref[...]
        w_hi = w.astype(BF16)
        whi_ref[...] = w_hi
        wlo_ref[...] = (w - w_hi.astype(F32)).astype(BF16)

    x = x_ref[...]
    ms = jnp.mean(x * x, axis=-1, keepdims=True)
    h = x * lax.rsqrt(ms + EPS) * g_ref[...]
    _store_row_slabs(hp_ref, _pack_halves(h))
    h_hi = h.astype(BF16)
    h_lo = (h - h_hi.astype(F32)).astype(BF16)
    lg_ref[...] = (jnp.dot(h_hi, whi_ref[...], preferred_element_type=F32)
                   + (jnp.dot(h_hi, wlo_ref[...], preferred_element_type=F32)
                      + jnp.dot(h_lo, whi_ref[...], preferred_element_type=F32)))


def _router(x, gain, w_router, *, tm=256):
    t, d = x.shape
    blk = 2 * (tm * d * 4 + d * 4 + d * ROUTER_COLS * 4 + tm * d * 2 + tm * ROUTER_COLS * 4) + 3 * tm * d * 4
    return pl.pallas_call(
        _router_kernel,
        grid=(t // tm,),
        in_specs=[pl.BlockSpec((tm, d), lambda i: (i, 0)),
                  pl.BlockSpec((1, d), lambda i: (0, 0)),
                  pl.BlockSpec((d, ROUTER_COLS), lambda i: (0, 0))],
        out_specs=[pl.BlockSpec((tm, d // 2 // LANES, LANES), lambda i: (i, 0, 0)),
                   pl.BlockSpec((tm, ROUTER_COLS), lambda i: (i, 0))],
        out_shape=[jax.ShapeDtypeStruct((t, d // 2 // LANES, LANES), U32),
                   jax.ShapeDtypeStruct((t, ROUTER_COLS), F32)],
        scratch_shapes=[pltpu.VMEM((d, ROUTER_COLS), BF16), pltpu.VMEM((d, ROUTER_COLS), BF16)],
        compiler_params=_params(("arbitrary",), _vmem_limit(blk)),
        name="moe_router",
    )(x, gain.reshape(1, d), w_router)


def _start_row_copies(idx_ref, base, n, src_ref, dst_at, sem):
    def issue(r, carry):
        pltpu.make_async_copy(src_ref.at[idx_ref[base + r]], dst_at(r), sem).start()
        return carry

    lax.fori_loop(0, n, issue, 0, unroll=8)


def _wait_row_copies(n, src_ref, dst_at, sem):
    def drain(r, carry):
        pltpu.make_async_copy(src_ref.at[0], dst_at(r), sem).wait()
        return carry

    lax.fori_loop(0, n, drain, 0, unroll=8)


def _dispatch_rows_kernel(dest_ref, pad0_ref, src_ref, xs_ref, buf, zeros, sem, zsem, *, bm):
    i = pl.program_id(0)
    nt = pl.num_programs(0)
    tm = src_ref.shape[0]
    slot = i % 2

    def fill_copy(e):
        row0 = pl.multiple_of(jnp.maximum(pad0_ref[e], 0), bm)
        return pltpu.make_async_copy(zeros, xs_ref.at[pl.ds(row0, bm)], zsem)

    @pl.when(i == 0)
    def _():
        zeros[...] = jnp.zeros_like(zeros)

        def fill(e, carry):
            @pl.when(pad0_ref[e] >= 0)
            def _():
                fill_copy(e).start()

            return carry

        lax.fori_loop(0, pad0_ref.shape[0], fill, 0)

        def fill_wait(e, carry):
            @pl.when(pad0_ref[e] >= 0)
            def _():
                fill_copy(e).wait()

            return carry

        lax.fori_loop(0, pad0_ref.shape[0], fill_wait, 0)

    def wait_tile(s):
        def drain(r, carry):
            for _ in range(TOP_K):
                pltpu.make_async_copy(buf.at[s, 0], xs_ref.at[0], sem.at[s]).wait()
            return carry

        lax.fori_loop(0, tm, drain, 0, unroll=4)

    @pl.when(i >= 2)
    def _():
        wait_tile(slot)

    buf[slot] = src_ref[...]

    def issue(r, carry):
        for k in range(TOP_K):
            row = dest_ref[(i * tm + r) * TOP_K + k]
            pltpu.make_async_copy(buf.at[slot, r], xs_ref.at[row], sem.at[slot]).start()
        return carry

    lax.fori_loop(0, tm, issue, 0, unroll=4)

    @pl.when(i == nt - 1)
    def _():
        wait_tile(slot)

        @pl.when(nt >= 2)
        def _():
            wait_tile(1 - slot)


def _dispatch_rows(dest, pad0, src, *, p, bm, tm=256):
    t = src.shape[0]
    slab = src.shape[1:]
    blk = 5 * tm * slab[0] * slab[1] * 4 + bm * slab[0] * slab[1] * 4
    return pl.pallas_call(
        functools.partial(_dispatch_rows_kernel, bm=bm),
        grid_spec=pltpu.PrefetchScalarGridSpec(
            num_scalar_prefetch=2,
            grid=(t // tm,),
            in_specs=[pl.BlockSpec((tm,) + slab, lambda i, dest, pad0: (i, 0, 0))],
            out_specs=pl.BlockSpec(memory_space=pltpu.HBM),
            scratch_shapes=[pltpu.VMEM((2, tm) + slab, src.dtype), pltpu.VMEM((bm,) + slab, src.dtype),
                            pltpu.SemaphoreType.DMA((2,)), pltpu.SemaphoreType.DMA(())],
        ),
        out_shape=jax.ShapeDtypeStruct((p,) + slab, src.dtype),
        compiler_params=_params(("arbitrary",), _vmem_limit(blk)),
        name="dispatch_rows",
    )(dest, pad0, src)


BLOCK_COPY_PRIORITY = 0


def _start_first_blocks(cnt_ref, start_ref, e, in_copy_at):
    n = cnt_ref[e] >> 1
    b = start_ref[e]

    @pl.when(n > 0)
    def _():
        in_copy_at(b, 0).start(priority=BLOCK_COPY_PRIORITY)

    @pl.when(n > 1)
    def _():
        in_copy_at(b + 1, 1).start(priority=BLOCK_COPY_PRIORITY)


def _expert_grid_step(cnt_ref, start_ref, done_ref, in_copy_at, out_copy_at, prologue, compute):
    e = pl.program_id(1)
    ne = pl.num_programs(1)
    step = pl.program_id(0) * ne + e
    last_step = pl.num_programs(0) * ne - 1
    nblk = cnt_ref[e] >> 1
    tail_half = (cnt_ref[e] & 1) == 1
    blk0 = start_ref[e]

    @pl.when(step == 0)
    def _():
        done_ref[0] = 0
        _start_first_blocks(cnt_ref, start_ref, e, in_copy_at)

    done = done_ref[0]

    @pl.when(nblk > 0)
    def _():
        prologue()

        def body(k, carry):
            in_slot = k % 2
            out_slot = (done + k) % 2
            in_copy_at(blk0 + k, in_slot).wait()

            @pl.when(done + k >= 2)
            def _():
                out_copy_at(0, out_slot).wait()

            half = tail_half & (k == nblk - 1)

            @pl.when(jnp.logical_not(half))
            def _():
                compute(in_slot, out_slot, False)

            @pl.when(half)
            def _():
                compute(in_slot, out_slot, True)

            out_copy_at(blk0 + k, out_slot).start(priority=BLOCK_COPY_PRIORITY)

            @pl.when(k + 2 < nblk)
            def _():
                in_copy_at(blk0 + k + 2, in_slot).start(priority=BLOCK_COPY_PRIORITY)

            return carry

        lax.fori_loop(0, nblk, body, 0)
        done_ref[0] = done + nblk

    @pl.when(step < last_step)
    def _():
        _start_first_blocks(cnt_ref, start_ref, (e + 1) % ne, in_copy_at)

    @pl.when(step == last_step)
    def _():
        total = done + nblk

        @pl.when(total >= 2)
        def _():
            out_copy_at(0, total % 2).wait()

        @pl.when(total >= 1)
        def _():
            out_copy_at(0, (total - 1) % 2).wait()


def _moe_up_kernel(cnt_ref, start_ref, xs_ref, w1_ref, w3_ref, h_ref, w1b_ref, w3b_ref, xbuf, obuf, xsem, osem,
                   done_ref, *, bm):
    f = pl.program_id(0)
    d, fc = w1b_ref.shape
    nslab = d // 2 // LANES
    xrows = bm * nslab

    def x_copy(blk, slot):
        row0 = pl.multiple_of(blk * xrows, xrows)
        return pltpu.make_async_copy(xs_ref.at[pl.ds(row0, xrows)], xbuf.at[slot], xsem.at[slot])

    def h_copy(blk, slot):
        row0 = pl.multiple_of(blk * bm, bm)
        col0 = pl.multiple_of(f * fc, fc)
        return pltpu.make_async_copy(obuf.at[slot], h_ref.at[pl.ds(row0, bm), pl.ds(col0, fc)], osem.at[slot])

    def cast_weights():
        w1b_ref[...] = w1_ref[0].astype(BF16)
        w3b_ref[...] = w3_ref[0].astype(BF16)

    def compute(in_slot, out_slot, half):
        rows = bm // 2 if half else bm
        xin = xbuf.at[in_slot, pl.ds(0, rows * nslab)] if half else xbuf.at[in_slot]
        hi, lo = _unpack_halves(_load_row_slabs(xin, nslab, 0, nslab))
        hi = hi.astype(BF16)
        lo = lo.astype(BF16)
        d2 = d // 2
        a = (jnp.dot(hi, w1b_ref[:d2, :], preferred_element_type=F32)
             + jnp.dot(lo, w1b_ref[d2:, :], preferred_element_type=F32))
        c = (jnp.dot(hi, w3b_ref[:d2, :], preferred_element_type=F32)
             + jnp.dot(lo, w3b_ref[d2:, :], preferred_element_type=F32))
        res = (a * jax.nn.sigmoid(a) * c).astype(obuf.dtype)
        if half:
            obuf[out_slot, :rows] = res
            obuf[out_slot, rows:] = jnp.zeros((bm - rows, fc), obuf.dtype)
        else:
            obuf[out_slot] = res

    _expert_grid_step(cnt_ref, start_ref, done_ref, x_copy, h_copy, cast_weights, compute)


def _moe_down_kernel(cnt_ref, start_ref, h_ref, w2_ref, y_ref, w2b_ref, hbuf, obuf, hsem, osem, done_ref, *, bm):
    c = pl.program_id(0)
    nj = obuf.shape[2]

    def h_copy(blk, slot):
        row0 = pl.multiple_of(blk * bm, bm)
        return pltpu.make_async_copy(h_ref.at[pl.ds(row0, bm)], hbuf.at[slot], hsem.at[slot])

    def y_copy(blk, slot):
        row0 = pl.multiple_of(blk * bm, bm)
        j0 = pl.multiple_of(c * nj, nj)
        return pltpu.make_async_copy(obuf.at[slot], y_ref.at[pl.ds(row0, bm), pl.ds(j0, nj)], osem.at[slot])

    def cast_weights():
        w2b_ref[...] = w2_ref[0].astype(BF16)

    def compute(in_slot, out_slot, half):
        rows = bm // 2 if half else bm
        hin = hbuf[in_slot, :rows] if half else hbuf[in_slot]
        packed = _pack_halves(jnp.dot(hin.astype(BF16), w2b_ref[...], preferred_element_type=F32))
        if half:
            _store_row_slabs(obuf.at[out_slot, pl.ds(0, rows)], packed)
            obuf[out_slot, rows:] = jnp.zeros((bm - rows, nj, LANES), obuf.dtype)
        else:
            _store_row_slabs(obuf.at[out_slot], packed)

    _expert_grid_step(cnt_ref, start_ref, done_ref, h_copy, y_copy, cast_weights, compute)


def _moe_experts(xs, blk_cnt, blk_start, w1, w3, w2, *, bm, fc=512, nc=2048):
    p, nslab, _ = xs.shape
    d = 2 * nslab * LANES
    ne, _, de = w1.shape
    any_spec = pl.BlockSpec(memory_space=pltpu.HBM)
    blk_up = (4 * d * fc * 4 + 2 * d * fc * 2 + 2 * bm * nslab * LANES * 4 + 2 * bm * fc * 4
              + 2 * bm * d * 2 + 3 * bm * fc * 4)
    h = pl.pallas_call(
        functools.partial(_moe_up_kernel, bm=bm),
        grid_spec=pltpu.PrefetchScalarGridSpec(
            num_scalar_prefetch=2,
            grid=(de // fc, ne),
            in_specs=[any_spec,
                      pl.BlockSpec((1, d, fc), lambda f, e, cnt, start: (e, 0, f)),
                      pl.BlockSpec((1, d, fc), lambda f, e, cnt, start: (e, 0, f))],
            out_specs=any_spec,
            scratch_shapes=[pltpu.VMEM((d, fc), BF16), pltpu.VMEM((d, fc), BF16),
                            pltpu.VMEM((2, bm * nslab, LANES), U32), pltpu.VMEM((2, bm, fc), F32),
                            pltpu.SemaphoreType.DMA((2,)), pltpu.SemaphoreType.DMA((2,)),
                            pltpu.SMEM((1,), jnp.int32)],
        ),
        out_shape=jax.ShapeDtypeStruct((p, de), F32),
        compiler_params=_params(("arbitrary", "arbitrary"), _vmem_limit(blk_up)),
        name="moe_up",
    )(blk_cnt, blk_start, xs.reshape(p * nslab, LANES), w1, w3)
    nj = nc // 2 // LANES
    blk_dn = 2 * de * nc * 4 + de * nc * 2 + 3 * bm * de * 4 + 2 * bm * nj * LANES * 4 + 2 * bm * nc * 4
    return pl.pallas_call(
        functools.partial(_moe_down_kernel, bm=bm),
        grid_spec=pltpu.PrefetchScalarGridSpec(
            num_scalar_prefetch=2,
            grid=(d // nc, ne),
            in_specs=[any_spec,
                      pl.BlockSpec((1, de, nc), lambda c, e, cnt, start: (e, 0, c))],
            out_specs=any_spec,
            scratch_shapes=[pltpu.VMEM((de, nc), BF16),
                            pltpu.VMEM((2, bm, de), F32), pltpu.VMEM((2, bm, nj, LANES), U32),
                            pltpu.SemaphoreType.DMA((2,)), pltpu.SemaphoreType.DMA((2,)),
                            pltpu.SMEM((1,), jnp.int32)],
        ),
        out_shape=jax.ShapeDtypeStruct((p, nslab, LANES), U32),
        compiler_params=_params(("arbitrary", "arbitrary"), _vmem_limit(blk_dn)),
        name="moe_down",
    )(blk_cnt, blk_start, h, w2)


def _combine_kernel(back_ref, x_ref, ys_ref, ew_ref, g_ref, o_ref, ybuf, sem, *, nc):
    i = pl.program_id(0)
    nt = pl.num_programs(0)
    tm, d = x_ref.shape
    t = nt * tm
    nslab = d // 2 // LANES
    slot = i % 2

    def row_at(s, k):
        def dst_at(r):
            return ybuf.at[s, pl.ds(pl.multiple_of((k * tm + r) * nslab, nslab), nslab)]
        return dst_at

    def start_tile(tile, s):
        for k in range(TOP_K):
            _start_row_copies(back_ref, k * t + tile * tm, tm, ys_ref, row_at(s, k), sem.at[s])

    @pl.when(i == 0)
    def _():
        start_tile(0, 0)

    @pl.when(i + 1 < nt)
    def _():
        start_tile(i + 1, 1 - slot)

    for k in range(TOP_K):
        _wait_row_copies(tm, ys_ref, row_at(slot, k), sem.at[slot])

    ew = ew_ref[...]
    w0 = ew[:, 0:1]
    w1 = ew[:, 1:2]
    half = nc // 2
    y0_ref = ybuf.at[slot, pl.ds(0, tm * nslab)]
    y1_ref = ybuf.at[slot, pl.ds(tm * nslab, tm * nslab)]
    pieces = []
    ss = None
    for c in range(x_ref.shape[1] // nc):
        hi0, lo0 = _unpack_halves(_load_row_slabs(y0_ref, nslab, c * half // LANES, half // LANES))
        hi1, lo1 = _unpack_halves(_load_row_slabs(y1_ref, nslab, c * half // LANES, half // LANES))
        for j, (p0, p1) in enumerate(((hi0, hi1), (lo0, lo1))):
            cols = slice(c * nc + j * half, c * nc + (j + 1) * half)
            z = x_ref[:, cols] + w0 * p0 + w1 * p1
            pieces.append((cols, z))
            s = jnp.sum(z * z, axis=-1, keepdims=True)
            ss = s if ss is None else ss + s
    inv = lax.rsqrt(ss / x_ref.shape[1] + EPS)
    for cols, z in pieces:
        o_ref[:, cols] = z * inv * g_ref[:, cols]


def _combine(x, ys, back, e_w, gain, *, nc, tm=256):
    t, d = x.shape
    nt = t // tm
    nslab = ys.shape[1]
    blk = 2 * (2 * tm * d * 4 + tm * LANES * 4 + d * 4) + 2 * TOP_K * tm * d * 2 + 3 * tm * d * 4
    return pl.pallas_call(
        functools.partial(_combine_kernel, nc=nc),
        grid_spec=pltpu.PrefetchScalarGridSpec(
            num_scalar_prefetch=1,
            grid=(nt,),
            in_specs=[pl.BlockSpec((tm, d), lambda i, back: (i, 0)),
                      pl.BlockSpec(memory_space=pltpu.HBM),
                      pl.BlockSpec((tm, TOP_K), lambda i, back: (i, 0)),
                      pl.BlockSpec((1, d), lambda i, back: (0, 0))],
            out_specs=pl.BlockSpec((tm, d), lambda i, back: (i, 0)),
            scratch_shapes=[pltpu.VMEM((2, TOP_K * tm * nslab, LANES), ys.dtype), pltpu.SemaphoreType.DMA((2,))],
        ),
        out_shape=jax.ShapeDtypeStruct((t, d), F32),
        compiler_params=_params(("arbitrary",), _vmem_limit(blk)),
        name="moe_combine",
    )(back, x, ys, e_w, gain.reshape(1, d))


def _route(logits, *, bm):
    t = logits.shape[0]
    g_logits = logits[:, :N_GROUPS]
    e_logits = logits[:, N_GROUPS:N_GROUPS + N_EXPERTS].reshape(t, N_GROUPS, EXPERTS_PER_GROUP)
    g_idx = jnp.argmax(g_logits, axis=-1).astype(jnp.int32)
    g_w = jnp.take_along_axis(jax.nn.softmax(g_logits, axis=-1), g_idx[:, None], axis=-1)
    e_sel = jnp.take_along_axis(e_logits, g_idx[:, None, None], axis=1)[:, 0]
    top_v, top_i = lax.top_k(e_sel, TOP_K)
    e_w = jax.nn.softmax(top_v, axis=-1) * g_w
    eid = g_idx[:, None] * EXPERTS_PER_GROUP + top_i.astype(jnp.int32)

    a = t * TOP_K
    flat_e = eid.reshape(a)
    onehot = (flat_e[:, None] == jnp.arange(N_EXPERTS, dtype=jnp.int32)[None, :]).astype(jnp.int32)
    csum = jnp.cumsum(onehot, axis=0)
    rank = jnp.take_along_axis(csum, flat_e[:, None], axis=1)[:, 0] - 1
    counts = csum[-1]
    padded = ((counts + bm - 1) // bm) * bm
    pad_end = jnp.cumsum(padded)
    start_pad = pad_end - padded
    dest = (start_pad[flat_e] + rank).astype(jnp.int32)
    tail_half = (counts > 0) & ((counts - 1) % bm < bm // 2)
    blk_cnt = (2 * (padded // bm) + tail_half).astype(jnp.int32)
    blk_start = (start_pad // bm).astype(jnp.int32)
    pad0 = jnp.where(padded > 0, pad_end - bm, -1).astype(jnp.int32)
    back = dest.reshape(t, TOP_K).T.reshape(a)
    return e_w, dest, back, pad0, blk_cnt, blk_start


MOE_BLOCK_ROWS = 256
MOE_DOWN_COLS = 4096


def kernel(x, mem, norm_mix, w_in, pool_w, pool_scale, lb_fwd, lb_bwd, hgrn_norm, w_out, norm_xattn, norm_mem,
           w_q, w_k, w_v, w_o, norm_moe, w_router_group, w_router_expert, w1, w3, w2, norm_final):
    bsz, seq, d = x.shape
    assert bsz == 1 and w_in.shape[0] == 1, "one sequence, one layer (the final norm is fused into the MoE combine)"
    l = 0
    pool_width = pool_w.shape[1] * pool_w.shape[2]
    hgrn_width = hgrn_norm.shape[1]
    xt = x.reshape(seq, d)
    mem_t = mem.reshape(mem.shape[1], d)
    h = _rmsnorm(xt, norm_mix[l], BF16)
    proj = _matmul([h], w_in[l], out_dtype=F32)
    y_pool = _pool_mixer(proj, pool_w[l], pool_scale[l])
    y_hgrn = _hgrn_mixer(proj, lb_fwd, lb_bwd, hgrn_norm[l], layer=l, width=hgrn_width, col0=pool_width)
    xt = _matmul([y_pool, y_hgrn], w_out[l], resid=xt, out_dtype=F32)
    mem_n = _rmsnorm(mem_t, norm_mem[l], BF16)
    hx = _rmsnorm(xt, norm_xattn[l], BF16)
    q = _matmul([hx], w_q[l], out_dtype=BF16)
    k = _matmul([mem_n], w_k[l], out_dtype=BF16)
    v = _matmul([mem_n], w_v[l], out_dtype=BF16)
    o = _xattn(q, k, v)
    xt = _matmul([o], w_o[l], resid=xt, out_dtype=F32)
    w_router = jnp.concatenate(
        [w_router_group[l],
         jnp.transpose(w_router_expert[l], (1, 0, 2)).reshape(d, N_EXPERTS),
         jnp.zeros((d, ROUTER_COLS - N_GROUPS - N_EXPERTS), F32)], axis=1)
    hp, logits = _router(xt, norm_moe[l], w_router)
    e_w, dest, back, pad0, blk_cnt, blk_start = _route(logits, bm=MOE_BLOCK_ROWS)
    xs = _dispatch_rows(dest, pad0, hp, p=TOP_K * seq + N_EXPERTS * MOE_BLOCK_ROWS, bm=MOE_BLOCK_ROWS)
    ys = _moe_experts(xs, blk_cnt, blk_start, w1[l], w3[l], w2[l], bm=MOE_BLOCK_ROWS, nc=MOE_DOWN_COLS)
    out = _combine(xt, ys, back, e_w, norm_final, nc=MOE_DOWN_COLS)
    return out.reshape(bsz, seq, d)
```

```python
import functools

import jax
import jax.numpy as jnp
from jax import lax
from jax.experimental import pallas as pl
from jax.experimental.pallas import tpu as pltpu

F32 = jnp.float32
BF16 = jnp.bfloat16
U32 = jnp.uint32

EPS = 1e-6
LANES = 128
SUBLANES = 8
VMEM_BYTES_V7X = 64 * 1024 * 1024

POOL_WINDOWS = (2, 4, 8, 16)
HEAD = 128
CHUNK = 64
XATTN_HEADS = 4
N_GROUPS = 4
EXPERTS_PER_GROUP = 8
N_EXPERTS = N_GROUPS * EXPERTS_PER_GROUP
TOP_K = 2
ROUTER_COLS = LANES


def _params(sem, vmem_bytes):
    return pltpu.CompilerParams(dimension_semantics=sem, vmem_limit_bytes=int(vmem_bytes))


def _vmem_limit(block_bytes):
    return min(int(block_bytes * 1.25) + (6 << 20), VMEM_BYTES_V7X - (4 << 20))


def _rmsnorm_kernel(x_ref, g_ref, o_ref):
    x = x_ref[...]
    ms = jnp.mean(x * x, axis=-1, keepdims=True)
    o_ref[...] = (x * lax.rsqrt(ms + EPS) * g_ref[...]).astype(o_ref.dtype)


def _rmsnorm(x, gain, out_dtype, tm=512):
    t, d = x.shape
    tm = min(tm, t)
    blk = tm * d * (4 + jnp.dtype(out_dtype).itemsize) * 2
    return pl.pallas_call(
        _rmsnorm_kernel,
        grid=(t // tm,),
        in_specs=[pl.BlockSpec((tm, d), lambda i: (i, 0)),
                  pl.BlockSpec((1, d), lambda i: (0, 0))],
        out_specs=pl.BlockSpec((tm, d), lambda i: (i, 0)),
        out_shape=jax.ShapeDtypeStruct((t, d), out_dtype),
        compiler_params=_params(("arbitrary",), _vmem_limit(blk)),
        name="rmsnorm",
    )(x, gain.reshape(1, d))


def _mm_kernel(*refs, n_a, has_resid):
    a_refs = refs[:n_a]
    w_ref = refs[n_a]
    r_ref = refs[n_a + 1] if has_resid else None
    o_ref, wb_ref = refs[-2], refs[-1]

    @pl.when(pl.program_id(1) == 0)
    def _():
        wb_ref[...] = w_ref[...].astype(BF16)

    acc = None
    k0 = 0
    for a_ref in a_refs:
        kk = a_ref.shape[1]
        part = jnp.dot(a_ref[...], wb_ref[k0:k0 + kk, :], preferred_element_type=F32)
        acc = part if acc is None else acc + part
        k0 += kk
    if has_resid:
        acc = acc + r_ref[...]
    o_ref[...] = acc.astype(o_ref.dtype)


def _matmul(a_parts, w, *, resid=None, out_dtype=F32, tm=1024, tn=512):
    m = a_parts[0].shape[0]
    k, n = w.shape
    assert sum(a.shape[1] for a in a_parts) == k
    tm = min(tm, m)
    in_specs = [pl.BlockSpec((tm, a.shape[1]), lambda j, i: (i, 0)) for a in a_parts]
    in_specs.append(pl.BlockSpec((k, tn), lambda j, i: (0, j)))
    args = list(a_parts) + [w]
    if resid is not None:
        in_specs.append(pl.BlockSpec((tm, tn), lambda j, i: (i, j)))
        args.append(resid)
    osz = jnp.dtype(out_dtype).itemsize
    blk = (2 * tm * k * 2 + 2 * k * tn * 4 + k * tn * 2 + 2 * tm * tn * osz
           + (2 * tm * tn * 4 if resid is not None else 0) + tm * tn * 4)
    return pl.pallas_call(
        functools.partial(_mm_kernel, n_a=len(a_parts), has_resid=resid is not None),
        grid=(n // tn, m // tm),
        in_specs=in_specs,
        out_specs=pl.BlockSpec((tm, tn), lambda j, i: (i, j)),
        out_shape=jax.ShapeDtypeStruct((m, n), out_dtype),
        scratch_shapes=[pltpu.VMEM((k, tn), BF16)],
        compiler_params=_params(("arbitrary", "arbitrary"), _vmem_limit(blk)),
        name="matmul",
    )(*args)


def _pool_kernel(u_ref, up_ref, un_ref, w_ref, s_ref, o_ref, *, tm, seq, group):
    i = pl.program_id(0)
    last = pl.num_programs(0) - 1
    n = tm + 2 * SUBLANES
    row = lax.broadcasted_iota(jnp.int32, (tm, 1), 0) + i * tm
    for gi, w in enumerate(POOL_WINDOWS):
        cs = slice(gi * group, (gi + 1) * group)
        u = u_ref[:, cs]
        prev = jnp.where(i > 0, up_ref[:, cs], 0.0)
        nxt = jnp.where(i < last, un_ref[:, cs], 0.0)
        f = jnp.concatenate([prev, u, nxt], axis=0)
        step = 1
        while step < w:
            f = f + pltpu.roll(f, n - step, 0)
            step *= 2
        half = w // 2
        first = SUBLANES - half
        if first:
            f = pltpu.roll(f, n - first, 0)
        win = f[:tm]
        lo = jnp.maximum(row - half, 0)
        hi = jnp.minimum(row + half - 1, seq - 1)
        cnt = (hi - lo + 1).astype(F32)
        mixed = win / cnt - u
        y = jnp.dot(mixed.astype(BF16), w_ref[gi].astype(BF16), preferred_element_type=F32)
        o_ref[:, cs] = (y * s_ref[:, cs]).astype(o_ref.dtype)


def _pool_mixer(proj, pool_w, pool_scale, *, tm=512):
    t = proj.shape[0]
    ng, group, _ = pool_w.shape
    width = ng * group
    rpb = tm // SUBLANES
    nhalo = t // SUBLANES
    blk = 2 * (tm * width * 4 + 2 * SUBLANES * width * 4 + ng * group * group * 4 + tm * width * 2) + 6 * tm * group * 4
    return pl.pallas_call(
        functools.partial(_pool_kernel, tm=tm, seq=t, group=group),
        grid=(t // tm,),
        in_specs=[pl.BlockSpec((tm, width), lambda i: (i, 0)),
                  pl.BlockSpec((SUBLANES, width), lambda i: (jnp.maximum(i * rpb - 1, 0), 0)),
                  pl.BlockSpec((SUBLANES, width), lambda i: (jnp.minimum((i + 1) * rpb, nhalo - 1), 0)),
                  pl.BlockSpec((ng, group, group), lambda i: (0, 0, 0)),
                  pl.BlockSpec((1, width), lambda i: (0, 0))],
        out_specs=pl.BlockSpec((tm, width), lambda i: (i, 0)),
        out_shape=jax.ShapeDtypeStruct((t, width), BF16),
        compiler_params=_params(("arbitrary",), _vmem_limit(blk)),
        name="pool_mixer",
    )(proj, proj, proj, pool_w, pool_scale.reshape(1, width))


def _chunk_cumsum(x, reverse):
    n = x.shape[0]
    pos = lax.broadcasted_iota(jnp.int32, x.shape, 0) % CHUNK
    s = 1
    while s < CHUNK:
        if reverse:
            x = x + jnp.where(pos < CHUNK - s, pltpu.roll(x, n - s, 0), 0.0)
        else:
            x = x + jnp.where(pos >= s, pltpu.roll(x, s, 0), 0.0)
        s *= 2
    return x


def _lower_bound(lb_raw, layer):
    e = jnp.exp(lb_raw - jnp.max(lb_raw, axis=0, keepdims=True))
    return jnp.sum(e[:layer + 1], axis=0, keepdims=True) / jnp.sum(e, axis=0, keepdims=True)


def _hgrn_direction(q_raw, f_raw, v, lb, st_ref, h, mask, *, reverse):
    tb = q_raw.shape[0]
    nc = tb // CHUNK
    q = q_raw * jax.nn.sigmoid(q_raw)
    f = lb + (1.0 - lb) * jax.nn.sigmoid(f_raw)
    logf = jnp.log(f)
    k = 1.0 - f
    b = _chunk_cumsum(logf, reverse)
    b3 = b.reshape(nc, CHUNK, HEAD)
    edge = 0 if reverse else CHUNK - 1
    b_last = b3[:, edge:edge + 1, :]
    qd = (q * jnp.exp(b)).astype(BF16)
    kd = (k * jnp.exp(-b)).astype(BF16)
    k_end = (k.reshape(nc, CHUNK, HEAD) * jnp.exp(b_last - b3)).astype(BF16)
    decay = jnp.exp(b_last)
    vb = v.astype(BF16)

    a = lax.dot_general(qd, kd, (((1,), (1,)), ((), ())), preferred_element_type=F32)
    a = jnp.where(mask, a, 0.0)
    o_intra = jnp.dot(a.astype(BF16), vb, preferred_element_type=F32)

    st = st_ref[h]
    o_inter = [None] * nc
    order = range(nc - 1, -1, -1) if reverse else range(nc)
    for c in order:
        rows = slice(c * CHUNK, (c + 1) * CHUNK)
        o_inter[c] = lax.dot_general(qd[rows], st.astype(BF16), (((1,), (1,)), ((), ())),
                                     preferred_element_type=F32)
        d_st = lax.dot_general(vb[rows], k_end[c], (((0,), (0,)), ((), ())), preferred_element_type=F32)
        st = st * decay[c] + d_st
    st_ref[h] = st
    return o_intra + jnp.concatenate(o_inter, axis=0)


def _intra_mask(tb, reverse):
    r = lax.broadcasted_iota(jnp.int32, (tb, tb), 0)
    c = lax.broadcasted_iota(jnp.int32, (tb, tb), 1)
    same = (r // CHUNK) == (c // CHUNK)
    return same & ((c >= r) if reverse else (c <= r))


def _hgrn_fwd_kernel(q_ref, f_ref, i_ref, lb_ref, o_ref, st_ref, *, layer, heads):
    @pl.when(pl.program_id(0) == 0)
    def _():
        st_ref[...] = jnp.zeros_like(st_ref)

    mask = _intra_mask(q_ref.shape[0], False)

    for h in range(heads):
        hs = slice(h * HEAD, (h + 1) * HEAD)
        lb = _lower_bound(lb_ref[:, hs], layer)
        o_ref[:, hs] = _hgrn_direction(q_ref[:, hs], f_ref[:, hs], i_ref[:, hs], lb, st_ref, h, mask,
                                       reverse=False)


def _hgrn_bwd_kernel(q_ref, f_ref, i_ref, g_ref, of_ref, lb_ref, gain_ref, y_ref, st_ref, *, layer, heads):
    @pl.when(pl.program_id(0) == 0)
    def _():
        st_ref[...] = jnp.zeros_like(st_ref)

    mask = _intra_mask(q_ref.shape[0], True)

    for h in range(heads):
        hs = slice(h * HEAD, (h + 1) * HEAD)
        lb = _lower_bound(lb_ref[:, hs], layer)
        o = of_ref[:, hs] + _hgrn_direction(q_ref[:, hs], f_ref[:, hs], i_ref[:, hs], lb, st_ref, h, mask,
                                            reverse=True)
        o = o * lax.rsqrt(jnp.mean(o * o, axis=-1, keepdims=True) + EPS) * gain_ref[:, hs]
        g = g_ref[:, hs]
        y_ref[:, hs] = (o * (g * jax.nn.sigmoid(g))).astype(y_ref.dtype)


def _hgrn_mixer(proj, lb_fwd, lb_bwd, hgrn_norm, *, layer, width, col0, tb=256):
    t = proj.shape[0]
    heads = width // HEAD
    nb = t // tb
    c = col0 // width
    layers = lb_fwd.shape[0]
    blk_f = 2 * (3 * tb * width * 4 + layers * width * 4 + tb * width * 4) + heads * HEAD * HEAD * 4
    o_f = pl.pallas_call(
        functools.partial(_hgrn_fwd_kernel, layer=layer, heads=heads),
        grid=(nb,),
        in_specs=[pl.BlockSpec((tb, width), lambda b: (b, c)),
                  pl.BlockSpec((tb, width), lambda b: (b, c + 1)),
                  pl.BlockSpec((tb, width), lambda b: (b, c + 3)),
                  pl.BlockSpec((layers, width), lambda b: (0, 0))],
        out_specs=pl.BlockSpec((tb, width), lambda b: (b, 0)),
        out_shape=jax.ShapeDtypeStruct((t, width), F32),
        scratch_shapes=[pltpu.VMEM((heads, HEAD, HEAD), F32)],
        compiler_params=_params(("arbitrary",), _vmem_limit(blk_f)),
        name="hgrn_fwd",
    )(proj, proj, proj, lb_fwd)
    blk_b = 2 * (5 * tb * width * 4 + layers * width * 4 + width * 4 + tb * width * 2) + heads * HEAD * HEAD * 4
    return pl.pallas_call(
        functools.partial(_hgrn_bwd_kernel, layer=layer, heads=heads),
        grid=(nb,),
        in_specs=[pl.BlockSpec((tb, width), lambda b: (nb - 1 - b, c)),
                  pl.BlockSpec((tb, width), lambda b: (nb - 1 - b, c + 2)),
                  pl.BlockSpec((tb, width), lambda b: (nb - 1 - b, c + 3)),
                  pl.BlockSpec((tb, width), lambda b: (nb - 1 - b, c + 4)),
                  pl.BlockSpec((tb, width), lambda b: (nb - 1 - b, 0)),
                  pl.BlockSpec((layers, width), lambda b: (0, 0)),
                  pl.BlockSpec((1, width), lambda b: (0, 0))],
        out_specs=pl.BlockSpec((tb, width), lambda b: (nb - 1 - b, 0)),
        out_shape=jax.ShapeDtypeStruct((t, width), BF16),
        scratch_shapes=[pltpu.VMEM((heads, HEAD, HEAD), F32)],
        compiler_params=_params(("arbitrary",), _vmem_limit(blk_b)),
        name="hgrn_bwd",
    )(proj, proj, proj, proj, o_f, lb_bwd, hgrn_norm.reshape(1, width))


def _xattn_kernel(q_ref, k_ref, v_ref, o_ref, *, scale):
    s = lax.dot_general(q_ref[...], k_ref[...], (((1,), (1,)), ((), ())), preferred_element_type=F32) * scale
    s = s - jnp.max(s, axis=-1, keepdims=True)
    p = jnp.exp(s)
    p = p / jnp.sum(p, axis=-1, keepdims=True)
    o_ref[...] = jnp.dot(p.astype(BF16), v_ref[...], preferred_element_type=F32).astype(o_ref.dtype)


def _xattn(q, k, v, *, tm=1024):
    t, d = q.shape
    m = k.shape[0]
    dh = d // XATTN_HEADS
    blk = 2 * (2 * tm * dh * 2 + 2 * m * dh * 2) + 4 * tm * m * 4 + tm * dh * 4
    return pl.pallas_call(
        functools.partial(_xattn_kernel, scale=dh ** -0.5),
        grid=(XATTN_HEADS, t // tm),
        in_specs=[pl.BlockSpec((tm, dh), lambda h, i: (i, h)),
                  pl.BlockSpec((m, dh), lambda h, i: (0, h)),
                  pl.BlockSpec((m, dh), lambda h, i: (0, h))],
        out_specs=pl.BlockSpec((tm, dh), lambda h, i: (i, h)),
        out_shape=jax.ShapeDtypeStruct((t, d), BF16),
        compiler_params=_params(("arbitrary", "arbitrary"), _vmem_limit(blk)),
        name="xattn",
    )(q, k, v)


def _pack_halves(x):
    w = x.shape[1] // 2
    hi = lax.bitcast_convert_type(x[:, :w].astype(BF16).astype(F32), U32)
    lo = lax.bitcast_convert_type(x[:, w:].astype(BF16).astype(F32), U32)
    return hi | (lo >> 16)


def _unpack_halves(p):
    hi = lax.bitcast_convert_type(p & jnp.uint32(0xFFFF0000), F32)
    lo = lax.bitcast_convert_type(p << 16, F32)
    return hi, lo


def _load_row_slabs(flat_ref, s, j0, nj):
    rows = flat_ref.shape[0] // s
    return jnp.concatenate([flat_ref[pl.ds(j0 + j, rows, stride=s), :] for j in range(nj)], axis=1)


def _store_row_slabs(ref, val):
    for j in range(ref.shape[1]):
        ref[:, j, :] = val[:, j * LANES:(j + 1) * LANES]


def _router_kernel(x_ref, g_ref, wr_ref, hp_ref, lg_ref, whi_ref, wlo_ref):
    @pl.when(pl.program_id(0) == 0)
    def _():
        w = wr_ref[...]
        w_hi = w.astype(BF16)
        whi_ref[...] = w_hi
        wlo_ref[...] = (w - w_hi.astype(F32)).astype(BF16)

    x = x_ref[...]
    ms = jnp.mean(x * x, axis=-1, keepdims=True)
    h = x * lax.rsqrt(ms + EPS) * g_ref[...]
    _store_row_slabs(hp_ref, _pack_halves(h))
    h_hi = h.astype(BF16)
    h_lo = (h - h_hi.astype(F32)).astype(BF16)
    lg_ref[...] = (jnp.dot(h_hi, whi_ref[...], preferred_element_type=F32)
                   + (jnp.dot(h_hi, wlo_ref[...], preferred_element_type=F32)
                      + jnp.dot(h_lo, whi_ref[...], preferred_element_type=F32)))


def _router(x, gain, w_router, *, tm=256):
    t, d = x.shape
    blk = 2 * (tm * d * 4 + d * 4 + d * ROUTER_COLS * 4 + tm * d * 2 + tm * ROUTER_COLS * 4) + 3 * tm * d * 4
    return pl.pallas_call(
        _router_kernel,
        grid=(t // tm,),
        in_specs=[pl.BlockSpec((tm, d), lambda i: (i, 0)),
                  pl.BlockSpec((1, d), lambda i: (0, 0)),
                  pl.BlockSpec((d, ROUTER_COLS), lambda i: (0, 0))],
        out_specs=[pl.BlockSpec((tm, d // 2 // LANES, LANES), lambda i: (i, 0, 0)),
                   pl.BlockSpec((tm, ROUTER_COLS), lambda i: (i, 0))],
        out_shape=[jax.ShapeDtypeStruct((t, d // 2 // LANES, LANES), U32),
                   jax.ShapeDtypeStruct((t, ROUTER_COLS), F32)],
        scratch_shapes=[pltpu.VMEM((d, ROUTER_COLS), BF16), pltpu.VMEM((d, ROUTER_COLS), BF16)],
        compiler_params=_params(("arbitrary",), _vmem_limit(blk)),
        name="moe_router",
    )(x, gain.reshape(1, d), w_router)


def _start_row_copies(idx_ref, base, n, src_ref, dst_at, sem):
    def issue(r, carry):
        pltpu.make_async_copy(src_ref.at[idx_ref[base + r]], dst_at(r), sem).start()
        return carry

    lax.fori_loop(0, n, issue, 0, unroll=8)


def _wait_row_copies(n, src_ref, dst_at, sem):
    def drain(r, carry):
        pltpu.make_async_copy(src_ref.at[0], dst_at(r), sem).wait()
        return carry

    lax.fori_loop(0, n, drain, 0, unroll=8)


def _dispatch_rows_kernel(dest_ref, pad0_ref, src_ref, xs_ref, buf, zeros, sem, zsem, *, bm):
    i = pl.program_id(0)
    nt = pl.num_programs(0)
    tm = src_ref.shape[0]
    slot = i % 2

    def fill_copy(e):
        row0 = pl.multiple_of(jnp.maximum(pad0_ref[e], 0), bm)
        return pltpu.make_async_copy(zeros, xs_ref.at[pl.ds(row0, bm)], zsem)

    @pl.when(i == 0)
    def _():
        zeros[...] = jnp.zeros_like(zeros)

        def fill(e, carry):
            @pl.when(pad0_ref[e] >= 0)
            def _():
                fill_copy(e).start()

            return carry

        lax.fori_loop(0, pad0_ref.shape[0], fill, 0)

        def fill_wait(e, carry):
            @pl.when(pad0_ref[e] >= 0)
            def _():
                fill_copy(e).wait()

            return carry

        lax.fori_loop(0, pad0_ref.shape[0], fill_wait, 0)

    def wait_tile(s):
        def drain(r, carry):
            for _ in range(TOP_K):
                pltpu.make_async_copy(buf.at[s, 0], xs_ref.at[0], sem.at[s]).wait()
            return carry

        lax.fori_loop(0, tm, drain, 0, unroll=4)

    @pl.when(i >= 2)
    def _():
        wait_tile(slot)

    buf[slot] = src_ref[...]

    def issue(r, carry):
        for k in range(TOP_K):
            row = dest_ref[(i * tm + r) * TOP_K + k]
            pltpu.make_async_copy(buf.at[slot, r], xs_ref.at[row], sem.at[slot]).start()
        return carry

    lax.fori_loop(0, tm, issue, 0, unroll=4)

    @pl.when(i == nt - 1)
    def _():
        wait_tile(slot)

        @pl.when(nt >= 2)
        def _():
            wait_tile(1 - slot)


def _dispatch_rows(dest, pad0, src, *, p, bm, tm=256):
    t = src.shape[0]
    slab = src.shape[1:]
    blk = 5 * tm * slab[0] * slab[1] * 4 + bm * slab[0] * slab[1] * 4
    return pl.pallas_call(
        functools.partial(_dispatch_rows_kernel, bm=bm),
        grid_spec=pltpu.PrefetchScalarGridSpec(
            num_scalar_prefetch=2,
            grid=(t // tm,),
            in_specs=[pl.BlockSpec((tm,) + slab, lambda i, dest, pad0: (i, 0, 0))],
            out_specs=pl.BlockSpec(memory_space=pltpu.HBM),
            scratch_shapes=[pltpu.VMEM((2, tm) + slab, src.dtype), pltpu.VMEM((bm,) + slab, src.dtype),
                            pltpu.SemaphoreType.DMA((2,)), pltpu.SemaphoreType.DMA(())],
        ),
        out_shape=jax.ShapeDtypeStruct((p,) + slab, src.dtype),
        compiler_params=_params(("arbitrary",), _vmem_limit(blk)),
        name="dispatch_rows",
    )(dest, pad0, src)


BLOCK_COPY_PRIORITY = 0


def _start_first_blocks(cnt_ref, start_ref, e, in_copy_at):
    n = cnt_ref[e] >> 1
    b = start_ref[e]

    @pl.when(n > 0)
    def _():
        in_copy_at(b, 0).start(priority=BLOCK_COPY_PRIORITY)

    @pl.when(n > 1)
    def _():
        in_copy_at(b + 1, 1).start(priority=BLOCK_COPY_PRIORITY)


def _expert_grid_step(cnt_ref, start_ref, done_ref, in_copy_at, out_copy_at, prologue, compute):
    e = pl.program_id(1)
    ne = pl.num_programs(1)
    step = pl.program_id(0) * ne + e
    last_step = pl.num_programs(0) * ne - 1
    nblk = cnt_ref[e] >> 1
    tail_half = (cnt_ref[e] & 1) == 1
    blk0 = start_ref[e]

    @pl.when(step == 0)
    def _():
        done_ref[0] = 0
        _start_first_blocks(cnt_ref, start_ref, e, in_copy_at)

    done = done_ref[0]

    @pl.when(nblk > 0)
    def _():
        prologue()

        def body(k, carry):
            in_slot = k % 2
            out_slot = (done + k) % 2
            in_copy_at(blk0 + k, in_slot).wait()

            @pl.when(done + k >= 2)
            def _():
                out_copy_at(0, out_slot).wait()

            half = tail_half & (k == nblk - 1)

            @pl.when(jnp.logical_not(half))
            def _():
                compute(in_slot, out_slot, False)

            @pl.when(half)
            def _():
                compute(in_slot, out_slot, True)

            out_copy_at(blk0 + k, out_slot).start(priority=BLOCK_COPY_PRIORITY)

            @pl.when(k + 2 < nblk)
            def _():
                in_copy_at(blk0 + k + 2, in_slot).start(priority=BLOCK_COPY_PRIORITY)

            return carry

        lax.fori_loop(0, nblk, body, 0)
        done_ref[0] = done + nblk

    @pl.when(step < last_step)
    def _():
        _start_first_blocks(cnt_ref, start_ref, (e + 1) % ne, in_copy_at)

    @pl.when(step == last_step)
    def _():
        total = done + nblk

        @pl.when(total >= 2)
        def _():
            out_copy_at(0, total % 2).wait()

        @pl.when(total >= 1)
        def _():
            out_copy_at(0, (total - 1) % 2).wait()


def _moe_up_kernel(cnt_ref, start_ref, xs_ref, w1_ref, w3_ref, h_ref, w1b_ref, w3b_ref, xbuf, obuf, xsem, osem,
                   done_ref, *, bm):
    f = pl.program_id(0)
    d, fc = w1b_ref.shape
    nslab = d // 2 // LANES
    xrows = bm * nslab

    def x_copy(blk, slot):
        row0 = pl.multiple_of(blk * xrows, xrows)
        return pltpu.make_async_copy(xs_ref.at[pl.ds(row0, xrows)], xbuf.at[slot], xsem.at[slot])

    def h_copy(blk, slot):
        row0 = pl.multiple_of(blk * bm, bm)
        col0 = pl.multiple_of(f * fc, fc)
        return pltpu.make_async_copy(obuf.at[slot], h_ref.at[pl.ds(row0, bm), pl.ds(col0, fc)], osem.at[slot])

    def cast_weights():
        w1b_ref[...] = w1_ref[0].astype(BF16)
        w3b_ref[...] = w3_ref[0].astype(BF16)

    def compute(in_slot, out_slot, half):
        rows = bm // 2 if half else bm
        xin = xbuf.at[in_slot, pl.ds(0, rows * nslab)] if half else xbuf.at[in_slot]
        hi, lo = _unpack_halves(_load_row_slabs(xin, nslab, 0, nslab))
        hi = hi.astype(BF16)
        lo = lo.astype(BF16)
        d2 = d // 2
        a = (jnp.dot(hi, w1b_ref[:d2, :], preferred_element_type=F32)
             + jnp.dot(lo, w1b_ref[d2:, :], preferred_element_type=F32))
        c = (jnp.dot(hi, w3b_ref[:d2, :], preferred_element_type=F32)
             + jnp.dot(lo, w3b_ref[d2:, :], preferred_element_type=F32))
        res = (a * jax.nn.sigmoid(a) * c).astype(obuf.dtype)
        if half:
            obuf[out_slot, :rows] = res
            obuf[out_slot, rows:] = jnp.zeros((bm - rows, fc), obuf.dtype)
        else:
            obuf[out_slot] = res

    _expert_grid_step(cnt_ref, start_ref, done_ref, x_copy, h_copy, cast_weights, compute)


def _moe_down_kernel(cnt_ref, start_ref, h_ref, w2_ref, y_ref, w2b_ref, hbuf, obuf, hsem, osem, done_ref, *, bm):
    c = pl.program_id(0)
    nj = obuf.shape[2]

    def h_copy(blk, slot):
        row0 = pl.multiple_of(blk * bm, bm)
        return pltpu.make_async_copy(h_ref.at[pl.ds(row0, bm)], hbuf.at[slot], hsem.at[slot])

    def y_copy(blk, slot):
        row0 = pl.multiple_of(blk * bm, bm)
        j0 = pl.multiple_of(c * nj, nj)
        return pltpu.make_async_copy(obuf.at[slot], y_ref.at[pl.ds(row0, bm), pl.ds(j0, nj)], osem.at[slot])

    def cast_weights():
        w2b_ref[...] = w2_ref[0].astype(BF16)

    def compute(in_slot, out_slot, half):
        rows = bm // 2 if half else bm
        hin = hbuf[in_slot, :rows] if half else hbuf[in_slot]
        packed = _pack_halves(jnp.dot(hin.astype(BF16), w2b_ref[...], preferred_element_type=F32))
        if half:
            _store_row_slabs(obuf.at[out_slot, pl.ds(0, rows)], packed)
            obuf[out_slot, rows:] = jnp.zeros((bm - rows, nj, LANES), obuf.dtype)
        else:
            _store_row_slabs(obuf.at[out_slot], packed)

    _expert_grid_step(cnt_ref, start_ref, done_ref, h_copy, y_copy, cast_weights, compute)


def _moe_experts(xs, blk_cnt, blk_start, w1, w3, w2, *, bm, fc=512, nc=2048):
    p, nslab, _ = xs.shape
    d = 2 * nslab * LANES
    ne, _, de = w1.shape
    any_spec = pl.BlockSpec(memory_space=pltpu.HBM)
    blk_up = (4 * d * fc * 4 + 2 * d * fc * 2 + 2 * bm * nslab * LANES * 4 + 2 * bm * fc * 4
              + 2 * bm * d * 2 + 3 * bm * fc * 4)
    h = pl.pallas_call(
        functools.partial(_moe_up_kernel, bm=bm),
        grid_spec=pltpu.PrefetchScalarGridSpec(
            num_scalar_prefetch=2,
            grid=(de // fc, ne),
            in_specs=[any_spec,
                      pl.BlockSpec((1, d, fc), lambda f, e, cnt, start: (e, 0, f)),
                      pl.BlockSpec((1, d, fc), lambda f, e, cnt, start: (e, 0, f))],
            out_specs=any_spec,
            scratch_shapes=[pltpu.VMEM((d, fc), BF16), pltpu.VMEM((d, fc), BF16),
                            pltpu.VMEM((2, bm * nslab, LANES), U32), pltpu.VMEM((2, bm, fc), F32),
                            pltpu.SemaphoreType.DMA((2,)), pltpu.SemaphoreType.DMA((2,)),
                            pltpu.SMEM((1,), jnp.int32)],
        ),
        out_shape=jax.ShapeDtypeStruct((p, de), F32),
        compiler_params=_params(("arbitrary", "arbitrary"), _vmem_limit(blk_up)),
        name="moe_up",
    )(blk_cnt, blk_start, xs.reshape(p * nslab, LANES), w1, w3)
    nj = nc // 2 // LANES
    blk_dn = 2 * de * nc * 4 + de * nc * 2 + 3 * bm * de * 4 + 2 * bm * nj * LANES * 4 + 2 * bm * nc * 4
    return pl.pallas_call(
        functools.partial(_moe_down_kernel, bm=bm),
        grid_spec=pltpu.PrefetchScalarGridSpec(
            num_scalar_prefetch=2,
            grid=(d // nc, ne),
            in_specs=[any_spec,
                      pl.BlockSpec((1, de, nc), lambda c, e, cnt, start: (e, 0, c))],
            out_specs=any_spec,
            scratch_shapes=[pltpu.VMEM((de, nc), BF16),
                            pltpu.VMEM((2, bm, de), F32), pltpu.VMEM((2, bm, nj, LANES), U32),
                            pltpu.SemaphoreType.DMA((2,)), pltpu.SemaphoreType.DMA((2,)),
                            pltpu.SMEM((1,), jnp.int32)],
        ),
        out_shape=jax.ShapeDtypeStruct((p, nslab, LANES), U32),
        compiler_params=_params(("arbitrary", "arbitrary"), _vmem_limit(blk_dn)),
        name="moe_down",
    )(blk_cnt, blk_start, h, w2)


def _combine_kernel(back_ref, x_ref, ys_ref, ew_ref, g_ref, o_ref, ybuf, sem, *, nc):
    i = pl.program_id(0)
    nt = pl.num_programs(0)
    tm, d = x_ref.shape
    t = nt * tm
    nslab = d // 2 // LANES
    slot = i % 2

    def row_at(s, k):
        def dst_at(r):
            return ybuf.at[s, pl.ds(pl.multiple_of((k * tm + r) * nslab, nslab), nslab)]
        return dst_at

    def start_tile(tile, s):
        for k in range(TOP_K):
            _start_row_copies(back_ref, k * t + tile * tm, tm, ys_ref, row_at(s, k), sem.at[s])

    @pl.when(i == 0)
    def _():
        start_tile(0, 0)

    @pl.when(i + 1 < nt)
    def _():
        start_tile(i + 1, 1 - slot)

    for k in range(TOP_K):
        _wait_row_copies(tm, ys_ref, row_at(slot, k), sem.at[slot])

    ew = ew_ref[...]
    w0 = ew[:, 0:1]
    w1 = ew[:, 1:2]
    half = nc // 2
    y0_ref = ybuf.at[slot, pl.ds(0, tm * nslab)]
    y1_ref = ybuf.at[slot, pl.ds(tm * nslab, tm * nslab)]
    pieces = []
    ss = None
    for c in range(x_ref.shape[1] // nc):
        hi0, lo0 = _unpack_halves(_load_row_slabs(y0_ref, nslab, c * half // LANES, half // LANES))
        hi1, lo1 = _unpack_halves(_load_row_slabs(y1_ref, nslab, c * half // LANES, half // LANES))
        for j, (p0, p1) in enumerate(((hi0, hi1), (lo0, lo1))):
            cols = slice(c * nc + j * half, c * nc + (j + 1) * half)
            z = x_ref[:, cols] + w0 * p0 + w1 * p1
            pieces.append((cols, z))
            s = jnp.sum(z * z, axis=-1, keepdims=True)
            ss = s if ss is None else ss + s
    inv = lax.rsqrt(ss / x_ref.shape[1] + EPS)
    for cols, z in pieces:
        o_ref[:, cols] = z * inv * g_ref[:, cols]


def _combine(x, ys, back, e_w, gain, *, nc, tm=256):
    t, d = x.shape
    nt = t // tm
    nslab = ys.shape[1]
    blk = 2 * (2 * tm * d * 4 + tm * LANES * 4 + d * 4) + 2 * TOP_K * tm * d * 2 + 3 * tm * d * 4
    return pl.pallas_call(
        functools.partial(_combine_kernel, nc=nc),
        grid_spec=pltpu.PrefetchScalarGridSpec(
            num_scalar_prefetch=1,
            grid=(nt,),
            in_specs=[pl.BlockSpec((tm, d), lambda i, back: (i, 0)),
                      pl.BlockSpec(memory_space=pltpu.HBM),
                      pl.BlockSpec((tm, TOP_K), lambda i, back: (i, 0)),
                      pl.BlockSpec((1, d), lambda i, back: (0, 0))],
            out_specs=pl.BlockSpec((tm, d), lambda i, back: (i, 0)),
            scratch_shapes=[pltpu.VMEM((2, TOP_K * tm * nslab, LANES), ys.dtype), pltpu.SemaphoreType.DMA((2,))],
        ),
        out_shape=jax.ShapeDtypeStruct((t, d), F32),
        compiler_params=_params(("arbitrary",), _vmem_limit(blk)),
        name="moe_combine",
    )(back, x, ys, e_w, gain.reshape(1, d))


def _route(logits, *, bm):
    t = logits.shape[0]
    g_logits = logits[:, :N_GROUPS]
    e_logits = logits[:, N_GROUPS:N_GROUPS + N_EXPERTS].reshape(t, N_GROUPS, EXPERTS_PER_GROUP)
    g_idx = jnp.argmax(g_logits, axis=-1).astype(jnp.int32)
    g_w = jnp.take_along_axis(jax.nn.softmax(g_logits, axis=-1), g_idx[:, None], axis=-1)
    e_sel = jnp.take_along_axis(e_logits, g_idx[:, None, None], axis=1)[:, 0]
    top_v, top_i = lax.top_k(e_sel, TOP_K)
    e_w = jax.nn.softmax(top_v, axis=-1) * g_w
    eid = g_idx[:, None] * EXPERTS_PER_GROUP + top_i.astype(jnp.int32)

    a = t * TOP_K
    flat_e = eid.reshape(a)
    onehot = (flat_e[:, None] == jnp.arange(N_EXPERTS, dtype=jnp.int32)[None, :]).astype(jnp.int32)
    csum = jnp.cumsum(onehot, axis=0)
    rank = jnp.take_along_axis(csum, flat_e[:, None], axis=1)[:, 0] - 1
    counts = csum[-1]
    padded = ((counts + bm - 1) // bm) * bm
    pad_end = jnp.cumsum(padded)
    start_pad = pad_end - padded
    dest = (start_pad[flat_e] + rank).astype(jnp.int32)
    tail_half = (counts > 0) & ((counts - 1) % bm < bm // 2)
    blk_cnt = (2 * (padded // bm) + tail_half).astype(jnp.int32)
    blk_start = (start_pad // bm).astype(jnp.int32)
    pad0 = jnp.where(padded > 0, pad_end - bm, -1).astype(jnp.int32)
    back = dest.reshape(t, TOP_K).T.reshape(a)
    return e_w, dest, back, pad0, blk_cnt, blk_start


MOE_BLOCK_ROWS = 256
MOE_DOWN_COLS = 4096


def kernel(x, mem, norm_mix, w_in, pool_w, pool_scale, lb_fwd, lb_bwd, hgrn_norm, w_out, norm_xattn, norm_mem,
           w_q, w_k, w_v, w_o, norm_moe, w_router_group, w_router_expert, w1, w3, w2, norm_final):
    bsz, seq, d = x.shape
    assert bsz == 1 and w_in.shape[0] == 1, "one sequence, one layer (the final norm is fused into the MoE combine)"
    l = 0
    pool_width = pool_w.shape[1] * pool_w.shape[2]
    hgrn_width = hgrn_norm.shape[1]
    xt = x.reshape(seq, d)
    mem_t = mem.reshape(mem.shape[1], d)
    h = _rmsnorm(xt, norm_mix[l], BF16)
    proj = _matmul([h], w_in[l], out_dtype=F32)
    y_pool = _pool_mixer(proj, pool_w[l], pool_scale[l])
    y_hgrn = _hgrn_mixer(proj, lb_fwd, lb_bwd, hgrn_norm[l], layer=l, width=hgrn_width, col0=pool_width)
    xt = _matmul([y_pool, y_hgrn], w_out[l], resid=xt, out_dtype=F32)
    mem_n = _rmsnorm(mem_t, norm_mem[l], BF16)
    hx = _rmsnorm(xt, norm_xattn[l], BF16)
    q = _matmul([hx], w_q[l], out_dtype=BF16)
    k = _matmul([mem_n], w_k[l], out_dtype=BF16)
    v = _matmul([mem_n], w_v[l], out_dtype=BF16)
    o = _xattn(q, k, v)
    xt = _matmul([o], w_o[l], resid=xt, out_dtype=F32)
    w_router = jnp.concatenate(
        [w_router_group[l],
         jnp.transpose(w_router_expert[l], (1, 0, 2)).reshape(d, N_EXPERTS),
         jnp.zeros((d, ROUTER_COLS - N_GROUPS - N_EXPERTS), F32)], axis=1)
    hp, logits = _router(xt, norm_moe[l], w_router)
    e_w, dest, back, pad0, blk_cnt, blk_start = _route(logits, bm=MOE_BLOCK_ROWS)
    xs = _dispatch_rows(dest, pad0, hp, p=TOP_K * seq + N_EXPERTS * MOE_BLOCK_ROWS, bm=MOE_BLOCK_ROWS)
    ys = _moe_experts(xs, blk_cnt, blk_start, w1[l], w3[l], w2[l], bm=MOE_BLOCK_ROWS, nc=MOE_DOWN_COLS)
    out = _combine(xt, ys, back, e_w, norm_final, nc=MOE_DOWN_COLS)
    return out.reshape(bsz, seq, d)
```

```python
import functools

import jax
import jax.numpy as jnp
from jax import lax
from jax.experimental import pallas as pl
from jax.experimental.pallas import tpu as pltpu

F32 = jnp.float32
BF16 = jnp.bfloat16
U32 = jnp.uint32

EPS = 1e-6
LANES = 128
SUBLANES = 8
VMEM_BYTES_V7X = 64 * 1024 * 1024

POOL_WINDOWS = (2, 4, 8, 16)
HEAD = 128
CHUNK = 64
XATTN_HEADS = 4
N_GROUPS = 4
EXPERTS_PER_GROUP = 8
N_EXPERTS = N_GROUPS * EXPERTS_PER_GROUP
TOP_K = 2
ROUTER_COLS = LANES


def _params(sem, vmem_bytes):
    return pltpu.CompilerParams(dimension_semantics=sem, vmem_limit_bytes=int(vmem_bytes))


def _vmem_limit(block_bytes):
    return min(int(block_bytes * 1.25) + (6 << 20), VMEM_BYTES_V7X - (4 << 20))


def _rmsnorm_kernel(x_ref, g_ref, o_ref):
    x = x_ref[...]
    ms = jnp.mean(x * x, axis=-1, keepdims=True)
    o_ref[...] = (x * lax.rsqrt(ms + EPS) * g_ref[...]).astype(o_ref.dtype)


def _rmsnorm(x, gain, out_dtype, tm=512):
    t, d = x.shape
    tm = min(tm, t)
    blk = tm * d * (4 + jnp.dtype(out_dtype).itemsize) * 2
    return pl.pallas_call(
        _rmsnorm_kernel,
        grid=(t // tm,),
        in_specs=[pl.BlockSpec((tm, d), lambda i: (i, 0)),
                  pl.BlockSpec((1, d), lambda i: (0, 0))],
        out_specs=pl.BlockSpec((tm, d), lambda i: (i, 0)),
        out_shape=jax.ShapeDtypeStruct((t, d), out_dtype),
        compiler_params=_params(("arbitrary",), _vmem_limit(blk)),
        name="rmsnorm",
    )(x, gain.reshape(1, d))


def _mm_kernel(*refs, n_a, has_resid):
    a_refs = refs[:n_a]
    w_ref = refs[n_a]
    r_ref = refs[n_a + 1] if has_resid else None
    o_ref, wb_ref = refs[-2], refs[-1]

    @pl.when(pl.program_id(1) == 0)
    def _():
        wb_ref[...] = w_ref[...].astype(BF16)

    acc = None
    k0 = 0
    for a_ref in a_refs:
        kk = a_ref.shape[1]
        part = jnp.dot(a_ref[...], wb_ref[k0:k0 + kk, :], preferred_element_type=F32)
        acc = part if acc is None else acc + part
        k0 += kk
    if has_resid:
        acc = acc + r_ref[...]
    o_ref[...] = acc.astype(o_ref.dtype)


def _matmul(a_parts, w, *, resid=None, out_dtype=F32, tm=1024, tn=512):
    m = a_parts[0].shape[0]
    k, n = w.shape
    assert sum(a.shape[1] for a in a_parts) == k
    tm = min(tm, m)
    in_specs = [pl.BlockSpec((tm, a.shape[1]), lambda j, i: (i, 0)) for a in a_parts]
    in_specs.append(pl.BlockSpec((k, tn), lambda j, i: (0, j)))
    args = list(a_parts) + [w]
    if resid is not None:
        in_specs.append(pl.BlockSpec((tm, tn), lambda j, i: (i, j)))
        args.append(resid)
    osz = jnp.dtype(out_dtype).itemsize
    blk = (2 * tm * k * 2 + 2 * k * tn * 4 + k * tn * 2 + 2 * tm * tn * osz
           + (2 * tm * tn * 4 if resid is not None else 0) + tm * tn * 4)
    return pl.pallas_call(
        functools.partial(_mm_kernel, n_a=len(a_parts), has_resid=resid is not None),
        grid=(n // tn, m // tm),
        in_specs=in_specs,
        out_specs=pl.BlockSpec((tm, tn), lambda j, i: (i, j)),
        out_shape=jax.ShapeDtypeStruct((m, n), out_dtype),
        scratch_shapes=[pltpu.VMEM((k, tn), BF16)],
        compiler_params=_params(("arbitrary", "arbitrary"), _vmem_limit(blk)),
        name="matmul",
    )(*args)


def _pool_kernel(u_ref, up_ref, un_ref, w_ref, s_ref, o_ref, *, tm, seq, group):
    i = pl.program_id(0)
    last = pl.num_programs(0) - 1
    n = tm + 2 * SUBLANES
    row = lax.broadcasted_iota(jnp.int32, (tm, 1), 0) + i * tm
    for gi, w in enumerate(POOL_WINDOWS):
        cs = slice(gi * group, (gi + 1) * group)
        u = u_ref[:, cs]
        prev = jnp.where(i > 0, up_ref[:, cs], 0.0)
        nxt = jnp.where(i < last, un_ref[:, cs], 0.0)
        f = jnp.concatenate([prev, u, nxt], axis=0)
        step = 1
        while step < w:
            f = f + pltpu.roll(f, n - step, 0)
            step *= 2
        half = w // 2
        first = SUBLANES - half
        if first:
            f = pltpu.roll(f, n - first, 0)
        win = f[:tm]
        lo = jnp.maximum(row - half, 0)
        hi = jnp.minimum(row + half - 1, seq - 1)
        cnt = (hi - lo + 1).astype(F32)
        mixed = win / cnt - u
        y = jnp.dot(mixed.astype(BF16), w_ref[gi].astype(BF16), preferred_element_type=F32)
        o_ref[:, cs] = (y * s_ref[:, cs]).astype(o_ref.dtype)


def _pool_mixer(proj, pool_w, pool_scale, *, tm=512):
    t = proj.shape[0]
    ng, group, _ = pool_w.shape
    width = ng * group
    rpb = tm // SUBLANES
    nhalo = t // SUBLANES
    blk = 2 * (tm * width * 4 + 2 * SUBLANES * width * 4 + ng * group * group * 4 + tm * width * 2) + 6 * tm * group * 4
    return pl.pallas_call(
        functools.partial(_pool_kernel, tm=tm, seq=t, group=group),
        grid=(t // tm,),
        in_specs=[pl.BlockSpec((tm, width), lambda i: (i, 0)),
                  pl.BlockSpec((SUBLANES, width), lambda i: (jnp.maximum(i * rpb - 1, 0), 0)),
                  pl.BlockSpec((SUBLANES, width), lambda i: (jnp.minimum((i + 1) * rpb, nhalo - 1), 0)),
                  pl.BlockSpec((ng, group, group), lambda i: (0, 0, 0)),
                  pl.BlockSpec((1, width), lambda i: (0, 0))],
        out_specs=pl.BlockSpec((tm, width), lambda i: (i, 0)),
        out_shape=jax.ShapeDtypeStruct((t, width), BF16),
        compiler_params=_params(("arbitrary",), _vmem_limit(blk)),
        name="pool_mixer",
    )(proj, proj, proj, pool_w, pool_scale.reshape(1, width))


def _chunk_cumsum(x, reverse):
    n = x.shape[0]
    pos = lax.broadcasted_iota(jnp.int32, x.shape, 0) % CHUNK
    s = 1
    while s < CHUNK:
        if reverse:
            x = x + jnp.where(pos < CHUNK - s, pltpu.roll(x, n - s, 0), 0.0)
        else:
            x = x + jnp.where(pos >= s, pltpu.roll(x, s, 0), 0.0)
        s *= 2
    return x


def _lower_bound(lb_raw, layer):
    e = jnp.exp(lb_raw - jnp.max(lb_raw, axis=0, keepdims=True))
    return jnp.sum(e[:layer + 1], axis=0, keepdims=True) / jnp.sum(e, axis=0, keepdims=True)


def _hgrn_direction(q_raw, f_raw, v, lb, st_ref, h, mask, *, reverse):
    tb = q_raw.shape[0]
    nc = tb // CHUNK
    q = q_raw * jax.nn.sigmoid(q_raw)
    f = lb + (1.0 - lb) * jax.nn.sigmoid(f_raw)
    logf = jnp.log(f)
    k = 1.0 - f
    b = _chunk_cumsum(logf, reverse)
    b3 = b.reshape(nc, CHUNK, HEAD)
    edge = 0 if reverse else CHUNK - 1
    b_last = b3[:, edge:edge + 1, :]
    qd = (q * jnp.exp(b)).astype(BF16)
    kd = (k * jnp.exp(-b)).astype(BF16)
    k_end = (k.reshape(nc, CHUNK, HEAD) * jnp.exp(b_last - b3)).astype(BF16)
    decay = jnp.exp(b_last)
    vb = v.astype(BF16)

    a = lax.dot_general(qd, kd, (((1,), (1,)), ((), ())), preferred_element_type=F32)
    a = jnp.where(mask, a, 0.0)
    o_intra = jnp.dot(a.astype(BF16), vb, preferred_element_type=F32)

    st = st_ref[h]
    o_inter = [None] * nc
    order = range(nc - 1, -1, -1) if reverse else range(nc)
    for c in order:
        rows = slice(c * CHUNK, (c + 1) * CHUNK)
        o_inter[c] = lax.dot_general(qd[rows], st.astype(BF16), (((1,), (1,)), ((), ())),
                                     preferred_element_type=F32)
        d_st = lax.dot_general(vb[rows], k_end[c], (((0,), (0,)), ((), ())), preferred_element_type=F32)
        st = st * decay[c] + d_st
    st_ref[h] = st
    return o_intra + jnp.concatenate(o_inter, axis=0)


def _intra_mask(tb, reverse):
    r = lax.broadcasted_iota(jnp.int32, (tb, tb), 0)
    c = lax.broadcasted_iota(jnp.int32, (tb, tb), 1)
    same = (r // CHUNK) == (c // CHUNK)
    return same & ((c >= r) if reverse else (c <= r))


def _hgrn_fwd_kernel(q_ref, f_ref, i_ref, lb_ref, o_ref, st_ref, *, layer, heads):
    @pl.when(pl.program_id(0) == 0)
    def _():
        st_ref[...] = jnp.zeros_like(st_ref)

    mask = _intra_mask(q_ref.shape[0], False)

    for h in range(heads):
        hs = slice(h * HEAD, (h + 1) * HEAD)
        lb = _lower_bound(lb_ref[:, hs], layer)
        o_ref[:, hs] = _hgrn_direction(q_ref[:, hs], f_ref[:, hs], i_ref[:, hs], lb, st_ref, h, mask,
                                       reverse=False)


def _hgrn_bwd_kernel(q_ref, f_ref, i_ref, g_ref, of_ref, lb_ref, gain_ref, y_ref, st_ref, *, layer, heads):
    @pl.when(pl.program_id(0) == 0)
    def _():
        st_ref[...] = jnp.zeros_like(st_ref)

    mask = _intra_mask(q_ref.shape[0], True)

    for h in range(heads):
        hs = slice(h * HEAD, (h + 1) * HEAD)
        lb = _lower_bound(lb_ref[:, hs], layer)
        o = of_ref[:, hs] + _hgrn_direction(q_ref[:, hs], f_ref[:, hs], i_ref[:, hs], lb, st_ref, h, mask,
                                            reverse=True)
        o = o * lax.rsqrt(jnp.mean(o * o, axis=-1, keepdims=True) + EPS) * gain_ref[:, hs]
        g = g_ref[:, hs]
        y_ref[:, hs] = (o * (g * jax.nn.sigmoid(g))).astype(y_ref.dtype)


def _hgrn_mixer(proj, lb_fwd, lb_bwd, hgrn_norm, *, layer, width, col0, tb=256):
    t = proj.shape[0]
    heads = width // HEAD
    nb = t // tb
    c = col0 // width
    layers = lb_fwd.shape[0]
    blk_f = 2 * (3 * tb * width * 4 + layers * width * 4 + tb * width * 4) + heads * HEAD * HEAD * 4
    o_f = pl.pallas_call(
        functools.partial(_hgrn_fwd_kernel, layer=layer, heads=heads),
        grid=(nb,),
        in_specs=[pl.BlockSpec((tb, width), lambda b: (b, c)),
                  pl.BlockSpec((tb, width), lambda b: (b, c + 1)),
                  pl.BlockSpec((tb, width), lambda b: (b, c + 3)),
                  pl.BlockSpec((layers, width), lambda b: (0, 0))],
        out_specs=pl.BlockSpec((tb, width), lambda b: (b, 0)),
        out_shape=jax.ShapeDtypeStruct((t, width), F32),
        scratch_shapes=[pltpu.VMEM((heads, HEAD, HEAD), F32)],
        compiler_params=_params(("arbitrary",), _vmem_limit(blk_f)),
        name="hgrn_fwd",
    )(proj, proj, proj, lb_fwd)
    blk_b = 2 * (5 * tb * width * 4 + layers * width * 4 + width * 4 + tb * width * 2) + heads * HEAD * HEAD * 4
    return pl.pallas_call(
        functools.partial(_hgrn_bwd_kernel, layer=layer, heads=heads),
        grid=(nb,),
        in_specs=[pl.BlockSpec((tb, width), lambda b: (nb - 1 - b, c)),
                  pl.BlockSpec((tb, width), lambda b: (nb - 1 - b, c + 2)),
                  pl.BlockSpec((tb, width), lambda b: (nb - 1 - b, c + 3)),
                  pl.BlockSpec((tb, width), lambda b: (nb - 1 - b, c + 4)),
                  pl.BlockSpec((tb, width), lambda b: (nb - 1 - b, 0)),
                  pl.BlockSpec((layers, width), lambda b: (0, 0)),
                  pl.BlockSpec((1, width), lambda b: (0, 0))],
        out_specs=pl.BlockSpec((tb, width), lambda b: (nb - 1 - b, 0)),
        out_shape=jax.ShapeDtypeStruct((t, width), BF16),
        scratch_shapes=[pltpu.VMEM((heads, HEAD, HEAD), F32)],
        compiler_params=_params(("arbitrary",), _vmem_limit(blk_b)),
        name="hgrn_bwd",
    )(proj, proj, proj, proj, o_f, lb_bwd, hgrn_norm.reshape(1, width))


def _xattn_kernel(q_ref, k_ref, v_ref, o_ref, *, scale):
    s = lax.dot_general(q_ref[...], k_ref[...], (((1,), (1,)), ((), ())), preferred_element_type=F32) * scale
    s = s - jnp.max(s, axis=-1, keepdims=True)
    p = jnp.exp(s)
    p = p / jnp.sum(p, axis=-1, keepdims=True)
    o_ref[...] = jnp.dot(p.astype(BF16), v_ref[...], preferred_element_type=F32).astype(o_ref.dtype)


def _xattn(q, k, v, *, tm=1024):
    t, d = q.shape
    m = k.shape[0]
    dh = d // XATTN_HEADS
    blk = 2 * (2 * tm * dh * 2 + 2 * m * dh * 2) + 4 * tm * m * 4 + tm * dh * 4
    return pl.pallas_call(
        functools.partial(_xattn_kernel, scale=dh ** -0.5),
        grid=(XATTN_HEADS, t // tm),
        in_specs=[pl.BlockSpec((tm, dh), lambda h, i: (i, h)),
                  pl.BlockSpec((m, dh), lambda h, i: (0, h)),
                  pl.BlockSpec((m, dh), lambda h, i: (0, h))],
        out_specs=pl.BlockSpec((tm, dh), lambda h, i: (i, h)),
        out_shape=jax.ShapeDtypeStruct((t, d), BF16),
        compiler_params=_params(("arbitrary", "arbitrary"), _vmem_limit(blk)),
        name="xattn",
    )(q, k, v)


def _pack_halves(x):
    w = x.shape[1] // 2
    hi = lax.bitcast_convert_type(x[:, :w].astype(BF16).astype(F32), U32)
    lo = lax.bitcast_convert_type(x[:, w:].astype(BF16).astype(F32), U32)
    return hi | (lo >> 16)


def _unpack_halves(p):
    hi = lax.bitcast_convert_type(p & jnp.uint32(0xFFFF0000), F32)
    lo = lax.bitcast_convert_type(p << 16, F32)
    return hi, lo


def _load_row_slabs(flat_ref, s, j0, nj):
    rows = flat_ref.shape[0] // s
    return jnp.concatenate([flat_ref[pl.ds(j0 + j, rows, stride=s), :] for j in range(nj)], axis=1)


def _store_row_slabs(flat_ref, s, val):
    rows = flat_ref.shape[0] // s
    for j in range(s):
        flat_ref[pl.ds(j, rows, stride=s), :] = val[:, j * LANES:(j + 1) * LANES]


def _router_kernel(x_ref, g_ref, wr_ref, hp_ref, lg_ref, whi_ref, wlo_ref):
    @pl.when(pl.program_id(0) == 0)
    def _():
        w = wr_ref[...]
        w_hi = w.astype(BF16)
        whi_ref[...] = w_hi
        wlo_ref[...] = (w - w_hi.astype(F32)).astype(BF16)

    x = x_ref[...]
    ms = jnp.mean(x * x, axis=-1, keepdims=True)
    h = x * lax.rsqrt(ms + EPS) * g_ref[...]
    _store_row_slabs(hp_ref, hp_ref.shape[0] // x_ref.shape[0], _pack_halves(h))
    h_hi = h.astype(BF16)
    h_lo = (h - h_hi.astype(F32)).astype(BF16)
    lg_ref[...] = (jnp.dot(h_hi, whi_ref[...], preferred_element_type=F32)
                   + (jnp.dot(h_hi, wlo_ref[...], preferred_element_type=F32)
                      + jnp.dot(h_lo, whi_ref[...], preferred_element_type=F32)))


def _router(x, gain, w_router, *, tm=256):
    t, d = x.shape
    nslab = d // 2 // LANES
    blk = 2 * (tm * d * 4 + d * 4 + d * ROUTER_COLS * 4 + tm * d * 2 + tm * ROUTER_COLS * 4) + 3 * tm * d * 4
    hp, logits = pl.pallas_call(
        _router_kernel,
        grid=(t // tm,),
        in_specs=[pl.BlockSpec((tm, d), lambda i: (i, 0)),
                  pl.BlockSpec((1, d), lambda i: (0, 0)),
                  pl.BlockSpec((d, ROUTER_COLS), lambda i: (0, 0))],
        out_specs=[pl.BlockSpec((tm * nslab, LANES), lambda i: (i, 0)),
                   pl.BlockSpec((tm, ROUTER_COLS), lambda i: (i, 0))],
        out_shape=[jax.ShapeDtypeStruct((t * nslab, LANES), U32),
                   jax.ShapeDtypeStruct((t, ROUTER_COLS), F32)],
        scratch_shapes=[pltpu.VMEM((d, ROUTER_COLS), BF16), pltpu.VMEM((d, ROUTER_COLS), BF16)],
        compiler_params=_params(("arbitrary",), _vmem_limit(blk)),
        name="moe_router",
    )(x, gain.reshape(1, d), w_router)
    return hp.reshape(t, nslab, LANES), logits


def _start_row_copies(idx_ref, base, n, src_ref, dst_at, sem):
    def issue(r, carry):
        pltpu.make_async_copy(src_ref.at[idx_ref[base + r]], dst_at(r), sem).start()
        return carry

    lax.fori_loop(0, n, issue, 0, unroll=8)


def _wait_row_copies(n, src_ref, dst_at, sem):
    def drain(r, carry):
        pltpu.make_async_copy(src_ref.at[0], dst_at(r), sem).wait()
        return carry

    lax.fori_loop(0, n, drain, 0, unroll=8)


def _dispatch_rows_kernel(dest_ref, pad0_ref, src_ref, xs_ref, buf, zeros, sem, zsem, *, bm):
    i = pl.program_id(0)
    nt = pl.num_programs(0)
    tm = src_ref.shape[0]
    slot = i % 2

    def fill_copy(e):
        row0 = pl.multiple_of(jnp.maximum(pad0_ref[e], 0), bm)
        return pltpu.make_async_copy(zeros, xs_ref.at[pl.ds(row0, bm)], zsem)

    @pl.when(i == 0)
    def _():
        zeros[...] = jnp.zeros_like(zeros)

        def fill(e, carry):
            @pl.when(pad0_ref[e] >= 0)
            def _():
                fill_copy(e).start()

            return carry

        lax.fori_loop(0, pad0_ref.shape[0], fill, 0)

        def fill_wait(e, carry):
            @pl.when(pad0_ref[e] >= 0)
            def _():
                fill_copy(e).wait()

            return carry

        lax.fori_loop(0, pad0_ref.shape[0], fill_wait, 0)

    def wait_tile(s):
        def drain(r, carry):
            for _ in range(TOP_K):
                pltpu.make_async_copy(buf.at[s, 0], xs_ref.at[0], sem.at[s]).wait()
            return carry

        lax.fori_loop(0, tm, drain, 0, unroll=4)

    @pl.when(i >= 2)
    def _():
        wait_tile(slot)

    buf[slot] = src_ref[...]

    def issue(r, carry):
        for k in range(TOP_K):
            row = dest_ref[(i * tm + r) * TOP_K + k]
            pltpu.make_async_copy(buf.at[slot, r], xs_ref.at[row], sem.at[slot]).start()
        return carry

    lax.fori_loop(0, tm, issue, 0, unroll=4)

    @pl.when(i == nt - 1)
    def _():
        wait_tile(slot)

        @pl.when(nt >= 2)
        def _():
            wait_tile(1 - slot)


def _dispatch_rows(dest, pad0, src, *, p, bm, tm=256):
    t = src.shape[0]
    slab = src.shape[1:]
    blk = 5 * tm * slab[0] * slab[1] * 4 + bm * slab[0] * slab[1] * 4
    return pl.pallas_call(
        functools.partial(_dispatch_rows_kernel, bm=bm),
        grid_spec=pltpu.PrefetchScalarGridSpec(
            num_scalar_prefetch=2,
            grid=(t // tm,),
            in_specs=[pl.BlockSpec((tm,) + slab, lambda i, dest, pad0: (i, 0, 0))],
            out_specs=pl.BlockSpec(memory_space=pltpu.HBM),
            scratch_shapes=[pltpu.VMEM((2, tm) + slab, src.dtype), pltpu.VMEM((bm,) + slab, src.dtype),
                            pltpu.SemaphoreType.DMA((2,)), pltpu.SemaphoreType.DMA(())],
        ),
        out_shape=jax.ShapeDtypeStruct((p,) + slab, src.dtype),
        compiler_params=_params(("arbitrary",), _vmem_limit(blk)),
        name="dispatch_rows",
    )(dest, pad0, src)


BLOCK_COPY_PRIORITY = 0


def _start_first_blocks(cnt_ref, start_ref, e, in_copy_at):
    n = cnt_ref[e] >> 1
    b = start_ref[e]

    @pl.when(n > 0)
    def _():
        in_copy_at(b, 0).start(priority=BLOCK_COPY_PRIORITY)

    @pl.when(n > 1)
    def _():
        in_copy_at(b + 1, 1).start(priority=BLOCK_COPY_PRIORITY)


def _expert_grid_step(cnt_ref, start_ref, done_ref, in_copy_at, out_copy_at, prologue, compute):
    e = pl.program_id(1)
    ne = pl.num_programs(1)
    step = pl.program_id(0) * ne + e
    last_step = pl.num_programs(0) * ne - 1
    nblk = cnt_ref[e] >> 1
    tail_half = (cnt_ref[e] & 1) == 1
    blk0 = start_ref[e]

    @pl.when(step == 0)
    def _():
        done_ref[0] = 0
        _start_first_blocks(cnt_ref, start_ref, e, in_copy_at)

    done = done_ref[0]

    @pl.when(nblk > 0)
    def _():
        prologue()

        def body(k, carry):
            in_slot = k % 2
            out_slot = (done + k) % 2
            in_copy_at(blk0 + k, in_slot).wait()

            @pl.when(done + k >= 2)
            def _():
                out_copy_at(0, out_slot).wait()

            half = tail_half & (k == nblk - 1)

            @pl.when(jnp.logical_not(half))
            def _():
                compute(in_slot, out_slot, False)

            @pl.when(half)
            def _():
                compute(in_slot, out_slot, True)

            out_copy_at(blk0 + k, out_slot).start(priority=BLOCK_COPY_PRIORITY)

            @pl.when(k + 2 < nblk)
            def _():
                in_copy_at(blk0 + k + 2, in_slot).start(priority=BLOCK_COPY_PRIORITY)

            return carry

        lax.fori_loop(0, nblk, body, 0)
        done_ref[0] = done + nblk

    @pl.when(step < last_step)
    def _():
        _start_first_blocks(cnt_ref, start_ref, (e + 1) % ne, in_copy_at)

    @pl.when(step == last_step)
    def _():
        total = done + nblk

        @pl.when(total >= 2)
        def _():
            out_copy_at(0, total % 2).wait()

        @pl.when(total >= 1)
        def _():
            out_copy_at(0, (total - 1) % 2).wait()


def _moe_up_kernel(cnt_ref, start_ref, xs_ref, w1_ref, w3_ref, h_ref, w1b_ref, w3b_ref, xbuf, obuf, xsem, osem,
                   done_ref, *, bm):
    f = pl.program_id(0)
    d, fc = w1b_ref.shape
    nslab = d // 2 // LANES
    xrows = bm * nslab

    def x_copy(blk, slot):
        row0 = pl.multiple_of(blk * xrows, xrows)
        return pltpu.make_async_copy(xs_ref.at[pl.ds(row0, xrows)], xbuf.at[slot], xsem.at[slot])

    def h_copy(blk, slot):
        row0 = pl.multiple_of(blk * bm, bm)
        col0 = pl.multiple_of(f * fc, fc)
        return pltpu.make_async_copy(obuf.at[slot], h_ref.at[pl.ds(row0, bm), pl.ds(col0, fc)], osem.at[slot])

    def cast_weights():
        w1b_ref[...] = w1_ref[0].astype(BF16)
        w3b_ref[...] = w3_ref[0].astype(BF16)

    def compute(in_slot, out_slot, half):
        rows = bm // 2 if half else bm
        xin = xbuf.at[in_slot, pl.ds(0, rows * nslab)] if half else xbuf.at[in_slot]
        hi, lo = _unpack_halves(_load_row_slabs(xin, nslab, 0, nslab))
        hi = hi.astype(BF16)
        lo = lo.astype(BF16)
        d2 = d // 2
        a = (jnp.dot(hi, w1b_ref[:d2, :], preferred_element_type=F32)
             + jnp.dot(lo, w1b_ref[d2:, :], preferred_element_type=F32))
        c = (jnp.dot(hi, w3b_ref[:d2, :], preferred_element_type=F32)
             + jnp.dot(lo, w3b_ref[d2:, :], preferred_element_type=F32))
        res = (a * jax.nn.sigmoid(a) * c).astype(obuf.dtype)
        if half:
            obuf[out_slot, :rows] = res
            obuf[out_slot, rows:] = jnp.zeros((bm - rows, fc), obuf.dtype)
        else:
            obuf[out_slot] = res

    _expert_grid_step(cnt_ref, start_ref, done_ref, x_copy, h_copy, cast_weights, compute)


def _moe_down_kernel(cnt_ref, start_ref, h_ref, w2_ref, y_ref, w2b_ref, hbuf, obuf, hsem, osem, done_ref, *, bm):
    nslab = w2b_ref.shape[1] // 2 // LANES

    def h_copy(blk, slot):
        row0 = pl.multiple_of(blk * bm, bm)
        return pltpu.make_async_copy(h_ref.at[pl.ds(row0, bm)], hbuf.at[slot], hsem.at[slot])

    def y_copy(blk, slot):
        row0 = pl.multiple_of(blk * bm * nslab, bm * nslab)
        return pltpu.make_async_copy(obuf.at[slot], y_ref.at[pl.ds(row0, bm * nslab)], osem.at[slot])

    def cast_weights():
        w2b_ref[...] = w2_ref[0].astype(BF16)

    def compute(in_slot, out_slot, half):
        rows = bm // 2 if half else bm
        hin = hbuf[in_slot, :rows] if half else hbuf[in_slot]
        packed = _pack_halves(jnp.dot(hin.astype(BF16), w2b_ref[...], preferred_element_type=F32))
        if half:
            _store_row_slabs(obuf.at[out_slot, pl.ds(0, rows * nslab)], nslab, packed)
            obuf[out_slot, rows * nslab:] = jnp.zeros(((bm - rows) * nslab, LANES), obuf.dtype)
        else:
            _store_row_slabs(obuf.at[out_slot], nslab, packed)

    _expert_grid_step(cnt_ref, start_ref, done_ref, h_copy, y_copy, cast_weights, compute)


def _moe_experts(xs, blk_cnt, blk_start, w1, w3, w2, *, bm, fc=512, nc=2048):
    p, nslab, _ = xs.shape
    d = 2 * nslab * LANES
    ne, _, de = w1.shape
    any_spec = pl.BlockSpec(memory_space=pltpu.HBM)
    blk_up = (4 * d * fc * 4 + 2 * d * fc * 2 + 2 * bm * nslab * LANES * 4 + 2 * bm * fc * 4
              + 2 * bm * d * 2 + 3 * bm * fc * 4)
    h = pl.pallas_call(
        functools.partial(_moe_up_kernel, bm=bm),
        grid_spec=pltpu.PrefetchScalarGridSpec(
            num_scalar_prefetch=2,
            grid=(de // fc, ne),
            in_specs=[any_spec,
                      pl.BlockSpec((1, d, fc), lambda f, e, cnt, start: (e, 0, f)),
                      pl.BlockSpec((1, d, fc), lambda f, e, cnt, start: (e, 0, f))],
            out_specs=any_spec,
            scratch_shapes=[pltpu.VMEM((d, fc), BF16), pltpu.VMEM((d, fc), BF16),
                            pltpu.VMEM((2, bm * nslab, LANES), U32), pltpu.VMEM((2, bm, fc), F32),
                            pltpu.SemaphoreType.DMA((2,)), pltpu.SemaphoreType.DMA((2,)),
                            pltpu.SMEM((1,), jnp.int32)],
        ),
        out_shape=jax.ShapeDtypeStruct((p, de), F32),
        compiler_params=_params(("arbitrary", "arbitrary"), _vmem_limit(blk_up)),
        name="moe_up",
    )(blk_cnt, blk_start, xs.reshape(p * nslab, LANES), w1, w3)
    assert nc == d, "expert-down writes whole row slabs"
    blk_dn = 2 * de * nc * 4 + de * nc * 2 + 3 * bm * de * 4 + 2 * bm * nslab * LANES * 4 + 2 * bm * nc * 4
    ys = pl.pallas_call(
        functools.partial(_moe_down_kernel, bm=bm),
        grid_spec=pltpu.PrefetchScalarGridSpec(
            num_scalar_prefetch=2,
            grid=(d // nc, ne),
            in_specs=[any_spec,
                      pl.BlockSpec((1, de, nc), lambda c, e, cnt, start: (e, 0, c))],
            out_specs=any_spec,
            scratch_shapes=[pltpu.VMEM((de, nc), BF16),
                            pltpu.VMEM((2, bm, de), F32), pltpu.VMEM((2, bm * nslab, LANES), U32),
                            pltpu.SemaphoreType.DMA((2,)), pltpu.SemaphoreType.DMA((2,)),
                            pltpu.SMEM((1,), jnp.int32)],
        ),
        out_shape=jax.ShapeDtypeStruct((p * nslab, LANES), U32),
        compiler_params=_params(("arbitrary", "arbitrary"), _vmem_limit(blk_dn)),
        name="moe_down",
    )(blk_cnt, blk_start, h, w2)
    return ys.reshape(p, nslab, LANES)


def _combine_kernel(back_ref, x_ref, ys_ref, ew_ref, g_ref, o_ref, ybuf, sem, *, nc):
    i = pl.program_id(0)
    nt = pl.num_programs(0)
    tm, d = x_ref.shape
    t = nt * tm
    nslab = d // 2 // LANES
    slot = i % 2

    def row_at(s, k):
        def dst_at(r):
            return ybuf.at[s, pl.ds(pl.multiple_of((k * tm + r) * nslab, nslab), nslab)]
        return dst_at

    def start_tile(tile, s):
        for k in range(TOP_K):
            _start_row_copies(back_ref, k * t + tile * tm, tm, ys_ref, row_at(s, k), sem.at[s])

    @pl.when(i == 0)
    def _():
        start_tile(0, 0)

    @pl.when(i + 1 < nt)
    def _():
        start_tile(i + 1, 1 - slot)

    for k in range(TOP_K):
        _wait_row_copies(tm, ys_ref, row_at(slot, k), sem.at[slot])

    ew = ew_ref[...]
    w0 = ew[:, 0:1]
    w1 = ew[:, 1:2]
    half = nc // 2
    y0_ref = ybuf.at[slot, pl.ds(0, tm * nslab)]
    y1_ref = ybuf.at[slot, pl.ds(tm * nslab, tm * nslab)]
    pieces = []
    ss = None
    for c in range(x_ref.shape[1] // nc):
        hi0, lo0 = _unpack_halves(_load_row_slabs(y0_ref, nslab, c * half // LANES, half // LANES))
        hi1, lo1 = _unpack_halves(_load_row_slabs(y1_ref, nslab, c * half // LANES, half // LANES))
        for j, (p0, p1) in enumerate(((hi0, hi1), (lo0, lo1))):
            cols = slice(c * nc + j * half, c * nc + (j + 1) * half)
            z = x_ref[:, cols] + w0 * p0 + w1 * p1
            pieces.append((cols, z))
            s = jnp.sum(z * z, axis=-1, keepdims=True)
            ss = s if ss is None else ss + s
    inv = lax.rsqrt(ss / x_ref.shape[1] + EPS)
    for cols, z in pieces:
        o_ref[:, cols] = z * inv * g_ref[:, cols]


def _combine(x, ys, back, e_w, gain, *, nc, tm=256):
    t, d = x.shape
    nt = t // tm
    nslab = ys.shape[1]
    blk = 2 * (2 * tm * d * 4 + tm * LANES * 4 + d * 4) + 2 * TOP_K * tm * d * 2 + 3 * tm * d * 4
    return pl.pallas_call(
        functools.partial(_combine_kernel, nc=nc),
        grid_spec=pltpu.PrefetchScalarGridSpec(
            num_scalar_prefetch=1,
            grid=(nt,),
            in_specs=[pl.BlockSpec((tm, d), lambda i, back: (i, 0)),
                      pl.BlockSpec(memory_space=pltpu.HBM),
                      pl.BlockSpec((tm, TOP_K), lambda i, back: (i, 0)),
                      pl.BlockSpec((1, d), lambda i, back: (0, 0))],
            out_specs=pl.BlockSpec((tm, d), lambda i, back: (i, 0)),
            scratch_shapes=[pltpu.VMEM((2, TOP_K * tm * nslab, LANES), ys.dtype), pltpu.SemaphoreType.DMA((2,))],
        ),
        out_shape=jax.ShapeDtypeStruct((t, d), F32),
        compiler_params=_params(("arbitrary",), _vmem_limit(blk)),
        name="moe_combine",
    )(back, x, ys, e_w, gain.reshape(1, d))


def _route(logits, *, bm):
    t = logits.shape[0]
    g_logits = logits[:, :N_GROUPS]
    e_logits = logits[:, N_GROUPS:N_GROUPS + N_EXPERTS].reshape(t, N_GROUPS, EXPERTS_PER_GROUP)
    g_idx = jnp.argmax(g_logits, axis=-1).astype(jnp.int32)
    g_w = jnp.take_along_axis(jax.nn.softmax(g_logits, axis=-1), g_idx[:, None], axis=-1)
    e_sel = jnp.take_along_axis(e_logits, g_idx[:, None, None], axis=1)[:, 0]
    top_v, top_i = lax.top_k(e_sel, TOP_K)
    e_w = jax.nn.softmax(top_v, axis=-1) * g_w
    eid = g_idx[:, None] * EXPERTS_PER_GROUP + top_i.astype(jnp.int32)

    a = t * TOP_K
    flat_e = eid.reshape(a)
    onehot = (flat_e[:, None] == jnp.arange(N_EXPERTS, dtype=jnp.int32)[None, :]).astype(jnp.int32)
    csum = jnp.cumsum(onehot, axis=0)
    rank = jnp.take_along_axis(csum, flat_e[:, None], axis=1)[:, 0] - 1
    counts = csum[-1]
    padded = ((counts + bm - 1) // bm) * bm
    pad_end = jnp.cumsum(padded)
    start_pad = pad_end - padded
    dest = (start_pad[flat_e] + rank).astype(jnp.int32)
    tail_half = (counts > 0) & ((counts - 1) % bm < bm // 2)
    blk_cnt = (2 * (padded // bm) + tail_half).astype(jnp.int32)
    blk_start = (start_pad // bm).astype(jnp.int32)
    pad0 = jnp.where(padded > 0, pad_end - bm, -1).astype(jnp.int32)
    back = dest.reshape(t, TOP_K).T.reshape(a)
    return e_w, dest, back, pad0, blk_cnt, blk_start


MOE_BLOCK_ROWS = 256
MOE_DOWN_COLS = 4096


def kernel(x, mem, norm_mix, w_in, pool_w, pool_scale, lb_fwd, lb_bwd, hgrn_norm, w_out, norm_xattn, norm_mem,
           w_q, w_k, w_v, w_o, norm_moe, w_router_group, w_router_expert, w1, w3, w2, norm_final):
    bsz, seq, d = x.shape
    assert bsz == 1 and w_in.shape[0] == 1, "one sequence, one layer (the final norm is fused into the MoE combine)"
    l = 0
    pool_width = pool_w.shape[1] * pool_w.shape[2]
    hgrn_width = hgrn_norm.shape[1]
    xt = x.reshape(seq, d)
    mem_t = mem.reshape(mem.shape[1], d)
    h = _rmsnorm(xt, norm_mix[l], BF16)
    proj = _matmul([h], w_in[l], out_dtype=F32)
    y_pool = _pool_mixer(proj, pool_w[l], pool_scale[l])
    y_hgrn = _hgrn_mixer(proj, lb_fwd, lb_bwd, hgrn_norm[l], layer=l, width=hgrn_width, col0=pool_width)
    xt = _matmul([y_pool, y_hgrn], w_out[l], resid=xt, out_dtype=F32)
    mem_n = _rmsnorm(mem_t, norm_mem[l], BF16)
    hx = _rmsnorm(xt, norm_xattn[l], BF16)
    q = _matmul([hx], w_q[l], out_dtype=BF16)
    k = _matmul([mem_n], w_k[l], out_dtype=BF16)
    v = _matmul([mem_n], w_v[l], out_dtype=BF16)
    o = _xattn(q, k, v)
    xt = _matmul([o], w_o[l], resid=xt, out_dtype=F32)
    w_router = jnp.concatenate(
        [w_router_group[l],
         jnp.transpose(w_router_expert[l], (1, 0, 2)).reshape(d, N_EXPERTS),
         jnp.zeros((d, ROUTER_COLS - N_GROUPS - N_EXPERTS), F32)], axis=1)
    hp, logits = _router(xt, norm_moe[l], w_router)
    e_w, dest, back, pad0, blk_cnt, blk_start = _route(logits, bm=MOE_BLOCK_ROWS)
    xs = _dispatch_rows(dest, pad0, hp, p=TOP_K * seq + N_EXPERTS * MOE_BLOCK_ROWS, bm=MOE_BLOCK_ROWS)
    ys = _moe_experts(xs, blk_cnt, blk_start, w1[l], w3[l], w2[l], bm=MOE_BLOCK_ROWS, nc=MOE_DOWN_COLS)
    out = _combine(xt, ys, back, e_w, norm_final, nc=MOE_DOWN_COLS)
    return out.reshape(bsz, seq, d)
```

```python
import functools

import jax
import jax.numpy as jnp
from jax import lax
from jax.experimental import pallas as pl
from jax.experimental.pallas import tpu as pltpu

F32 = jnp.float32
BF16 = jnp.bfloat16
U32 = jnp.uint32

EPS = 1e-6
LANES = 128
SUBLANES = 8
VMEM_BYTES_V7X = 64 * 1024 * 1024

POOL_WINDOWS = (2, 4, 8, 16)
HEAD = 128
CHUNK = 64
XATTN_HEADS = 4
N_GROUPS = 4
EXPERTS_PER_GROUP = 8
N_EXPERTS = N_GROUPS * EXPERTS_PER_GROUP
TOP_K = 2
ROUTER_COLS = LANES


def _params(sem, vmem_bytes):
    return pltpu.CompilerParams(dimension_semantics=sem, vmem_limit_bytes=int(vmem_bytes))


def _vmem_limit(block_bytes):
    return min(int(block_bytes * 1.25) + (6 << 20), VMEM_BYTES_V7X - (4 << 20))


def _rmsnorm_kernel(x_ref, g_ref, o_ref):
    x = x_ref[...]
    ms = jnp.mean(x * x, axis=-1, keepdims=True)
    o_ref[...] = (x * lax.rsqrt(ms + EPS) * g_ref[...]).astype(o_ref.dtype)


def _rmsnorm(x, gain, out_dtype, tm=512):
    t, d = x.shape
    tm = min(tm, t)
    blk = tm * d * (4 + jnp.dtype(out_dtype).itemsize) * 2
    return pl.pallas_call(
        _rmsnorm_kernel,
        grid=(t // tm,),
        in_specs=[pl.BlockSpec((tm, d), lambda i: (i, 0)),
                  pl.BlockSpec((1, d), lambda i: (0, 0))],
        out_specs=pl.BlockSpec((tm, d), lambda i: (i, 0)),
        out_shape=jax.ShapeDtypeStruct((t, d), out_dtype),
        compiler_params=_params(("arbitrary",), _vmem_limit(blk)),
        name="rmsnorm",
    )(x, gain.reshape(1, d))


def _mm_kernel(*refs, n_a, has_resid):
    a_refs = refs[:n_a]
    w_ref = refs[n_a]
    r_ref = refs[n_a + 1] if has_resid else None
    o_ref, wb_ref = refs[-2], refs[-1]

    @pl.when(pl.program_id(1) == 0)
    def _():
        wb_ref[...] = w_ref[...].astype(BF16)

    acc = None
    k0 = 0
    for a_ref in a_refs:
        kk = a_ref.shape[1]
        part = jnp.dot(a_ref[...], wb_ref[k0:k0 + kk, :], preferred_element_type=F32)
        acc = part if acc is None else acc + part
        k0 += kk
    if has_resid:
        acc = acc + r_ref[...]
    o_ref[...] = acc.astype(o_ref.dtype)


def _matmul(a_parts, w, *, resid=None, out_dtype=F32, tm=1024, tn=512):
    m = a_parts[0].shape[0]
    k, n = w.shape
    assert sum(a.shape[1] for a in a_parts) == k
    tm = min(tm, m)
    in_specs = [pl.BlockSpec((tm, a.shape[1]), lambda j, i: (i, 0)) for a in a_parts]
    in_specs.append(pl.BlockSpec((k, tn), lambda j, i: (0, j)))
    args = list(a_parts) + [w]
    if resid is not None:
        in_specs.append(pl.BlockSpec((tm, tn), lambda j, i: (i, j)))
        args.append(resid)
    osz = jnp.dtype(out_dtype).itemsize
    blk = (2 * tm * k * 2 + 2 * k * tn * 4 + k * tn * 2 + 2 * tm * tn * osz
           + (2 * tm * tn * 4 if resid is not None else 0) + tm * tn * 4)
    return pl.pallas_call(
        functools.partial(_mm_kernel, n_a=len(a_parts), has_resid=resid is not None),
        grid=(n // tn, m // tm),
        in_specs=in_specs,
        out_specs=pl.BlockSpec((tm, tn), lambda j, i: (i, j)),
        out_shape=jax.ShapeDtypeStruct((m, n), out_dtype),
        scratch_shapes=[pltpu.VMEM((k, tn), BF16)],
        compiler_params=_params(("arbitrary", "arbitrary"), _vmem_limit(blk)),
        name="matmul",
    )(*args)


def _pool_kernel(u_ref, up_ref, un_ref, w_ref, s_ref, o_ref, *, tm, seq, group):
    i = pl.program_id(0)
    last = pl.num_programs(0) - 1
    n = tm + 2 * SUBLANES
    row = lax.broadcasted_iota(jnp.int32, (tm, 1), 0) + i * tm
    for gi, w in enumerate(POOL_WINDOWS):
        cs = slice(gi * group, (gi + 1) * group)
        u = u_ref[:, cs]
        prev = jnp.where(i > 0, up_ref[:, cs], 0.0)
        nxt = jnp.where(i < last, un_ref[:, cs], 0.0)
        f = jnp.concatenate([prev, u, nxt], axis=0)
        step = 1
        while step < w:
            f = f + pltpu.roll(f, n - step, 0)
            step *= 2
        half = w // 2
        first = SUBLANES - half
        if first:
            f = pltpu.roll(f, n - first, 0)
        win = f[:tm]
        lo = jnp.maximum(row - half, 0)
        hi = jnp.minimum(row + half - 1, seq - 1)
        cnt = (hi - lo + 1).astype(F32)
        mixed = win / cnt - u
        y = jnp.dot(mixed.astype(BF16), w_ref[gi].astype(BF16), preferred_element_type=F32)
        o_ref[:, cs] = (y * s_ref[:, cs]).astype(o_ref.dtype)


def _pool_mixer(proj, pool_w, pool_scale, *, tm=512):
    t = proj.shape[0]
    ng, group, _ = pool_w.shape
    width = ng * group
    rpb = tm // SUBLANES
    nhalo = t // SUBLANES
    blk = 2 * (tm * width * 4 + 2 * SUBLANES * width * 4 + ng * group * group * 4 + tm * width * 2) + 6 * tm * group * 4
    return pl.pallas_call(
        functools.partial(_pool_kernel, tm=tm, seq=t, group=group),
        grid=(t // tm,),
        in_specs=[pl.BlockSpec((tm, width), lambda i: (i, 0)),
                  pl.BlockSpec((SUBLANES, width), lambda i: (jnp.maximum(i * rpb - 1, 0), 0)),
                  pl.BlockSpec((SUBLANES, width), lambda i: (jnp.minimum((i + 1) * rpb, nhalo - 1), 0)),
                  pl.BlockSpec((ng, group, group), lambda i: (0, 0, 0)),
                  pl.BlockSpec((1, width), lambda i: (0, 0))],
        out_specs=pl.BlockSpec((tm, width), lambda i: (i, 0)),
        out_shape=jax.ShapeDtypeStruct((t, width), BF16),
        compiler_params=_params(("arbitrary",), _vmem_limit(blk)),
        name="pool_mixer",
    )(proj, proj, proj, pool_w, pool_scale.reshape(1, width))


def _chunk_cumsum(x, reverse):
    n = x.shape[0]
    pos = lax.broadcasted_iota(jnp.int32, x.shape, 0) % CHUNK
    s = 1
    while s < CHUNK:
        if reverse:
            x = x + jnp.where(pos < CHUNK - s, pltpu.roll(x, n - s, 0), 0.0)
        else:
            x = x + jnp.where(pos >= s, pltpu.roll(x, s, 0), 0.0)
        s *= 2
    return x


def _lower_bound(lb_raw, layer):
    e = jnp.exp(lb_raw - jnp.max(lb_raw, axis=0, keepdims=True))
    return jnp.sum(e[:layer + 1], axis=0, keepdims=True) / jnp.sum(e, axis=0, keepdims=True)


def _hgrn_direction(q_raw, f_raw, v, lb, st_ref, h, mask, *, reverse):
    tb = q_raw.shape[0]
    nc = tb // CHUNK
    q = q_raw * jax.nn.sigmoid(q_raw)
    f = lb + (1.0 - lb) * jax.nn.sigmoid(f_raw)
    logf = jnp.log(f)
    k = 1.0 - f
    b = _chunk_cumsum(logf, reverse)
    b3 = b.reshape(nc, CHUNK, HEAD)
    edge = 0 if reverse else CHUNK - 1
    b_last = b3[:, edge:edge + 1, :]
    qd = (q * jnp.exp(b)).astype(BF16)
    kd = (k * jnp.exp(-b)).astype(BF16)
    k_end = (k.reshape(nc, CHUNK, HEAD) * jnp.exp(b_last - b3)).astype(BF16)
    decay = jnp.exp(b_last)
    vb = v.astype(BF16)

    a = lax.dot_general(qd, kd, (((1,), (1,)), ((), ())), preferred_element_type=F32)
    a = jnp.where(mask, a, 0.0)
    o_intra = jnp.dot(a.astype(BF16), vb, preferred_element_type=F32)

    st = st_ref[h]
    o_inter = [None] * nc
    order = range(nc - 1, -1, -1) if reverse else range(nc)
    for c in order:
        rows = slice(c * CHUNK, (c + 1) * CHUNK)
        o_inter[c] = lax.dot_general(qd[rows], st.astype(BF16), (((1,), (1,)), ((), ())),
                                     preferred_element_type=F32)
        d_st = lax.dot_general(vb[rows], k_end[c], (((0,), (0,)), ((), ())), preferred_element_type=F32)
        st = st * decay[c] + d_st
    st_ref[h] = st
    return o_intra + jnp.concatenate(o_inter, axis=0)


def _intra_mask(tb, reverse):
    r = lax.broadcasted_iota(jnp.int32, (tb, tb), 0)
    c = lax.broadcasted_iota(jnp.int32, (tb, tb), 1)
    same = (r // CHUNK) == (c // CHUNK)
    return same & ((c >= r) if reverse else (c <= r))


def _hgrn_fwd_kernel(q_ref, f_ref, i_ref, lb_ref, o_ref, st_ref, *, layer, heads):
    @pl.when(pl.program_id(0) == 0)
    def _():
        st_ref[...] = jnp.zeros_like(st_ref)

    mask = _intra_mask(q_ref.shape[0], False)

    for h in range(heads):
        hs = slice(h * HEAD, (h + 1) * HEAD)
        lb = _lower_bound(lb_ref[:, hs], layer)
        o_ref[:, hs] = _hgrn_direction(q_ref[:, hs], f_ref[:, hs], i_ref[:, hs], lb, st_ref, h, mask,
                                       reverse=False)


def _hgrn_bwd_kernel(q_ref, f_ref, i_ref, g_ref, of_ref, lb_ref, gain_ref, y_ref, st_ref, *, layer, heads):
    @pl.when(pl.program_id(0) == 0)
    def _():
        st_ref[...] = jnp.zeros_like(st_ref)

    mask = _intra_mask(q_ref.shape[0], True)

    for h in range(heads):
        hs = slice(h * HEAD, (h + 1) * HEAD)
        lb = _lower_bound(lb_ref[:, hs], layer)
        o = of_ref[:, hs] + _hgrn_direction(q_ref[:, hs], f_ref[:, hs], i_ref[:, hs], lb, st_ref, h, mask,
                                            reverse=True)
        o = o * lax.rsqrt(jnp.mean(o * o, axis=-1, keepdims=True) + EPS) * gain_ref[:, hs]
        g = g_ref[:, hs]
        y_ref[:, hs] = (o * (g * jax.nn.sigmoid(g))).astype(y_ref.dtype)


def _hgrn_mixer(proj, lb_fwd, lb_bwd, hgrn_norm, *, layer, width, col0, tb=256):
    t = proj.shape[0]
    heads = width // HEAD
    nb = t // tb
    c = col0 // width
    layers = lb_fwd.shape[0]
    blk_f = 2 * (3 * tb * width * 4 + layers * width * 4 + tb * width * 4) + heads * HEAD * HEAD * 4
    o_f = pl.pallas_call(
        functools.partial(_hgrn_fwd_kernel, layer=layer, heads=heads),
        grid=(nb,),
        in_specs=[pl.BlockSpec((tb, width), lambda b: (b, c)),
                  pl.BlockSpec((tb, width), lambda b: (b, c + 1)),
                  pl.BlockSpec((tb, width), lambda b: (b, c + 3)),
                  pl.BlockSpec((layers, width), lambda b: (0, 0))],
        out_specs=pl.BlockSpec((tb, width), lambda b: (b, 0)),
        out_shape=jax.ShapeDtypeStruct((t, width), F32),
        scratch_shapes=[pltpu.VMEM((heads, HEAD, HEAD), F32)],
        compiler_params=_params(("arbitrary",), _vmem_limit(blk_f)),
        name="hgrn_fwd",
    )(proj, proj, proj, lb_fwd)
    blk_b = 2 * (5 * tb * width * 4 + layers * width * 4 + width * 4 + tb * width * 2) + heads * HEAD * HEAD * 4
    return pl.pallas_call(
        functools.partial(_hgrn_bwd_kernel, layer=layer, heads=heads),
        grid=(nb,),
        in_specs=[pl.BlockSpec((tb, width), lambda b: (nb - 1 - b, c)),
                  pl.BlockSpec((tb, width), lambda b: (nb - 1 - b, c + 2)),
                  pl.BlockSpec((tb, width), lambda b: (nb - 1 - b, c + 3)),
                  pl.BlockSpec((tb, width), lambda b: (nb - 1 - b, c + 4)),
                  pl.BlockSpec((tb, width), lambda b: (nb - 1 - b, 0)),
                  pl.BlockSpec((layers, width), lambda b: (0, 0)),
                  pl.BlockSpec((1, width), lambda b: (0, 0))],
        out_specs=pl.BlockSpec((tb, width), lambda b: (nb - 1 - b, 0)),
        out_shape=jax.ShapeDtypeStruct((t, width), BF16),
        scratch_shapes=[pltpu.VMEM((heads, HEAD, HEAD), F32)],
        compiler_params=_params(("arbitrary",), _vmem_limit(blk_b)),
        name="hgrn_bwd",
    )(proj, proj, proj, proj, o_f, lb_bwd, hgrn_norm.reshape(1, width))


def _xattn_kernel(q_ref, k_ref, v_ref, o_ref, *, scale):
    s = lax.dot_general(q_ref[...], k_ref[...], (((1,), (1,)), ((), ())), preferred_element_type=F32) * scale
    s = s - jnp.max(s, axis=-1, keepdims=True)
    p = jnp.exp(s)
    p = p / jnp.sum(p, axis=-1, keepdims=True)
    o_ref[...] = jnp.dot(p.astype(BF16), v_ref[...], preferred_element_type=F32).astype(o_ref.dtype)


def _xattn(q, k, v, *, tm=1024):
    t, d = q.shape
    m = k.shape[0]
    dh = d // XATTN_HEADS
    blk = 2 * (2 * tm * dh * 2 + 2 * m * dh * 2) + 4 * tm * m * 4 + tm * dh * 4
    return pl.pallas_call(
        functools.partial(_xattn_kernel, scale=dh ** -0.5),
        grid=(XATTN_HEADS, t // tm),
        in_specs=[pl.BlockSpec((tm, dh), lambda h, i: (i, h)),
                  pl.BlockSpec((m, dh), lambda h, i: (0, h)),
                  pl.BlockSpec((m, dh), lambda h, i: (0, h))],
        out_specs=pl.BlockSpec((tm, dh), lambda h, i: (i, h)),
        out_shape=jax.ShapeDtypeStruct((t, d), BF16),
        compiler_params=_params(("arbitrary", "arbitrary"), _vmem_limit(blk)),
        name="xattn",
    )(q, k, v)


def _pack_halves(x):
    w = x.shape[1] // 2
    hi = lax.bitcast_convert_type(x[:, :w].astype(BF16).astype(F32), U32)
    lo = lax.bitcast_convert_type(x[:, w:].astype(BF16).astype(F32), U32)
    return hi | (lo >> 16)


def _unpack_halves(p):
    hi = lax.bitcast_convert_type(p & jnp.uint32(0xFFFF0000), F32)
    lo = lax.bitcast_convert_type(p << 16, F32)
    return hi, lo


def _load_row_slabs(flat_ref, s, j0, nj):
    rows = flat_ref.shape[0] // s
    return jnp.concatenate([flat_ref[pl.ds(j0 + j, rows, stride=s), :] for j in range(nj)], axis=1)


def _store_row_slabs(flat_ref, s, val):
    rows = flat_ref.shape[0] // s
    for j in range(s):
        flat_ref[pl.ds(j, rows, stride=s), :] = val[:, j * LANES:(j + 1) * LANES]


def _router_kernel(x_ref, g_ref, wr_ref, hp_ref, lg_ref, whi_ref, wlo_ref):
    @pl.when(pl.program_id(0) == 0)
    def _():
        w = wr_ref[...]
        w_hi = w.astype(BF16)
        whi_ref[...] = w_hi
        wlo_ref[...] = (w - w_hi.astype(F32)).astype(BF16)

    x = x_ref[...]
    ms = jnp.mean(x * x, axis=-1, keepdims=True)
    h = x * lax.rsqrt(ms + EPS) * g_ref[...]
    _store_row_slabs(hp_ref, hp_ref.shape[0] // x_ref.shape[0], _pack_halves(h))
    h_hi = h.astype(BF16)
    h_lo = (h - h_hi.astype(F32)).astype(BF16)
    lg_ref[...] = (jnp.dot(h_hi, whi_ref[...], preferred_element_type=F32)
                   + (jnp.dot(h_hi, wlo_ref[...], preferred_element_type=F32)
                      + jnp.dot(h_lo, whi_ref[...], preferred_element_type=F32)))


def _router(x, gain, w_router, *, tm=256):
    t, d = x.shape
    nslab = d // 2 // LANES
    blk = 2 * (tm * d * 4 + d * 4 + d * ROUTER_COLS * 4 + tm * d * 2 + tm * ROUTER_COLS * 4) + 3 * tm * d * 4
    hp, logits = pl.pallas_call(
        _router_kernel,
        grid=(t // tm,),
        in_specs=[pl.BlockSpec((tm, d), lambda i: (i, 0)),
                  pl.BlockSpec((1, d), lambda i: (0, 0)),
                  pl.BlockSpec((d, ROUTER_COLS), lambda i: (0, 0))],
        out_specs=[pl.BlockSpec((tm * nslab, LANES), lambda i: (i, 0)),
                   pl.BlockSpec((tm, ROUTER_COLS), lambda i: (i, 0))],
        out_shape=[jax.ShapeDtypeStruct((t * nslab, LANES), U32),
                   jax.ShapeDtypeStruct((t, ROUTER_COLS), F32)],
        scratch_shapes=[pltpu.VMEM((d, ROUTER_COLS), BF16), pltpu.VMEM((d, ROUTER_COLS), BF16)],
        compiler_params=_params(("arbitrary",), _vmem_limit(blk)),
        name="moe_router",
    )(x, gain.reshape(1, d), w_router)
    return hp.reshape(t, nslab, LANES), logits


def _start_row_copies(idx_ref, base, n, src_ref, dst_at, sem):
    def issue(r, carry):
        pltpu.make_async_copy(src_ref.at[idx_ref[base + r]], dst_at(r), sem).start()
        return carry

    lax.fori_loop(0, n, issue, 0, unroll=8)


def _wait_row_copies(n, src_ref, dst_at, sem):
    def drain(r, carry):
        pltpu.make_async_copy(src_ref.at[0], dst_at(r), sem).wait()
        return carry

    lax.fori_loop(0, n, drain, 0, unroll=8)


def _dispatch_rows_kernel(dest_ref, pad0_ref, src_ref, xs_ref, buf, zeros, sem, zsem, *, bm):
    i = pl.program_id(0)
    nt = pl.num_programs(0)
    tm = src_ref.shape[0]
    slot = i % 2

    def fill_copy(e):
        row0 = pl.multiple_of(jnp.maximum(pad0_ref[e], 0), bm)
        return pltpu.make_async_copy(zeros, xs_ref.at[pl.ds(row0, bm)], zsem)

    @pl.when(i == 0)
    def _():
        zeros[...] = jnp.zeros_like(zeros)

        def fill(e, carry):
            @pl.when(pad0_ref[e] >= 0)
            def _():
                fill_copy(e).start()

            return carry

        lax.fori_loop(0, pad0_ref.shape[0], fill, 0)

        def fill_wait(e, carry):
            @pl.when(pad0_ref[e] >= 0)
            def _():
                fill_copy(e).wait()

            return carry

        lax.fori_loop(0, pad0_ref.shape[0], fill_wait, 0)

    def wait_tile(s):
        def drain(r, carry):
            for _ in range(TOP_K):
                pltpu.make_async_copy(buf.at[s, 0], xs_ref.at[0], sem.at[s]).wait()
            return carry

        lax.fori_loop(0, tm, drain, 0, unroll=4)

    @pl.when(i >= 2)
    def _():
        wait_tile(slot)

    buf[slot] = src_ref[...]

    def issue(r, carry):
        for k in range(TOP_K):
            row = dest_ref[(i * tm + r) * TOP_K + k]
            pltpu.make_async_copy(buf.at[slot, r], xs_ref.at[row], sem.at[slot]).start()
        return carry

    lax.fori_loop(0, tm, issue, 0, unroll=4)

    @pl.when(i == nt - 1)
    def _():
        wait_tile(slot)

        @pl.when(nt >= 2)
        def _():
            wait_tile(1 - slot)


def _dispatch_rows(dest, pad0, src, *, p, bm, tm=256):
    t = src.shape[0]
    slab = src.shape[1:]
    blk = 5 * tm * slab[0] * slab[1] * 4 + bm * slab[0] * slab[1] * 4
    return pl.pallas_call(
        functools.partial(_dispatch_rows_kernel, bm=bm),
        grid_spec=pltpu.PrefetchScalarGridSpec(
            num_scalar_prefetch=2,
            grid=(t // tm,),
            in_specs=[pl.BlockSpec((tm,) + slab, lambda i, dest, pad0: (i, 0, 0))],
            out_specs=pl.BlockSpec(memory_space=pltpu.HBM),
            scratch_shapes=[pltpu.VMEM((2, tm) + slab, src.dtype), pltpu.VMEM((bm,) + slab, src.dtype),
                            pltpu.SemaphoreType.DMA((2,)), pltpu.SemaphoreType.DMA(())],
        ),
        out_shape=jax.ShapeDtypeStruct((p,) + slab, src.dtype),
        compiler_params=_params(("arbitrary",), _vmem_limit(blk)),
        name="dispatch_rows",
    )(dest, pad0, src)


BLOCK_COPY_PRIORITY = 0
UP_GROUP_SLABS = 2


def _start_first_blocks(cnt_ref, start_ref, e, in_copy_at):
    n = cnt_ref[e] >> 1
    b = start_ref[e]

    @pl.when(n > 0)
    def _():
        in_copy_at(b, 0).start(priority=BLOCK_COPY_PRIORITY)

    @pl.when(n > 1)
    def _():
        in_copy_at(b + 1, 1).start(priority=BLOCK_COPY_PRIORITY)


def _expert_grid_step(cnt_ref, start_ref, done_ref, in_copy_at, out_copy_at, prologue, compute):
    e = pl.program_id(1)
    ne = pl.num_programs(1)
    step = pl.program_id(0) * ne + e
    last_step = pl.num_programs(0) * ne - 1
    nblk = cnt_ref[e] >> 1
    tail_half = (cnt_ref[e] & 1) == 1
    blk0 = start_ref[e]

    @pl.when(step == 0)
    def _():
        done_ref[0] = 0
        _start_first_blocks(cnt_ref, start_ref, e, in_copy_at)

    done = done_ref[0]

    @pl.when(nblk > 0)
    def _():
        prologue()

        def body(k, carry):
            in_slot = k % 2
            out_slot = (done + k) % 2
            in_copy_at(blk0 + k, in_slot).wait()

            @pl.when(done + k >= 2)
            def _():
                out_copy_at(0, out_slot).wait()

            half = tail_half & (k == nblk - 1)

            @pl.when(jnp.logical_not(half))
            def _():
                compute(in_slot, out_slot, False)

            @pl.when(half)
            def _():
                compute(in_slot, out_slot, True)

            out_copy_at(blk0 + k, out_slot).start(priority=BLOCK_COPY_PRIORITY)

            @pl.when(k + 2 < nblk)
            def _():
                in_copy_at(blk0 + k + 2, in_slot).start(priority=BLOCK_COPY_PRIORITY)

            return carry

        lax.fori_loop(0, nblk, body, 0)
        done_ref[0] = done + nblk

    @pl.when(step < last_step)
    def _():
        _start_first_blocks(cnt_ref, start_ref, (e + 1) % ne, in_copy_at)

    @pl.when(step == last_step)
    def _():
        total = done + nblk

        @pl.when(total >= 2)
        def _():
            out_copy_at(0, total % 2).wait()

        @pl.when(total >= 1)
        def _():
            out_copy_at(0, (total - 1) % 2).wait()


def _moe_up_kernel(cnt_ref, start_ref, xs_ref, w1_ref, w3_ref, h_ref, w1b_ref, w3b_ref, xbuf, obuf, xsem, osem,
                   done_ref, *, bm):
    f = pl.program_id(0)
    d, fc = w1b_ref.shape
    nslab = d // 2 // LANES
    xrows = bm * nslab

    def x_copy(blk, slot):
        row0 = pl.multiple_of(blk * xrows, xrows)
        return pltpu.make_async_copy(xs_ref.at[pl.ds(row0, xrows)], xbuf.at[slot], xsem.at[slot])

    def h_copy(blk, slot):
        row0 = pl.multiple_of(blk * bm, bm)
        col0 = pl.multiple_of(f * fc, fc)
        return pltpu.make_async_copy(obuf.at[slot], h_ref.at[pl.ds(row0, bm), pl.ds(col0, fc)], osem.at[slot])

    def cast_weights():
        w1b_ref[...] = w1_ref[0].astype(BF16)
        w3b_ref[...] = w3_ref[0].astype(BF16)

    def compute(in_slot, out_slot, half):
        rows = bm // 2 if half else bm
        xin = xbuf.at[in_slot, pl.ds(0, rows * nslab)] if half else xbuf.at[in_slot]
        d2 = d // 2
        a = c = None
        for j0 in range(0, nslab, UP_GROUP_SLABS):
            hi, lo = _unpack_halves(_load_row_slabs(xin, nslab, j0, UP_GROUP_SLABS))
            hi = hi.astype(BF16)
            lo = lo.astype(BF16)
            k0, k1 = j0 * LANES, (j0 + UP_GROUP_SLABS) * LANES
            pa = (jnp.dot(hi, w1b_ref[k0:k1, :], preferred_element_type=F32)
                  + jnp.dot(lo, w1b_ref[d2 + k0:d2 + k1, :], preferred_element_type=F32))
            pc = (jnp.dot(hi, w3b_ref[k0:k1, :], preferred_element_type=F32)
                  + jnp.dot(lo, w3b_ref[d2 + k0:d2 + k1, :], preferred_element_type=F32))
            a = pa if a is None else a + pa
            c = pc if c is None else c + pc
        res = (a * jax.nn.sigmoid(a) * c).astype(obuf.dtype)
        if half:
            obuf[out_slot, :rows] = res
            obuf[out_slot, rows:] = jnp.zeros((bm - rows, fc), obuf.dtype)
        else:
            obuf[out_slot] = res

    _expert_grid_step(cnt_ref, start_ref, done_ref, x_copy, h_copy, cast_weights, compute)


def _moe_down_kernel(cnt_ref, start_ref, h_ref, w2_ref, y_ref, w2b_ref, hbuf, obuf, hsem, osem, done_ref, *, bm):
    nslab = w2b_ref.shape[1] // 2 // LANES

    def h_copy(blk, slot):
        row0 = pl.multiple_of(blk * bm, bm)
        return pltpu.make_async_copy(h_ref.at[pl.ds(row0, bm)], hbuf.at[slot], hsem.at[slot])

    def y_copy(blk, slot):
        row0 = pl.multiple_of(blk * bm * nslab, bm * nslab)
        return pltpu.make_async_copy(obuf.at[slot], y_ref.at[pl.ds(row0, bm * nslab)], osem.at[slot])

    def cast_weights():
        w2b_ref[...] = w2_ref[0].astype(BF16)

    def compute(in_slot, out_slot, half):
        rows = bm // 2 if half else bm
        hin = hbuf[in_slot, :rows] if half else hbuf[in_slot]
        packed = _pack_halves(jnp.dot(hin.astype(BF16), w2b_ref[...], preferred_element_type=F32))
        if half:
            _store_row_slabs(obuf.at[out_slot, pl.ds(0, rows * nslab)], nslab, packed)
            obuf[out_slot, rows * nslab:] = jnp.zeros(((bm - rows) * nslab, LANES), obuf.dtype)
        else:
            _store_row_slabs(obuf.at[out_slot], nslab, packed)

    _expert_grid_step(cnt_ref, start_ref, done_ref, h_copy, y_copy, cast_weights, compute)


def _moe_experts(xs, blk_cnt, blk_start, w1, w3, w2, *, bm, fc=512, nc=2048):
    p, nslab, _ = xs.shape
    d = 2 * nslab * LANES
    ne, _, de = w1.shape
    any_spec = pl.BlockSpec(memory_space=pltpu.HBM)
    blk_up = (4 * d * fc * 4 + 2 * d * fc * 2 + 2 * bm * nslab * LANES * 4 + 2 * bm * fc * 4
              + 2 * bm * d * 2 + 3 * bm * fc * 4)
    h = pl.pallas_call(
        functools.partial(_moe_up_kernel, bm=bm),
        grid_spec=pltpu.PrefetchScalarGridSpec(
            num_scalar_prefetch=2,
            grid=(de // fc, ne),
            in_specs=[any_spec,
                      pl.BlockSpec((1, d, fc), lambda f, e, cnt, start: (e, 0, f)),
                      pl.BlockSpec((1, d, fc), lambda f, e, cnt, start: (e, 0, f))],
            out_specs=any_spec,
            scratch_shapes=[pltpu.VMEM((d, fc), BF16), pltpu.VMEM((d, fc), BF16),
                            pltpu.VMEM((2, bm * nslab, LANES), U32), pltpu.VMEM((2, bm, fc), F32),
                            pltpu.SemaphoreType.DMA((2,)), pltpu.SemaphoreType.DMA((2,)),
                            pltpu.SMEM((1,), jnp.int32)],
        ),
        out_shape=jax.ShapeDtypeStruct((p, de), F32),
        compiler_params=_params(("arbitrary", "arbitrary"), _vmem_limit(blk_up)),
        name="moe_up",
    )(blk_cnt, blk_start, xs.reshape(p * nslab, LANES), w1, w3)
    assert nc == d, "expert-down writes whole row slabs"
    blk_dn = 2 * de * nc * 4 + de * nc * 2 + 3 * bm * de * 4 + 2 * bm * nslab * LANES * 4 + 2 * bm * nc * 4
    ys = pl.pallas_call(
        functools.partial(_moe_down_kernel, bm=bm),
        grid_spec=pltpu.PrefetchScalarGridSpec(
            num_scalar_prefetch=2,
            grid=(d // nc, ne),
            in_specs=[any_spec,
                      pl.BlockSpec((1, de, nc), lambda c, e, cnt, start: (e, 0, c))],
            out_specs=any_spec,
            scratch_shapes=[pltpu.VMEM((de, nc), BF16),
                            pltpu.VMEM((2, bm, de), F32), pltpu.VMEM((2, bm * nslab, LANES), U32),
                            pltpu.SemaphoreType.DMA((2,)), pltpu.SemaphoreType.DMA((2,)),
                            pltpu.SMEM((1,), jnp.int32)],
        ),
        out_shape=jax.ShapeDtypeStruct((p * nslab, LANES), U32),
        compiler_params=_params(("arbitrary", "arbitrary"), _vmem_limit(blk_dn)),
        name="moe_down",
    )(blk_cnt, blk_start, h, w2)
    return ys.reshape(p, nslab, LANES)


def _combine_kernel(back_ref, x_ref, ys_ref, ew_ref, g_ref, o_ref, ybuf, sem, *, nc):
    i = pl.program_id(0)
    nt = pl.num_programs(0)
    tm, d = x_ref.shape
    t = nt * tm
    nslab = d // 2 // LANES
    slot = i % 2

    def row_at(s, k):
        def dst_at(r):
            return ybuf.at[s, pl.ds(pl.multiple_of((k * tm + r) * nslab, nslab), nslab)]
        return dst_at

    def start_tile(tile, s):
        for k in range(TOP_K):
            _start_row_copies(back_ref, k * t + tile * tm, tm, ys_ref, row_at(s, k), sem.at[s])

    @pl.when(i == 0)
    def _():
        start_tile(0, 0)

    @pl.when(i + 1 < nt)
    def _():
        start_tile(i + 1, 1 - slot)

    for k in range(TOP_K):
        _wait_row_copies(tm, ys_ref, row_at(slot, k), sem.at[slot])

    ew = ew_ref[...]
    w0 = ew[:, 0:1]
    w1 = ew[:, 1:2]
    half = nc // 2
    y0_ref = ybuf.at[slot, pl.ds(0, tm * nslab)]
    y1_ref = ybuf.at[slot, pl.ds(tm * nslab, tm * nslab)]
    pieces = []
    ss = None
    for c in range(x_ref.shape[1] // nc):
        hi0, lo0 = _unpack_halves(_load_row_slabs(y0_ref, nslab, c * half // LANES, half // LANES))
        hi1, lo1 = _unpack_halves(_load_row_slabs(y1_ref, nslab, c * half // LANES, half // LANES))
        for j, (p0, p1) in enumerate(((hi0, hi1), (lo0, lo1))):
            cols = slice(c * nc + j * half, c * nc + (j + 1) * half)
            z = x_ref[:, cols] + w0 * p0 + w1 * p1
            pieces.append((cols, z))
            s = jnp.sum(z * z, axis=-1, keepdims=True)
            ss = s if ss is None else ss + s
    inv = lax.rsqrt(ss / x_ref.shape[1] + EPS)
    for cols, z in pieces:
        o_ref[:, cols] = z * inv * g_ref[:, cols]


def _combine(x, ys, back, e_w, gain, *, nc, tm=256):
    t, d = x.shape
    nt = t // tm
    nslab = ys.shape[1]
    blk = 2 * (2 * tm * d * 4 + tm * LANES * 4 + d * 4) + 2 * TOP_K * tm * d * 2 + 3 * tm * d * 4
    return pl.pallas_call(
        functools.partial(_combine_kernel, nc=nc),
        grid_spec=pltpu.PrefetchScalarGridSpec(
            num_scalar_prefetch=1,
            grid=(nt,),
            in_specs=[pl.BlockSpec((tm, d), lambda i, back: (i, 0)),
                      pl.BlockSpec(memory_space=pltpu.HBM),
                      pl.BlockSpec((tm, TOP_K), lambda i, back: (i, 0)),
                      pl.BlockSpec((1, d), lambda i, back: (0, 0))],
            out_specs=pl.BlockSpec((tm, d), lambda i, back: (i, 0)),
            scratch_shapes=[pltpu.VMEM((2, TOP_K * tm * nslab, LANES), ys.dtype), pltpu.SemaphoreType.DMA((2,))],
        ),
        out_shape=jax.ShapeDtypeStruct((t, d), F32),
        compiler_params=_params(("arbitrary",), _vmem_limit(blk)),
        name="moe_combine",
    )(back, x, ys, e_w, gain.reshape(1, d))


def _route(logits, *, bm):
    t = logits.shape[0]
    g_logits = logits[:, :N_GROUPS]
    e_logits = logits[:, N_GROUPS:N_GROUPS + N_EXPERTS].reshape(t, N_GROUPS, EXPERTS_PER_GROUP)
    g_idx = jnp.argmax(g_logits, axis=-1).astype(jnp.int32)
    g_w = jnp.take_along_axis(jax.nn.softmax(g_logits, axis=-1), g_idx[:, None], axis=-1)
    e_sel = jnp.take_along_axis(e_logits, g_idx[:, None, None], axis=1)[:, 0]
    top_v, top_i = lax.top_k(e_sel, TOP_K)
    e_w = jax.nn.softmax(top_v, axis=-1) * g_w
    eid = g_idx[:, None] * EXPERTS_PER_GROUP + top_i.astype(jnp.int32)

    a = t * TOP_K
    flat_e = eid.reshape(a)
    onehot = (flat_e[:, None] == jnp.arange(N_EXPERTS, dtype=jnp.int32)[None, :]).astype(jnp.int32)
    csum = jnp.cumsum(onehot, axis=0)
    rank = jnp.take_along_axis(csum, flat_e[:, None], axis=1)[:, 0] - 1
    counts = csum[-1]
    padded = ((counts + bm - 1) // bm) * bm
    pad_end = jnp.cumsum(padded)
    start_pad = pad_end - padded
    dest = (start_pad[flat_e] + rank).astype(jnp.int32)
    tail_half = (counts > 0) & ((counts - 1) % bm < bm // 2)
    blk_cnt = (2 * (padded // bm) + tail_half).astype(jnp.int32)
    blk_start = (start_pad // bm).astype(jnp.int32)
    pad0 = jnp.where(padded > 0, pad_end - bm, -1).astype(jnp.int32)
    back = dest.reshape(t, TOP_K).T.reshape(a)
    return e_w, dest, back, pad0, blk_cnt, blk_start


MOE_BLOCK_ROWS = 256
MOE_DOWN_COLS = 4096


def kernel(x, mem, norm_mix, w_in, pool_w, pool_scale, lb_fwd, lb_bwd, hgrn_norm, w_out, norm_xattn, norm_mem,
           w_q, w_k, w_v, w_o, norm_moe, w_router_group, w_router_expert, w1, w3, w2, norm_final):
    bsz, seq, d = x.shape
    assert bsz == 1 and w_in.shape[0] == 1, "one sequence, one layer (the final norm is fused into the MoE combine)"
    l = 0
    pool_width = pool_w.shape[1] * pool_w.shape[2]
    hgrn_width = hgrn_norm.shape[1]
    xt = x.reshape(seq, d)
    mem_t = mem.reshape(mem.shape[1], d)
    h = _rmsnorm(xt, norm_mix[l], BF16)
    proj = _matmul([h], w_in[l], out_dtype=F32)
    y_pool = _pool_mixer(proj, pool_w[l], pool_scale[l])
    y_hgrn = _hgrn_mixer(proj, lb_fwd, lb_bwd, hgrn_norm[l], layer=l, width=hgrn_width, col0=pool_width)
    xt = _matmul([y_pool, y_hgrn], w_out[l], resid=xt, out_dtype=F32)
    mem_n = _rmsnorm(mem_t, norm_mem[l], BF16)
    hx = _rmsnorm(xt, norm_xattn[l], BF16)
    q = _matmul([hx], w_q[l], out_dtype=BF16)
    k = _matmul([mem_n], w_k[l], out_dtype=BF16)
    v = _matmul([mem_n], w_v[l], out_dtype=BF16)
    o = _xattn(q, k, v)
    xt = _matmul([o], w_o[l], resid=xt, out_dtype=F32)
    w_router = jnp.concatenate(
        [w_router_group[l],
         jnp.transpose(w_router_expert[l], (1, 0, 2)).reshape(d, N_EXPERTS),
         jnp.zeros((d, ROUTER_COLS - N_GROUPS - N_EXPERTS), F32)], axis=1)
    hp, logits = _router(xt, norm_moe[l], w_router)
    e_w, dest, back, pad0, blk_cnt, blk_start = _route(logits, bm=MOE_BLOCK_ROWS)
    xs = _dispatch_rows(dest, pad0, hp, p=TOP_K * seq + N_EXPERTS * MOE_BLOCK_ROWS, bm=MOE_BLOCK_ROWS)
    ys = _moe_experts(xs, blk_cnt, blk_start, w1[l], w3[l], w2[l], bm=MOE_BLOCK_ROWS, nc=MOE_DOWN_COLS)
    out = _combine(xt, ys, back, e_w, norm_final, nc=MOE_DOWN_COLS)
    return out.reshape(bsz, seq, d)
```

```python
import functools

import jax
import jax.numpy as jnp
from jax import lax
from jax.experimental import pallas as pl
from jax.experimental.pallas import tpu as pltpu

F32 = jnp.float32
BF16 = jnp.bfloat16
U32 = jnp.uint32

EPS = 1e-6
LANES = 128
SUBLANES = 8
VMEM_BYTES_V7X = 64 * 1024 * 1024

POOL_WINDOWS = (2, 4, 8, 16)
HEAD = 128
CHUNK = 64
XATTN_HEADS = 4
N_GROUPS = 4
EXPERTS_PER_GROUP = 8
N_EXPERTS = N_GROUPS * EXPERTS_PER_GROUP
TOP_K = 2
ROUTER_COLS = LANES


def _params(sem, vmem_bytes):
    return pltpu.CompilerParams(dimension_semantics=sem, vmem_limit_bytes=int(vmem_bytes))


def _vmem_limit(block_bytes):
    return min(int(block_bytes * 1.25) + (6 << 20), VMEM_BYTES_V7X - (4 << 20))


def _rmsnorm_kernel(x_ref, g_ref, o_ref):
    x = x_ref[...]
    ms = jnp.mean(x * x, axis=-1, keepdims=True)
    o_ref[...] = (x * lax.rsqrt(ms + EPS) * g_ref[...]).astype(o_ref.dtype)


def _rmsnorm(x, gain, out_dtype, tm=512):
    t, d = x.shape
    tm = min(tm, t)
    blk = tm * d * (4 + jnp.dtype(out_dtype).itemsize) * 2
    return pl.pallas_call(
        _rmsnorm_kernel,
        grid=(t // tm,),
        in_specs=[pl.BlockSpec((tm, d), lambda i: (i, 0)),
                  pl.BlockSpec((1, d), lambda i: (0, 0))],
        out_specs=pl.BlockSpec((tm, d), lambda i: (i, 0)),
        out_shape=jax.ShapeDtypeStruct((t, d), out_dtype),
        compiler_params=_params(("arbitrary",), _vmem_limit(blk)),
        name="rmsnorm",
    )(x, gain.reshape(1, d))


def _mm_kernel(*refs, n_a, has_resid):
    a_refs = refs[:n_a]
    w_ref = refs[n_a]
    r_ref = refs[n_a + 1] if has_resid else None
    o_ref, wb_ref = refs[-2], refs[-1]

    @pl.when(pl.program_id(1) == 0)
    def _():
        wb_ref[...] = w_ref[...].astype(BF16)

    acc = None
    k0 = 0
    for a_ref in a_refs:
        kk = a_ref.shape[1]
        part = jnp.dot(a_ref[...], wb_ref[k0:k0 + kk, :], preferred_element_type=F32)
        acc = part if acc is None else acc + part
        k0 += kk
    if has_resid:
        acc = acc + r_ref[...]
    o_ref[...] = acc.astype(o_ref.dtype)


def _matmul(a_parts, w, *, resid=None, out_dtype=F32, tm=1024, tn=512):
    m = a_parts[0].shape[0]
    k, n = w.shape
    assert sum(a.shape[1] for a in a_parts) == k
    tm = min(tm, m)
    in_specs = [pl.BlockSpec((tm, a.shape[1]), lambda j, i: (i, 0)) for a in a_parts]
    in_specs.append(pl.BlockSpec((k, tn), lambda j, i: (0, j)))
    args = list(a_parts) + [w]
    if resid is not None:
        in_specs.append(pl.BlockSpec((tm, tn), lambda j, i: (i, j)))
        args.append(resid)
    osz = jnp.dtype(out_dtype).itemsize
    blk = (2 * tm * k * 2 + 2 * k * tn * 4 + k * tn * 2 + 2 * tm * tn * osz
           + (2 * tm * tn * 4 if resid is not None else 0) + tm * tn * 4)
    return pl.pallas_call(
        functools.partial(_mm_kernel, n_a=len(a_parts), has_resid=resid is not None),
        grid=(n // tn, m // tm),
        in_specs=in_specs,
        out_specs=pl.BlockSpec((tm, tn), lambda j, i: (i, j)),
        out_shape=jax.ShapeDtypeStruct((m, n), out_dtype),
        scratch_shapes=[pltpu.VMEM((k, tn), BF16)],
        compiler_params=_params(("arbitrary", "arbitrary"), _vmem_limit(blk)),
        name="matmul",
    )(*args)


def _pool_kernel(u_ref, up_ref, un_ref, w_ref, s_ref, o_ref, *, tm, seq, group):
    i = pl.program_id(0)
    last = pl.num_programs(0) - 1
    n = tm + 2 * SUBLANES
    row = lax.broadcasted_iota(jnp.int32, (tm, 1), 0) + i * tm
    for gi, w in enumerate(POOL_WINDOWS):
        cs = slice(gi * group, (gi + 1) * group)
        u = u_ref[:, cs]
        prev = jnp.where(i > 0, up_ref[:, cs], 0.0)
        nxt = jnp.where(i < last, un_ref[:, cs], 0.0)
        f = jnp.concatenate([prev, u, nxt], axis=0)
        step = 1
        while step < w:
            f = f + pltpu.roll(f, n - step, 0)
            step *= 2
        half = w // 2
        first = SUBLANES - half
        if first:
            f = pltpu.roll(f, n - first, 0)
        win = f[:tm]
        lo = jnp.maximum(row - half, 0)
        hi = jnp.minimum(row + half - 1, seq - 1)
        cnt = (hi - lo + 1).astype(F32)
        mixed = win / cnt - u
        y = jnp.dot(mixed.astype(BF16), w_ref[gi].astype(BF16), preferred_element_type=F32)
        o_ref[:, cs] = (y * s_ref[:, cs]).astype(o_ref.dtype)


def _pool_mixer(proj, pool_w, pool_scale, *, tm=512):
    t = proj.shape[0]
    ng, group, _ = pool_w.shape
    width = ng * group
    rpb = tm // SUBLANES
    nhalo = t // SUBLANES
    blk = 2 * (tm * width * 4 + 2 * SUBLANES * width * 4 + ng * group * group * 4 + tm * width * 2) + 6 * tm * group * 4
    return pl.pallas_call(
        functools.partial(_pool_kernel, tm=tm, seq=t, group=group),
        grid=(t // tm,),
        in_specs=[pl.BlockSpec((tm, width), lambda i: (i, 0)),
                  pl.BlockSpec((SUBLANES, width), lambda i: (jnp.maximum(i * rpb - 1, 0), 0)),
                  pl.BlockSpec((SUBLANES, width), lambda i: (jnp.minimum((i + 1) * rpb, nhalo - 1), 0)),
                  pl.BlockSpec((ng, group, group), lambda i: (0, 0, 0)),
                  pl.BlockSpec((1, width), lambda i: (0, 0))],
        out_specs=pl.BlockSpec((tm, width), lambda i: (i, 0)),
        out_shape=jax.ShapeDtypeStruct((t, width), BF16),
        compiler_params=_params(("arbitrary",), _vmem_limit(blk)),
        name="pool_mixer",
    )(proj, proj, proj, pool_w, pool_scale.reshape(1, width))


def _chunk_cumsum(x, reverse):
    n = x.shape[0]
    pos = lax.broadcasted_iota(jnp.int32, x.shape, 0) % CHUNK
    s = 1
    while s < CHUNK:
        if reverse:
            x = x + jnp.where(pos < CHUNK - s, pltpu.roll(x, n - s, 0), 0.0)
        else:
            x = x + jnp.where(pos >= s, pltpu.roll(x, s, 0), 0.0)
        s *= 2
    return x


def _lower_bound(lb_raw, layer):
    e = jnp.exp(lb_raw - jnp.max(lb_raw, axis=0, keepdims=True))
    return jnp.sum(e[:layer + 1], axis=0, keepdims=True) / jnp.sum(e, axis=0, keepdims=True)


def _hgrn_direction(q_raw, f_raw, v, lb, st_ref, h, mask, *, reverse):
    tb = q_raw.shape[0]
    nc = tb // CHUNK
    q = q_raw * jax.nn.sigmoid(q_raw)
    f = lb + (1.0 - lb) * jax.nn.sigmoid(f_raw)
    logf = jnp.log(f)
    k = 1.0 - f
    b = _chunk_cumsum(logf, reverse)
    b3 = b.reshape(nc, CHUNK, HEAD)
    edge = 0 if reverse else CHUNK - 1
    b_last = b3[:, edge:edge + 1, :]
    qd = (q * jnp.exp(b)).astype(BF16)
    kd = (k * jnp.exp(-b)).astype(BF16)
    k_end = (k.reshape(nc, CHUNK, HEAD) * jnp.exp(b_last - b3)).astype(BF16)
    decay = jnp.exp(b_last)
    vb = v.astype(BF16)

    a = lax.dot_general(qd, kd, (((1,), (1,)), ((), ())), preferred_element_type=F32)
    a = jnp.where(mask, a, 0.0)
    o_intra = jnp.dot(a.astype(BF16), vb, preferred_element_type=F32)

    st = st_ref[h]
    o_inter = [None] * nc
    order = range(nc - 1, -1, -1) if reverse else range(nc)
    for c in order:
        rows = slice(c * CHUNK, (c + 1) * CHUNK)
        o_inter[c] = lax.dot_general(qd[rows], st.astype(BF16), (((1,), (1,)), ((), ())),
                                     preferred_element_type=F32)
        d_st = lax.dot_general(vb[rows], k_end[c], (((0,), (0,)), ((), ())), preferred_element_type=F32)
        st = st * decay[c] + d_st
    st_ref[h] = st
    return o_intra + jnp.concatenate(o_inter, axis=0)


def _intra_mask(tb, reverse):
    r = lax.broadcasted_iota(jnp.int32, (tb, tb), 0)
    c = lax.broadcasted_iota(jnp.int32, (tb, tb), 1)
    same = (r // CHUNK) == (c // CHUNK)
    return same & ((c >= r) if reverse else (c <= r))


def _hgrn_fwd_kernel(q_ref, f_ref, i_ref, lb_ref, o_ref, st_ref, *, layer, heads):
    @pl.when(pl.program_id(0) == 0)
    def _():
        st_ref[...] = jnp.zeros_like(st_ref)

    mask = _intra_mask(q_ref.shape[0], False)

    for h in range(heads):
        hs = slice(h * HEAD, (h + 1) * HEAD)
        lb = _lower_bound(lb_ref[:, hs], layer)
        o_ref[:, hs] = _hgrn_direction(q_ref[:, hs], f_ref[:, hs], i_ref[:, hs], lb, st_ref, h, mask,
                                       reverse=False)


def _hgrn_bwd_kernel(q_ref, f_ref, i_ref, g_ref, of_ref, lb_ref, gain_ref, y_ref, st_ref, *, layer, heads):
    @pl.when(pl.program_id(0) == 0)
    def _():
        st_ref[...] = jnp.zeros_like(st_ref)

    mask = _intra_mask(q_ref.shape[0], True)

    for h in range(heads):
        hs = slice(h * HEAD, (h + 1) * HEAD)
        lb = _lower_bound(lb_ref[:, hs], layer)
        o = of_ref[:, hs] + _hgrn_direction(q_ref[:, hs], f_ref[:, hs], i_ref[:, hs], lb, st_ref, h, mask,
                                            reverse=True)
        o = o * lax.rsqrt(jnp.mean(o * o, axis=-1, keepdims=True) + EPS) * gain_ref[:, hs]
        g = g_ref[:, hs]
        y_ref[:, hs] = (o * (g * jax.nn.sigmoid(g))).astype(y_ref.dtype)


def _hgrn_mixer(proj, lb_fwd, lb_bwd, hgrn_norm, *, layer, width, col0, tb=256):
    t = proj.shape[0]
    heads = width // HEAD
    nb = t // tb
    c = col0 // width
    layers = lb_fwd.shape[0]
    blk_f = 2 * (3 * tb * width * 4 + layers * width * 4 + tb * width * 4) + heads * HEAD * HEAD * 4
    o_f = pl.pallas_call(
        functools.partial(_hgrn_fwd_kernel, layer=layer, heads=heads),
        grid=(nb,),
        in_specs=[pl.BlockSpec((tb, width), lambda b: (b, c)),
                  pl.BlockSpec((tb, width), lambda b: (b, c + 1)),
                  pl.BlockSpec((tb, width), lambda b: (b, c + 3)),
                  pl.BlockSpec((layers, width), lambda b: (0, 0))],
        out_specs=pl.BlockSpec((tb, width), lambda b: (b, 0)),
        out_shape=jax.ShapeDtypeStruct((t, width), F32),
        scratch_shapes=[pltpu.VMEM((heads, HEAD, HEAD), F32)],
        compiler_params=_params(("arbitrary",), _vmem_limit(blk_f)),
        name="hgrn_fwd",
    )(proj, proj, proj, lb_fwd)
    blk_b = 2 * (5 * tb * width * 4 + layers * width * 4 + width * 4 + tb * width * 2) + heads * HEAD * HEAD * 4
    return pl.pallas_call(
        functools.partial(_hgrn_bwd_kernel, layer=layer, heads=heads),
        grid=(nb,),
        in_specs=[pl.BlockSpec((tb, width), lambda b: (nb - 1 - b, c)),
                  pl.BlockSpec((tb, width), lambda b: (nb - 1 - b, c + 2)),
                  pl.BlockSpec((tb, width), lambda b: (nb - 1 - b, c + 3)),
                  pl.BlockSpec((tb, width), lambda b: (nb - 1 - b, c + 4)),
                  pl.BlockSpec((tb, width), lambda b: (nb - 1 - b, 0)),
                  pl.BlockSpec((layers, width), lambda b: (0, 0)),
                  pl.BlockSpec((1, width), lambda b: (0, 0))],
        out_specs=pl.BlockSpec((tb, width), lambda b: (nb - 1 - b, 0)),
        out_shape=jax.ShapeDtypeStruct((t, width), BF16),
        scratch_shapes=[pltpu.VMEM((heads, HEAD, HEAD), F32)],
        compiler_params=_params(("arbitrary",), _vmem_limit(blk_b)),
        name="hgrn_bwd",
    )(proj, proj, proj, proj, o_f, lb_bwd, hgrn_norm.reshape(1, width))


def _xattn_kernel(q_ref, k_ref, v_ref, o_ref, *, scale):
    s = lax.dot_general(q_ref[...], k_ref[...], (((1,), (1,)), ((), ())), preferred_element_type=F32) * scale
    s = s - jnp.max(s, axis=-1, keepdims=True)
    p = jnp.exp(s)
    p = p / jnp.sum(p, axis=-1, keepdims=True)
    o_ref[...] = jnp.dot(p.astype(BF16), v_ref[...], preferred_element_type=F32).astype(o_ref.dtype)


def _xattn(q, k, v, *, tm=1024):
    t, d = q.shape
    m = k.shape[0]
    dh = d // XATTN_HEADS
    blk = 2 * (2 * tm * dh * 2 + 2 * m * dh * 2) + 4 * tm * m * 4 + tm * dh * 4
    return pl.pallas_call(
        functools.partial(_xattn_kernel, scale=dh ** -0.5),
        grid=(XATTN_HEADS, t // tm),
        in_specs=[pl.BlockSpec((tm, dh), lambda h, i: (i, h)),
                  pl.BlockSpec((m, dh), lambda h, i: (0, h)),
                  pl.BlockSpec((m, dh), lambda h, i: (0, h))],
        out_specs=pl.BlockSpec((tm, dh), lambda h, i: (i, h)),
        out_shape=jax.ShapeDtypeStruct((t, d), BF16),
        compiler_params=_params(("arbitrary", "arbitrary"), _vmem_limit(blk)),
        name="xattn",
    )(q, k, v)


def _pack_halves(x):
    w = x.shape[1] // 2
    hi = lax.bitcast_convert_type(x[:, :w].astype(BF16).astype(F32), U32)
    lo = lax.bitcast_convert_type(x[:, w:].astype(BF16).astype(F32), U32)
    return hi | (lo >> 16)


def _unpack_halves(p):
    hi = lax.bitcast_convert_type(p & jnp.uint32(0xFFFF0000), F32)
    lo = lax.bitcast_convert_type(p << 16, F32)
    return hi, lo


def _load_row_slabs(flat_ref, s, j0, nj):
    rows = flat_ref.shape[0] // s
    return jnp.concatenate([flat_ref[pl.ds(j0 + j, rows, stride=s), :] for j in range(nj)], axis=1)


def _store_row_slabs(flat_ref, s, val):
    rows = flat_ref.shape[0] // s
    for j in range(s):
        flat_ref[pl.ds(j, rows, stride=s), :] = val[:, j * LANES:(j + 1) * LANES]


def _router_kernel(x_ref, g_ref, wr_ref, hp_ref, lg_ref, whi_ref, wlo_ref):
    @pl.when(pl.program_id(0) == 0)
    def _():
        w = wr_ref[...]
        w_hi = w.astype(BF16)
        whi_ref[...] = w_hi
        wlo_ref[...] = (w - w_hi.astype(F32)).astype(BF16)

    x = x_ref[...]
    ms = jnp.mean(x * x, axis=-1, keepdims=True)
    h = x * lax.rsqrt(ms + EPS) * g_ref[...]
    _store_row_slabs(hp_ref, hp_ref.shape[0] // x_ref.shape[0], _pack_halves(h))
    h_hi = h.astype(BF16)
    h_lo = (h - h_hi.astype(F32)).astype(BF16)
    lg_ref[...] = (jnp.dot(h_hi, whi_ref[...], preferred_element_type=F32)
                   + (jnp.dot(h_hi, wlo_ref[...], preferred_element_type=F32)
                      + jnp.dot(h_lo, whi_ref[...], preferred_element_type=F32)))


def _router(x, gain, w_router, *, tm=256):
    t, d = x.shape
    nslab = d // 2 // LANES
    blk = 2 * (tm * d * 4 + d * 4 + d * ROUTER_COLS * 4 + tm * d * 2 + tm * ROUTER_COLS * 4) + 3 * tm * d * 4
    hp, logits = pl.pallas_call(
        _router_kernel,
        grid=(t // tm,),
        in_specs=[pl.BlockSpec((tm, d), lambda i: (i, 0)),
                  pl.BlockSpec((1, d), lambda i: (0, 0)),
                  pl.BlockSpec((d, ROUTER_COLS), lambda i: (0, 0))],
        out_specs=[pl.BlockSpec((tm * nslab, LANES), lambda i: (i, 0)),
                   pl.BlockSpec((tm, ROUTER_COLS), lambda i: (i, 0))],
        out_shape=[jax.ShapeDtypeStruct((t * nslab, LANES), U32),
                   jax.ShapeDtypeStruct((t, ROUTER_COLS), F32)],
        scratch_shapes=[pltpu.VMEM((d, ROUTER_COLS), BF16), pltpu.VMEM((d, ROUTER_COLS), BF16)],
        compiler_params=_params(("arbitrary",), _vmem_limit(blk)),
        name="moe_router",
    )(x, gain.reshape(1, d), w_router)
    return hp.reshape(t, nslab, LANES), logits


def _start_row_copies(idx_ref, base, n, src_ref, dst_at, sem):
    def issue(r, carry):
        pltpu.make_async_copy(src_ref.at[idx_ref[base + r]], dst_at(r), sem).start()
        return carry

    lax.fori_loop(0, n, issue, 0, unroll=8)


def _wait_row_copies(n, src_ref, dst_at, sem):
    def drain(r, carry):
        pltpu.make_async_copy(src_ref.at[0], dst_at(r), sem).wait()
        return carry

    lax.fori_loop(0, n, drain, 0, unroll=8)


def _dispatch_rows_kernel(dest_ref, pad0_ref, src_ref, xs_ref, buf, zeros, sem, zsem, *, bm):
    i = pl.program_id(0)
    nt = pl.num_programs(0)
    tm = src_ref.shape[0]
    slot = i % 2

    def fill_copy(e):
        row0 = pl.multiple_of(jnp.maximum(pad0_ref[e], 0), bm)
        return pltpu.make_async_copy(zeros, xs_ref.at[pl.ds(row0, bm)], zsem)

    @pl.when(i == 0)
    def _():
        zeros[...] = jnp.zeros_like(zeros)

        def fill(e, carry):
            @pl.when(pad0_ref[e] >= 0)
            def _():
                fill_copy(e).start()

            return carry

        lax.fori_loop(0, pad0_ref.shape[0], fill, 0)

        def fill_wait(e, carry):
            @pl.when(pad0_ref[e] >= 0)
            def _():
                fill_copy(e).wait()

            return carry

        lax.fori_loop(0, pad0_ref.shape[0], fill_wait, 0)

    def wait_tile(s):
        def drain(r, carry):
            for _ in range(TOP_K):
                pltpu.make_async_copy(buf.at[s, 0], xs_ref.at[0], sem.at[s]).wait()
            return carry

        lax.fori_loop(0, tm, drain, 0, unroll=4)

    @pl.when(i >= 2)
    def _():
        wait_tile(slot)

    buf[slot] = src_ref[...]

    def issue(r, carry):
        for k in range(TOP_K):
            row = dest_ref[(i * tm + r) * TOP_K + k]
            pltpu.make_async_copy(buf.at[slot, r], xs_ref.at[row], sem.at[slot]).start()
        return carry

    lax.fori_loop(0, tm, issue, 0, unroll=4)

    @pl.when(i == nt - 1)
    def _():
        wait_tile(slot)

        @pl.when(nt >= 2)
        def _():
            wait_tile(1 - slot)


def _dispatch_rows(dest, pad0, src, *, p, bm, tm=256):
    t = src.shape[0]
    slab = src.shape[1:]
    blk = 5 * tm * slab[0] * slab[1] * 4 + bm * slab[0] * slab[1] * 4
    return pl.pallas_call(
        functools.partial(_dispatch_rows_kernel, bm=bm),
        grid_spec=pltpu.PrefetchScalarGridSpec(
            num_scalar_prefetch=2,
            grid=(t // tm,),
            in_specs=[pl.BlockSpec((tm,) + slab, lambda i, dest, pad0: (i, 0, 0))],
            out_specs=pl.BlockSpec(memory_space=pltpu.HBM),
            scratch_shapes=[pltpu.VMEM((2, tm) + slab, src.dtype), pltpu.VMEM((bm,) + slab, src.dtype),
                            pltpu.SemaphoreType.DMA((2,)), pltpu.SemaphoreType.DMA(())],
        ),
        out_shape=jax.ShapeDtypeStruct((p,) + slab, src.dtype),
        compiler_params=_params(("arbitrary",), _vmem_limit(blk)),
        name="dispatch_rows",
    )(dest, pad0, src)


BLOCK_COPY_PRIORITY = 0


def _start_first_blocks(cnt_ref, start_ref, e, in_copy_at):
    n = cnt_ref[e] >> 1
    b = start_ref[e]

    @pl.when(n > 0)
    def _():
        in_copy_at(b, 0).start(priority=BLOCK_COPY_PRIORITY)

    @pl.when(n > 1)
    def _():
        in_copy_at(b + 1, 1).start(priority=BLOCK_COPY_PRIORITY)


def _expert_grid_step(cnt_ref, start_ref, done_ref, in_copy_at, out_copy_at, prologue, compute, tail_fill):
    e = pl.program_id(1)
    ne = pl.num_programs(1)
    step = pl.program_id(0) * ne + e
    last_step = pl.num_programs(0) * ne - 1
    nblk = cnt_ref[e] >> 1
    tail_half = (cnt_ref[e] & 1) == 1
    blk0 = start_ref[e]

    @pl.when(step == 0)
    def _():
        done_ref[0] = 0
        _start_first_blocks(cnt_ref, start_ref, e, in_copy_at)

    done = done_ref[0]

    @pl.when(nblk > 0)
    def _():
        prologue()

        def body(k, carry):
            in_slot = k % 2
            out_slot = (done + k) % 2
            in_copy_at(blk0 + k, in_slot).wait()

            @pl.when(done + k >= 2)
            def _():
                out_copy_at(0, out_slot).wait()

            half = tail_half & (k == nblk - 1)

            @pl.when(jnp.logical_not(half))
            def _():
                compute(in_slot, out_slot, False)

            @pl.when(half)
            def _():
                compute(in_slot, out_slot, True)

            out_copy_at(blk0 + k, out_slot).start(priority=BLOCK_COPY_PRIORITY)

            @pl.when(k + 2 < nblk)
            def _():
                in_copy_at(blk0 + k + 2, in_slot).start(priority=BLOCK_COPY_PRIORITY)

            return carry

        lax.fori_loop(0, nblk, body, 0)
        done_ref[0] = done + nblk

    @pl.when(e == ne - 1)
    def _():
        tail_fill(start_ref[ne - 1] + (cnt_ref[ne - 1] >> 1))

    @pl.when(step < last_step)
    def _():
        _start_first_blocks(cnt_ref, start_ref, (e + 1) % ne, in_copy_at)

    @pl.when(step == last_step)
    def _():
        total = done + nblk

        @pl.when(total >= 2)
        def _():
            out_copy_at(0, total % 2).wait()

        @pl.when(total >= 1)
        def _():
            out_copy_at(0, (total - 1) % 2).wait()


def _zero_fill_blocks(first, nblocks, zbuf, dst_at, zsem):
    zbuf[...] = jnp.zeros_like(zbuf)

    def fill(b, carry):
        pltpu.make_async_copy(zbuf, dst_at(b), zsem).start()
        return carry

    lax.fori_loop(first, nblocks, fill, 0)

    def fill_wait(b, carry):
        pltpu.make_async_copy(zbuf, dst_at(b), zsem).wait()
        return carry

    lax.fori_loop(first, nblocks, fill_wait, 0)


def _moe_up_kernel(cnt_ref, start_ref, xs_ref, w1_ref, w3_ref, h_ref, w1b_ref, w3b_ref, xbuf, obuf, xsem, osem,
                   done_ref, zbuf, zsem, *, bm):
    f = pl.program_id(0)
    d, fc = w1b_ref.shape
    nslab = d // 2 // LANES
    xrows = bm * nslab

    def x_copy(blk, slot):
        row0 = pl.multiple_of(blk * xrows, xrows)
        return pltpu.make_async_copy(xs_ref.at[pl.ds(row0, xrows)], xbuf.at[slot], xsem.at[slot])

    def h_copy(blk, slot):
        row0 = pl.multiple_of(blk * bm, bm)
        col0 = pl.multiple_of(f * fc, fc)
        return pltpu.make_async_copy(obuf.at[slot], h_ref.at[pl.ds(row0, bm), pl.ds(col0, fc)], osem.at[slot])

    def cast_weights():
        w1b_ref[...] = w1_ref[0].astype(BF16)
        w3b_ref[...] = w3_ref[0].astype(BF16)

    def compute(in_slot, out_slot, half):
        rows = bm // 2 if half else bm
        xin = xbuf.at[in_slot, pl.ds(0, rows * nslab)] if half else xbuf.at[in_slot]
        hi, lo = _unpack_halves(_load_row_slabs(xin, nslab, 0, nslab))
        hi = hi.astype(BF16)
        lo = lo.astype(BF16)
        d2 = d // 2
        a = (jnp.dot(hi, w1b_ref[:d2, :], preferred_element_type=F32)
             + jnp.dot(lo, w1b_ref[d2:, :], preferred_element_type=F32))
        c = (jnp.dot(hi, w3b_ref[:d2, :], preferred_element_type=F32)
             + jnp.dot(lo, w3b_ref[d2:, :], preferred_element_type=F32))
        res = (a * jax.nn.sigmoid(a) * c).astype(obuf.dtype)
        if half:
            obuf[out_slot, :rows] = res
            obuf[out_slot, rows:] = jnp.zeros((bm - rows, fc), obuf.dtype)
        else:
            obuf[out_slot] = res

    def tail_fill(first):
        col0 = pl.multiple_of(f * fc, fc)
        _zero_fill_blocks(first, h_ref.shape[0] // bm, zbuf,
                          lambda b: h_ref.at[pl.ds(pl.multiple_of(b * bm, bm), bm), pl.ds(col0, fc)], zsem)

    _expert_grid_step(cnt_ref, start_ref, done_ref, x_copy, h_copy, cast_weights, compute, tail_fill)


def _moe_down_kernel(cnt_ref, start_ref, h_ref, w2_ref, y_ref, w2b_ref, hbuf, obuf, hsem, osem, done_ref, zbuf, zsem,
                     *, bm):
    nslab = w2b_ref.shape[1] // 2 // LANES

    def h_copy(blk, slot):
        row0 = pl.multiple_of(blk * bm, bm)
        return pltpu.make_async_copy(h_ref.at[pl.ds(row0, bm)], hbuf.at[slot], hsem.at[slot])

    def y_copy(blk, slot):
        row0 = pl.multiple_of(blk * bm * nslab, bm * nslab)
        return pltpu.make_async_copy(obuf.at[slot], y_ref.at[pl.ds(row0, bm * nslab)], osem.at[slot])

    def cast_weights():
        w2b_ref[...] = w2_ref[0].astype(BF16)

    def compute(in_slot, out_slot, half):
        rows = bm // 2 if half else bm
        hin = hbuf[in_slot, :rows] if half else hbuf[in_slot]
        packed = _pack_halves(jnp.dot(hin.astype(BF16), w2b_ref[...], preferred_element_type=F32))
        if half:
            _store_row_slabs(obuf.at[out_slot, pl.ds(0, rows * nslab)], nslab, packed)
            obuf[out_slot, rows * nslab:] = jnp.zeros(((bm - rows) * nslab, LANES), obuf.dtype)
        else:
            _store_row_slabs(obuf.at[out_slot], nslab, packed)

    def tail_fill(first):
        rows = bm * nslab
        _zero_fill_blocks(first, y_ref.shape[0] // rows, zbuf,
                          lambda b: y_ref.at[pl.ds(pl.multiple_of(b * rows, rows), rows)], zsem)

    _expert_grid_step(cnt_ref, start_ref, done_ref, h_copy, y_copy, cast_weights, compute, tail_fill)


def _moe_experts(xs, blk_cnt, blk_start, w1, w3, w2, *, bm, fc=512, nc=2048):
    p, nslab, _ = xs.shape
    d = 2 * nslab * LANES
    ne, _, de = w1.shape
    any_spec = pl.BlockSpec(memory_space=pltpu.HBM)
    blk_up = (4 * d * fc * 4 + 2 * d * fc * 2 + 2 * bm * nslab * LANES * 4 + 2 * bm * fc * 4
              + 2 * bm * d * 2 + 3 * bm * fc * 4)
    h = pl.pallas_call(
        functools.partial(_moe_up_kernel, bm=bm),
        grid_spec=pltpu.PrefetchScalarGridSpec(
            num_scalar_prefetch=2,
            grid=(de // fc, ne),
            in_specs=[any_spec,
                      pl.BlockSpec((1, d, fc), lambda f, e, cnt, start: (e, 0, f)),
                      pl.BlockSpec((1, d, fc), lambda f, e, cnt, start: (e, 0, f))],
            out_specs=any_spec,
            scratch_shapes=[pltpu.VMEM((d, fc), BF16), pltpu.VMEM((d, fc), BF16),
                            pltpu.VMEM((2, bm * nslab, LANES), U32), pltpu.VMEM((2, bm, fc), F32),
                            pltpu.SemaphoreType.DMA((2,)), pltpu.SemaphoreType.DMA((2,)),
                            pltpu.SMEM((1,), jnp.int32),
                            pltpu.VMEM((bm, fc), F32), pltpu.SemaphoreType.DMA(())],
        ),
        out_shape=jax.ShapeDtypeStruct((p, de), F32),
        compiler_params=_params(("arbitrary", "arbitrary"), _vmem_limit(blk_up)),
        name="moe_up",
    )(blk_cnt, blk_start, xs.reshape(p * nslab, LANES), w1, w3)
    assert nc == d, "expert-down writes whole row slabs"
    blk_dn = 2 * de * nc * 4 + de * nc * 2 + 3 * bm * de * 4 + 2 * bm * nslab * LANES * 4 + 2 * bm * nc * 4
    ys = pl.pallas_call(
        functools.partial(_moe_down_kernel, bm=bm),
        grid_spec=pltpu.PrefetchScalarGridSpec(
            num_scalar_prefetch=2,
            grid=(d // nc, ne),
            in_specs=[any_spec,
                      pl.BlockSpec((1, de, nc), lambda c, e, cnt, start: (e, 0, c))],
            out_specs=any_spec,
            scratch_shapes=[pltpu.VMEM((de, nc), BF16),
                            pltpu.VMEM((2, bm, de), F32), pltpu.VMEM((2, bm * nslab, LANES), U32),
                            pltpu.SemaphoreType.DMA((2,)), pltpu.SemaphoreType.DMA((2,)),
                            pltpu.SMEM((1,), jnp.int32),
                            pltpu.VMEM((bm * nslab, LANES), U32), pltpu.SemaphoreType.DMA(())],
        ),
        out_shape=jax.ShapeDtypeStruct((p * nslab, LANES), U32),
        compiler_params=_params(("arbitrary", "arbitrary"), _vmem_limit(blk_dn)),
        name="moe_down",
    )(blk_cnt, blk_start, h, w2)
    return ys.reshape(p, nslab, LANES)


def _combine_kernel(back_ref, x_ref, ys_ref, ew_ref, g_ref, o_ref, ybuf, sem, *, nc):
    i = pl.program_id(0)
    nt = pl.num_programs(0)
    tm, d = x_ref.shape
    t = nt * tm
    nslab = d // 2 // LANES
    slot = i % 2

    def row_at(s, k):
        def dst_at(r):
            return ybuf.at[s, pl.ds(pl.multiple_of((k * tm + r) * nslab, nslab), nslab)]
        return dst_at

    def start_tile(tile, s):
        for k in range(TOP_K):
            _start_row_copies(back_ref, k * t + tile * tm, tm, ys_ref, row_at(s, k), sem.at[s])

    @pl.when(i == 0)
    def _():
        start_tile(0, 0)

    @pl.when(i + 1 < nt)
    def _():
        start_tile(i + 1, 1 - slot)

    for k in range(TOP_K):
        _wait_row_copies(tm, ys_ref, row_at(slot, k), sem.at[slot])

    ew = ew_ref[...]
    w0 = ew[:, 0:1]
    w1 = ew[:, 1:2]
    half = nc // 2
    y0_ref = ybuf.at[slot, pl.ds(0, tm * nslab)]
    y1_ref = ybuf.at[slot, pl.ds(tm * nslab, tm * nslab)]
    pieces = []
    ss = None
    for c in range(x_ref.shape[1] // nc):
        hi0, lo0 = _unpack_halves(_load_row_slabs(y0_ref, nslab, c * half // LANES, half // LANES))
        hi1, lo1 = _unpack_halves(_load_row_slabs(y1_ref, nslab, c * half // LANES, half // LANES))
        for j, (p0, p1) in enumerate(((hi0, hi1), (lo0, lo1))):
            cols = slice(c * nc + j * half, c * nc + (j + 1) * half)
            z = x_ref[:, cols] + w0 * p0 + w1 * p1
            pieces.append((cols, z))
            s = jnp.sum(z * z, axis=-1, keepdims=True)
            ss = s if ss is None else ss + s
    inv = lax.rsqrt(ss / x_ref.shape[1] + EPS)
    for cols, z in pieces:
        o_ref[:, cols] = z * inv * g_ref[:, cols]


def _combine(x, ys, back, e_w, gain, *, nc, tm=256):
    t, d = x.shape
    nt = t // tm
    nslab = ys.shape[1]
    blk = 2 * (2 * tm * d * 4 + tm * LANES * 4 + d * 4) + 2 * TOP_K * tm * d * 2 + 3 * tm * d * 4
    return pl.pallas_call(
        functools.partial(_combine_kernel, nc=nc),
        grid_spec=pltpu.PrefetchScalarGridSpec(
            num_scalar_prefetch=1,
            grid=(nt,),
            in_specs=[pl.BlockSpec((tm, d), lambda i, back: (i, 0)),
                      pl.BlockSpec(memory_space=pltpu.HBM),
                      pl.BlockSpec((tm, TOP_K), lambda i, back: (i, 0)),
                      pl.BlockSpec((1, d), lambda i, back: (0, 0))],
            out_specs=pl.BlockSpec((tm, d), lambda i, back: (i, 0)),
            scratch_shapes=[pltpu.VMEM((2, TOP_K * tm * nslab, LANES), ys.dtype), pltpu.SemaphoreType.DMA((2,))],
        ),
        out_shape=jax.ShapeDtypeStruct((t, d), F32),
        compiler_params=_params(("arbitrary",), _vmem_limit(blk)),
        name="moe_combine",
    )(back, x, ys, e_w, gain.reshape(1, d))


def _route(logits, *, bm):
    t = logits.shape[0]
    g_logits = logits[:, :N_GROUPS]
    e_logits = logits[:, N_GROUPS:N_GROUPS + N_EXPERTS].reshape(t, N_GROUPS, EXPERTS_PER_GROUP)
    g_idx = jnp.argmax(g_logits, axis=-1).astype(jnp.int32)
    g_w = jnp.take_along_axis(jax.nn.softmax(g_logits, axis=-1), g_idx[:, None], axis=-1)
    e_sel = jnp.take_along_axis(e_logits, g_idx[:, None, None], axis=1)[:, 0]
    top_v, top_i = lax.top_k(e_sel, TOP_K)
    e_w = jax.nn.softmax(top_v, axis=-1) * g_w
    eid = g_idx[:, None] * EXPERTS_PER_GROUP + top_i.astype(jnp.int32)

    a = t * TOP_K
    flat_e = eid.reshape(a)
    onehot = (flat_e[:, None] == jnp.arange(N_EXPERTS, dtype=jnp.int32)[None, :]).astype(jnp.int32)
    csum = jnp.cumsum(onehot, axis=0)
    rank = jnp.take_along_axis(csum, flat_e[:, None], axis=1)[:, 0] - 1
    counts = csum[-1]
    padded = ((counts + bm - 1) // bm) * bm
    pad_end = jnp.cumsum(padded)
    start_pad = pad_end - padded
    dest = (start_pad[flat_e] + rank).astype(jnp.int32)
    tail_half = (counts > 0) & ((counts - 1) % bm < bm // 2)
    blk_cnt = (2 * (padded // bm) + tail_half).astype(jnp.int32)
    blk_start = (start_pad // bm).astype(jnp.int32)
    tail = jnp.arange(0, a + N_EXPERTS * bm, bm, dtype=jnp.int32)
    pad0 = jnp.concatenate([jnp.where(padded > 0, pad_end - bm, -1),
                            jnp.where(tail >= pad_end[-1], tail, -1)]).astype(jnp.int32)
    back = dest.reshape(t, TOP_K).T.reshape(a)
    return e_w, dest, back, pad0, blk_cnt, blk_start


MOE_BLOCK_ROWS = 256
MOE_DOWN_COLS = 4096


def kernel(x, mem, norm_mix, w_in, pool_w, pool_scale, lb_fwd, lb_bwd, hgrn_norm, w_out, norm_xattn, norm_mem,
           w_q, w_k, w_v, w_o, norm_moe, w_router_group, w_router_expert, w1, w3, w2, norm_final):
    bsz, seq, d = x.shape
    assert bsz == 1 and w_in.shape[0] == 1, "one sequence, one layer (the final norm is fused into the MoE combine)"
    l = 0
    pool_width = pool_w.shape[1] * pool_w.shape[2]
    hgrn_width = hgrn_norm.shape[1]
    xt = x.reshape(seq, d)
    mem_t = mem.reshape(mem.shape[1], d)
    h = _rmsnorm(xt, norm_mix[l], BF16)
    proj = _matmul([h], w_in[l], out_dtype=F32)
    y_pool = _pool_mixer(proj, pool_w[l], pool_scale[l])
    y_hgrn = _hgrn_mixer(proj, lb_fwd, lb_bwd, hgrn_norm[l], layer=l, width=hgrn_width, col0=pool_width)
    xt = _matmul([y_pool, y_hgrn], w_out[l], resid=xt, out_dtype=F32)
    mem_n = _rmsnorm(mem_t, norm_mem[l], BF16)
    hx = _rmsnorm(xt, norm_xattn[l], BF16)
    q = _matmul([hx], w_q[l], out_dtype=BF16)
    k = _matmul([mem_n], w_k[l], out_dtype=BF16)
    v = _matmul([mem_n], w_v[l], out_dtype=BF16)
    o = _xattn(q, k, v)
    xt = _matmul([o], w_o[l], resid=xt, out_dtype=F32)
    w_router = jnp.concatenate(
        [w_router_group[l],
         jnp.transpose(w_router_expert[l], (1, 0, 2)).reshape(d, N_EXPERTS),
         jnp.zeros((d, ROUTER_COLS - N_GROUPS - N_EXPERTS), F32)], axis=1)
    hp, logits = _router(xt, norm_moe[l], w_router)
    e_w, dest, back, pad0, blk_cnt, blk_start = _route(logits, bm=MOE_BLOCK_ROWS)
    xs = _dispatch_rows(dest, pad0, hp, p=TOP_K * seq + N_EXPERTS * MOE_BLOCK_ROWS, bm=MOE_BLOCK_ROWS)
    ys = _moe_experts(xs, blk_cnt, blk_start, w1[l], w3[l], w2[l], bm=MOE_BLOCK_ROWS, nc=MOE_DOWN_COLS)
    out = _combine(xt, ys, back, e_w, norm_final, nc=MOE_DOWN_COLS)
    return out.reshape(bsz, seq, d)
```

```python
import functools

import jax
import jax.numpy as jnp
from jax import lax
from jax.experimental import pallas as pl
from jax.experimental.pallas import tpu as pltpu

F32 = jnp.float32
BF16 = jnp.bfloat16
U32 = jnp.uint32

EPS = 1e-6
LANES = 128
SUBLANES = 8
VMEM_BYTES_V7X = 64 * 1024 * 1024

POOL_WINDOWS = (2, 4, 8, 16)
HEAD = 128
CHUNK = 64
XATTN_HEADS = 4
N_GROUPS = 4
EXPERTS_PER_GROUP = 8
N_EXPERTS = N_GROUPS * EXPERTS_PER_GROUP
TOP_K = 2
ROUTER_COLS = LANES


def _params(sem, vmem_bytes):
    return pltpu.CompilerParams(dimension_semantics=sem, vmem_limit_bytes=int(vmem_bytes))


def _vmem_limit(block_bytes):
    return min(int(block_bytes * 1.25) + (6 << 20), VMEM_BYTES_V7X - (4 << 20))


def _rmsnorm_kernel(x_ref, g_ref, o_ref):
    x = x_ref[...]
    ms = jnp.mean(x * x, axis=-1, keepdims=True)
    o_ref[...] = (x * lax.rsqrt(ms + EPS) * g_ref[...]).astype(o_ref.dtype)


def _rmsnorm(x, gain, out_dtype, tm=512):
    t, d = x.shape
    tm = min(tm, t)
    blk = tm * d * (4 + jnp.dtype(out_dtype).itemsize) * 2
    return pl.pallas_call(
        _rmsnorm_kernel,
        grid=(t // tm,),
        in_specs=[pl.BlockSpec((tm, d), lambda i: (i, 0)),
                  pl.BlockSpec((1, d), lambda i: (0, 0))],
        out_specs=pl.BlockSpec((tm, d), lambda i: (i, 0)),
        out_shape=jax.ShapeDtypeStruct((t, d), out_dtype),
        compiler_params=_params(("arbitrary",), _vmem_limit(blk)),
        name="rmsnorm",
    )(x, gain.reshape(1, d))


def _mm_kernel(*refs, n_a, has_resid):
    a_refs = refs[:n_a]
    w_ref = refs[n_a]
    r_ref = refs[n_a + 1] if has_resid else None
    o_ref, wb_ref = refs[-2], refs[-1]

    @pl.when(pl.program_id(1) == 0)
    def _():
        wb_ref[...] = w_ref[...].astype(BF16)

    acc = None
    k0 = 0
    for a_ref in a_refs:
        kk = a_ref.shape[1]
        part = jnp.dot(a_ref[...], wb_ref[k0:k0 + kk, :], preferred_element_type=F32)
        acc = part if acc is None else acc + part
        k0 += kk
    if has_resid:
        acc = acc + r_ref[...]
    o_ref[...] = acc.astype(o_ref.dtype)


def _matmul(a_parts, w, *, resid=None, out_dtype=F32, tm=1024, tn=512):
    m = a_parts[0].shape[0]
    k, n = w.shape
    assert sum(a.shape[1] for a in a_parts) == k
    tm = min(tm, m)
    in_specs = [pl.BlockSpec((tm, a.shape[1]), lambda j, i: (i, 0)) for a in a_parts]
    in_specs.append(pl.BlockSpec((k, tn), lambda j, i: (0, j)))
    args = list(a_parts) + [w]
    if resid is not None:
        in_specs.append(pl.BlockSpec((tm, tn), lambda j, i: (i, j)))
        args.append(resid)
    osz = jnp.dtype(out_dtype).itemsize
    blk = (2 * tm * k * 2 + 2 * k * tn * 4 + k * tn * 2 + 2 * tm * tn * osz
           + (2 * tm * tn * 4 if resid is not None else 0) + tm * tn * 4)
    return pl.pallas_call(
        functools.partial(_mm_kernel, n_a=len(a_parts), has_resid=resid is not None),
        grid=(n // tn, m // tm),
        in_specs=in_specs,
        out_specs=pl.BlockSpec((tm, tn), lambda j, i: (i, j)),
        out_shape=jax.ShapeDtypeStruct((m, n), out_dtype),
        scratch_shapes=[pltpu.VMEM((k, tn), BF16)],
        compiler_params=_params(("arbitrary", "arbitrary"), _vmem_limit(blk)),
        name="matmul",
    )(*args)


def _pool_kernel(u_ref, up_ref, un_ref, w_ref, s_ref, o_ref, *, tm, seq, group):
    i = pl.program_id(0)
    last = pl.num_programs(0) - 1
    n = tm + 2 * SUBLANES
    row = lax.broadcasted_iota(jnp.int32, (tm, 1), 0) + i * tm
    for gi, w in enumerate(POOL_WINDOWS):
        cs = slice(gi * group, (gi + 1) * group)
        u = u_ref[:, cs]
        prev = jnp.where(i > 0, up_ref[:, cs], 0.0)
        nxt = jnp.where(i < last, un_ref[:, cs], 0.0)
        f = jnp.concatenate([prev, u, nxt], axis=0)
        step = 1
        while step < w:
            f = f + pltpu.roll(f, n - step, 0)
            step *= 2
        half = w // 2
        first = SUBLANES - half
        if first:
            f = pltpu.roll(f, n - first, 0)
        win = f[:tm]
        lo = jnp.maximum(row - half, 0)
        hi = jnp.minimum(row + half - 1, seq - 1)
        cnt = (hi - lo + 1).astype(F32)
        mixed = win / cnt - u
        y = jnp.dot(mixed.astype(BF16), w_ref[gi].astype(BF16), preferred_element_type=F32)
        o_ref[:, cs] = (y * s_ref[:, cs]).astype(o_ref.dtype)


def _pool_mixer(proj, pool_w, pool_scale, *, tm=512):
    t = proj.shape[0]
    ng, group, _ = pool_w.shape
    width = ng * group
    rpb = tm // SUBLANES
    nhalo = t // SUBLANES
    blk = 2 * (tm * width * 4 + 2 * SUBLANES * width * 4 + ng * group * group * 4 + tm * width * 2) + 6 * tm * group * 4
    return pl.pallas_call(
        functools.partial(_pool_kernel, tm=tm, seq=t, group=group),
        grid=(t // tm,),
        in_specs=[pl.BlockSpec((tm, width), lambda i: (i, 0)),
                  pl.BlockSpec((SUBLANES, width), lambda i: (jnp.maximum(i * rpb - 1, 0), 0)),
                  pl.BlockSpec((SUBLANES, width), lambda i: (jnp.minimum((i + 1) * rpb, nhalo - 1), 0)),
                  pl.BlockSpec((ng, group, group), lambda i: (0, 0, 0)),
                  pl.BlockSpec((1, width), lambda i: (0, 0))],
        out_specs=pl.BlockSpec((tm, width), lambda i: (i, 0)),
        out_shape=jax.ShapeDtypeStruct((t, width), BF16),
        compiler_params=_params(("arbitrary",), _vmem_limit(blk)),
        name="pool_mixer",
    )(proj, proj, proj, pool_w, pool_scale.reshape(1, width))


def _chunk_cumsum(x, reverse):
    n = x.shape[0]
    pos = lax.broadcasted_iota(jnp.int32, x.shape, 0) % CHUNK
    s = 1
    while s < CHUNK:
        if reverse:
            x = x + jnp.where(pos < CHUNK - s, pltpu.roll(x, n - s, 0), 0.0)
        else:
            x = x + jnp.where(pos >= s, pltpu.roll(x, s, 0), 0.0)
        s *= 2
    return x


def _lower_bound(lb_raw, layer):
    e = jnp.exp(lb_raw - jnp.max(lb_raw, axis=0, keepdims=True))
    return jnp.sum(e[:layer + 1], axis=0, keepdims=True) / jnp.sum(e, axis=0, keepdims=True)


def _hgrn_direction(q_raw, f_raw, v, lb, st_ref, h, mask, *, reverse):
    tb = q_raw.shape[0]
    nc = tb // CHUNK
    q = q_raw * jax.nn.sigmoid(q_raw)
    f = lb + (1.0 - lb) * jax.nn.sigmoid(f_raw)
    logf = jnp.log(f)
    k = 1.0 - f
    b = _chunk_cumsum(logf, reverse)
    b3 = b.reshape(nc, CHUNK, HEAD)
    edge = 0 if reverse else CHUNK - 1
    b_last = b3[:, edge:edge + 1, :]
    qd = (q * jnp.exp(b)).astype(BF16)
    kd = (k * jnp.exp(-b)).astype(BF16)
    k_end = (k.reshape(nc, CHUNK, HEAD) * jnp.exp(b_last - b3)).astype(BF16)
    decay = jnp.exp(b_last)
    vb = v.astype(BF16)

    a = lax.dot_general(qd, kd, (((1,), (1,)), ((), ())), preferred_element_type=F32)
    a = jnp.where(mask, a, 0.0)
    o_intra = jnp.dot(a.astype(BF16), vb, preferred_element_type=F32)

    st = st_ref[h]
    o_inter = [None] * nc
    order = range(nc - 1, -1, -1) if reverse else range(nc)
    for c in order:
        rows = slice(c * CHUNK, (c + 1) * CHUNK)
        o_inter[c] = lax.dot_general(qd[rows], st.astype(BF16), (((1,), (1,)), ((), ())),
                                     preferred_element_type=F32)
        d_st = lax.dot_general(vb[rows], k_end[c], (((0,), (0,)), ((), ())), preferred_element_type=F32)
        st = st * decay[c] + d_st
    st_ref[h] = st
    return o_intra + jnp.concatenate(o_inter, axis=0)


def _intra_mask(tb, reverse):
    r = lax.broadcasted_iota(jnp.int32, (tb, tb), 0)
    c = lax.broadcasted_iota(jnp.int32, (tb, tb), 1)
    same = (r // CHUNK) == (c // CHUNK)
    return same & ((c >= r) if reverse else (c <= r))


def _hgrn_fwd_kernel(q_ref, f_ref, i_ref, lb_ref, o_ref, st_ref, *, layer, heads):
    @pl.when(pl.program_id(0) == 0)
    def _():
        st_ref[...] = jnp.zeros_like(st_ref)

    mask = _intra_mask(q_ref.shape[0], False)

    for h in range(heads):
        hs = slice(h * HEAD, (h + 1) * HEAD)
        lb = _lower_bound(lb_ref[:, hs], layer)
        o_ref[:, hs] = _hgrn_direction(q_ref[:, hs], f_ref[:, hs], i_ref[:, hs], lb, st_ref, h, mask,
                                       reverse=False)


def _hgrn_bwd_kernel(q_ref, f_ref, i_ref, g_ref, of_ref, lb_ref, gain_ref, y_ref, st_ref, *, layer, heads):
    @pl.when(pl.program_id(0) == 0)
    def _():
        st_ref[...] = jnp.zeros_like(st_ref)

    mask = _intra_mask(q_ref.shape[0], True)

    for h in range(heads):
        hs = slice(h * HEAD, (h + 1) * HEAD)
        lb = _lower_bound(lb_ref[:, hs], layer)
        o = of_ref[:, hs] + _hgrn_direction(q_ref[:, hs], f_ref[:, hs], i_ref[:, hs], lb, st_ref, h, mask,
                                            reverse=True)
        o = o * lax.rsqrt(jnp.mean(o * o, axis=-1, keepdims=True) + EPS) * gain_ref[:, hs]
        g = g_ref[:, hs]
        y_ref[:, hs] = (o * (g * jax.nn.sigmoid(g))).astype(y_ref.dtype)


def _hgrn_mixer(proj, lb_fwd, lb_bwd, hgrn_norm, *, layer, width, col0, tb=256):
    t = proj.shape[0]
    heads = width // HEAD
    nb = t // tb
    c = col0 // width
    layers = lb_fwd.shape[0]
    blk_f = 2 * (3 * tb * width * 4 + layers * width * 4 + tb * width * 4) + heads * HEAD * HEAD * 4
    o_f = pl.pallas_call(
        functools.partial(_hgrn_fwd_kernel, layer=layer, heads=heads),
        grid=(nb,),
        in_specs=[pl.BlockSpec((tb, width), lambda b: (b, c)),
                  pl.BlockSpec((tb, width), lambda b: (b, c + 1)),
                  pl.BlockSpec((tb, width), lambda b: (b, c + 3)),
                  pl.BlockSpec((layers, width), lambda b: (0, 0))],
        out_specs=pl.BlockSpec((tb, width), lambda b: (b, 0)),
        out_shape=jax.ShapeDtypeStruct((t, width), F32),
        scratch_shapes=[pltpu.VMEM((heads, HEAD, HEAD), F32)],
        compiler_params=_params(("arbitrary",), _vmem_limit(blk_f)),
        name="hgrn_fwd",
    )(proj, proj, proj, lb_fwd)
    blk_b = 2 * (5 * tb * width * 4 + layers * width * 4 + width * 4 + tb * width * 2) + heads * HEAD * HEAD * 4
    return pl.pallas_call(
        functools.partial(_hgrn_bwd_kernel, layer=layer, heads=heads),
        grid=(nb,),
        in_specs=[pl.BlockSpec((tb, width), lambda b: (nb - 1 - b, c)),
                  pl.BlockSpec((tb, width), lambda b: (nb - 1 - b, c + 2)),
                  pl.BlockSpec((tb, width), lambda b: (nb - 1 - b, c + 3)),
                  pl.BlockSpec((tb, width), lambda b: (nb - 1 - b, c + 4)),
                  pl.BlockSpec((tb, width), lambda b: (nb - 1 - b, 0)),
                  pl.BlockSpec((layers, width), lambda b: (0, 0)),
                  pl.BlockSpec((1, width), lambda b: (0, 0))],
        out_specs=pl.BlockSpec((tb, width), lambda b: (nb - 1 - b, 0)),
        out_shape=jax.ShapeDtypeStruct((t, width), BF16),
        scratch_shapes=[pltpu.VMEM((heads, HEAD, HEAD), F32)],
        compiler_params=_params(("arbitrary",), _vmem_limit(blk_b)),
        name="hgrn_bwd",
    )(proj, proj, proj, proj, o_f, lb_bwd, hgrn_norm.reshape(1, width))


def _xattn_kernel(q_ref, k_ref, v_ref, o_ref, *, scale):
    s = lax.dot_general(q_ref[...], k_ref[...], (((1,), (1,)), ((), ())), preferred_element_type=F32) * scale
    s = s - jnp.max(s, axis=-1, keepdims=True)
    p = jnp.exp(s)
    p = p / jnp.sum(p, axis=-1, keepdims=True)
    o_ref[...] = jnp.dot(p.astype(BF16), v_ref[...], preferred_element_type=F32).astype(o_ref.dtype)


def _xattn(q, k, v, *, tm=1024):
    t, d = q.shape
    m = k.shape[0]
    dh = d // XATTN_HEADS
    blk = 2 * (2 * tm * dh * 2 + 2 * m * dh * 2) + 4 * tm * m * 4 + tm * dh * 4
    return pl.pallas_call(
        functools.partial(_xattn_kernel, scale=dh ** -0.5),
        grid=(XATTN_HEADS, t // tm),
        in_specs=[pl.BlockSpec((tm, dh), lambda h, i: (i, h)),
                  pl.BlockSpec((m, dh), lambda h, i: (0, h)),
                  pl.BlockSpec((m, dh), lambda h, i: (0, h))],
        out_specs=pl.BlockSpec((tm, dh), lambda h, i: (i, h)),
        out_shape=jax.ShapeDtypeStruct((t, d), BF16),
        compiler_params=_params(("arbitrary", "arbitrary"), _vmem_limit(blk)),
        name="xattn",
    )(q, k, v)


def _pack_halves(x):
    w = x.shape[1] // 2
    hi = lax.bitcast_convert_type(x[:, :w].astype(BF16).astype(F32), U32)
    lo = lax.bitcast_convert_type(x[:, w:].astype(BF16).astype(F32), U32)
    return hi | (lo >> 16)


def _unpack_halves(p):
    hi = lax.bitcast_convert_type(p & jnp.uint32(0xFFFF0000), F32)
    lo = lax.bitcast_convert_type(p << 16, F32)
    return hi, lo


def _load_row_slabs(flat_ref, s, j0, nj):
    rows = flat_ref.shape[0] // s
    return jnp.concatenate([flat_ref[pl.ds(j0 + j, rows, stride=s), :] for j in range(nj)], axis=1)


def _store_row_slabs(flat_ref, s, val):
    rows = flat_ref.shape[0] // s
    for j in range(s):
        flat_ref[pl.ds(j, rows, stride=s), :] = val[:, j * LANES:(j + 1) * LANES]


def _router_kernel(x_ref, g_ref, wr_ref, hp_ref, lg_ref, whi_ref, wlo_ref):
    @pl.when(pl.program_id(0) == 0)
    def _():
        w = wr_ref[...]
        w_hi = w.astype(BF16)
        whi_ref[...] = w_hi
        wlo_ref[...] = (w - w_hi.astype(F32)).astype(BF16)

    x = x_ref[...]
    ms = jnp.mean(x * x, axis=-1, keepdims=True)
    h = x * lax.rsqrt(ms + EPS) * g_ref[...]
    _store_row_slabs(hp_ref, hp_ref.shape[0] // x_ref.shape[0], _pack_halves(h))
    h_hi = h.astype(BF16)
    h_lo = (h - h_hi.astype(F32)).astype(BF16)
    lg_ref[...] = (jnp.dot(h_hi, whi_ref[...], preferred_element_type=F32)
                   + (jnp.dot(h_hi, wlo_ref[...], preferred_element_type=F32)
                      + jnp.dot(h_lo, whi_ref[...], preferred_element_type=F32)))


def _router(x, gain, w_router, *, tm=256):
    t, d = x.shape
    nslab = d // 2 // LANES
    blk = 2 * (tm * d * 4 + d * 4 + d * ROUTER_COLS * 4 + tm * d * 2 + tm * ROUTER_COLS * 4) + 3 * tm * d * 4
    hp, logits = pl.pallas_call(
        _router_kernel,
        grid=(t // tm,),
        in_specs=[pl.BlockSpec((tm, d), lambda i: (i, 0)),
                  pl.BlockSpec((1, d), lambda i: (0, 0)),
                  pl.BlockSpec((d, ROUTER_COLS), lambda i: (0, 0))],
        out_specs=[pl.BlockSpec((tm * nslab, LANES), lambda i: (i, 0)),
                   pl.BlockSpec((tm, ROUTER_COLS), lambda i: (i, 0))],
        out_shape=[jax.ShapeDtypeStruct((t * nslab, LANES), U32),
                   jax.ShapeDtypeStruct((t, ROUTER_COLS), F32)],
        scratch_shapes=[pltpu.VMEM((d, ROUTER_COLS), BF16), pltpu.VMEM((d, ROUTER_COLS), BF16)],
        compiler_params=_params(("arbitrary",), _vmem_limit(blk)),
        name="moe_router",
    )(x, gain.reshape(1, d), w_router)
    return hp.reshape(t, nslab, LANES), logits


def _start_row_copies(idx_ref, base, n, src_ref, dst_at, sem):
    def issue(r, carry):
        pltpu.make_async_copy(src_ref.at[idx_ref[base + r]], dst_at(r), sem).start()
        return carry

    lax.fori_loop(0, n, issue, 0, unroll=8)


def _wait_row_copies(n, src_ref, dst_at, sem):
    def drain(r, carry):
        pltpu.make_async_copy(src_ref.at[0], dst_at(r), sem).wait()
        return carry

    lax.fori_loop(0, n, drain, 0, unroll=8)


def _dispatch_rows_kernel(dest_ref, pad0_ref, src_ref, xs_ref, buf, zeros, sem, zsem, *, bm):
    i = pl.program_id(0)
    nt = pl.num_programs(0)
    tm = src_ref.shape[0]
    slot = i % 2

    def fill_copy(e):
        row0 = pl.multiple_of(jnp.maximum(pad0_ref[e], 0), bm)
        return pltpu.make_async_copy(zeros, xs_ref.at[pl.ds(row0, bm)], zsem)

    @pl.when(i == 0)
    def _():
        zeros[...] = jnp.zeros_like(zeros)

        def fill(e, carry):
            @pl.when(pad0_ref[e] >= 0)
            def _():
                fill_copy(e).start()

            return carry

        lax.fori_loop(0, pad0_ref.shape[0], fill, 0)

        def fill_wait(e, carry):
            @pl.when(pad0_ref[e] >= 0)
            def _():
                fill_copy(e).wait()

            return carry

        lax.fori_loop(0, pad0_ref.shape[0], fill_wait, 0)

    def wait_tile(s):
        def drain(r, carry):
            for _ in range(TOP_K):
                pltpu.make_async_copy(buf.at[s, 0], xs_ref.at[0], sem.at[s]).wait()
            return carry

        lax.fori_loop(0, tm, drain, 0, unroll=4)

    @pl.when(i >= 2)
    def _():
        wait_tile(slot)

    buf[slot] = src_ref[...]

    def issue(r, carry):
        for k in range(TOP_K):
            row = dest_ref[(i * tm + r) * TOP_K + k]
            pltpu.make_async_copy(buf.at[slot, r], xs_ref.at[row], sem.at[slot]).start()
        return carry

    lax.fori_loop(0, tm, issue, 0, unroll=4)

    @pl.when(i == nt - 1)
    def _():
        wait_tile(slot)

        @pl.when(nt >= 2)
        def _():
            wait_tile(1 - slot)


def _dispatch_rows(dest, pad0, src, *, p, bm, tm=256):
    t = src.shape[0]
    slab = src.shape[1:]
    blk = 5 * tm * slab[0] * slab[1] * 4 + bm * slab[0] * slab[1] * 4
    return pl.pallas_call(
        functools.partial(_dispatch_rows_kernel, bm=bm),
        grid_spec=pltpu.PrefetchScalarGridSpec(
            num_scalar_prefetch=2,
            grid=(t // tm,),
            in_specs=[pl.BlockSpec((tm,) + slab, lambda i, dest, pad0: (i, 0, 0))],
            out_specs=pl.BlockSpec(memory_space=pltpu.HBM),
            scratch_shapes=[pltpu.VMEM((2, tm) + slab, src.dtype), pltpu.VMEM((bm,) + slab, src.dtype),
                            pltpu.SemaphoreType.DMA((2,)), pltpu.SemaphoreType.DMA(())],
        ),
        out_shape=jax.ShapeDtypeStruct((p,) + slab, src.dtype),
        compiler_params=_params(("arbitrary",), _vmem_limit(blk)),
        name="dispatch_rows",
    )(dest, pad0, src)


BLOCK_COPY_PRIORITY = 0
FILL_COPY_PRIORITY = 1


def _start_first_blocks(cnt_ref, start_ref, e, in_copy_at):
    n = cnt_ref[e] >> 1
    b = start_ref[e]

    @pl.when(n > 0)
    def _():
        in_copy_at(b, 0).start(priority=BLOCK_COPY_PRIORITY)

    @pl.when(n > 1)
    def _():
        in_copy_at(b + 1, 1).start(priority=BLOCK_COPY_PRIORITY)


def _expert_grid_step(cnt_ref, start_ref, done_ref, in_copy_at, out_copy_at, prologue, compute, tail_fill):
    e = pl.program_id(1)
    ne = pl.num_programs(1)
    step = pl.program_id(0) * ne + e
    last_step = pl.num_programs(0) * ne - 1
    nblk = cnt_ref[e] >> 1
    tail_half = (cnt_ref[e] & 1) == 1
    blk0 = start_ref[e]

    @pl.when(step == 0)
    def _():
        done_ref[0] = 0
        _start_first_blocks(cnt_ref, start_ref, e, in_copy_at)

    done = done_ref[0]

    @pl.when(nblk > 0)
    def _():
        prologue()

        def body(k, carry):
            in_slot = k % 2
            out_slot = (done + k) % 2
            in_copy_at(blk0 + k, in_slot).wait()

            @pl.when(done + k >= 2)
            def _():
                out_copy_at(0, out_slot).wait()

            half = tail_half & (k == nblk - 1)

            @pl.when(jnp.logical_not(half))
            def _():
                compute(in_slot, out_slot, False)

            @pl.when(half)
            def _():
                compute(in_slot, out_slot, True)

            out_copy_at(blk0 + k, out_slot).start(priority=BLOCK_COPY_PRIORITY)

            @pl.when(k + 2 < nblk)
            def _():
                in_copy_at(blk0 + k + 2, in_slot).start(priority=BLOCK_COPY_PRIORITY)

            return carry

        lax.fori_loop(0, nblk, body, 0)
        done_ref[0] = done + nblk

    first_unused = start_ref[ne - 1] + (cnt_ref[ne - 1] >> 1)

    @pl.when(e == 0)
    def _():
        tail_fill(first_unused, True)

    @pl.when(e == ne - 1)
    def _():
        tail_fill(first_unused, False)

    @pl.when(step < last_step)
    def _():
        _start_first_blocks(cnt_ref, start_ref, (e + 1) % ne, in_copy_at)

    @pl.when(step == last_step)
    def _():
        total = done + nblk

        @pl.when(total >= 2)
        def _():
            out_copy_at(0, total % 2).wait()

        @pl.when(total >= 1)
        def _():
            out_copy_at(0, (total - 1) % 2).wait()


def _zero_fill_blocks(first, nblocks, zbuf, dst_at, zsem, *, start):
    if start:
        zbuf[...] = jnp.zeros_like(zbuf)

    def step(b, carry):
        copy = pltpu.make_async_copy(zbuf, dst_at(b), zsem)
        if start:
            copy.start(priority=FILL_COPY_PRIORITY)
        else:
            copy.wait()
        return carry

    lax.fori_loop(first, nblocks, step, 0)


def _moe_up_kernel(cnt_ref, start_ref, xs_ref, w1_ref, w3_ref, h_ref, w1b_ref, w3b_ref, xbuf, obuf, xsem, osem,
                   done_ref, zbuf, zsem, *, bm):
    f = pl.program_id(0)
    d, fc = w1b_ref.shape
    nslab = d // 2 // LANES
    xrows = bm * nslab

    def x_copy(blk, slot):
        row0 = pl.multiple_of(blk * xrows, xrows)
        return pltpu.make_async_copy(xs_ref.at[pl.ds(row0, xrows)], xbuf.at[slot], xsem.at[slot])

    def h_copy(blk, slot):
        row0 = pl.multiple_of(blk * bm, bm)
        col0 = pl.multiple_of(f * fc, fc)
        return pltpu.make_async_copy(obuf.at[slot], h_ref.at[pl.ds(row0, bm), pl.ds(col0, fc)], osem.at[slot])

    def cast_weights():
        w1b_ref[...] = w1_ref[0].astype(BF16)
        w3b_ref[...] = w3_ref[0].astype(BF16)

    def compute(in_slot, out_slot, half):
        rows = bm // 2 if half else bm
        xin = xbuf.at[in_slot, pl.ds(0, rows * nslab)] if half else xbuf.at[in_slot]
        hi, lo = _unpack_halves(_load_row_slabs(xin, nslab, 0, nslab))
        hi = hi.astype(BF16)
        lo = lo.astype(BF16)
        d2 = d // 2
        a = (jnp.dot(hi, w1b_ref[:d2, :], preferred_element_type=F32)
             + jnp.dot(lo, w1b_ref[d2:, :], preferred_element_type=F32))
        c = (jnp.dot(hi, w3b_ref[:d2, :], preferred_element_type=F32)
             + jnp.dot(lo, w3b_ref[d2:, :], preferred_element_type=F32))
        res = (a * jax.nn.sigmoid(a) * c).astype(obuf.dtype)
        if half:
            obuf[out_slot, :rows] = res
            obuf[out_slot, rows:] = jnp.zeros((bm - rows, fc), obuf.dtype)
        else:
            obuf[out_slot] = res

    def tail_fill(first, start):
        col0 = pl.multiple_of(f * fc, fc)
        _zero_fill_blocks(first, h_ref.shape[0] // bm, zbuf,
                          lambda b: h_ref.at[pl.ds(pl.multiple_of(b * bm, bm), bm), pl.ds(col0, fc)], zsem,
                          start=start)

    _expert_grid_step(cnt_ref, start_ref, done_ref, x_copy, h_copy, cast_weights, compute, tail_fill)


def _moe_down_kernel(cnt_ref, start_ref, h_ref, w2_ref, y_ref, w2b_ref, hbuf, obuf, hsem, osem, done_ref, zbuf, zsem,
                     *, bm):
    nslab = w2b_ref.shape[1] // 2 // LANES

    def h_copy(blk, slot):
        row0 = pl.multiple_of(blk * bm, bm)
        return pltpu.make_async_copy(h_ref.at[pl.ds(row0, bm)], hbuf.at[slot], hsem.at[slot])

    def y_copy(blk, slot):
        row0 = pl.multiple_of(blk * bm * nslab, bm * nslab)
        return pltpu.make_async_copy(obuf.at[slot], y_ref.at[pl.ds(row0, bm * nslab)], osem.at[slot])

    def cast_weights():
        w2b_ref[...] = w2_ref[0].astype(BF16)

    def compute(in_slot, out_slot, half):
        rows = bm // 2 if half else bm
        hin = hbuf[in_slot, :rows] if half else hbuf[in_slot]
        packed = _pack_halves(jnp.dot(hin.astype(BF16), w2b_ref[...], preferred_element_type=F32))
        if half:
            _store_row_slabs(obuf.at[out_slot, pl.ds(0, rows * nslab)], nslab, packed)
            obuf[out_slot, rows * nslab:] = jnp.zeros(((bm - rows) * nslab, LANES), obuf.dtype)
        else:
            _store_row_slabs(obuf.at[out_slot], nslab, packed)

    def tail_fill(first, start):
        rows = bm * nslab
        _zero_fill_blocks(first, y_ref.shape[0] // rows, zbuf,
                          lambda b: y_ref.at[pl.ds(pl.multiple_of(b * rows, rows), rows)], zsem, start=start)

    _expert_grid_step(cnt_ref, start_ref, done_ref, h_copy, y_copy, cast_weights, compute, tail_fill)


def _moe_experts(xs, blk_cnt, blk_start, w1, w3, w2, *, bm, fc=512, nc=2048):
    p, nslab, _ = xs.shape
    d = 2 * nslab * LANES
    ne, _, de = w1.shape
    any_spec = pl.BlockSpec(memory_space=pltpu.HBM)
    blk_up = (4 * d * fc * 4 + 2 * d * fc * 2 + 2 * bm * nslab * LANES * 4 + 2 * bm * fc * 4
              + 2 * bm * d * 2 + 3 * bm * fc * 4)
    h = pl.pallas_call(
        functools.partial(_moe_up_kernel, bm=bm),
        grid_spec=pltpu.PrefetchScalarGridSpec(
            num_scalar_prefetch=2,
            grid=(de // fc, ne),
            in_specs=[any_spec,
                      pl.BlockSpec((1, d, fc), lambda f, e, cnt, start: (e, 0, f)),
                      pl.BlockSpec((1, d, fc), lambda f, e, cnt, start: (e, 0, f))],
            out_specs=any_spec,
            scratch_shapes=[pltpu.VMEM((d, fc), BF16), pltpu.VMEM((d, fc), BF16),
                            pltpu.VMEM((2, bm * nslab, LANES), U32), pltpu.VMEM((2, bm, fc), F32),
                            pltpu.SemaphoreType.DMA((2,)), pltpu.SemaphoreType.DMA((2,)),
                            pltpu.SMEM((1,), jnp.int32),
                            pltpu.VMEM((bm, fc), F32), pltpu.SemaphoreType.DMA(())],
        ),
        out_shape=jax.ShapeDtypeStruct((p, de), F32),
        compiler_params=_params(("arbitrary", "arbitrary"), _vmem_limit(blk_up)),
        name="moe_up",
    )(blk_cnt, blk_start, xs.reshape(p * nslab, LANES), w1, w3)
    assert nc == d, "expert-down writes whole row slabs"
    blk_dn = 2 * de * nc * 4 + de * nc * 2 + 3 * bm * de * 4 + 2 * bm * nslab * LANES * 4 + 2 * bm * nc * 4
    ys = pl.pallas_call(
        functools.partial(_moe_down_kernel, bm=bm),
        grid_spec=pltpu.PrefetchScalarGridSpec(
            num_scalar_prefetch=2,
            grid=(d // nc, ne),
            in_specs=[any_spec,
                      pl.BlockSpec((1, de, nc), lambda c, e, cnt, start: (e, 0, c))],
            out_specs=any_spec,
            scratch_shapes=[pltpu.VMEM((de, nc), BF16),
                            pltpu.VMEM((2, bm, de), F32), pltpu.VMEM((2, bm * nslab, LANES), U32),
                            pltpu.SemaphoreType.DMA((2,)), pltpu.SemaphoreType.DMA((2,)),
                            pltpu.SMEM((1,), jnp.int32),
                            pltpu.VMEM((bm * nslab, LANES), U32), pltpu.SemaphoreType.DMA(())],
        ),
        out_shape=jax.ShapeDtypeStruct((p * nslab, LANES), U32),
        compiler_params=_params(("arbitrary", "arbitrary"), _vmem_limit(blk_dn)),
        name="moe_down",
    )(blk_cnt, blk_start, h, w2)
    return ys.reshape(p, nslab, LANES)


def _combine_kernel(back_ref, x_ref, ys_ref, ew_ref, g_ref, o_ref, ybuf, sem, *, nc):
    i = pl.program_id(0)
    nt = pl.num_programs(0)
    tm, d = x_ref.shape
    t = nt * tm
    nslab = d // 2 // LANES
    slot = i % 2

    def row_at(s, k):
        def dst_at(r):
            return ybuf.at[s, pl.ds(pl.multiple_of((k * tm + r) * nslab, nslab), nslab)]
        return dst_at

    def start_tile(tile, s):
        for k in range(TOP_K):
            _start_row_copies(back_ref, k * t + tile * tm, tm, ys_ref, row_at(s, k), sem.at[s])

    @pl.when(i == 0)
    def _():
        start_tile(0, 0)

    @pl.when(i + 1 < nt)
    def _():
        start_tile(i + 1, 1 - slot)

    for k in range(TOP_K):
        _wait_row_copies(tm, ys_ref, row_at(slot, k), sem.at[slot])

    ew = ew_ref[...]
    w0 = ew[:, 0:1]
    w1 = ew[:, 1:2]
    half = nc // 2
    y0_ref = ybuf.at[slot, pl.ds(0, tm * nslab)]
    y1_ref = ybuf.at[slot, pl.ds(tm * nslab, tm * nslab)]
    pieces = []
    ss = None
    for c in range(x_ref.shape[1] // nc):
        hi0, lo0 = _unpack_halves(_load_row_slabs(y0_ref, nslab, c * half // LANES, half // LANES))
        hi1, lo1 = _unpack_halves(_load_row_slabs(y1_ref, nslab, c * half // LANES, half // LANES))
        for j, (p0, p1) in enumerate(((hi0, hi1), (lo0, lo1))):
            cols = slice(c * nc + j * half, c * nc + (j + 1) * half)
            z = x_ref[:, cols] + w0 * p0 + w1 * p1
            pieces.append((cols, z))
            s = jnp.sum(z * z, axis=-1, keepdims=True)
            ss = s if ss is None else ss + s
    inv = lax.rsqrt(ss / x_ref.shape[1] + EPS)
    for cols, z in pieces:
        o_ref[:, cols] = z * inv * g_ref[:, cols]


def _combine(x, ys, back, e_w, gain, *, nc, tm=256):
    t, d = x.shape
    nt = t // tm
    nslab = ys.shape[1]
    blk = 2 * (2 * tm * d * 4 + tm * LANES * 4 + d * 4) + 2 * TOP_K * tm * d * 2 + 3 * tm * d * 4
    return pl.pallas_call(
        functools.partial(_combine_kernel, nc=nc),
        grid_spec=pltpu.PrefetchScalarGridSpec(
            num_scalar_prefetch=1,
            grid=(nt,),
            in_specs=[pl.BlockSpec((tm, d), lambda i, back: (i, 0)),
                      pl.BlockSpec(memory_space=pltpu.HBM),
                      pl.BlockSpec((tm, TOP_K), lambda i, back: (i, 0)),
                      pl.BlockSpec((1, d), lambda i, back: (0, 0))],
            out_specs=pl.BlockSpec((tm, d), lambda i, back: (i, 0)),
            scratch_shapes=[pltpu.VMEM((2, TOP_K * tm * nslab, LANES), ys.dtype), pltpu.SemaphoreType.DMA((2,))],
        ),
        out_shape=jax.ShapeDtypeStruct((t, d), F32),
        compiler_params=_params(("arbitrary",), _vmem_limit(blk)),
        name="moe_combine",
    )(back, x, ys, e_w, gain.reshape(1, d))


def _route(logits, *, bm):
    t = logits.shape[0]
    g_logits = logits[:, :N_GROUPS]
    e_logits = logits[:, N_GROUPS:N_GROUPS + N_EXPERTS].reshape(t, N_GROUPS, EXPERTS_PER_GROUP)
    g_idx = jnp.argmax(g_logits, axis=-1).astype(jnp.int32)
    g_w = jnp.take_along_axis(jax.nn.softmax(g_logits, axis=-1), g_idx[:, None], axis=-1)
    e_sel = jnp.take_along_axis(e_logits, g_idx[:, None, None], axis=1)[:, 0]
    top_v, top_i = lax.top_k(e_sel, TOP_K)
    e_w = jax.nn.softmax(top_v, axis=-1) * g_w
    eid = g_idx[:, None] * EXPERTS_PER_GROUP + top_i.astype(jnp.int32)

    a = t * TOP_K
    flat_e = eid.reshape(a)
    onehot = (flat_e[:, None] == jnp.arange(N_EXPERTS, dtype=jnp.int32)[None, :]).astype(jnp.int32)
    csum = jnp.cumsum(onehot, axis=0)
    rank = jnp.take_along_axis(csum, flat_e[:, None], axis=1)[:, 0] - 1
    counts = csum[-1]
    padded = ((counts + bm - 1) // bm) * bm
    pad_end = jnp.cumsum(padded)
    start_pad = pad_end - padded
    dest = (start_pad[flat_e] + rank).astype(jnp.int32)
    tail_half = (counts > 0) & ((counts - 1) % bm < bm // 2)
    blk_cnt = (2 * (padded // bm) + tail_half).astype(jnp.int32)
    blk_start = (start_pad // bm).astype(jnp.int32)
    tail = jnp.arange(0, a + N_EXPERTS * bm, bm, dtype=jnp.int32)
    pad0 = jnp.concatenate([jnp.where(padded > 0, pad_end - bm, -1),
                            jnp.where(tail >= pad_end[-1], tail, -1)]).astype(jnp.int32)
    back = dest.reshape(t, TOP_K).T.reshape(a)
    return e_w, dest, back, pad0, blk_cnt, blk_start


MOE_BLOCK_ROWS = 256
MOE_DOWN_COLS = 4096


def kernel(x, mem, norm_mix, w_in, pool_w, pool_scale, lb_fwd, lb_bwd, hgrn_norm, w_out, norm_xattn, norm_mem,
           w_q, w_k, w_v, w_o, norm_moe, w_router_group, w_router_expert, w1, w3, w2, norm_final):
    bsz, seq, d = x.shape
    assert bsz == 1 and w_in.shape[0] == 1, "one sequence, one layer (the final norm is fused into the MoE combine)"
    l = 0
    pool_width = pool_w.shape[1] * pool_w.shape[2]
    hgrn_width = hgrn_norm.shape[1]
    xt = x.reshape(seq, d)
    mem_t = mem.reshape(mem.shape[1], d)
    h = _rmsnorm(xt, norm_mix[l], BF16)
    proj = _matmul([h], w_in[l], out_dtype=F32)
    y_pool = _pool_mixer(proj, pool_w[l], pool_scale[l])
    y_hgrn = _hgrn_mixer(proj, lb_fwd, lb_bwd, hgrn_norm[l], layer=l, width=hgrn_width, col0=pool_width)
    xt = _matmul([y_pool, y_hgrn], w_out[l], resid=xt, out_dtype=F32)
    mem_n = _rmsnorm(mem_t, norm_mem[l], BF16)
    hx = _rmsnorm(xt, norm_xattn[l], BF16)
    q = _matmul([hx], w_q[l], out_dtype=BF16)
    k = _matmul([mem_n], w_k[l], out_dtype=BF16)
    v = _matmul([mem_n], w_v[l], out_dtype=BF16)
    o = _xattn(q, k, v)
    xt = _matmul([o], w_o[l], resid=xt, out_dtype=F32)
    w_router = jnp.concatenate(
        [w_router_group[l],
         jnp.transpose(w_router_expert[l], (1, 0, 2)).reshape(d, N_EXPERTS),
         jnp.zeros((d, ROUTER_COLS - N_GROUPS - N_EXPERTS), F32)], axis=1)
    hp, logits = _router(xt, norm_moe[l], w_router)
    e_w, dest, back, pad0, blk_cnt, blk_start = _route(logits, bm=MOE_BLOCK_ROWS)
    xs = _dispatch_rows(dest, pad0, hp, p=TOP_K * seq + N_EXPERTS * MOE_BLOCK_ROWS, bm=MOE_BLOCK_ROWS)
    ys = _moe_experts(xs, blk_cnt, blk_start, w1[l], w3[l], w2[l], bm=MOE_BLOCK_ROWS, nc=MOE_DOWN_COLS)
    out = _combine(xt, ys, back, e_w, norm_final, nc=MOE_DOWN_COLS)
    return out.reshape(bsz, seq, d)
```

```python
import functools

import jax
import jax.numpy as jnp
from jax import lax
from jax.experimental import pallas as pl
from jax.experimental.pallas import tpu as pltpu

F32 = jnp.float32
BF16 = jnp.bfloat16
U32 = jnp.uint32

EPS = 1e-6
LANES = 128
SUBLANES = 8
VMEM_BYTES_V7X = 64 * 1024 * 1024

POOL_WINDOWS = (2, 4, 8, 16)
HEAD = 128
CHUNK = 64
XATTN_HEADS = 4
XATTN_SUBTILES = 4
N_GROUPS = 4
EXPERTS_PER_GROUP = 8
N_EXPERTS = N_GROUPS * EXPERTS_PER_GROUP
TOP_K = 2
ROUTER_COLS = LANES


def _params(sem, vmem_bytes):
    return pltpu.CompilerParams(dimension_semantics=sem, vmem_limit_bytes=int(vmem_bytes))


def _vmem_limit(block_bytes):
    return min(int(block_bytes * 1.25) + (6 << 20), VMEM_BYTES_V7X - (4 << 20))


def _rmsnorm_kernel(x_ref, g_ref, o_ref):
    x = x_ref[...]
    ms = jnp.mean(x * x, axis=-1, keepdims=True)
    o_ref[...] = (x * lax.rsqrt(ms + EPS) * g_ref[...]).astype(o_ref.dtype)


def _rmsnorm(x, gain, out_dtype, tm=512):
    t, d = x.shape
    tm = min(tm, t)
    blk = tm * d * (4 + jnp.dtype(out_dtype).itemsize) * 2
    return pl.pallas_call(
        _rmsnorm_kernel,
        grid=(t // tm,),
        in_specs=[pl.BlockSpec((tm, d), lambda i: (i, 0)),
                  pl.BlockSpec((1, d), lambda i: (0, 0))],
        out_specs=pl.BlockSpec((tm, d), lambda i: (i, 0)),
        out_shape=jax.ShapeDtypeStruct((t, d), out_dtype),
        compiler_params=_params(("arbitrary",), _vmem_limit(blk)),
        name="rmsnorm",
    )(x, gain.reshape(1, d))


def _mm_kernel(*refs, n_a, has_resid):
    a_refs = refs[:n_a]
    w_ref = refs[n_a]
    r_ref = refs[n_a + 1] if has_resid else None
    o_ref, wb_ref = refs[-2], refs[-1]

    @pl.when(pl.program_id(1) == 0)
    def _():
        wb_ref[...] = w_ref[...].astype(BF16)

    acc = None
    k0 = 0
    for a_ref in a_refs:
        kk = a_ref.shape[1]
        part = jnp.dot(a_ref[...], wb_ref[k0:k0 + kk, :], preferred_element_type=F32)
        acc = part if acc is None else acc + part
        k0 += kk
    if has_resid:
        acc = acc + r_ref[...]
    o_ref[...] = acc.astype(o_ref.dtype)


def _matmul(a_parts, w, *, resid=None, out_dtype=F32, tm=1024, tn=512):
    m = a_parts[0].shape[0]
    k, n = w.shape
    assert sum(a.shape[1] for a in a_parts) == k
    tm = min(tm, m)
    in_specs = [pl.BlockSpec((tm, a.shape[1]), lambda j, i: (i, 0)) for a in a_parts]
    in_specs.append(pl.BlockSpec((k, tn), lambda j, i: (0, j)))
    args = list(a_parts) + [w]
    if resid is not None:
        in_specs.append(pl.BlockSpec((tm, tn), lambda j, i: (i, j)))
        args.append(resid)
    osz = jnp.dtype(out_dtype).itemsize
    blk = (2 * tm * k * 2 + 2 * k * tn * 4 + k * tn * 2 + 2 * tm * tn * osz
           + (2 * tm * tn * 4 if resid is not None else 0) + tm * tn * 4)
    return pl.pallas_call(
        functools.partial(_mm_kernel, n_a=len(a_parts), has_resid=resid is not None),
        grid=(n // tn, m // tm),
        in_specs=in_specs,
        out_specs=pl.BlockSpec((tm, tn), lambda j, i: (i, j)),
        out_shape=jax.ShapeDtypeStruct((m, n), out_dtype),
        scratch_shapes=[pltpu.VMEM((k, tn), BF16)],
        compiler_params=_params(("arbitrary", "arbitrary"), _vmem_limit(blk)),
        name="matmul",
    )(*args)


def _pool_kernel(u_ref, up_ref, un_ref, w_ref, s_ref, o_ref, *, tm, seq, group):
    i = pl.program_id(0)
    last = pl.num_programs(0) - 1
    n = tm + 2 * SUBLANES
    row = lax.broadcasted_iota(jnp.int32, (tm, 1), 0) + i * tm
    for gi, w in enumerate(POOL_WINDOWS):
        cs = slice(gi * group, (gi + 1) * group)
        u = u_ref[:, cs]
        prev = jnp.where(i > 0, up_ref[:, cs], 0.0)
        nxt = jnp.where(i < last, un_ref[:, cs], 0.0)
        f = jnp.concatenate([prev, u, nxt], axis=0)
        step = 1
        while step < w:
            f = f + pltpu.roll(f, n - step, 0)
            step *= 2
        half = w // 2
        first = SUBLANES - half
        if first:
            f = pltpu.roll(f, n - first, 0)
        win = f[:tm]
        lo = jnp.maximum(row - half, 0)
        hi = jnp.minimum(row + half - 1, seq - 1)
        cnt = (hi - lo + 1).astype(F32)
        mixed = win / cnt - u
        y = jnp.dot(mixed.astype(BF16), w_ref[gi].astype(BF16), preferred_element_type=F32)
        o_ref[:, cs] = (y * s_ref[:, cs]).astype(o_ref.dtype)


def _pool_mixer(proj, pool_w, pool_scale, *, tm=512):
    t = proj.shape[0]
    ng, group, _ = pool_w.shape
    width = ng * group
    rpb = tm // SUBLANES
    nhalo = t // SUBLANES
    blk = 2 * (tm * width * 4 + 2 * SUBLANES * width * 4 + ng * group * group * 4 + tm * width * 2) + 6 * tm * group * 4
    return pl.pallas_call(
        functools.partial(_pool_kernel, tm=tm, seq=t, group=group),
        grid=(t // tm,),
        in_specs=[pl.BlockSpec((tm, width), lambda i: (i, 0)),
                  pl.BlockSpec((SUBLANES, width), lambda i: (jnp.maximum(i * rpb - 1, 0), 0)),
                  pl.BlockSpec((SUBLANES, width), lambda i: (jnp.minimum((i + 1) * rpb, nhalo - 1), 0)),
                  pl.BlockSpec((ng, group, group), lambda i: (0, 0, 0)),
                  pl.BlockSpec((1, width), lambda i: (0, 0))],
        out_specs=pl.BlockSpec((tm, width), lambda i: (i, 0)),
        out_shape=jax.ShapeDtypeStruct((t, width), BF16),
        compiler_params=_params(("arbitrary",), _vmem_limit(blk)),
        name="pool_mixer",
    )(proj, proj, proj, pool_w, pool_scale.reshape(1, width))


def _chunk_cumsum(x, reverse):
    n = x.shape[0]
    pos = lax.broadcasted_iota(jnp.int32, x.shape, 0) % CHUNK
    s = 1
    while s < CHUNK:
        if reverse:
            x = x + jnp.where(pos < CHUNK - s, pltpu.roll(x, n - s, 0), 0.0)
        else:
            x = x + jnp.where(pos >= s, pltpu.roll(x, s, 0), 0.0)
        s *= 2
    return x


def _lower_bound(lb_raw, layer):
    e = jnp.exp(lb_raw - jnp.max(lb_raw, axis=0, keepdims=True))
    return jnp.sum(e[:layer + 1], axis=0, keepdims=True) / jnp.sum(e, axis=0, keepdims=True)


def _hgrn_direction(q_raw, f_raw, v, lb, st_ref, h, mask, *, reverse):
    tb = q_raw.shape[0]
    nc = tb // CHUNK
    q = q_raw * jax.nn.sigmoid(q_raw)
    f = lb + (1.0 - lb) * jax.nn.sigmoid(f_raw)
    logf = jnp.log(f)
    k = 1.0 - f
    b = _chunk_cumsum(logf, reverse)
    b3 = b.reshape(nc, CHUNK, HEAD)
    edge = 0 if reverse else CHUNK - 1
    b_last = b3[:, edge:edge + 1, :]
    qd = (q * jnp.exp(b)).astype(BF16)
    kd = (k * jnp.exp(-b)).astype(BF16)
    k_end = (k.reshape(nc, CHUNK, HEAD) * jnp.exp(b_last - b3)).astype(BF16)
    decay = jnp.exp(b_last)
    vb = v.astype(BF16)

    a = lax.dot_general(qd, kd, (((1,), (1,)), ((), ())), preferred_element_type=F32)
    a = jnp.where(mask, a, 0.0)
    o_intra = jnp.dot(a.astype(BF16), vb, preferred_element_type=F32)

    st = st_ref[h]
    o_inter = [None] * nc
    order = range(nc - 1, -1, -1) if reverse else range(nc)
    for c in order:
        rows = slice(c * CHUNK, (c + 1) * CHUNK)
        o_inter[c] = lax.dot_general(qd[rows], st.astype(BF16), (((1,), (1,)), ((), ())),
                                     preferred_element_type=F32)
        d_st = lax.dot_general(vb[rows], k_end[c], (((0,), (0,)), ((), ())), preferred_element_type=F32)
        st = st * decay[c] + d_st
    st_ref[h] = st
    return o_intra + jnp.concatenate(o_inter, axis=0)


def _intra_mask(tb, reverse):
    r = lax.broadcasted_iota(jnp.int32, (tb, tb), 0)
    c = lax.broadcasted_iota(jnp.int32, (tb, tb), 1)
    same = (r // CHUNK) == (c // CHUNK)
    return same & ((c >= r) if reverse else (c <= r))


def _hgrn_fwd_kernel(q_ref, f_ref, i_ref, lb_ref, o_ref, st_ref, *, layer, heads):
    @pl.when(pl.program_id(0) == 0)
    def _():
        st_ref[...] = jnp.zeros_like(st_ref)

    mask = _intra_mask(q_ref.shape[0], False)

    for h in range(heads):
        hs = slice(h * HEAD, (h + 1) * HEAD)
        lb = _lower_bound(lb_ref[:, hs], layer)
        o_ref[:, hs] = _hgrn_direction(q_ref[:, hs], f_ref[:, hs], i_ref[:, hs], lb, st_ref, h, mask,
                                       reverse=False)


def _hgrn_bwd_kernel(q_ref, f_ref, i_ref, g_ref, of_ref, lb_ref, gain_ref, y_ref, st_ref, *, layer, heads):
    @pl.when(pl.program_id(0) == 0)
    def _():
        st_ref[...] = jnp.zeros_like(st_ref)

    mask = _intra_mask(q_ref.shape[0], True)

    for h in range(heads):
        hs = slice(h * HEAD, (h + 1) * HEAD)
        lb = _lower_bound(lb_ref[:, hs], layer)
        o = of_ref[:, hs] + _hgrn_direction(q_ref[:, hs], f_ref[:, hs], i_ref[:, hs], lb, st_ref, h, mask,
                                            reverse=True)
        o = o * lax.rsqrt(jnp.mean(o * o, axis=-1, keepdims=True) + EPS) * gain_ref[:, hs]
        g = g_ref[:, hs]
        y_ref[:, hs] = (o * (g * jax.nn.sigmoid(g))).astype(y_ref.dtype)


def _hgrn_mixer(proj, lb_fwd, lb_bwd, hgrn_norm, *, layer, width, col0, tb=256):
    t = proj.shape[0]
    heads = width // HEAD
    nb = t // tb
    c = col0 // width
    layers = lb_fwd.shape[0]
    blk_f = 2 * (3 * tb * width * 4 + layers * width * 4 + tb * width * 4) + heads * HEAD * HEAD * 4
    o_f = pl.pallas_call(
        functools.partial(_hgrn_fwd_kernel, layer=layer, heads=heads),
        grid=(nb,),
        in_specs=[pl.BlockSpec((tb, width), lambda b: (b, c)),
                  pl.BlockSpec((tb, width), lambda b: (b, c + 1)),
                  pl.BlockSpec((tb, width), lambda b: (b, c + 3)),
                  pl.BlockSpec((layers, width), lambda b: (0, 0))],
        out_specs=pl.BlockSpec((tb, width), lambda b: (b, 0)),
        out_shape=jax.ShapeDtypeStruct((t, width), F32),
        scratch_shapes=[pltpu.VMEM((heads, HEAD, HEAD), F32)],
        compiler_params=_params(("arbitrary",), _vmem_limit(blk_f)),
        name="hgrn_fwd",
    )(proj, proj, proj, lb_fwd)
    blk_b = 2 * (5 * tb * width * 4 + layers * width * 4 + width * 4 + tb * width * 2) + heads * HEAD * HEAD * 4
    return pl.pallas_call(
        functools.partial(_hgrn_bwd_kernel, layer=layer, heads=heads),
        grid=(nb,),
        in_specs=[pl.BlockSpec((tb, width), lambda b: (nb - 1 - b, c)),
                  pl.BlockSpec((tb, width), lambda b: (nb - 1 - b, c + 2)),
                  pl.BlockSpec((tb, width), lambda b: (nb - 1 - b, c + 3)),
                  pl.BlockSpec((tb, width), lambda b: (nb - 1 - b, c + 4)),
                  pl.BlockSpec((tb, width), lambda b: (nb - 1 - b, 0)),
                  pl.BlockSpec((layers, width), lambda b: (0, 0)),
                  pl.BlockSpec((1, width), lambda b: (0, 0))],
        out_specs=pl.BlockSpec((tb, width), lambda b: (nb - 1 - b, 0)),
        out_shape=jax.ShapeDtypeStruct((t, width), BF16),
        scratch_shapes=[pltpu.VMEM((heads, HEAD, HEAD), F32)],
        compiler_params=_params(("arbitrary",), _vmem_limit(blk_b)),
        name="hgrn_bwd",
    )(proj, proj, proj, proj, o_f, lb_bwd, hgrn_norm.reshape(1, width))


def _xattn_kernel(q_ref, k_ref, v_ref, o_ref, *, scale):
    sub = q_ref.shape[0] // XATTN_SUBTILES
    for r in range(XATTN_SUBTILES):
        rows = slice(r * sub, (r + 1) * sub)
        s = lax.dot_general(q_ref[rows, :], k_ref[...], (((1,), (1,)), ((), ())), preferred_element_type=F32) * scale
        s = s - jnp.max(s, axis=-1, keepdims=True)
        p = jnp.exp(s)
        p = p / jnp.sum(p, axis=-1, keepdims=True)
        o_ref[rows, :] = jnp.dot(p.astype(BF16), v_ref[...], preferred_element_type=F32).astype(o_ref.dtype)


def _xattn(q, k, v, *, tm=1024):
    t, d = q.shape
    m = k.shape[0]
    dh = d // XATTN_HEADS
    blk = 2 * (2 * tm * dh * 2 + 2 * m * dh * 2) + 4 * tm * m * 4 + tm * dh * 4
    return pl.pallas_call(
        functools.partial(_xattn_kernel, scale=dh ** -0.5),
        grid=(XATTN_HEADS, t // tm),
        in_specs=[pl.BlockSpec((tm, dh), lambda h, i: (i, h)),
                  pl.BlockSpec((m, dh), lambda h, i: (0, h)),
                  pl.BlockSpec((m, dh), lambda h, i: (0, h))],
        out_specs=pl.BlockSpec((tm, dh), lambda h, i: (i, h)),
        out_shape=jax.ShapeDtypeStruct((t, d), BF16),
        compiler_params=_params(("arbitrary", "arbitrary"), _vmem_limit(blk)),
        name="xattn",
    )(q, k, v)


def _pack_halves(x):
    w = x.shape[1] // 2
    hi = lax.bitcast_convert_type(x[:, :w].astype(BF16).astype(F32), U32)
    lo = lax.bitcast_convert_type(x[:, w:].astype(BF16).astype(F32), U32)
    return hi | (lo >> 16)


def _unpack_halves(p):
    hi = lax.bitcast_convert_type(p & jnp.uint32(0xFFFF0000), F32)
    lo = lax.bitcast_convert_type(p << 16, F32)
    return hi, lo


def _load_row_slabs(flat_ref, s, j0, nj):
    rows = flat_ref.shape[0] // s
    return jnp.concatenate([flat_ref[pl.ds(j0 + j, rows, stride=s), :] for j in range(nj)], axis=1)


def _store_row_slabs(flat_ref, s, val):
    rows = flat_ref.shape[0] // s
    for j in range(s):
        flat_ref[pl.ds(j, rows, stride=s), :] = val[:, j * LANES:(j + 1) * LANES]


def _router_kernel(x_ref, g_ref, wr_ref, hp_ref, lg_ref, whi_ref, wlo_ref):
    @pl.when(pl.program_id(0) == 0)
    def _():
        w = wr_ref[...]
        w_hi = w.astype(BF16)
        whi_ref[...] = w_hi
        wlo_ref[...] = (w - w_hi.astype(F32)).astype(BF16)

    x = x_ref[...]
    ms = jnp.mean(x * x, axis=-1, keepdims=True)
    h = x * lax.rsqrt(ms + EPS) * g_ref[...]
    _store_row_slabs(hp_ref, hp_ref.shape[0] // x_ref.shape[0], _pack_halves(h))
    h_hi = h.astype(BF16)
    h_lo = (h - h_hi.astype(F32)).astype(BF16)
    lg_ref[...] = (jnp.dot(h_hi, whi_ref[...], preferred_element_type=F32)
                   + (jnp.dot(h_hi, wlo_ref[...], preferred_element_type=F32)
                      + jnp.dot(h_lo, whi_ref[...], preferred_element_type=F32)))


def _router(x, gain, w_router, *, tm=256):
    t, d = x.shape
    nslab = d // 2 // LANES
    blk = 2 * (tm * d * 4 + d * 4 + d * ROUTER_COLS * 4 + tm * d * 2 + tm * ROUTER_COLS * 4) + 3 * tm * d * 4
    hp, logits = pl.pallas_call(
        _router_kernel,
        grid=(t // tm,),
        in_specs=[pl.BlockSpec((tm, d), lambda i: (i, 0)),
                  pl.BlockSpec((1, d), lambda i: (0, 0)),
                  pl.BlockSpec((d, ROUTER_COLS), lambda i: (0, 0))],
        out_specs=[pl.BlockSpec((tm * nslab, LANES), lambda i: (i, 0)),
                   pl.BlockSpec((tm, ROUTER_COLS), lambda i: (i, 0))],
        out_shape=[jax.ShapeDtypeStruct((t * nslab, LANES), U32),
                   jax.ShapeDtypeStruct((t, ROUTER_COLS), F32)],
        scratch_shapes=[pltpu.VMEM((d, ROUTER_COLS), BF16), pltpu.VMEM((d, ROUTER_COLS), BF16)],
        compiler_params=_params(("arbitrary",), _vmem_limit(blk)),
        name="moe_router",
    )(x, gain.reshape(1, d), w_router)
    return hp.reshape(t, nslab, LANES), logits


def _start_row_copies(idx_ref, base, n, src_ref, dst_at, sem):
    def issue(r, carry):
        pltpu.make_async_copy(src_ref.at[idx_ref[base + r]], dst_at(r), sem).start()
        return carry

    lax.fori_loop(0, n, issue, 0, unroll=8)


def _wait_row_copies(n, src_ref, dst_at, sem):
    def drain(r, carry):
        pltpu.make_async_copy(src_ref.at[0], dst_at(r), sem).wait()
        return carry

    lax.fori_loop(0, n, drain, 0, unroll=8)


def _dispatch_rows_kernel(dest_ref, pad0_ref, src_ref, xs_ref, buf, zeros, sem, zsem, *, bm):
    i = pl.program_id(0)
    nt = pl.num_programs(0)
    tm = src_ref.shape[0]
    slot = i % 2

    def fill_copy(e):
        row0 = pl.multiple_of(jnp.maximum(pad0_ref[e], 0), bm)
        return pltpu.make_async_copy(zeros, xs_ref.at[pl.ds(row0, bm)], zsem)

    @pl.when(i == 0)
    def _():
        zeros[...] = jnp.zeros_like(zeros)

        def fill(e, carry):
            @pl.when(pad0_ref[e] >= 0)
            def _():
                fill_copy(e).start()

            return carry

        lax.fori_loop(0, pad0_ref.shape[0], fill, 0)

        def fill_wait(e, carry):
            @pl.when(pad0_ref[e] >= 0)
            def _():
                fill_copy(e).wait()

            return carry

        lax.fori_loop(0, pad0_ref.shape[0], fill_wait, 0)

    def wait_tile(s):
        def drain(r, carry):
            for _ in range(TOP_K):
                pltpu.make_async_copy(buf.at[s, 0], xs_ref.at[0], sem.at[s]).wait()
            return carry

        lax.fori_loop(0, tm, drain, 0, unroll=4)

    @pl.when(i >= 2)
    def _():
        wait_tile(slot)

    buf[slot] = src_ref[...]

    def issue(r, carry):
        for k in range(TOP_K):
            row = dest_ref[(i * tm + r) * TOP_K + k]
            pltpu.make_async_copy(buf.at[slot, r], xs_ref.at[row], sem.at[slot]).start()
        return carry

    lax.fori_loop(0, tm, issue, 0, unroll=4)

    @pl.when(i == nt - 1)
    def _():
        wait_tile(slot)

        @pl.when(nt >= 2)
        def _():
            wait_tile(1 - slot)


def _dispatch_rows(dest, pad0, src, *, p, bm, tm=256):
    t = src.shape[0]
    slab = src.shape[1:]
    blk = 5 * tm * slab[0] * slab[1] * 4 + bm * slab[0] * slab[1] * 4
    return pl.pallas_call(
        functools.partial(_dispatch_rows_kernel, bm=bm),
        grid_spec=pltpu.PrefetchScalarGridSpec(
            num_scalar_prefetch=2,
            grid=(t // tm,),
            in_specs=[pl.BlockSpec((tm,) + slab, lambda i, dest, pad0: (i, 0, 0))],
            out_specs=pl.BlockSpec(memory_space=pltpu.HBM),
            scratch_shapes=[pltpu.VMEM((2, tm) + slab, src.dtype), pltpu.VMEM((bm,) + slab, src.dtype),
                            pltpu.SemaphoreType.DMA((2,)), pltpu.SemaphoreType.DMA(())],
        ),
        out_shape=jax.ShapeDtypeStruct((p,) + slab, src.dtype),
        compiler_params=_params(("arbitrary",), _vmem_limit(blk)),
        name="dispatch_rows",
    )(dest, pad0, src)


BLOCK_COPY_PRIORITY = 0
UP_GROUP_SLABS = 2


def _start_first_blocks(cnt_ref, start_ref, e, in_copy_at):
    n = cnt_ref[e] >> 1
    b = start_ref[e]

    @pl.when(n > 0)
    def _():
        in_copy_at(b, 0).start(priority=BLOCK_COPY_PRIORITY)

    @pl.when(n > 1)
    def _():
        in_copy_at(b + 1, 1).start(priority=BLOCK_COPY_PRIORITY)


def _expert_grid_step(cnt_ref, start_ref, done_ref, in_copy_at, out_copy_at, prologue, compute, tail_fill):
    e = pl.program_id(1)
    ne = pl.num_programs(1)
    step = pl.program_id(0) * ne + e
    last_step = pl.num_programs(0) * ne - 1
    nblk = cnt_ref[e] >> 1
    tail_half = (cnt_ref[e] & 1) == 1
    blk0 = start_ref[e]

    @pl.when(step == 0)
    def _():
        done_ref[0] = 0
        _start_first_blocks(cnt_ref, start_ref, e, in_copy_at)

    done = done_ref[0]

    @pl.when(nblk > 0)
    def _():
        prologue()

        def body(k, carry):
            in_slot = k % 2
            out_slot = (done + k) % 2
            in_copy_at(blk0 + k, in_slot).wait()

            @pl.when(done + k >= 2)
            def _():
                out_copy_at(0, out_slot).wait()

            half = tail_half & (k == nblk - 1)

            @pl.when(jnp.logical_not(half))
            def _():
                compute(in_slot, out_slot, False)

            @pl.when(half)
            def _():
                compute(in_slot, out_slot, True)

            out_copy_at(blk0 + k, out_slot).start(priority=BLOCK_COPY_PRIORITY)

            @pl.when(k + 2 < nblk)
            def _():
                in_copy_at(blk0 + k + 2, in_slot).start(priority=BLOCK_COPY_PRIORITY)

            return carry

        lax.fori_loop(0, nblk, body, 0)
        done_ref[0] = done + nblk

    @pl.when(e == ne - 1)
    def _():
        tail_fill(start_ref[ne - 1] + (cnt_ref[ne - 1] >> 1))

    @pl.when(step < last_step)
    def _():
        _start_first_blocks(cnt_ref, start_ref, (e + 1) % ne, in_copy_at)

    @pl.when(step == last_step)
    def _():
        total = done + nblk

        @pl.when(total >= 2)
        def _():
            out_copy_at(0, total % 2).wait()

        @pl.when(total >= 1)
        def _():
            out_copy_at(0, (total - 1) % 2).wait()


def _zero_fill_blocks(first, nblocks, zbuf, dst_at, zsem):
    zbuf[...] = jnp.zeros_like(zbuf)

    def fill(b, carry):
        pltpu.make_async_copy(zbuf, dst_at(b), zsem).start()
        return carry

    lax.fori_loop(first, nblocks, fill, 0)

    def fill_wait(b, carry):
        pltpu.make_async_copy(zbuf, dst_at(b), zsem).wait()
        return carry

    lax.fori_loop(first, nblocks, fill_wait, 0)


def _moe_up_kernel(cnt_ref, start_ref, xs_ref, w1_ref, w3_ref, h_ref, w1b_ref, w3b_ref, xbuf, obuf, xsem, osem,
                   done_ref, zbuf, zsem, *, bm):
    f = pl.program_id(0)
    d, fc = w1b_ref.shape
    nslab = d // 2 // LANES
    xrows = bm * nslab

    def x_copy(blk, slot):
        row0 = pl.multiple_of(blk * xrows, xrows)
        return pltpu.make_async_copy(xs_ref.at[pl.ds(row0, xrows)], xbuf.at[slot], xsem.at[slot])

    def h_copy(blk, slot):
        row0 = pl.multiple_of(blk * bm, bm)
        col0 = pl.multiple_of(f * fc, fc)
        return pltpu.make_async_copy(obuf.at[slot], h_ref.at[pl.ds(row0, bm), pl.ds(col0, fc)], osem.at[slot])

    def cast_weights():
        w1b_ref[...] = w1_ref[0].astype(BF16)
        w3b_ref[...] = w3_ref[0].astype(BF16)

    def compute(in_slot, out_slot, half):
        rows = bm // 2 if half else bm
        xin = xbuf.at[in_slot, pl.ds(0, rows * nslab)] if half else xbuf.at[in_slot]
        d2 = d // 2
        a = c = None
        for j0 in range(0, nslab, UP_GROUP_SLABS):
            hi, lo = _unpack_halves(_load_row_slabs(xin, nslab, j0, UP_GROUP_SLABS))
            hi = hi.astype(BF16)
            lo = lo.astype(BF16)
            k0, k1 = j0 * LANES, (j0 + UP_GROUP_SLABS) * LANES
            pa = (jnp.dot(hi, w1b_ref[k0:k1, :], preferred_element_type=F32)
                  + jnp.dot(lo, w1b_ref[d2 + k0:d2 + k1, :], preferred_element_type=F32))
            pc = (jnp.dot(hi, w3b_ref[k0:k1, :], preferred_element_type=F32)
                  + jnp.dot(lo, w3b_ref[d2 + k0:d2 + k1, :], preferred_element_type=F32))
            a = pa if a is None else a + pa
            c = pc if c is None else c + pc
        res = (a * jax.nn.sigmoid(a) * c).astype(obuf.dtype)
        if half:
            obuf[out_slot, :rows] = res
            obuf[out_slot, rows:] = jnp.zeros((bm - rows, fc), obuf.dtype)
        else:
            obuf[out_slot] = res

    def tail_fill(first):
        col0 = pl.multiple_of(f * fc, fc)
        _zero_fill_blocks(first, h_ref.shape[0] // bm, zbuf,
                          lambda b: h_ref.at[pl.ds(pl.multiple_of(b * bm, bm), bm), pl.ds(col0, fc)], zsem)

    _expert_grid_step(cnt_ref, start_ref, done_ref, x_copy, h_copy, cast_weights, compute, tail_fill)


def _moe_down_kernel(cnt_ref, start_ref, h_ref, w2_ref, y_ref, w2b_ref, hbuf, obuf, hsem, osem, done_ref, zbuf, zsem,
                     *, bm):
    nslab = w2b_ref.shape[1] // 2 // LANES

    def h_copy(blk, slot):
        row0 = pl.multiple_of(blk * bm, bm)
        return pltpu.make_async_copy(h_ref.at[pl.ds(row0, bm)], hbuf.at[slot], hsem.at[slot])

    def y_copy(blk, slot):
        row0 = pl.multiple_of(blk * bm * nslab, bm * nslab)
        return pltpu.make_async_copy(obuf.at[slot], y_ref.at[pl.ds(row0, bm * nslab)], osem.at[slot])

    def cast_weights():
        w2b_ref[...] = w2_ref[0].astype(BF16)

    def compute(in_slot, out_slot, half):
        rows = bm // 2 if half else bm
        hin = hbuf[in_slot, :rows] if half else hbuf[in_slot]
        packed = _pack_halves(jnp.dot(hin.astype(BF16), w2b_ref[...], preferred_element_type=F32))
        if half:
            _store_row_slabs(obuf.at[out_slot, pl.ds(0, rows * nslab)], nslab, packed)
            obuf[out_slot, rows * nslab:] = jnp.zeros(((bm - rows) * nslab, LANES), obuf.dtype)
        else:
            _store_row_slabs(obuf.at[out_slot], nslab, packed)

    def tail_fill(first):
        rows = bm * nslab
        _zero_fill_blocks(first, y_ref.shape[0] // rows, zbuf,
                          lambda b: y_ref.at[pl.ds(pl.multiple_of(b * rows, rows), rows)], zsem)

    _expert_grid_step(cnt_ref, start_ref, done_ref, h_copy, y_copy, cast_weights, compute, tail_fill)


def _moe_experts(xs, blk_cnt, blk_start, w1, w3, w2, *, bm, fc=512, nc=2048):
    p, nslab, _ = xs.shape
    d = 2 * nslab * LANES
    ne, _, de = w1.shape
    any_spec = pl.BlockSpec(memory_space=pltpu.HBM)
    blk_up = (4 * d * fc * 4 + 2 * d * fc * 2 + 2 * bm * nslab * LANES * 4 + 2 * bm * fc * 4
              + 2 * bm * d * 2 + 3 * bm * fc * 4)
    h = pl.pallas_call(
        functools.partial(_moe_up_kernel, bm=bm),
        grid_spec=pltpu.PrefetchScalarGridSpec(
            num_scalar_prefetch=2,
            grid=(de // fc, ne),
            in_specs=[any_spec,
                      pl.BlockSpec((1, d, fc), lambda f, e, cnt, start: (e, 0, f)),
                      pl.BlockSpec((1, d, fc), lambda f, e, cnt, start: (e, 0, f))],
            out_specs=any_spec,
            scratch_shapes=[pltpu.VMEM((d, fc), BF16), pltpu.VMEM((d, fc), BF16),
                            pltpu.VMEM((2, bm * nslab, LANES), U32), pltpu.VMEM((2, bm, fc), F32),
                            pltpu.SemaphoreType.DMA((2,)), pltpu.SemaphoreType.DMA((2,)),
                            pltpu.SMEM((1,), jnp.int32),
                            pltpu.VMEM((bm, fc), F32), pltpu.SemaphoreType.DMA(())],
        ),
        out_shape=jax.ShapeDtypeStruct((p, de), F32),
        compiler_params=_params(("arbitrary", "arbitrary"), _vmem_limit(blk_up)),
        name="moe_up",
    )(blk_cnt, blk_start, xs.reshape(p * nslab, LANES), w1, w3)
    assert nc == d, "expert-down writes whole row slabs"
    blk_dn = 2 * de * nc * 4 + de * nc * 2 + 3 * bm * de * 4 + 2 * bm * nslab * LANES * 4 + 2 * bm * nc * 4
    ys = pl.pallas_call(
        functools.partial(_moe_down_kernel, bm=bm),
        grid_spec=pltpu.PrefetchScalarGridSpec(
            num_scalar_prefetch=2,
            grid=(d // nc, ne),
            in_specs=[any_spec,
                      pl.BlockSpec((1, de, nc), lambda c, e, cnt, start: (e, 0, c))],
            out_specs=any_spec,
            scratch_shapes=[pltpu.VMEM((de, nc), BF16),
                            pltpu.VMEM((2, bm, de), F32), pltpu.VMEM((2, bm * nslab, LANES), U32),
                            pltpu.SemaphoreType.DMA((2,)), pltpu.SemaphoreType.DMA((2,)),
                            pltpu.SMEM((1,), jnp.int32),
                            pltpu.VMEM((bm * nslab, LANES), U32), pltpu.SemaphoreType.DMA(())],
        ),
        out_shape=jax.ShapeDtypeStruct((p * nslab, LANES), U32),
        compiler_params=_params(("arbitrary", "arbitrary"), _vmem_limit(blk_dn)),
        name="moe_down",
    )(blk_cnt, blk_start, h, w2)
    return ys.reshape(p, nslab, LANES)


def _combine_kernel(back_ref, x_ref, ys_ref, ew_ref, g_ref, o_ref, ybuf, sem, *, nc):
    i = pl.program_id(0)
    nt = pl.num_programs(0)
    tm, d = x_ref.shape
    t = nt * tm
    nslab = d // 2 // LANES
    slot = i % 2

    def row_at(s, k):
        def dst_at(r):
            return ybuf.at[s, pl.ds(pl.multiple_of((k * tm + r) * nslab, nslab), nslab)]
        return dst_at

    def start_tile(tile, s):
        for k in range(TOP_K):
            _start_row_copies(back_ref, k * t + tile * tm, tm, ys_ref, row_at(s, k), sem.at[s])

    @pl.when(i == 0)
    def _():
        start_tile(0, 0)

    @pl.when(i + 1 < nt)
    def _():
        start_tile(i + 1, 1 - slot)

    for k in range(TOP_K):
        _wait_row_copies(tm, ys_ref, row_at(slot, k), sem.at[slot])

    ew = ew_ref[...]
    w0 = ew[:, 0:1]
    w1 = ew[:, 1:2]
    half = nc // 2
    y0_ref = ybuf.at[slot, pl.ds(0, tm * nslab)]
    y1_ref = ybuf.at[slot, pl.ds(tm * nslab, tm * nslab)]
    pieces = []
    ss = None
    for c in range(x_ref.shape[1] // nc):
        hi0, lo0 = _unpack_halves(_load_row_slabs(y0_ref, nslab, c * half // LANES, half // LANES))
        hi1, lo1 = _unpack_halves(_load_row_slabs(y1_ref, nslab, c * half // LANES, half // LANES))
        for j, (p0, p1) in enumerate(((hi0, hi1), (lo0, lo1))):
            cols = slice(c * nc + j * half, c * nc + (j + 1) * half)
            z = x_ref[:, cols] + w0 * p0 + w1 * p1
            pieces.append((cols, z))
            s = jnp.sum(z * z, axis=-1, keepdims=True)
            ss = s if ss is None else ss + s
    inv = lax.rsqrt(ss / x_ref.shape[1] + EPS)
    for cols, z in pieces:
        o_ref[:, cols] = z * inv * g_ref[:, cols]


def _combine(x, ys, back, e_w, gain, *, nc, tm=256):
    t, d = x.shape
    nt = t // tm
    nslab = ys.shape[1]
    blk = 2 * (2 * tm * d * 4 + tm * LANES * 4 + d * 4) + 2 * TOP_K * tm * d * 2 + 3 * tm * d * 4
    return pl.pallas_call(
        functools.partial(_combine_kernel, nc=nc),
        grid_spec=pltpu.PrefetchScalarGridSpec(
            num_scalar_prefetch=1,
            grid=(nt,),
            in_specs=[pl.BlockSpec((tm, d), lambda i, back: (i, 0)),
                      pl.BlockSpec(memory_space=pltpu.HBM),
                      pl.BlockSpec((tm, TOP_K), lambda i, back: (i, 0)),
                      pl.BlockSpec((1, d), lambda i, back: (0, 0))],
            out_specs=pl.BlockSpec((tm, d), lambda i, back: (i, 0)),
            scratch_shapes=[pltpu.VMEM((2, TOP_K * tm * nslab, LANES), ys.dtype), pltpu.SemaphoreType.DMA((2,))],
        ),
        out_shape=jax.ShapeDtypeStruct((t, d), F32),
        compiler_params=_params(("arbitrary",), _vmem_limit(blk)),
        name="moe_combine",
    )(back, x, ys, e_w, gain.reshape(1, d))


def _route(logits, *, bm):
    t = logits.shape[0]
    g_logits = logits[:, :N_GROUPS]
    e_logits = logits[:, N_GROUPS:N_GROUPS + N_EXPERTS].reshape(t, N_GROUPS, EXPERTS_PER_GROUP)
    g_idx = jnp.argmax(g_logits, axis=-1).astype(jnp.int32)
    g_w = jnp.take_along_axis(jax.nn.softmax(g_logits, axis=-1), g_idx[:, None], axis=-1)
    e_sel = jnp.take_along_axis(e_logits, g_idx[:, None, None], axis=1)[:, 0]
    top_v, top_i = lax.top_k(e_sel, TOP_K)
    e_w = jax.nn.softmax(top_v, axis=-1) * g_w
    eid = g_idx[:, None] * EXPERTS_PER_GROUP + top_i.astype(jnp.int32)

    a = t * TOP_K
    flat_e = eid.reshape(a)
    onehot = (flat_e[:, None] == jnp.arange(N_EXPERTS, dtype=jnp.int32)[None, :]).astype(jnp.int32)
    csum = jnp.cumsum(onehot, axis=0)
    rank = jnp.take_along_axis(csum, flat_e[:, None], axis=1)[:, 0] - 1
    counts = csum[-1]
    padded = ((counts + bm - 1) // bm) * bm
    pad_end = jnp.cumsum(padded)
    start_pad = pad_end - padded
    dest = (start_pad[flat_e] + rank).astype(jnp.int32)
    tail_half = (counts > 0) & ((counts - 1) % bm < bm // 2)
    blk_cnt = (2 * (padded // bm) + tail_half).astype(jnp.int32)
    blk_start = (start_pad // bm).astype(jnp.int32)
    tail = jnp.arange(0, a + N_EXPERTS * bm, bm, dtype=jnp.int32)
    pad0 = jnp.concatenate([jnp.where(padded > 0, pad_end - bm, -1),
                            jnp.where(tail >= pad_end[-1], tail, -1)]).astype(jnp.int32)
    back = dest.reshape(t, TOP_K).T.reshape(a)
    return e_w, dest, back, pad0, blk_cnt, blk_start


MOE_BLOCK_ROWS = 256
MOE_DOWN_COLS = 4096


def kernel(x, mem, norm_mix, w_in, pool_w, pool_scale, lb_fwd, lb_bwd, hgrn_norm, w_out, norm_xattn, norm_mem,
           w_q, w_k, w_v, w_o, norm_moe, w_router_group, w_router_expert, w1, w3, w2, norm_final):
    bsz, seq, d = x.shape
    assert bsz == 1 and w_in.shape[0] == 1, "one sequence, one layer (the final norm is fused into the MoE combine)"
    l = 0
    pool_width = pool_w.shape[1] * pool_w.shape[2]
    hgrn_width = hgrn_norm.shape[1]
    xt = x.reshape(seq, d)
    mem_t = mem.reshape(mem.shape[1], d)
    h = _rmsnorm(xt, norm_mix[l], BF16)
    proj = _matmul([h], w_in[l], out_dtype=F32)
    y_pool = _pool_mixer(proj, pool_w[l], pool_scale[l])
    y_hgrn = _hgrn_mixer(proj, lb_fwd, lb_bwd, hgrn_norm[l], layer=l, width=hgrn_width, col0=pool_width)
    xt = _matmul([y_pool, y_hgrn], w_out[l], resid=xt, out_dtype=F32)
    mem_n = _rmsnorm(mem_t, norm_mem[l], BF16)
    hx = _rmsnorm(xt, norm_xattn[l], BF16)
    q = _matmul([hx], w_q[l], out_dtype=BF16)
    k = _matmul([mem_n], w_k[l], out_dtype=BF16)
    v = _matmul([mem_n], w_v[l], out_dtype=BF16)
    o = _xattn(q, k, v)
    xt = _matmul([o], w_o[l], resid=xt, out_dtype=F32)
    w_router = jnp.concatenate(
        [w_router_group[l],
         jnp.transpose(w_router_expert[l], (1, 0, 2)).reshape(d, N_EXPERTS),
         jnp.zeros((d, ROUTER_COLS - N_GROUPS - N_EXPERTS), F32)], axis=1)
    hp, logits = _router(xt, norm_moe[l], w_router)
    e_w, dest, back, pad0, blk_cnt, blk_start = _route(logits, bm=MOE_BLOCK_ROWS)
    xs = _dispatch_rows(dest, pad0, hp, p=TOP_K * seq + N_EXPERTS * MOE_BLOCK_ROWS, bm=MOE_BLOCK_ROWS)
    ys = _moe_experts(xs, blk_cnt, blk_start, w1[l], w3[l], w2[l], bm=MOE_BLOCK_ROWS, nc=MOE_DOWN_COLS)
    out = _combine(xt, ys, back, e_w, norm_final, nc=MOE_DOWN_COLS)
    return out.reshape(bsz, seq, d)
```

```python
import functools

import jax
import jax.numpy as jnp
from jax import lax
from jax.experimental import pallas as pl
from jax.experimental.pallas import tpu as pltpu

F32 = jnp.float32
BF16 = jnp.bfloat16
U32 = jnp.uint32

EPS = 1e-6
LANES = 128
SUBLANES = 8
VMEM_BYTES_V7X = 64 * 1024 * 1024

POOL_WINDOWS = (2, 4, 8, 16)
HEAD = 128
CHUNK = 64
XATTN_HEADS = 4
N_GROUPS = 4
EXPERTS_PER_GROUP = 8
N_EXPERTS = N_GROUPS * EXPERTS_PER_GROUP
TOP_K = 2
ROUTER_COLS = LANES


def _params(sem, vmem_bytes):
    return pltpu.CompilerParams(dimension_semantics=sem, vmem_limit_bytes=int(vmem_bytes))


def _vmem_limit(block_bytes):
    return min(int(block_bytes * 1.25) + (6 << 20), VMEM_BYTES_V7X - (4 << 20))


def _rmsnorm_kernel(x_ref, g_ref, o_ref):
    x = x_ref[...]
    ms = jnp.mean(x * x, axis=-1, keepdims=True)
    o_ref[...] = (x * lax.rsqrt(ms + EPS) * g_ref[...]).astype(o_ref.dtype)


def _rmsnorm(x, gain, out_dtype, tm=512):
    t, d = x.shape
    tm = min(tm, t)
    blk = tm * d * (4 + jnp.dtype(out_dtype).itemsize) * 2
    return pl.pallas_call(
        _rmsnorm_kernel,
        grid=(t // tm,),
        in_specs=[pl.BlockSpec((tm, d), lambda i: (i, 0)),
                  pl.BlockSpec((1, d), lambda i: (0, 0))],
        out_specs=pl.BlockSpec((tm, d), lambda i: (i, 0)),
        out_shape=jax.ShapeDtypeStruct((t, d), out_dtype),
        compiler_params=_params(("arbitrary",), _vmem_limit(blk)),
        name="rmsnorm",
    )(x, gain.reshape(1, d))


def _mm_kernel(*refs, n_a, has_resid):
    a_refs = refs[:n_a]
    w_ref = refs[n_a]
    r_ref = refs[n_a + 1] if has_resid else None
    o_ref, wb_ref = refs[-2], refs[-1]

    @pl.when(pl.program_id(1) == 0)
    def _():
        wb_ref[...] = w_ref[...].astype(BF16)

    acc = None
    k0 = 0
    for a_ref in a_refs:
        kk = a_ref.shape[1]
        part = jnp.dot(a_ref[...], wb_ref[k0:k0 + kk, :], preferred_element_type=F32)
        acc = part if acc is None else acc + part
        k0 += kk
    if has_resid:
        acc = acc + r_ref[...]
    o_ref[...] = acc.astype(o_ref.dtype)


def _matmul(a_parts, w, *, resid=None, out_dtype=F32, tm=1024, tn=512):
    m = a_parts[0].shape[0]
    k, n = w.shape
    assert sum(a.shape[1] for a in a_parts) == k
    tm = min(tm, m)
    in_specs = [pl.BlockSpec((tm, a.shape[1]), lambda j, i: (i, 0)) for a in a_parts]
    in_specs.append(pl.BlockSpec((k, tn), lambda j, i: (0, j)))
    args = list(a_parts) + [w]
    if resid is not None:
        in_specs.append(pl.BlockSpec((tm, tn), lambda j, i: (i, j)))
        args.append(resid)
    osz = jnp.dtype(out_dtype).itemsize
    blk = (2 * tm * k * 2 + 2 * k * tn * 4 + k * tn * 2 + 2 * tm * tn * osz
           + (2 * tm * tn * 4 if resid is not None else 0) + tm * tn * 4)
    return pl.pallas_call(
        functools.partial(_mm_kernel, n_a=len(a_parts), has_resid=resid is not None),
        grid=(n // tn, m // tm),
        in_specs=in_specs,
        out_specs=pl.BlockSpec((tm, tn), lambda j, i: (i, j)),
        out_shape=jax.ShapeDtypeStruct((m, n), out_dtype),
        scratch_shapes=[pltpu.VMEM((k, tn), BF16)],
        compiler_params=_params(("arbitrary", "arbitrary"), _vmem_limit(blk)),
        name="matmul",
    )(*args)


def _mem_kv_kernel(mem_ref, g_ref, wk_ref, wv_ref, k_ref, v_ref, memn_ref):
    @pl.when(pl.program_id(0) == 0)
    def _():
        x = mem_ref[...]
        ms = jnp.mean(x * x, axis=-1, keepdims=True)
        memn_ref[...] = (x * lax.rsqrt(ms + EPS) * g_ref[...]).astype(memn_ref.dtype)

    a = memn_ref[...]
    k_ref[...] = jnp.dot(a, wk_ref[...].astype(BF16), preferred_element_type=F32).astype(k_ref.dtype)
    v_ref[...] = jnp.dot(a, wv_ref[...].astype(BF16), preferred_element_type=F32).astype(v_ref.dtype)


def _mem_kv(mem, gain, w_k, w_v, *, tn=512):
    m, d = mem.shape
    n = w_k.shape[1]
    blk = 2 * m * d * 4 + m * d * 2 + 4 * d * tn * 4 + 2 * d * tn * 2 + 4 * m * tn * 2 + 2 * m * tn * 4
    return pl.pallas_call(
        _mem_kv_kernel,
        grid=(n // tn,),
        in_specs=[pl.BlockSpec((m, d), lambda j: (0, 0)),
                  pl.BlockSpec((1, d), lambda j: (0, 0)),
                  pl.BlockSpec((d, tn), lambda j: (0, j)),
                  pl.BlockSpec((d, tn), lambda j: (0, j))],
        out_specs=[pl.BlockSpec((m, tn), lambda j: (0, j)),
                   pl.BlockSpec((m, tn), lambda j: (0, j))],
        out_shape=[jax.ShapeDtypeStruct((m, n), BF16), jax.ShapeDtypeStruct((m, n), BF16)],
        scratch_shapes=[pltpu.VMEM((m, d), BF16)],
        compiler_params=_params(("arbitrary",), _vmem_limit(blk)),
        name="mem_kv",
    )(mem, gain.reshape(1, d), w_k, w_v)


def _pool_kernel(u_ref, up_ref, un_ref, w_ref, s_ref, o_ref, *, tm, seq, group):
    i = pl.program_id(0)
    last = pl.num_programs(0) - 1
    n = tm + 2 * SUBLANES
    row = lax.broadcasted_iota(jnp.int32, (tm, 1), 0) + i * tm
    for gi, w in enumerate(POOL_WINDOWS):
        cs = slice(gi * group, (gi + 1) * group)
        u = u_ref[:, cs]
        prev = jnp.where(i > 0, up_ref[:, cs], 0.0)
        nxt = jnp.where(i < last, un_ref[:, cs], 0.0)
        f = jnp.concatenate([prev, u, nxt], axis=0)
        step = 1
        while step < w:
            f = f + pltpu.roll(f, n - step, 0)
            step *= 2
        half = w // 2
        first = SUBLANES - half
        if first:
            f = pltpu.roll(f, n - first, 0)
        win = f[:tm]
        lo = jnp.maximum(row - half, 0)
        hi = jnp.minimum(row + half - 1, seq - 1)
        cnt = (hi - lo + 1).astype(F32)
        mixed = win / cnt - u
        y = jnp.dot(mixed.astype(BF16), w_ref[gi].astype(BF16), preferred_element_type=F32)
        o_ref[:, cs] = (y * s_ref[:, cs]).astype(o_ref.dtype)


def _pool_mixer(proj, pool_w, pool_scale, *, tm=512):
    t = proj.shape[0]
    ng, group, _ = pool_w.shape
    width = ng * group
    rpb = tm // SUBLANES
    nhalo = t // SUBLANES
    blk = 2 * (tm * width * 4 + 2 * SUBLANES * width * 4 + ng * group * group * 4 + tm * width * 2) + 6 * tm * group * 4
    return pl.pallas_call(
        functools.partial(_pool_kernel, tm=tm, seq=t, group=group),
        grid=(t // tm,),
        in_specs=[pl.BlockSpec((tm, width), lambda i: (i, 0)),
                  pl.BlockSpec((SUBLANES, width), lambda i: (jnp.maximum(i * rpb - 1, 0), 0)),
                  pl.BlockSpec((SUBLANES, width), lambda i: (jnp.minimum((i + 1) * rpb, nhalo - 1), 0)),
                  pl.BlockSpec((ng, group, group), lambda i: (0, 0, 0)),
                  pl.BlockSpec((1, width), lambda i: (0, 0))],
        out_specs=pl.BlockSpec((tm, width), lambda i: (i, 0)),
        out_shape=jax.ShapeDtypeStruct((t, width), BF16),
        compiler_params=_params(("arbitrary",), _vmem_limit(blk)),
        name="pool_mixer",
    )(proj, proj, proj, pool_w, pool_scale.reshape(1, width))


def _chunk_cumsum(x, reverse):
    n = x.shape[0]
    pos = lax.broadcasted_iota(jnp.int32, x.shape, 0) % CHUNK
    s = 1
    while s < CHUNK:
        if reverse:
            x = x + jnp.where(pos < CHUNK - s, pltpu.roll(x, n - s, 0), 0.0)
        else:
            x = x + jnp.where(pos >= s, pltpu.roll(x, s, 0), 0.0)
        s *= 2
    return x


def _lower_bound(lb_raw, layer):
    e = jnp.exp(lb_raw - jnp.max(lb_raw, axis=0, keepdims=True))
    return jnp.sum(e[:layer + 1], axis=0, keepdims=True) / jnp.sum(e, axis=0, keepdims=True)


def _hgrn_direction(q_raw, f_raw, v, lb, st_ref, h, mask, *, reverse):
    tb = q_raw.shape[0]
    nc = tb // CHUNK
    q = q_raw * jax.nn.sigmoid(q_raw)
    f = lb + (1.0 - lb) * jax.nn.sigmoid(f_raw)
    logf = jnp.log(f)
    k = 1.0 - f
    b = _chunk_cumsum(logf, reverse)
    b3 = b.reshape(nc, CHUNK, HEAD)
    edge = 0 if reverse else CHUNK - 1
    b_last = b3[:, edge:edge + 1, :]
    qd = (q * jnp.exp(b)).astype(BF16)
    kd = (k * jnp.exp(-b)).astype(BF16)
    k_end = (k.reshape(nc, CHUNK, HEAD) * jnp.exp(b_last - b3)).astype(BF16)
    decay = jnp.exp(b_last)
    vb = v.astype(BF16)

    a = lax.dot_general(qd, kd, (((1,), (1,)), ((), ())), preferred_element_type=F32)
    a = jnp.where(mask, a, 0.0)
    o_intra = jnp.dot(a.astype(BF16), vb, preferred_element_type=F32)

    st = st_ref[h]
    o_inter = [None] * nc
    order = range(nc - 1, -1, -1) if reverse else range(nc)
    for c in order:
        rows = slice(c * CHUNK, (c + 1) * CHUNK)
        o_inter[c] = lax.dot_general(qd[rows], st.astype(BF16), (((1,), (1,)), ((), ())),
                                     preferred_element_type=F32)
        d_st = lax.dot_general(vb[rows], k_end[c], (((0,), (0,)), ((), ())), preferred_element_type=F32)
        st = st * decay[c] + d_st
    st_ref[h] = st
    return o_intra + jnp.concatenate(o_inter, axis=0)


def _intra_mask(tb, reverse):
    r = lax.broadcasted_iota(jnp.int32, (tb, tb), 0)
    c = lax.broadcasted_iota(jnp.int32, (tb, tb), 1)
    same = (r // CHUNK) == (c // CHUNK)
    return same & ((c >= r) if reverse else (c <= r))


def _hgrn_fwd_kernel(q_ref, f_ref, i_ref, lb_ref, o_ref, st_ref, *, layer, heads):
    @pl.when(pl.program_id(0) == 0)
    def _():
        st_ref[...] = jnp.zeros_like(st_ref)

    mask = _intra_mask(q_ref.shape[0], False)

    for h in range(heads):
        hs = slice(h * HEAD, (h + 1) * HEAD)
        lb = _lower_bound(lb_ref[:, hs], layer)
        o_ref[:, hs] = _hgrn_direction(q_ref[:, hs], f_ref[:, hs], i_ref[:, hs], lb, st_ref, h, mask,
                                       reverse=False)


def _hgrn_bwd_kernel(q_ref, f_ref, i_ref, g_ref, of_ref, lb_ref, gain_ref, y_ref, st_ref, *, layer, heads):
    @pl.when(pl.program_id(0) == 0)
    def _():
        st_ref[...] = jnp.zeros_like(st_ref)

    mask = _intra_mask(q_ref.shape[0], True)

    for h in range(heads):
        hs = slice(h * HEAD, (h + 1) * HEAD)
        lb = _lower_bound(lb_ref[:, hs], layer)
        o = of_ref[:, hs] + _hgrn_direction(q_ref[:, hs], f_ref[:, hs], i_ref[:, hs], lb, st_ref, h, mask,
                                            reverse=True)
        o = o * lax.rsqrt(jnp.mean(o * o, axis=-1, keepdims=True) + EPS) * gain_ref[:, hs]
        g = g_ref[:, hs]
        y_ref[:, hs] = (o * (g * jax.nn.sigmoid(g))).astype(y_ref.dtype)


def _hgrn_mixer(proj, lb_fwd, lb_bwd, hgrn_norm, *, layer, width, col0, tb=256):
    t = proj.shape[0]
    heads = width // HEAD
    nb = t // tb
    c = col0 // width
    layers = lb_fwd.shape[0]
    blk_f = 2 * (3 * tb * width * 4 + layers * width * 4 + tb * width * 4) + heads * HEAD * HEAD * 4
    o_f = pl.pallas_call(
        functools.partial(_hgrn_fwd_kernel, layer=layer, heads=heads),
        grid=(nb,),
        in_specs=[pl.BlockSpec((tb, width), lambda b: (b, c)),
                  pl.BlockSpec((tb, width), lambda b: (b, c + 1)),
                  pl.BlockSpec((tb, width), lambda b: (b, c + 3)),
                  pl.BlockSpec((layers, width), lambda b: (0, 0))],
        out_specs=pl.BlockSpec((tb, width), lambda b: (b, 0)),
        out_shape=jax.ShapeDtypeStruct((t, width), F32),
        scratch_shapes=[pltpu.VMEM((heads, HEAD, HEAD), F32)],
        compiler_params=_params(("arbitrary",), _vmem_limit(blk_f)),
        name="hgrn_fwd",
    )(proj, proj, proj, lb_fwd)
    blk_b = 2 * (5 * tb * width * 4 + layers * width * 4 + width * 4 + tb * width * 2) + heads * HEAD * HEAD * 4
    return pl.pallas_call(
        functools.partial(_hgrn_bwd_kernel, layer=layer, heads=heads),
        grid=(nb,),
        in_specs=[pl.BlockSpec((tb, width), lambda b: (nb - 1 - b, c)),
                  pl.BlockSpec((tb, width), lambda b: (nb - 1 - b, c + 2)),
                  pl.BlockSpec((tb, width), lambda b: (nb - 1 - b, c + 3)),
                  pl.BlockSpec((tb, width), lambda b: (nb - 1 - b, c + 4)),
                  pl.BlockSpec((tb, width), lambda b: (nb - 1 - b, 0)),
                  pl.BlockSpec((layers, width), lambda b: (0, 0)),
                  pl.BlockSpec((1, width), lambda b: (0, 0))],
        out_specs=pl.BlockSpec((tb, width), lambda b: (nb - 1 - b, 0)),
        out_shape=jax.ShapeDtypeStruct((t, width), BF16),
        scratch_shapes=[pltpu.VMEM((heads, HEAD, HEAD), F32)],
        compiler_params=_params(("arbitrary",), _vmem_limit(blk_b)),
        name="hgrn_bwd",
    )(proj, proj, proj, proj, o_f, lb_bwd, hgrn_norm.reshape(1, width))


def _xattn_kernel(q_ref, k_ref, v_ref, o_ref, *, scale):
    s = lax.dot_general(q_ref[...], k_ref[...], (((1,), (1,)), ((), ())), preferred_element_type=F32) * scale
    s = s - jnp.max(s, axis=-1, keepdims=True)
    p = jnp.exp(s)
    p = p / jnp.sum(p, axis=-1, keepdims=True)
    o_ref[...] = jnp.dot(p.astype(BF16), v_ref[...], preferred_element_type=F32).astype(o_ref.dtype)


def _xattn(q, k, v, *, tm=1024):
    t, d = q.shape
    m = k.shape[0]
    dh = d // XATTN_HEADS
    blk = 2 * (2 * tm * dh * 2 + 2 * m * dh * 2) + 4 * tm * m * 4 + tm * dh * 4
    return pl.pallas_call(
        functools.partial(_xattn_kernel, scale=dh ** -0.5),
        grid=(XATTN_HEADS, t // tm),
        in_specs=[pl.BlockSpec((tm, dh), lambda h, i: (i, h)),
                  pl.BlockSpec((m, dh), lambda h, i: (0, h)),
                  pl.BlockSpec((m, dh), lambda h, i: (0, h))],
        out_specs=pl.BlockSpec((tm, dh), lambda h, i: (i, h)),
        out_shape=jax.ShapeDtypeStruct((t, d), BF16),
        compiler_params=_params(("arbitrary", "arbitrary"), _vmem_limit(blk)),
        name="xattn",
    )(q, k, v)


def _pack_halves(x):
    w = x.shape[1] // 2
    hi = lax.bitcast_convert_type(x[:, :w].astype(BF16).astype(F32), U32)
    lo = lax.bitcast_convert_type(x[:, w:].astype(BF16).astype(F32), U32)
    return hi | (lo >> 16)


def _unpack_halves(p):
    hi = lax.bitcast_convert_type(p & jnp.uint32(0xFFFF0000), F32)
    lo = lax.bitcast_convert_type(p << 16, F32)
    return hi, lo


def _load_row_slabs(flat_ref, s, j0, nj):
    rows = flat_ref.shape[0] // s
    return jnp.concatenate([flat_ref[pl.ds(j0 + j, rows, stride=s), :] for j in range(nj)], axis=1)


def _store_row_slabs(flat_ref, s, val):
    rows = flat_ref.shape[0] // s
    for j in range(s):
        flat_ref[pl.ds(j, rows, stride=s), :] = val[:, j * LANES:(j + 1) * LANES]


def _router_kernel(x_ref, g_ref, wr_ref, hp_ref, lg_ref, whi_ref, wlo_ref):
    @pl.when(pl.program_id(0) == 0)
    def _():
        w = wr_ref[...]
        w_hi = w.astype(BF16)
        whi_ref[...] = w_hi
        wlo_ref[...] = (w - w_hi.astype(F32)).astype(BF16)

    x = x_ref[...]
    ms = jnp.mean(x * x, axis=-1, keepdims=True)
    h = x * lax.rsqrt(ms + EPS) * g_ref[...]
    _store_row_slabs(hp_ref, hp_ref.shape[0] // x_ref.shape[0], _pack_halves(h))
    h_hi = h.astype(BF16)
    h_lo = (h - h_hi.astype(F32)).astype(BF16)
    lg_ref[...] = (jnp.dot(h_hi, whi_ref[...], preferred_element_type=F32)
                   + (jnp.dot(h_hi, wlo_ref[...], preferred_element_type=F32)
                      + jnp.dot(h_lo, whi_ref[...], preferred_element_type=F32)))


def _router(x, gain, w_router, *, tm=256):
    t, d = x.shape
    nslab = d // 2 // LANES
    blk = 2 * (tm * d * 4 + d * 4 + d * ROUTER_COLS * 4 + tm * d * 2 + tm * ROUTER_COLS * 4) + 3 * tm * d * 4
    hp, logits = pl.pallas_call(
        _router_kernel,
        grid=(t // tm,),
        in_specs=[pl.BlockSpec((tm, d), lambda i: (i, 0)),
                  pl.BlockSpec((1, d), lambda i: (0, 0)),
                  pl.BlockSpec((d, ROUTER_COLS), lambda i: (0, 0))],
        out_specs=[pl.BlockSpec((tm * nslab, LANES), lambda i: (i, 0)),
                   pl.BlockSpec((tm, ROUTER_COLS), lambda i: (i, 0))],
        out_shape=[jax.ShapeDtypeStruct((t * nslab, LANES), U32),
                   jax.ShapeDtypeStruct((t, ROUTER_COLS), F32)],
        scratch_shapes=[pltpu.VMEM((d, ROUTER_COLS), BF16), pltpu.VMEM((d, ROUTER_COLS), BF16)],
        compiler_params=_params(("arbitrary",), _vmem_limit(blk)),
        name="moe_router",
    )(x, gain.reshape(1, d), w_router)
    return hp.reshape(t, nslab, LANES), logits


def _start_row_copies(idx_ref, base, n, src_ref, dst_at, sem):
    def issue(r, carry):
        pltpu.make_async_copy(src_ref.at[idx_ref[base + r]], dst_at(r), sem).start()
        return carry

    lax.fori_loop(0, n, issue, 0, unroll=8)


def _wait_row_copies(n, src_ref, dst_at, sem):
    def drain(r, carry):
        pltpu.make_async_copy(src_ref.at[0], dst_at(r), sem).wait()
        return carry

    lax.fori_loop(0, n, drain, 0, unroll=8)


def _dispatch_rows_kernel(dest_ref, pad0_ref, src_ref, xs_ref, buf, zeros, sem, zsem, *, bm):
    i = pl.program_id(0)
    nt = pl.num_programs(0)
    tm = src_ref.shape[0]
    slot = i % 2

    def fill_copy(e):
        row0 = pl.multiple_of(jnp.maximum(pad0_ref[e], 0), bm)
        return pltpu.make_async_copy(zeros, xs_ref.at[pl.ds(row0, bm)], zsem)

    @pl.when(i == 0)
    def _():
        zeros[...] = jnp.zeros_like(zeros)

        def fill(e, carry):
            @pl.when(pad0_ref[e] >= 0)
            def _():
                fill_copy(e).start()

            return carry

        lax.fori_loop(0, pad0_ref.shape[0], fill, 0)

        def fill_wait(e, carry):
            @pl.when(pad0_ref[e] >= 0)
            def _():
                fill_copy(e).wait()

            return carry

        lax.fori_loop(0, pad0_ref.shape[0], fill_wait, 0)

    def wait_tile(s):
        def drain(r, carry):
            for _ in range(TOP_K):
                pltpu.make_async_copy(buf.at[s, 0], xs_ref.at[0], sem.at[s]).wait()
            return carry

        lax.fori_loop(0, tm, drain, 0, unroll=4)

    @pl.when(i >= 2)
    def _():
        wait_tile(slot)

    buf[slot] = src_ref[...]

    def issue(r, carry):
        for k in range(TOP_K):
            row = dest_ref[(i * tm + r) * TOP_K + k]
            pltpu.make_async_copy(buf.at[slot, r], xs_ref.at[row], sem.at[slot]).start()
        return carry

    lax.fori_loop(0, tm, issue, 0, unroll=4)

    @pl.when(i == nt - 1)
    def _():
        wait_tile(slot)

        @pl.when(nt >= 2)
        def _():
            wait_tile(1 - slot)


def _dispatch_rows(dest, pad0, src, *, p, bm, tm=256):
    t = src.shape[0]
    slab = src.shape[1:]
    blk = 5 * tm * slab[0] * slab[1] * 4 + bm * slab[0] * slab[1] * 4
    return pl.pallas_call(
        functools.partial(_dispatch_rows_kernel, bm=bm),
        grid_spec=pltpu.PrefetchScalarGridSpec(
            num_scalar_prefetch=2,
            grid=(t // tm,),
            in_specs=[pl.BlockSpec((tm,) + slab, lambda i, dest, pad0: (i, 0, 0))],
            out_specs=pl.BlockSpec(memory_space=pltpu.HBM),
            scratch_shapes=[pltpu.VMEM((2, tm) + slab, src.dtype), pltpu.VMEM((bm,) + slab, src.dtype),
                            pltpu.SemaphoreType.DMA((2,)), pltpu.SemaphoreType.DMA(())],
        ),
        out_shape=jax.ShapeDtypeStruct((p,) + slab, src.dtype),
        compiler_params=_params(("arbitrary",), _vmem_limit(blk)),
        name="dispatch_rows",
    )(dest, pad0, src)


BLOCK_COPY_PRIORITY = 0


def _start_first_blocks(cnt_ref, start_ref, e, in_copy_at):
    n = cnt_ref[e] >> 1
    b = start_ref[e]

    @pl.when(n > 0)
    def _():
        in_copy_at(b, 0).start(priority=BLOCK_COPY_PRIORITY)

    @pl.when(n > 1)
    def _():
        in_copy_at(b + 1, 1).start(priority=BLOCK_COPY_PRIORITY)


def _expert_grid_step(cnt_ref, start_ref, done_ref, in_copy_at, out_copy_at, prologue, compute, tail_fill):
    e = pl.program_id(1)
    ne = pl.num_programs(1)
    step = pl.program_id(0) * ne + e
    last_step = pl.num_programs(0) * ne - 1
    nblk = cnt_ref[e] >> 1
    tail_half = (cnt_ref[e] & 1) == 1
    blk0 = start_ref[e]

    @pl.when(step == 0)
    def _():
        done_ref[0] = 0
        _start_first_blocks(cnt_ref, start_ref, e, in_copy_at)

    done = done_ref[0]

    @pl.when(nblk > 0)
    def _():
        prologue()

        def body(k, carry):
            in_slot = k % 2
            out_slot = (done + k) % 2
            in_copy_at(blk0 + k, in_slot).wait()

            @pl.when(done + k >= 2)
            def _():
                out_copy_at(0, out_slot).wait()

            half = tail_half & (k == nblk - 1)

            @pl.when(jnp.logical_not(half))
            def _():
                compute(in_slot, out_slot, False)

            @pl.when(half)
            def _():
                compute(in_slot, out_slot, True)

            out_copy_at(blk0 + k, out_slot).start(priority=BLOCK_COPY_PRIORITY)

            @pl.when(k + 2 < nblk)
            def _():
                in_copy_at(blk0 + k + 2, in_slot).start(priority=BLOCK_COPY_PRIORITY)

            return carry

        lax.fori_loop(0, nblk, body, 0)
        done_ref[0] = done + nblk

    @pl.when(e == ne - 1)
    def _():
        tail_fill(start_ref[ne - 1] + (cnt_ref[ne - 1] >> 1))

    @pl.when(step < last_step)
    def _():
        _start_first_blocks(cnt_ref, start_ref, (e + 1) % ne, in_copy_at)

    @pl.when(step == last_step)
    def _():
        total = done + nblk

        @pl.when(total >= 2)
        def _():
            out_copy_at(0, total % 2).wait()

        @pl.when(total >= 1)
        def _():
            out_copy_at(0, (total - 1) % 2).wait()


def _zero_fill_blocks(first, nblocks, zbuf, dst_at, zsem):
    zbuf[...] = jnp.zeros_like(zbuf)

    def fill(b, carry):
        pltpu.make_async_copy(zbuf, dst_at(b), zsem).start()
        return carry

    lax.fori_loop(first, nblocks, fill, 0)

    def fill_wait(b, carry):
        pltpu.make_async_copy(zbuf, dst_at(b), zsem).wait()
        return carry

    lax.fori_loop(first, nblocks, fill_wait, 0)


def _moe_up_kernel(cnt_ref, start_ref, xs_ref, w1_ref, w3_ref, h_ref, w1b_ref, w3b_ref, xbuf, obuf, xsem, osem,
                   done_ref, zbuf, zsem, *, bm):
    f = pl.program_id(0)
    d, fc = w1b_ref.shape
    nslab = d // 2 // LANES
    xrows = bm * nslab

    def x_copy(blk, slot):
        row0 = pl.multiple_of(blk * xrows, xrows)
        return pltpu.make_async_copy(xs_ref.at[pl.ds(row0, xrows)], xbuf.at[slot], xsem.at[slot])

    def h_copy(blk, slot):
        row0 = pl.multiple_of(blk * bm, bm)
        col0 = pl.multiple_of(f * fc, fc)
        return pltpu.make_async_copy(obuf.at[slot], h_ref.at[pl.ds(row0, bm), pl.ds(col0, fc)], osem.at[slot])

    def cast_weights():
        w1b_ref[...] = w1_ref[0].astype(BF16)
        w3b_ref[...] = w3_ref[0].astype(BF16)

    def compute(in_slot, out_slot, half):
        rows = bm // 2 if half else bm
        xin = xbuf.at[in_slot, pl.ds(0, rows * nslab)] if half else xbuf.at[in_slot]
        hi, lo = _unpack_halves(_load_row_slabs(xin, nslab, 0, nslab))
        hi = hi.astype(BF16)
        lo = lo.astype(BF16)
        d2 = d // 2
        a = (jnp.dot(hi, w1b_ref[:d2, :], preferred_element_type=F32)
             + jnp.dot(lo, w1b_ref[d2:, :], preferred_element_type=F32))
        c = (jnp.dot(hi, w3b_ref[:d2, :], preferred_element_type=F32)
             + jnp.dot(lo, w3b_ref[d2:, :], preferred_element_type=F32))
        res = (a * jax.nn.sigmoid(a) * c).astype(obuf.dtype)
        if half:
            obuf[out_slot, :rows] = res
            obuf[out_slot, rows:] = jnp.zeros((bm - rows, fc), obuf.dtype)
        else:
            obuf[out_slot] = res

    def tail_fill(first):
        col0 = pl.multiple_of(f * fc, fc)
        _zero_fill_blocks(first, h_ref.shape[0] // bm, zbuf,
                          lambda b: h_ref.at[pl.ds(pl.multiple_of(b * bm, bm), bm), pl.ds(col0, fc)], zsem)

    _expert_grid_step(cnt_ref, start_ref, done_ref, x_copy, h_copy, cast_weights, compute, tail_fill)


def _moe_down_kernel(cnt_ref, start_ref, h_ref, w2_ref, y_ref, w2b_ref, hbuf, obuf, hsem, osem, done_ref, zbuf, zsem,
                     *, bm):
    nslab = w2b_ref.shape[1] // 2 // LANES

    def h_copy(blk, slot):
        row0 = pl.multiple_of(blk * bm, bm)
        return pltpu.make_async_copy(h_ref.at[pl.ds(row0, bm)], hbuf.at[slot], hsem.at[slot])

    def y_copy(blk, slot):
        row0 = pl.multiple_of(blk * bm * nslab, bm * nslab)
        return pltpu.make_async_copy(obuf.at[slot], y_ref.at[pl.ds(row0, bm * nslab)], osem.at[slot])

    def cast_weights():
        w2b_ref[...] = w2_ref[0].astype(BF16)

    def compute(in_slot, out_slot, half):
        rows = bm // 2 if half else bm
        hin = hbuf[in_slot, :rows] if half else hbuf[in_slot]
        packed = _pack_halves(jnp.dot(hin.astype(BF16), w2b_ref[...], preferred_element_type=F32))
        if half:
            _store_row_slabs(obuf.at[out_slot, pl.ds(0, rows * nslab)], nslab, packed)
            obuf[out_slot, rows * nslab:] = jnp.zeros(((bm - rows) * nslab, LANES), obuf.dtype)
        else:
            _store_row_slabs(obuf.at[out_slot], nslab, packed)

    def tail_fill(first):
        rows = bm * nslab
        _zero_fill_blocks(first, y_ref.shape[0] // rows, zbuf,
                          lambda b: y_ref.at[pl.ds(pl.multiple_of(b * rows, rows), rows)], zsem)

    _expert_grid_step(cnt_ref, start_ref, done_ref, h_copy, y_copy, cast_weights, compute, tail_fill)


def _moe_experts(xs, blk_cnt, blk_start, w1, w3, w2, *, bm, fc=512, nc=2048):
    p, nslab, _ = xs.shape
    d = 2 * nslab * LANES
    ne, _, de = w1.shape
    any_spec = pl.BlockSpec(memory_space=pltpu.HBM)
    blk_up = (4 * d * fc * 4 + 2 * d * fc * 2 + 2 * bm * nslab * LANES * 4 + 2 * bm * fc * 4
              + 2 * bm * d * 2 + 3 * bm * fc * 4)
    h = pl.pallas_call(
        functools.partial(_moe_up_kernel, bm=bm),
        grid_spec=pltpu.PrefetchScalarGridSpec(
            num_scalar_prefetch=2,
            grid=(de // fc, ne),
            in_specs=[any_spec,
                      pl.BlockSpec((1, d, fc), lambda f, e, cnt, start: (e, 0, f)),
                      pl.BlockSpec((1, d, fc), lambda f, e, cnt, start: (e, 0, f))],
            out_specs=any_spec,
            scratch_shapes=[pltpu.VMEM((d, fc), BF16), pltpu.VMEM((d, fc), BF16),
                            pltpu.VMEM((2, bm * nslab, LANES), U32), pltpu.VMEM((2, bm, fc), F32),
                            pltpu.SemaphoreType.DMA((2,)), pltpu.SemaphoreType.DMA((2,)),
                            pltpu.SMEM((1,), jnp.int32),
                            pltpu.VMEM((bm, fc), F32), pltpu.SemaphoreType.DMA(())],
        ),
        out_shape=jax.ShapeDtypeStruct((p, de), F32),
        compiler_params=_params(("arbitrary", "arbitrary"), _vmem_limit(blk_up)),
        name="moe_up",
    )(blk_cnt, blk_start, xs.reshape(p * nslab, LANES), w1, w3)
    assert nc == d, "expert-down writes whole row slabs"
    blk_dn = 2 * de * nc * 4 + de * nc * 2 + 3 * bm * de * 4 + 2 * bm * nslab * LANES * 4 + 2 * bm * nc * 4
    ys = pl.pallas_call(
        functools.partial(_moe_down_kernel, bm=bm),
        grid_spec=pltpu.PrefetchScalarGridSpec(
            num_scalar_prefetch=2,
            grid=(d // nc, ne),
            in_specs=[any_spec,
                      pl.BlockSpec((1, de, nc), lambda c, e, cnt, start: (e, 0, c))],
            out_specs=any_spec,
            scratch_shapes=[pltpu.VMEM((de, nc), BF16),
                            pltpu.VMEM((2, bm, de), F32), pltpu.VMEM((2, bm * nslab, LANES), U32),
                            pltpu.SemaphoreType.DMA((2,)), pltpu.SemaphoreType.DMA((2,)),
                            pltpu.SMEM((1,), jnp.int32),
                            pltpu.VMEM((bm * nslab, LANES), U32), pltpu.SemaphoreType.DMA(())],
        ),
        out_shape=jax.ShapeDtypeStruct((p * nslab, LANES), U32),
        compiler_params=_params(("arbitrary", "arbitrary"), _vmem_limit(blk_dn)),
        name="moe_down",
    )(blk_cnt, blk_start, h, w2)
    return ys.reshape(p, nslab, LANES)


def _combine_kernel(back_ref, x_ref, ys_ref, ew_ref, g_ref, o_ref, ybuf, sem, *, nc):
    i = pl.program_id(0)
    nt = pl.num_programs(0)
    tm, d = x_ref.shape
    t = nt * tm
    nslab = d // 2 // LANES
    slot = i % 2

    def row_at(s, k):
        def dst_at(r):
            return ybuf.at[s, pl.ds(pl.multiple_of((k * tm + r) * nslab, nslab), nslab)]
        return dst_at

    def start_tile(tile, s):
        for k in range(TOP_K):
            _start_row_copies(back_ref, k * t + tile * tm, tm, ys_ref, row_at(s, k), sem.at[s])

    @pl.when(i == 0)
    def _():
        start_tile(0, 0)

    @pl.when(i + 1 < nt)
    def _():
        start_tile(i + 1, 1 - slot)

    for k in range(TOP_K):
        _wait_row_copies(tm, ys_ref, row_at(slot, k), sem.at[slot])

    ew = ew_ref[...]
    w0 = ew[:, 0:1]
    w1 = ew[:, 1:2]
    half = nc // 2
    y0_ref = ybuf.at[slot, pl.ds(0, tm * nslab)]
    y1_ref = ybuf.at[slot, pl.ds(tm * nslab, tm * nslab)]
    pieces = []
    ss = None
    for c in range(x_ref.shape[1] // nc):
        hi0, lo0 = _unpack_halves(_load_row_slabs(y0_ref, nslab, c * half // LANES, half // LANES))
        hi1, lo1 = _unpack_halves(_load_row_slabs(y1_ref, nslab, c * half // LANES, half // LANES))
        for j, (p0, p1) in enumerate(((hi0, hi1), (lo0, lo1))):
            cols = slice(c * nc + j * half, c * nc + (j + 1) * half)
            z = x_ref[:, cols] + w0 * p0 + w1 * p1
            pieces.append((cols, z))
            s = jnp.sum(z * z, axis=-1, keepdims=True)
            ss = s if ss is None else ss + s
    inv = lax.rsqrt(ss / x_ref.shape[1] + EPS)
    for cols, z in pieces:
        o_ref[:, cols] = z * inv * g_ref[:, cols]


def _combine(x, ys, back, e_w, gain, *, nc, tm=256):
    t, d = x.shape
    nt = t // tm
    nslab = ys.shape[1]
    blk = 2 * (2 * tm * d * 4 + tm * LANES * 4 + d * 4) + 2 * TOP_K * tm * d * 2 + 3 * tm * d * 4
    return pl.pallas_call(
        functools.partial(_combine_kernel, nc=nc),
        grid_spec=pltpu.PrefetchScalarGridSpec(
            num_scalar_prefetch=1,
            grid=(nt,),
            in_specs=[pl.BlockSpec((tm, d), lambda i, back: (i, 0)),
                      pl.BlockSpec(memory_space=pltpu.HBM),
                      pl.BlockSpec((tm, TOP_K), lambda i, back: (i, 0)),
                      pl.BlockSpec((1, d), lambda i, back: (0, 0))],
            out_specs=pl.BlockSpec((tm, d), lambda i, back: (i, 0)),
            scratch_shapes=[pltpu.VMEM((2, TOP_K * tm * nslab, LANES), ys.dtype), pltpu.SemaphoreType.DMA((2,))],
        ),
        out_shape=jax.ShapeDtypeStruct((t, d), F32),
        compiler_params=_params(("arbitrary",), _vmem_limit(blk)),
        name="moe_combine",
    )(back, x, ys, e_w, gain.reshape(1, d))


def _route(logits, *, bm):
    t = logits.shape[0]
    g_logits = logits[:, :N_GROUPS]
    e_logits = logits[:, N_GROUPS:N_GROUPS + N_EXPERTS].reshape(t, N_GROUPS, EXPERTS_PER_GROUP)
    g_idx = jnp.argmax(g_logits, axis=-1).astype(jnp.int32)
    g_w = jnp.take_along_axis(jax.nn.softmax(g_logits, axis=-1), g_idx[:, None], axis=-1)
    e_sel = jnp.take_along_axis(e_logits, g_idx[:, None, None], axis=1)[:, 0]
    top_v, top_i = lax.top_k(e_sel, TOP_K)
    e_w = jax.nn.softmax(top_v, axis=-1) * g_w
    eid = g_idx[:, None] * EXPERTS_PER_GROUP + top_i.astype(jnp.int32)

    a = t * TOP_K
    flat_e = eid.reshape(a)
    onehot = (flat_e[:, None] == jnp.arange(N_EXPERTS, dtype=jnp.int32)[None, :]).astype(jnp.int32)
    csum = jnp.cumsum(onehot, axis=0)
    rank = jnp.take_along_axis(csum, flat_e[:, None], axis=1)[:, 0] - 1
    counts = csum[-1]
    padded = ((counts + bm - 1) // bm) * bm
    pad_end = jnp.cumsum(padded)
    start_pad = pad_end - padded
    dest = (start_pad[flat_e] + rank).astype(jnp.int32)
    tail_half = (counts > 0) & ((counts - 1) % bm < bm // 2)
    blk_cnt = (2 * (padded // bm) + tail_half).astype(jnp.int32)
    blk_start = (start_pad // bm).astype(jnp.int32)
    tail = jnp.arange(0, a + N_EXPERTS * bm, bm, dtype=jnp.int32)
    pad0 = jnp.concatenate([jnp.where(padded > 0, pad_end - bm, -1),
                            jnp.where(tail >= pad_end[-1], tail, -1)]).astype(jnp.int32)
    back = dest.reshape(t, TOP_K).T.reshape(a)
    return e_w, dest, back, pad0, blk_cnt, blk_start


MOE_BLOCK_ROWS = 256
MOE_DOWN_COLS = 4096


def kernel(x, mem, norm_mix, w_in, pool_w, pool_scale, lb_fwd, lb_bwd, hgrn_norm, w_out, norm_xattn, norm_mem,
           w_q, w_k, w_v, w_o, norm_moe, w_router_group, w_router_expert, w1, w3, w2, norm_final):
    bsz, seq, d = x.shape
    assert bsz == 1 and w_in.shape[0] == 1, "one sequence, one layer (the final norm is fused into the MoE combine)"
    l = 0
    pool_width = pool_w.shape[1] * pool_w.shape[2]
    hgrn_width = hgrn_norm.shape[1]
    xt = x.reshape(seq, d)
    mem_t = mem.reshape(mem.shape[1], d)
    h = _rmsnorm(xt, norm_mix[l], BF16)
    proj = _matmul([h], w_in[l], out_dtype=F32)
    y_pool = _pool_mixer(proj, pool_w[l], pool_scale[l])
    y_hgrn = _hgrn_mixer(proj, lb_fwd, lb_bwd, hgrn_norm[l], layer=l, width=hgrn_width, col0=pool_width)
    xt = _matmul([y_pool, y_hgrn], w_out[l], resid=xt, out_dtype=F32)
    k, v = _mem_kv(mem_t, norm_mem[l], w_k[l], w_v[l])
    hx = _rmsnorm(xt, norm_xattn[l], BF16)
    q = _matmul([hx], w_q[l], out_dtype=BF16)
    o = _xattn(q, k, v)
    xt = _matmul([o], w_o[l], resid=xt, out_dtype=F32)
    w_router = jnp.concatenate(
        [w_router_group[l],
         jnp.transpose(w_router_expert[l], (1, 0, 2)).reshape(d, N_EXPERTS),
         jnp.zeros((d, ROUTER_COLS - N_GROUPS - N_EXPERTS), F32)], axis=1)
    hp, logits = _router(xt, norm_moe[l], w_router)
    e_w, dest, back, pad0, blk_cnt, blk_start = _route(logits, bm=MOE_BLOCK_ROWS)
    xs = _dispatch_rows(dest, pad0, hp, p=TOP_K * seq + N_EXPERTS * MOE_BLOCK_ROWS, bm=MOE_BLOCK_ROWS)
    ys = _moe_experts(xs, blk_cnt, blk_start, w1[l], w3[l], w2[l], bm=MOE_BLOCK_ROWS, nc=MOE_DOWN_COLS)
    out = _combine(xt, ys, back, e_w, norm_final, nc=MOE_DOWN_COLS)
    return out.reshape(bsz, seq, d)
```
